```python
import jax
import jax.numpy as jnp
from jax import lax
import numpy as np

D_MODEL = 2048
BATCH = 2
SEQ = 4096
DEPTH = 1

GRID_W = 64
CTX_LEN = 256
HEAD_DIM = 64
MIX_DIM = D_MODEL
RWKV_DIM = MIX_DIM // 2
RWKV_HEADS = RWKV_DIM // HEAD_DIM
ATTN_DIM = MIX_DIM - RWKV_DIM
ATTN_HEADS = ATTN_DIM // HEAD_DIM
ATTN_KV_HEADS = 4
ATTN_GROUP = ATTN_HEADS // ATTN_KV_HEADS
KV_DIM = ATTN_KV_HEADS * HEAD_DIM
W_LORA = 96
A_LORA = 96
G_LORA = 256
RWKV_SPLITS = (RWKV_DIM, 2 * RWKV_DIM, 3 * RWKV_DIM, 3 * RWKV_DIM + W_LORA, 3 * RWKV_DIM + W_LORA + A_LORA)
RWKV_COLS = 3 * RWKV_DIM + W_LORA + A_LORA + G_LORA
IN_SPLITS = (RWKV_COLS, RWKV_COLS + ATTN_DIM, RWKV_COLS + ATTN_DIM + KV_DIM)
IN_COLS = RWKV_COLS + ATTN_DIM + 2 * KV_DIM
WINDOW = 128
BLOCK = 128
ROPE_BASE = 10000.0
N_EXPERTS = 64
N_GROUPS = 8
TOPK_GROUPS = 4
TOP_K = 8
EXPERT_FF = 512
SHARED_FF = 512
ROUTED_SCALE = 2.5
EXPERT_BLOCK = 128
NORM_EPS = 1e-6
LNX_EPS = 64e-5

kernel_name = 'hybrid_rwkv7_swa_moe_dit_block'


def rmsnorm(x, g):
    x32 = x.astype(jnp.float32)
    y = x32 * lax.rsqrt(jnp.mean(x32 * x32, axis=-1, keepdims=True) + NORM_EPS)
    return (y * g.astype(jnp.float32)).astype(x.dtype)


def modulate(h, shift, scale):
    return h * (1.0 + scale) + shift


def heads(t, n):
    return t.reshape(t.shape[:-1] + (n, t.shape[-1] // n))


def centred_shift(p, mu):
    prev = jnp.pad(p[:, :-1], ((0, 0), (1, 0), (0, 0)))
    nxt = jnp.pad(p[:, 1:], ((0, 0), (0, 1), (0, 0)))
    return p + mu * (0.5 * (prev + nxt) - p)


def rope_pairs(x, ang):
    x1, x2 = jnp.split(x, 2, axis=-1)
    cos = jnp.cos(ang)[None, :, None, :].astype(x.dtype)
    sin = jnp.sin(ang)[None, :, None, :].astype(x.dtype)
    return jnp.concatenate([x1 * cos - x2 * sin, x2 * cos + x1 * sin], axis=-1)


def axial_rope(x, row, col):
    n_freq = HEAD_DIM // 4
    freqs = ROPE_BASE ** (-jnp.arange(n_freq, dtype=jnp.float32) / n_freq)
    xr, xc = jnp.split(x, 2, axis=-1)
    return jnp.concatenate([rope_pairs(xr, row[:, None] * freqs), rope_pairs(xc, col[:, None] * freqs)], axis=-1)


def wkv7_scan(s0, r, w, k, v, a_, b_, reverse, emit):
    def step(s, inp):
        r_t, w_t, k_t, v_t, a_t, b_t = inp
        sa = jnp.einsum('bhvk,bhk->bhv', s, a_t)
        s = s * w_t[:, :, None, :] + sa[..., None] * b_t[:, :, None, :] + v_t[..., None] * k_t[:, :, None, :]
        y = jnp.einsum('bhvk,bhk->bhv', s, r_t) if emit else None
        return s, y
    xs = tuple(jnp.swapaxes(t, 0, 1) for t in (r, w, k, v, a_, b_))
    s, ys = lax.scan(step, s0, xs, reverse=reverse)
    return s, (jnp.swapaxes(ys, 0, 1) if emit else None)


def rwkv_streams(p, shift_mu, decay_bias, decay_up, iclr_bias, iclr_up, k_k, k_a):
    p = centred_shift(p, shift_mu)
    r, k, v, xw, xa, xg = jnp.split(p, RWKV_SPLITS, axis=-1)
    kk = heads((k * k_k).astype(jnp.float32), RWKV_HEADS)
    kk = kk * lax.rsqrt(jnp.sum(kk * kk, axis=-1, keepdims=True) + 1e-12)
    tw = jnp.tanh(xw)
    dirs = []
    for d in range(2):
        logw = -jax.nn.softplus(-(decay_bias[d] + tw @ decay_up[d])) - 0.5
        decay = jnp.exp(-jnp.exp(logw.astype(jnp.float32)))
        a = jax.nn.sigmoid((iclr_bias[d] + xa @ iclr_up[d]).astype(jnp.float32))
        k_d = k.astype(jnp.float32) * (1.0 + (a - 1.0) * k_a.astype(jnp.float32))
        dirs.append((heads(decay, RWKV_HEADS), heads(k_d, RWKV_HEADS), kk * heads(a, RWKV_HEADS)))
    r = heads(r.astype(jnp.float32), RWKV_HEADS)
    v = heads(v.astype(jnp.float32), RWKV_HEADS)
    return r, v, xg, -kk, dirs


def rwkv_output(y, r, v, k_sum, xg, gate_up, r_k, lnx_g, lnx_b):
    b, t = y.shape[:2]
    mean = jnp.mean(y, axis=-1, keepdims=True)
    var = jnp.mean((y - mean) ** 2, axis=-1, keepdims=True)
    yn = ((y - mean) * lax.rsqrt(var + LNX_EPS)).reshape(b, t, RWKV_DIM)
    bonus = jnp.sum(r * k_sum * r_k.astype(jnp.float32), axis=-1, keepdims=True) * v
    out = yn * lnx_g.astype(jnp.float32) + lnx_b.astype(jnp.float32) + bonus.reshape(b, t, RWKV_DIM)
    g = jax.nn.sigmoid(xg) @ gate_up
    return out.astype(xg.dtype) * g


def window_ctx_attention(q, k, v, k_ctx, v_ctx, sink):
    b, t = q.shape[:2]
    nb = t // BLOCK
    qb = q.reshape(b, nb, BLOCK, ATTN_KV_HEADS, ATTN_GROUP, HEAD_DIM)

    def band(z):
        zp = jnp.pad(z, ((0, 0), (BLOCK, BLOCK), (0, 0), (0, 0))).reshape(b, nb + 2, BLOCK, ATTN_KV_HEADS, HEAD_DIM)
        return jnp.concatenate([zp[:, :-2], zp[:, 1:-1], zp[:, 2:]], axis=2)

    kw, vw = band(k), band(v)
    qpos = jnp.arange(t).reshape(nb, BLOCK)
    kpos = (jnp.arange(nb)[:, None] - 1) * BLOCK + jnp.arange(3 * BLOCK)[None, :]
    valid = (jnp.abs(qpos[:, :, None] - kpos[:, None, :]) <= WINDOW) & (kpos[:, None, :] >= 0) & (kpos[:, None, :] < t)
    scale = HEAD_DIM ** -0.5
    s_w = jnp.einsum('bnqkgd,bnskd->bnkgqs', qb, kw).astype(jnp.float32) * scale
    s_w = jnp.where(valid[None, :, None, None], s_w, -jnp.inf)
    s_c = jnp.einsum('bnqkgd,bckd->bnkgqc', qb, k_ctx).astype(jnp.float32) * scale
    sk = sink.reshape(ATTN_KV_HEADS, ATTN_GROUP)[None, None, :, :, None, None].astype(jnp.float32)
    m = jnp.maximum(jnp.maximum(jnp.max(s_w, -1, keepdims=True), jnp.max(s_c, -1, keepdims=True)), sk)
    e_w = jnp.exp(s_w - m)
    e_c = jnp.exp(s_c - m)
    denom = jnp.sum(e_w, -1, keepdims=True) + jnp.sum(e_c, -1, keepdims=True) + jnp.exp(sk - m)
    o = (jnp.einsum('bnkgqs,bnskd->bnqkgd', (e_w / denom).astype(v.dtype), vw)
         + jnp.einsum('bnkgqc,bckd->bnqkgd', (e_c / denom).astype(v.dtype), v_ctx))
    return o.reshape(b, t, ATTN_DIM)


def ctx_attention(q, k, v, sink):
    b, c = q.shape[:2]
    qg = q.reshape(b, c, ATTN_KV_HEADS, ATTN_GROUP, HEAD_DIM)
    s = jnp.einsum('bqkgd,bckd->bkgqc', qg, k).astype(jnp.float32) * HEAD_DIM ** -0.5
    sk = sink.reshape(ATTN_KV_HEADS, ATTN_GROUP)[None, :, :, None, None].astype(jnp.float32)
    m = jnp.maximum(jnp.max(s, -1, keepdims=True), sk)
    e = jnp.exp(s - m)
    p = e / (jnp.sum(e, -1, keepdims=True) + jnp.exp(sk - m))
    return jnp.einsum('bkgqc,bckd->bqkgd', p.astype(v.dtype), v).reshape(b, c, ATTN_DIM)


def token_mixers(px, pc, row, col, update_ctx, shift_mu, k_k, k_a, r_k, decay_bias, decay_up,
                 iclr_bias, iclr_up, gate_up, lnx_g, lnx_b, attn_sink):
    rx, qx, kx, vx = jnp.split(px, IN_SPLITS, axis=-1)
    rc, qc, kc, vc = jnp.split(pc, IN_SPLITS, axis=-1)
    lora = (shift_mu, decay_bias, decay_up, iclr_bias, iclr_up, k_k, k_a)
    r_x, v_x, xg_x, na_x, dirs_x = rwkv_streams(rx, *lora)
    r_c, v_c, xg_c, na_c, dirs_c = rwkv_streams(rc, *lora)
    s0 = jnp.zeros((px.shape[0], RWKV_HEADS, HEAD_DIM, HEAD_DIM), jnp.float32)
    ys_x, ys_c = [], []
    for d, reverse in enumerate((False, True)):
        dec_c, k_c, b_c = dirs_c[d]
        s_ctx, y_c = wkv7_scan(s0, r_c, dec_c, k_c, v_c, na_c, b_c, reverse, update_ctx)
        dec_x, k_x, b_x = dirs_x[d]
        _, y_x = wkv7_scan(s_ctx, r_x, dec_x, k_x, v_x, na_x, b_x, reverse, True)
        ys_x.append(y_x)
        ys_c.append(y_c)
    rwkv_x = rwkv_output(ys_x[0] + ys_x[1], r_x, v_x, dirs_x[0][1] + dirs_x[1][1], xg_x, gate_up, r_k, lnx_g, lnx_b)
    q = axial_rope(heads(qx, ATTN_HEADS), row, col)
    k = axial_rope(heads(kx, ATTN_KV_HEADS), row, col)
    k_ctx = heads(kc, ATTN_KV_HEADS)
    v_ctx = heads(vc, ATTN_KV_HEADS)
    attn_x = window_ctx_attention(q, k, heads(vx, ATTN_KV_HEADS), k_ctx, v_ctx, attn_sink)
    mix_x = jnp.concatenate([rwkv_x, attn_x], axis=-1)
    if not update_ctx:
        return mix_x, None
    rwkv_c = rwkv_output(ys_c[0] + ys_c[1], r_c, v_c, dirs_c[0][1] + dirs_c[1][1], xg_c, gate_up, r_k, lnx_g, lnx_b)
    attn_c = ctx_attention(heads(qc, ATTN_HEADS), k_ctx, v_ctx, attn_sink)
    return mix_x, jnp.concatenate([rwkv_c, attn_c], axis=-1)


def swiglu(h, wg, wu, wd):
    return (jax.nn.silu(h @ wg) * (h @ wu)) @ wd


def grouped_experts(h, idx, gate, e_gate, e_up, e_down):
    n_tok, d = h.shape
    m = n_tok * TOP_K
    n_blk = -(-m // EXPERT_BLOCK) + N_EXPERTS
    e = idx.reshape(m)
    tok = jnp.repeat(jnp.arange(n_tok, dtype=jnp.int32), TOP_K)
    wt = gate.reshape(m)
    order = jnp.argsort(e)
    e_s, tok_s, wt_s = e[order], tok[order], wt[order]
    counts = jnp.bincount(e, length=N_EXPERTS)
    padded = (counts + EXPERT_BLOCK - 1) // EXPERT_BLOCK * EXPERT_BLOCK
    start = jnp.cumsum(counts) - counts
    pend = jnp.cumsum(padded)
    pstart = pend - padded
    dest = pstart[e_s] + (jnp.arange(m) - start[e_s])
    pad_tok = jnp.full((n_blk * EXPERT_BLOCK,), n_tok, jnp.int32).at[dest].set(tok_s)
    pad_wt = jnp.zeros((n_blk * EXPERT_BLOCK,), h.dtype).at[dest].set(wt_s)
    blk_expert = jnp.minimum(jnp.searchsorted(pend, jnp.arange(n_blk) * EXPERT_BLOCK, side='right'), N_EXPERTS - 1)
    h_pad = jnp.concatenate([h, jnp.zeros((1, d), h.dtype)], axis=0)

    def body(y, blk):
        tok_b, wt_b, e_b = blk
        xb = h_pad[tok_b]
        out = swiglu(xb, e_gate[e_b], e_up[e_b], e_down[e_b])
        return y.at[tok_b].add(out * wt_b[:, None]), None

    y, _ = lax.scan(body, jnp.zeros((n_tok + 1, d), h.dtype),
                    (pad_tok.reshape(n_blk, EXPERT_BLOCK), pad_wt.reshape(n_blk, EXPERT_BLOCK), blk_expert))
    return y[:n_tok]


def moe_ffn(h, w_router, expert_bias, e_gate, e_up, e_down, s_gate, s_up, s_down):
    n_tok = h.shape[0]
    scores = jax.nn.sigmoid((h @ w_router).astype(jnp.float32))
    biased = scores + expert_bias.astype(jnp.float32)
    grp = biased.reshape(n_tok, N_GROUPS, N_EXPERTS // N_GROUPS)
    grp_score = jnp.sum(lax.top_k(grp, 2)[0], axis=-1)
    _, top_grp = lax.top_k(grp_score, TOPK_GROUPS)
    grp_mask = jnp.any(top_grp[..., None] == jnp.arange(N_GROUPS), axis=1)
    masked = jnp.where(jnp.repeat(grp_mask, N_EXPERTS // N_GROUPS, axis=1), biased, -jnp.inf)
    _, idx = lax.top_k(masked, TOP_K)
    gate = jnp.take_along_axis(scores, idx, axis=1)
    gate = gate / jnp.sum(gate, axis=-1, keepdims=True) * ROUTED_SCALE
    routed = grouped_experts(h, idx, gate.astype(h.dtype), e_gate, e_up, e_down)
    return routed + swiglu(h, s_gate, s_up, s_down)


def setup_inputs(seed: int = 0) -> dict:
    key = jax.random.key(seed)
    ks = jax.random.split(key, 32)
    f32 = jnp.float32

    def nrm(k, shape, scale):
        return jax.random.normal(k, shape, f32) * scale

    d = D_MODEL
    return {
        'x': nrm(ks[0], (BATCH, SEQ, d), 1.0),
        'c': nrm(ks[1], (BATCH, d), 1.0),
        'ctx': nrm(ks[2], (BATCH, CTX_LEN, d), 1.0),
        'c_ctx': nrm(ks[3], (d,), 1.0),
        'w_mod': nrm(ks[4], (DEPTH, d, 6 * d), 0.5 * d ** -0.5),
        'b_mod': nrm(ks[5], (DEPTH, 6 * d), 0.02),
        'norm1_g': 1.0 + nrm(ks[6], (DEPTH, d), 0.02),
        'norm2_g': 1.0 + nrm(ks[7], (DEPTH, d), 0.02),
        'w_in': nrm(ks[8], (DEPTH, d, IN_COLS), d ** -0.5),
        'shift_mu': jax.random.uniform(ks[9], (DEPTH, RWKV_COLS), f32),
        'k_k': 0.85 + nrm(ks[10], (DEPTH, RWKV_DIM), 0.02),
        'k_a': 1.0 + nrm(ks[11], (DEPTH, RWKV_DIM), 0.02),
        'r_k': nrm(ks[12], (DEPTH, RWKV_HEADS, HEAD_DIM), 0.1),
        'decay_bias': -3.0 + nrm(ks[13], (DEPTH, 2, RWKV_DIM), 1.0),
        'decay_up': nrm(ks[14], (DEPTH, 2, W_LORA, RWKV_DIM), W_LORA ** -0.5),
        'iclr_bias': nrm(ks[15], (DEPTH, 2, RWKV_DIM), 0.5),
        'iclr_up': nrm(ks[16], (DEPTH, 2, A_LORA, RWKV_DIM), A_LORA ** -0.5),
        'gate_up': nrm(ks[17], (DEPTH, G_LORA, RWKV_DIM), G_LORA ** -0.5),
        'lnx_g': 1.0 + nrm(ks[18], (DEPTH, RWKV_DIM), 0.02),
        'lnx_b': nrm(ks[19], (DEPTH, RWKV_DIM), 0.02),
        'attn_sink': nrm(ks[20], (DEPTH, ATTN_HEADS), 1.0),
        'w_out': nrm(ks[21], (DEPTH, MIX_DIM, d), MIX_DIM ** -0.5),
        'w_router': nrm(ks[22], (DEPTH, d, N_EXPERTS), d ** -0.5),
        'expert_bias': nrm(ks[23], (DEPTH, N_EXPERTS), 0.01),
        'e_gate': nrm(ks[24], (DEPTH, N_EXPERTS, d, EXPERT_FF), d ** -0.5),
        'e_up': nrm(ks[25], (DEPTH, N_EXPERTS, d, EXPERT_FF), d ** -0.5),
        'e_down': nrm(ks[26], (DEPTH, N_EXPERTS, EXPERT_FF, d), EXPERT_FF ** -0.5),
        's_gate': nrm(ks[27], (DEPTH, d, SHARED_FF), d ** -0.5),
        's_up': nrm(ks[28], (DEPTH, d, SHARED_FF), d ** -0.5),
        's_down': nrm(ks[29], (DEPTH, SHARED_FF, d), SHARED_FF ** -0.5),
        'final_g': 1.0 + nrm(ks[30], (d,), 0.02),
    }


def reference(x, c, ctx, c_ctx, w_mod, b_mod, norm1_g, norm2_g, w_in, shift_mu, k_k, k_a, r_k,
              decay_bias, decay_up, iclr_bias, iclr_up, gate_up, lnx_g, lnx_b, attn_sink, w_out,
              w_router, expert_bias, e_gate, e_up, e_down, s_gate, s_up, s_down, final_g):
    b, t, d = x.shape
    rows = t // GRID_W
    row = jnp.repeat(jnp.arange(rows, dtype=jnp.float32), GRID_W)
    col = jnp.tile(jnp.arange(GRID_W, dtype=jnp.float32), rows)
    for layer in range(DEPTH):
        update_ctx = layer < DEPTH - 1
        mod_x = (jax.nn.silu(c) @ w_mod[layer] + b_mod[layer])[:, None, :]
        mod_c = jax.nn.silu(c_ctx) @ w_mod[layer] + b_mod[layer]
        sh1, sc1, gt1, sh2, sc2, gt2 = jnp.split(mod_x, 6, axis=-1)
        csh1, csc1, cgt1, csh2, csc2, cgt2 = jnp.split(mod_c, 6, axis=-1)
        hx = modulate(rmsnorm(x, norm1_g[layer]), sh1, sc1)
        hc = modulate(rmsnorm(ctx, norm1_g[layer]), csh1, csc1)
        mix_x, mix_c = token_mixers(hx @ w_in[layer], hc @ w_in[layer], row, col, update_ctx,
                                    shift_mu[layer], k_k[layer], k_a[layer], r_k[layer],
                                    decay_bias[layer], decay_up[layer], iclr_bias[layer], iclr_up[layer],
                                    gate_up[layer], lnx_g[layer], lnx_b[layer], attn_sink[layer])
        x = x + gt1 * (mix_x @ w_out[layer])
        h2 = modulate(rmsnorm(x, norm2_g[layer]), sh2, sc2)
        moe_w = (w_router[layer], expert_bias[layer], e_gate[layer], e_up[layer], e_down[layer],
                 s_gate[layer], s_up[layer], s_down[layer])
        if update_ctx:
            ctx = ctx + cgt1 * (mix_c @ w_out[layer])
            hc2 = modulate(rmsnorm(ctx, norm2_g[layer]), csh2, csc2)
            f = moe_ffn(jnp.concatenate([h2.reshape(-1, d), hc2.reshape(-1, d)], axis=0), *moe_w)
            x = x + gt2 * f[:b * t].reshape(x.shape)
            ctx = ctx + cgt2 * f[b * t:].reshape(ctx.shape)
        else:
            x = x + gt2 * moe_ffn(h2.reshape(-1, d), *moe_w).reshape(x.shape)
    return rmsnorm(x, final_g)
```

```python
import functools
import math

import jax
import jax.numpy as jnp
from jax import lax
from jax.experimental import pallas as pl
from jax.experimental.pallas import tpu as pltpu

GRID_W = 64
WINDOW = 128
ROPE_BASE = 10000.0
N_GROUPS = 8
TOPK_GROUPS = 4
TOP_K = 8
ROUTED_SCALE = 2.5
NORM_EPS = 1e-6
LNX_EPS = 64e-5
KK_EPS = 1e-12

LANES = 128
SUBLANES = 8
VMEM_LIMIT_BYTES = 56 * 1024 * 1024

HEAD_DIM = 64
CHUNK = 64
QBLOCK = 128

F32 = jnp.float32
BF16 = jnp.bfloat16
NEG_BIG = -1e30


def _bf(x):
    return x.astype(BF16)


def _dot(a, b, precision=None):
    return jnp.dot(a, b, preferred_element_type=F32, precision=precision)


def _dot_nt(a, b, precision=None):
    return lax.dot_general(a, b, (((1,), (1,)), ((), ())), preferred_element_type=F32,
                           precision=precision)


def _dot_tn(a, b):
    return lax.dot_general(a, b, (((0,), (0,)), ((), ())), preferred_element_type=F32)


def _params(*sem):
    return pltpu.CompilerParams(dimension_semantics=sem, vmem_limit_bytes=VMEM_LIMIT_BYTES)


def _mod_kernel(c_ref, w_ref, b_ref, o_ref):
    c = c_ref[...]
    s = c * jax.nn.sigmoid(c)
    hi = _bf(s)
    lo = _bf(s - hi.astype(F32))
    r = _dot(jnp.concatenate([hi, lo], axis=0), _bf(w_ref[...]))
    o_ref[...] = r[:SUBLANES] + r[SUBLANES:] + b_ref[...]


def _modulation(c8, w_mod, b_mod):
    d, n = w_mod.shape
    tn = 1024 if n % 1024 == 0 else n
    return pl.pallas_call(
        _mod_kernel,
        out_shape=jax.ShapeDtypeStruct((SUBLANES, n), F32),
        grid=(n // tn,),
        in_specs=[pl.BlockSpec((SUBLANES, d), lambda j: (0, 0)),
                  pl.BlockSpec((d, tn), lambda j: (0, j)),
                  pl.BlockSpec((1, tn), lambda j: (0, j))],
        out_specs=pl.BlockSpec((SUBLANES, tn), lambda j: (0, j)),
        compiler_params=_params("parallel"),
        name="adaln_modulation",
    )(c8, w_mod, b_mod)


def _inproj_kernel(x_ref, g_ref, m_ref, w_ref, o_ref):
    x = x_ref[0]
    ms = jnp.mean(x * x, axis=-1, keepdims=True)
    y = x * lax.rsqrt(ms + NORM_EPS) * g_ref[...]
    m = m_ref[0, 0]
    h = y * (1.0 + m[1:2]) + m[0:1]
    o_ref[0] = _dot(_bf(h), w_ref[...])


def _in_projection(h_all, g1, m1, w_in_p, ctx_len, tr):
    b, ta, d = h_all.shape
    ncols = w_in_p.shape[1]
    tn = ncols // 4 if (ncols // 4) % LANES == 0 and ncols % 4 == 0 else ncols
    nct = ctx_len // tr
    return pl.pallas_call(
        _inproj_kernel,
        out_shape=jax.ShapeDtypeStruct((b, ta, ncols), F32),
        grid=(ncols // tn, b, ta // tr),
        in_specs=[pl.BlockSpec((1, tr, d), lambda n, bb, i: (bb, i, 0)),
                  pl.BlockSpec((1, d), lambda n, bb, i: (0, 0)),
                  pl.BlockSpec((1, 1, 2, d), lambda n, bb, i: (bb, jnp.where(i >= nct, 1, 0), 0, 0)),
                  pl.BlockSpec((d, tn), lambda n, bb, i: (0, n))],
        out_specs=pl.BlockSpec((1, tr, tn), lambda n, bb, i: (bb, i, n)),
        compiler_params=_params("parallel", "parallel", "parallel"),
        name="norm_modulate_inproj",
    )(h_all, g1, m1, w_in_p)


def _wkv_prep_kernel(r_c, r_p, r_n, k_c, k_p, k_n, v_c, v_p, v_n, w_c, w_p, w_n, a_c, a_p, a_n,
                     g_c, g_p, g_n, mu_r, mu_k, mu_v, mu_w, mu_a, mu_g, kk_ref, ka_ref, rk_ref,
                     db_ref, du_ref, ib_ref, iu_ref, gu_ref,
                     rp_out, y0_out, m_out, n_out, bonus_out, gate_out,
                     *, nctx_chunks, n_chunks, prec_t):
    L = CHUNK
    c = pl.program_id(1)
    has_prev = jnp.logical_and(c != 0, c != nctx_chunks)
    has_next = jnp.logical_and(c != nctx_chunks - 1, c != n_chunks - 1)

    def shifted(cur, prv, nxt, mu):
        x = cur[0]
        rows = lax.broadcasted_iota(jnp.int32, (L, 1), 0)
        pr = jnp.where(has_prev, prv[0, SUBLANES - 1:SUBLANES, :], 0.0)
        nx = jnp.where(has_next, nxt[0, 0:1, :], 0.0)
        up = jnp.where(rows == 0, pr, pltpu.roll(x, 1, 0))
        dn = jnp.where(rows == L - 1, nx, pltpu.roll(x, L - 1, 0))
        return x + mu[...] * (0.5 * (up + dn) - x)

    r = shifted(r_c, r_p, r_n, mu_r)
    k = shifted(k_c, k_p, k_n, mu_k)
    v = shifted(v_c, v_p, v_n, mu_v)
    xw = shifted(w_c, w_p, w_n, mu_w)
    xa = shifted(a_c, a_p, a_n, mu_a)
    xg = shifted(g_c, g_p, g_n, mu_g)

    lane = lax.broadcasted_iota(jnp.int32, (1, LANES), 1)
    m0 = lane < HEAD_DIM

    def head_sum(x):
        s0 = jnp.sum(jnp.where(m0, x, 0.0), axis=-1, keepdims=True)
        s1 = jnp.sum(jnp.where(m0, 0.0, x), axis=-1, keepdims=True)
        return jnp.where(m0, s0, s1)

    def stack2(x):
        return jnp.concatenate([jnp.where(m0, x, 0.0), jnp.where(m0, 0.0, x)], axis=0)

    kx = k * kk_ref[...]
    kk = kx * lax.rsqrt(head_sum(kx * kx) + KK_EPS)
    tw = _bf(jnp.tanh(xw))
    xab = _bf(xa)

    ri = lax.broadcasted_iota(jnp.int32, (L, L), 0)
    ci = lax.broadcasted_iota(jnp.int32, (L, L), 1)
    ri2 = lax.broadcasted_iota(jnp.int32, (2 * L, 2 * L), 0)
    ci2 = lax.broadcasted_iota(jnp.int32, (2 * L, 2 * L), 1)
    rt = jnp.bitwise_and(ri2, L - 1)
    ct = jnp.bitwise_and(ci2, L - 1)
    eye = ri2 == ci2

    vs = stack2(v)
    vsb = _bf(vs)
    ksum = jnp.zeros_like(k)
    n_sq = int(math.log2(L)) - 1

    for d in range(2):
        zw = db_ref[d:d + 1, :] + _dot(tw, du_ref[d])
        lw = (-math.exp(-0.5)) * jax.nn.sigmoid(zw)
        sa = jax.nn.sigmoid(ib_ref[d:d + 1, :] + _dot(xab, iu_ref[d]))
        kd = k * (1.0 + (sa - 1.0) * ka_ref[...])
        ksum = ksum + kd
        bvec = kk * sa
        avec = -kk

        if d == 0:
            tri = (ci <= ri).astype(F32)
            strict = ct < rt
            incl = ct <= rt
        else:
            tri = (ci >= ri).astype(F32)
            strict = ct > rt
            incl = ct >= rt

        cum = _dot(tri, lw, precision=lax.Precision.HIGHEST)
        cend = jnp.sum(lw, axis=0, keepdims=True)
        e_in = jnp.exp(cum)
        e_ex = jnp.exp(cum - lw)
        e_neg = jnp.exp(-cum)
        e_end = jnp.exp(cend - cum)

        a_s = stack2(avec * e_ex)
        r_s = stack2(r * e_in)
        b_s = stack2(bvec * e_neg)
        k_s = stack2(kd * e_neg)
        bh_s = stack2(bvec * e_end)
        kh_s = stack2(kd * e_end)

        g = _dot_nt(_bf(jnp.concatenate([a_s, r_s], axis=0)),
                    _bf(jnp.concatenate([b_s, k_s], axis=0)))
        a_ab = jnp.where(strict, g[:2 * L, :2 * L], 0.0)
        a_ak = jnp.where(strict, g[:2 * L, 2 * L:], 0.0)
        a_rb = jnp.where(incl, g[2 * L:, :2 * L], 0.0)
        a_rk = jnp.where(incl, g[2 * L:, 2 * L:], 0.0)

        if prec_t is None:
            tmat = jnp.where(eye, 1.0, 0.0) + a_ab
            apow = a_ab
            for _ in range(n_sq):
                ab = _bf(apow)
                apow = _dot(ab, ab)
                tmat = tmat + _dot(_bf(apow), _bf(tmat))
        else:
            tmat = jnp.where(eye, 1.0, 0.0) + a_ab
            apow = a_ab
            for _ in range(n_sq):
                apow = _dot(apow, apow, precision=prec_t)
                tmat = tmat + _dot(apow, tmat, precision=prec_t)

        x1 = _dot(_bf(a_ak), vsb)
        tu = _dot(_bf(tmat), _bf(jnp.concatenate([a_s, x1], axis=1)))
        ry = _dot(_bf(a_rb), _bf(tu))
        rp = r_s + ry[:, :LANES]
        y0s = ry[:, LANES:] + _dot(_bf(a_rk), vsb)
        at = tu[:, :LANES]
        u0 = tu[:, LANES:]
        mmat = jnp.where(eye, jnp.exp(cend), 0.0) + _dot_tn(_bf(at), _bf(bh_s))
        nmat = _dot_tn(_bf(jnp.concatenate([u0, vs], axis=0)),
                       _bf(jnp.concatenate([bh_s, kh_s], axis=0)))

        rp_out[d, 0, 0, 0] = _bf(rp)
        y0_out[d, 0, 0, 0] = y0s[:L] + y0s[L:]
        m_out[d, 0, 0, 0] = _bf(mmat)
        n_out[d, 0, 0, 0] = nmat

    bonus_out[0] = head_sum(r * ksum * rk_ref[...]) * v
    gate_out[0] = _dot(_bf(jax.nn.sigmoid(xg)), gu_ref[...])


def _wkv_prep(p_all, lay, prm, ctx_len, prec_t):
    b, ta, _ = p_all.shape
    L = CHUNK
    nc = ta // L
    nctx = ctx_len // L
    npair = lay["rd"] // LANES
    gl = lay["gl"]
    rows8 = ta // SUBLANES

    def cur(colfn, width=LANES):
        return pl.BlockSpec((1, L, width), lambda bb, c, p: (bb, c, colfn(p)))

    def prv(colfn, width=LANES):
        return pl.BlockSpec((1, SUBLANES, width),
                            lambda bb, c, p: (bb, jnp.maximum(c * (L // SUBLANES) - 1, 0), colfn(p)))

    def nxt(colfn, width=LANES):
        return pl.BlockSpec((1, SUBLANES, width),
                            lambda bb, c, p: (bb, jnp.minimum((c + 1) * (L // SUBLANES), rows8 - 1), colfn(p)))

    def trio(colfn, width=LANES):
        return [cur(colfn, width), prv(colfn, width), nxt(colfn, width)]

    rb, kb, vb = lay["r"] // LANES, lay["k"] // LANES, lay["v"] // LANES
    wb, ab = lay["xw"] // LANES, lay["xa"] // LANES
    assert lay["xg"] % gl == 0
    gb = lay["xg"] // gl

    def vec(width=LANES):
        return pl.BlockSpec((1, width), lambda bb, c, p: (0, p))

    def vec0(width=LANES):
        return pl.BlockSpec((1, width), lambda bb, c, p: (0, 0))

    in_specs = (trio(lambda p: rb + p) + trio(lambda p: kb + p) + trio(lambda p: vb + p)
                + trio(lambda p: wb) + trio(lambda p: ab) + trio(lambda p: gb, gl)
                + [vec(), vec(), vec(), vec0(), vec0(), vec0(gl),
                   vec(), vec(), vec(),
                   pl.BlockSpec((2, LANES), lambda bb, c, p: (0, p)),
                   pl.BlockSpec((2, LANES, LANES), lambda bb, c, p: (0, 0, p)),
                   pl.BlockSpec((2, LANES), lambda bb, c, p: (0, p)),
                   pl.BlockSpec((2, LANES, LANES), lambda bb, c, p: (0, 0, p)),
                   pl.BlockSpec((gl, LANES), lambda bb, c, p: (0, p))])

    def opspec(rows):
        return pl.BlockSpec((2, 1, 1, 1, rows, LANES), lambda bb, c, p: (0, c, bb, p, 0, 0))

    out_specs = [opspec(2 * L), opspec(L), opspec(LANES), opspec(LANES),
                 pl.BlockSpec((1, L, LANES), lambda bb, c, p: (bb, c, p)),
                 pl.BlockSpec((1, L, LANES), lambda bb, c, p: (bb, c, p))]
    out_shape = [jax.ShapeDtypeStruct((2, nc, b, npair, 2 * L, LANES), BF16),
                 jax.ShapeDtypeStruct((2, nc, b, npair, L, LANES), F32),
                 jax.ShapeDtypeStruct((2, nc, b, npair, LANES, LANES), BF16),
                 jax.ShapeDtypeStruct((2, nc, b, npair, LANES, LANES), F32),
                 jax.ShapeDtypeStruct((b, ta, lay["rd"]), F32),
                 jax.ShapeDtypeStruct((b, ta, lay["rd"]), F32)]
    kern = functools.partial(_wkv_prep_kernel, nctx_chunks=nctx, n_chunks=nc, prec_t=prec_t)
    args = [p_all] * 18 + [prm["mu_r"], prm["mu_k"], prm["mu_v"], prm["mu_w"], prm["mu_a"], prm["mu_g"],
                           prm["k_k"], prm["k_a"], prm["r_k"], prm["decay_bias"], prm["decay_up"],
                           prm["iclr_bias"], prm["iclr_up"], prm["gate_up"]]
    return pl.pallas_call(
        kern, out_shape=out_shape, grid=(b, nc, npair), in_specs=in_specs, out_specs=out_specs,
        compiler_params=_params("parallel", "parallel", "parallel"),
        name="wkv_chunk_operators",
    )(*args)


def _wkv_seq_kernel(rp_ref, y0_ref, m_ref, n_ref, y_ref, s_ref, *, nb, npair):
    L = CHUNK

    @pl.when(pl.program_id(1) == 0)
    def _():
        s_ref[...] = jnp.zeros_like(s_ref)

    for bb in range(nb):
        for p in range(npair):
            i = bb * npair + p
            s0 = _bf(s_ref[i])
            ys = _dot_nt(rp_ref[0, 0, bb, p], s0)
            y_ref[0, bb, :, p * LANES:(p + 1) * LANES] = ys[:L] + ys[L:] + y0_ref[0, 0, bb, p]
            s_ref[i] = _dot(s0, m_ref[0, 0, bb, p]) + n_ref[0, 0, bb, p]


def _wkv_sweep(rp, y0, mm, nn, ctx_len):
    _, nc, b, npair, _, _ = rp.shape
    L = CHUNK
    nctx = ctx_len // L

    def chunk_of(d, c):
        rev = jnp.where(c < nctx, nctx - 1 - c, nc - 1 - (c - nctx))
        return jnp.where(d == 0, c, rev)

    def opspec(rows):
        return pl.BlockSpec((1, 1, b, npair, rows, LANES), lambda d, c: (d, chunk_of(d, c), 0, 0, 0, 0))

    return pl.pallas_call(
        functools.partial(_wkv_seq_kernel, nb=b, npair=npair),
        out_shape=jax.ShapeDtypeStruct((2, b, nc * L, npair * LANES), F32),
        grid=(2, nc),
        in_specs=[opspec(2 * L), opspec(L), opspec(LANES), opspec(LANES)],
        out_specs=pl.BlockSpec((1, b, L, npair * LANES), lambda d, c: (d, 0, chunk_of(d, c), 0)),
        scratch_shapes=[pltpu.VMEM((b * npair, LANES, LANES), F32)],
        compiler_params=_params("parallel", "arbitrary"),
        name="wkv_state_sweep",
    )(rp, y0, mm, nn)


def _attn_kernel(q_ref, kp_ref, kc_ref, kn_ref, vp_ref, vc_ref, vn_ref, kx_ref, vx_ref,
                 cq_ref, sq_ref, cp_ref, sp_ref, cn_ref, sn_ref, sink_ref, o_ref,
                 *, n_blocks, kv_heads, group):
    n = pl.program_id(1)
    qb = QBLOCK

    def rope(x, cos, sin):
        w = x.shape[1]
        reps = w // LANES
        cs = jnp.concatenate([cos] * reps, axis=1) if reps > 1 else cos
        sn = jnp.concatenate([sin] * reps, axis=1) if reps > 1 else sin
        lane = lax.broadcasted_iota(jnp.int32, (1, w), 1)
        sw = jnp.where(jnp.bitwise_and(lane, 16) == 0, pltpu.roll(x, w - 16, 1), pltpu.roll(x, 16, 1))
        return x * cs + sw * sn

    q = rope(q_ref[0], cq_ref[...], sq_ref[...]) * (HEAD_DIM ** -0.5)
    kwin = jnp.concatenate([rope(kp_ref[0], cp_ref[...], sp_ref[...]),
                            rope(kc_ref[0], cq_ref[...], sq_ref[...]),
                            rope(kn_ref[0], cn_ref[...], sn_ref[...]),
                            kx_ref[0]], axis=0)
    vwin = jnp.concatenate([vp_ref[0], vc_ref[0], vn_ref[0], vx_ref[0]], axis=0)
    nkeys = kwin.shape[0]

    rows = group * qb
    qi = jnp.bitwise_and(lax.broadcasted_iota(jnp.int32, (rows, nkeys), 0), qb - 1)
    s_i = lax.broadcasted_iota(jnp.int32, (rows, nkeys), 1)
    blk = s_i // qb + (n - 1)
    in_win = (s_i >= qi) & (s_i <= qi + 2 * WINDOW) & (blk >= 0) & (blk < n_blocks)
    valid = in_win | (s_i >= 3 * qb)
    rgrp = lax.broadcasted_iota(jnp.int32, (rows, 1), 0) // qb

    pieces = []
    for kh in range(kv_heads):
        kk_ = _bf(kwin[:, kh * HEAD_DIM:(kh + 1) * HEAD_DIM])
        vv_ = _bf(vwin[:, kh * HEAD_DIM:(kh + 1) * HEAD_DIM])
        qs = jnp.concatenate([q[:, (kh * group + g) * HEAD_DIM:(kh * group + g + 1) * HEAD_DIM]
                              for g in range(group)], axis=0)
        sk = jnp.zeros((rows, 1), F32)
        for g in range(group):
            h = kh * group + g
            sk = jnp.where(rgrp == g, sink_ref[0:1, h:h + 1], sk)
        s = jnp.where(valid, _dot_nt(_bf(qs), kk_), NEG_BIG)
        m = jnp.maximum(jnp.max(s, axis=-1, keepdims=True), sk)
        e = jnp.exp(s - m)
        denom = jnp.sum(e, axis=-1, keepdims=True) + jnp.exp(sk - m)
        o = _dot(_bf(e), vv_) / denom
        for g in range(group):
            pieces.append(o[g * qb:(g + 1) * qb])
    o_ref[0] = _bf(jnp.concatenate(pieces, axis=1))


def _attention(p_all, lay, cos_t, sin_t, sink, ctx_len, seq):
    b = p_all.shape[0]
    qb = QBLOCK
    nb = seq // qb
    cb = ctx_len // qb
    ad, kvd = lay["ad"], lay["kvd"]
    kv_heads = kvd // HEAD_DIM
    group = (ad // HEAD_DIM) // kv_heads
    assert lay["q"] % ad == 0 and lay["ka"] % kvd == 0 and lay["va"] % kvd == 0
    qc, kc, vc = lay["q"] // ad, lay["ka"] // kvd, lay["va"] // kvd

    def clampb(n, off):
        return jnp.clip(n + off, 0, nb - 1)

    def win(col, off):
        return pl.BlockSpec((1, qb, kvd), lambda bb, n: (bb, cb + clampb(n, off), col))

    def tab(off):
        return pl.BlockSpec((qb, LANES), lambda bb, n: (clampb(n, off), 0))

    assert ctx_len % qb == 0
    ctxspec = lambda col: pl.BlockSpec((1, ctx_len, kvd), lambda bb, n: (bb, 0, col))
    in_specs = [pl.BlockSpec((1, qb, ad), lambda bb, n: (bb, cb + n, qc)),
                win(kc, -1), win(kc, 0), win(kc, 1), win(vc, -1), win(vc, 0), win(vc, 1),
                ctxspec(kc), ctxspec(vc),
                tab(0), tab(0), tab(-1), tab(-1), tab(1), tab(1),
                pl.BlockSpec((1, sink.shape[1]), lambda bb, n: (0, 0))]
    return pl.pallas_call(
        functools.partial(_attn_kernel, n_blocks=nb, kv_heads=kv_heads, group=group),
        out_shape=jax.ShapeDtypeStruct((b, seq, ad), BF16),
        grid=(b, nb),
        in_specs=in_specs,
        out_specs=pl.BlockSpec((1, qb, ad), lambda bb, n: (bb, n, 0)),
        compiler_params=_params("parallel", "parallel"),
        name="window_ctx_attention",
    )(p_all, p_all, p_all, p_all, p_all, p_all, p_all, p_all, p_all,
      cos_t, sin_t, cos_t, sin_t, cos_t, sin_t, sink)


def _outproj_kernel(yf_ref, yr_ref, bon_ref, gat_ref, att_ref, x_ref, lg_ref, lb_ref, w_ref,
                    m_ref, g2_ref, wr_ref, x1_ref, h2_ref, lt_ref):
    ysum = yf_ref[0, 0] + yr_ref[0, 0]
    ri = lax.broadcasted_iota(jnp.int32, (LANES, LANES), 0) // HEAD_DIM
    ci = lax.broadcasted_iota(jnp.int32, (LANES, LANES), 1) // HEAD_DIM
    hm = _bf(jnp.where(ri == ci, 1.0 / HEAD_DIM, 0.0))

    def head_mean(z):
        hi = _bf(z)
        lo = _bf(z - hi.astype(F32))
        return _dot(hi, hm) + _dot(lo, hm)

    blocks = []
    for j in range(ysum.shape[1] // LANES):
        yb = ysum[:, j * LANES:(j + 1) * LANES]
        dev = yb - head_mean(yb)
        var = head_mean(dev * dev)
        blocks.append(dev * lax.rsqrt(var + LNX_EPS))
    yn = jnp.concatenate(blocks, axis=1) if len(blocks) > 1 else blocks[0]
    rw = (yn * lg_ref[...] + lb_ref[...] + bon_ref[0]) * gat_ref[0]
    mix = jnp.concatenate([_bf(rw), att_ref[0]], axis=1)
    m = m_ref[0]
    x1 = x_ref[0] + m[0:1] * _dot(mix, w_ref[...])
    x1_ref[0] = x1
    ms = jnp.mean(x1 * x1, axis=-1, keepdims=True)
    h2 = x1 * lax.rsqrt(ms + NORM_EPS) * g2_ref[...] * (1.0 + m[2:3]) + m[1:2]
    h2_ref[0] = _bf(h2)
    lt_ref[...] = _dot_nt(wr_ref[...], h2, precision=lax.Precision.HIGHEST)


def _out_projection(y, bonus, gate, attn, x, lnx_g, lnx_b, w_out_b, m2, g2, w_router_t, ctx_len, tr):
    b, t, d = x.shape
    rd = y.shape[3]
    ad = attn.shape[2]
    ne = w_router_t.shape[0]
    co = ctx_len // tr
    nt = t // tr
    return pl.pallas_call(
        _outproj_kernel,
        out_shape=[jax.ShapeDtypeStruct((b, t, d), F32),
                   jax.ShapeDtypeStruct((b, t, d), BF16),
                   jax.ShapeDtypeStruct((ne, b * t), F32)],
        grid=(b, nt),
        in_specs=[pl.BlockSpec((1, 1, tr, rd), lambda bb, i: (0, bb, co + i, 0)),
                  pl.BlockSpec((1, 1, tr, rd), lambda bb, i: (1, bb, co + i, 0)),
                  pl.BlockSpec((1, tr, rd), lambda bb, i: (bb, co + i, 0)),
                  pl.BlockSpec((1, tr, rd), lambda bb, i: (bb, co + i, 0)),
                  pl.BlockSpec((1, tr, ad), lambda bb, i: (bb, i, 0)),
                  pl.BlockSpec((1, tr, d), lambda bb, i: (bb, i, 0)),
                  pl.BlockSpec((1, rd), lambda bb, i: (0, 0)),
                  pl.BlockSpec((1, rd), lambda bb, i: (0, 0)),
                  pl.BlockSpec((rd + ad, d), lambda bb, i: (0, 0)),
                  pl.BlockSpec((1, 4, d), lambda bb, i: (bb, 0, 0)),
                  pl.BlockSpec((1, d), lambda bb, i: (0, 0)),
                  pl.BlockSpec((ne, d), lambda bb, i: (0, 0))],
        out_specs=[pl.BlockSpec((1, tr, d), lambda bb, i: (bb, i, 0)),
                   pl.BlockSpec((1, tr, d), lambda bb, i: (bb, i, 0)),
                   pl.BlockSpec((ne, tr), lambda bb, i: (0, bb * nt + i))],
        compiler_params=_params("parallel", "parallel"),
        name="rwkv_out_outproj_norm2_router",
    )(y, y, bonus, gate, attn, x, lnx_g, lnx_b, w_out_b, m2, g2, w_router_t)


def _router_kernel(l_ref, b_ref, o_ref):
    sc = jax.nn.sigmoid(l_ref[...])
    ne, tn = sc.shape
    eg = ne // N_GROUPS
    bz = sc + b_ref[...]
    g3 = bz.reshape(N_GROUPS, eg, tn)
    io = lax.broadcasted_iota(jnp.int32, (N_GROUPS, eg, tn), 1).astype(F32)
    m1 = jnp.max(g3, axis=1, keepdims=True)
    i1 = jnp.min(jnp.where(g3 == m1, io, float(eg)), axis=1, keepdims=True)
    m2 = jnp.max(jnp.where(io == i1, -jnp.inf, g3), axis=1, keepdims=True)
    gs = (m1 + m2).reshape(N_GROUPS, tn)

    def topk_mask(vals, k):
        n = vals.shape[0]
        idx = lax.broadcasted_iota(jnp.int32, vals.shape, 0).astype(F32)
        sel = jnp.zeros(vals.shape, F32)
        cur = vals
        for _ in range(k):
            mx = jnp.max(cur, axis=0, keepdims=True)
            ix = jnp.min(jnp.where(cur == mx, idx, float(n)), axis=0, keepdims=True)
            hit = idx == ix
            sel = jnp.where(hit, 1.0, sel)
            cur = jnp.where(hit, -jnp.inf, cur)
        return sel

    gsel = topk_mask(gs, TOPK_GROUPS)
    masked = jnp.where(gsel.reshape(N_GROUPS, 1, tn) > 0.0, g3, -jnp.inf).reshape(ne, tn)
    chosen = topk_mask(masked, TOP_K)
    gate = chosen * sc
    o_ref[...] = gate / jnp.sum(gate, axis=0, keepdims=True) * ROUTED_SCALE


def _router(logits_t, bias_col):
    ne, n = logits_t.shape
    tn = 1024 if n % 1024 == 0 else n
    return pl.pallas_call(
        _router_kernel,
        out_shape=jax.ShapeDtypeStruct((ne, n), F32),
        grid=(n // tn,),
        in_specs=[pl.BlockSpec((ne, tn), lambda i: (0, i)),
                  pl.BlockSpec((ne, 1), lambda i: (0, 0))],
        out_specs=pl.BlockSpec((ne, tn), lambda i: (0, i)),
        compiler_params=_params("parallel"),
        name="group_topk_router",
    )(logits_t, bias_col)


def _moe_kernel(h_ref, g_ref, wg_ref, wu_ref, wd_ref, o_ref):
    e = pl.program_id(1)

    @pl.when(e == 0)
    def _():
        o_ref[...] = jnp.zeros_like(o_ref)

    h = h_ref[...]
    a = _dot(h, _bf(wg_ref[0]))
    u = _dot(h, _bf(wu_ref[0]))
    lane = lax.broadcasted_iota(jnp.int32, (1, g_ref.shape[1]), 1)
    gcol = jnp.sum(jnp.where(lane == e, g_ref[...], 0.0), axis=1, keepdims=True)
    act = a * jax.nn.sigmoid(a) * u * gcol
    o_ref[...] += _dot(_bf(act), _bf(wd_ref[0]))


def _moe_routed(h2, gates, e_gate, e_up, e_down, tm):
    n, d = h2.shape
    ne, _, ff = e_gate.shape
    return pl.pallas_call(
        _moe_kernel,
        out_shape=jax.ShapeDtypeStruct((n, d), F32),
        grid=(n // tm, ne),
        in_specs=[pl.BlockSpec((tm, d), lambda i, e: (i, 0)),
                  pl.BlockSpec((tm, gates.shape[1]), lambda i, e: (i, 0)),
                  pl.BlockSpec((1, d, ff), lambda i, e: (e, 0, 0)),
                  pl.BlockSpec((1, d, ff), lambda i, e: (e, 0, 0)),
                  pl.BlockSpec((1, ff, d), lambda i, e: (e, 0, 0))],
        out_specs=pl.BlockSpec((tm, d), lambda i, e: (i, 0)),
        compiler_params=_params("parallel", "arbitrary"),
        name="routed_experts",
    )(h2, gates, e_gate, e_up, e_down)


def _final_kernel(h_ref, x1_ref, y_ref, wg_ref, wu_ref, wd_ref, m_ref, fg_ref, o_ref):
    h = h_ref[0]
    a = _dot(h, wg_ref[...])
    u = _dot(h, wu_ref[...])
    shared = _dot(_bf(a * jax.nn.sigmoid(a) * u), wd_ref[...])
    x2 = x1_ref[0] + m_ref[0, 3:4, :] * (y_ref[0] + shared)
    ms = jnp.mean(x2 * x2, axis=-1, keepdims=True)
    o_ref[0] = x2 * lax.rsqrt(ms + NORM_EPS) * fg_ref[...]


def _final(h2, x1, routed, s_gate_b, s_up_b, s_down_b, m2, final_g, tr):
    b, t, d = x1.shape
    ff = s_gate_b.shape[1]
    row = lambda: pl.BlockSpec((1, tr, d), lambda bb, i: (bb, i, 0))
    return pl.pallas_call(
        _final_kernel,
        out_shape=jax.ShapeDtypeStruct((b, t, d), F32),
        grid=(b, t // tr),
        in_specs=[row(), row(), row(),
                  pl.BlockSpec((d, ff), lambda bb, i: (0, 0)),
                  pl.BlockSpec((d, ff), lambda bb, i: (0, 0)),
                  pl.BlockSpec((ff, d), lambda bb, i: (0, 0)),
                  pl.BlockSpec((1, 4, d), lambda bb, i: (bb, 0, 0)),
                  pl.BlockSpec((1, d), lambda bb, i: (0, 0))],
        out_specs=row(),
        compiler_params=_params("parallel", "parallel"),
        name="shared_expert_final_norm",
    )(h2, x1, routed, s_gate_b, s_up_b, s_down_b, m2, final_g)


def _pad_to(a, axis, n):
    pad = [(0, 0)] * a.ndim
    pad[axis] = (0, n - a.shape[axis])
    return jnp.pad(a, pad)


def _layout(rd, wl, al, gl, ad, kvd):
    assert wl <= LANES and al <= LANES
    lay = {"rd": rd, "gl": gl, "ad": ad, "kvd": kvd}
    off = 0
    for name, width in (("q", ad), ("r", rd), ("k", rd), ("v", rd), ("ka", kvd), ("va", kvd),
                        ("xw", LANES), ("xa", LANES), ("xg", gl)):
        assert width % LANES == 0
        lay[name] = off
        off += width
    lay["ncols"] = off
    return lay


def kernel(x, c, ctx, c_ctx, w_mod, b_mod, norm1_g, norm2_g, w_in, shift_mu, k_k, k_a, r_k, decay_bias, decay_up, iclr_bias, iclr_up, gate_up, lnx_g, lnx_b, attn_sink, w_out, w_router, expert_bias, e_gate, e_up, e_down, s_gate, s_up, s_down, final_g):
    assert w_mod.shape[0] == 1, "single-layer block only"
    b, t, d = x.shape
    ctx_len = ctx.shape[1]
    rh, hd = r_k.shape[1:]
    assert hd == HEAD_DIM and b + 1 <= SUBLANES
    rd = rh * hd
    wl, al, gl = decay_up.shape[2], iclr_up.shape[2], gate_up.shape[1]
    ad = attn_sink.shape[1] * hd
    kvd = (w_in.shape[2] - (3 * rd + wl + al + gl) - ad) // 2
    ne = w_router.shape[2]
    lay = _layout(rd, wl, al, gl, ad, kvd)
    tr = 256 if ctx_len % 256 == 0 and t % 256 == 0 else CHUNK
    assert ctx_len % CHUNK == 0 and t % CHUNK == 0 and t % QBLOCK == 0 and WINDOW == QBLOCK

    c8 = _pad_to(jnp.concatenate([c, c_ctx[None]], axis=0), 0, SUBLANES)
    mod = _modulation(c8, w_mod[0], b_mod[0][None])
    mod6 = mod.reshape(SUBLANES, 6, d)
    mx, mc = mod6[:b], mod6[b]
    m1 = jnp.stack([jnp.broadcast_to(mc[None, 0:2], (b, 2, d)), mx[:, 0:2]], axis=1)
    m2 = jnp.stack([mx[:, 2], mx[:, 3], mx[:, 4], mx[:, 5]], axis=1)

    wi = w_in[0]
    o_w, o_a, o_g, o_q = 3 * rd, 3 * rd + wl, 3 * rd + wl + al, 3 * rd + wl + al + gl
    w_in_p = _bf(jnp.concatenate([
        wi[:, o_q:o_q + ad], wi[:, :3 * rd], wi[:, o_q + ad:],
        _pad_to(wi[:, o_w:o_a], 1, LANES), _pad_to(wi[:, o_a:o_g], 1, LANES), wi[:, o_g:o_q]], axis=1))
    mu = shift_mu[0]
    prm = {
        "mu_r": mu[None, :rd], "mu_k": mu[None, rd:2 * rd], "mu_v": mu[None, 2 * rd:3 * rd],
        "mu_w": _pad_to(mu[None, o_w:o_a], 1, LANES), "mu_a": _pad_to(mu[None, o_a:o_g], 1, LANES),
        "mu_g": mu[None, o_g:o_q],
        "k_k": k_k, "k_a": k_a, "r_k": r_k.reshape(1, rd),
        "decay_bias": decay_bias[0], "decay_up": _bf(_pad_to(decay_up[0], 1, LANES)),
        "iclr_bias": iclr_bias[0], "iclr_up": _bf(_pad_to(iclr_up[0], 1, LANES)),
        "gate_up": _bf(gate_up[0]),
    }

    h_all = jnp.concatenate([ctx, x], axis=1)
    p_all = _in_projection(h_all, norm1_g, m1, w_in_p, ctx_len, tr)

    rp, y0, mm, nn, bonus, gate = _wkv_prep(p_all, lay, prm, ctx_len, None)
    y = _wkv_sweep(rp, y0, mm, nn, ctx_len)

    nf = HEAD_DIM // 4
    freqs = ROPE_BASE ** (-jnp.arange(nf, dtype=F32) / nf)
    pos = jnp.arange(t, dtype=jnp.int32)
    ang_r = (pos // GRID_W).astype(F32)[:, None] * freqs
    ang_c = (pos % GRID_W).astype(F32)[:, None] * freqs
    cos_h = jnp.concatenate([jnp.cos(ang_r)] * 2 + [jnp.cos(ang_c)] * 2, axis=1)
    sin_h = jnp.concatenate([-jnp.sin(ang_r), jnp.sin(ang_r), -jnp.sin(ang_c), jnp.sin(ang_c)], axis=1)
    cos_t = jnp.concatenate([cos_h, cos_h], axis=1)
    sin_t = jnp.concatenate([sin_h, sin_h], axis=1)
    attn = _attention(p_all, lay, cos_t, sin_t, attn_sink, ctx_len, t)

    x1, h2, logits_t = _out_projection(y, bonus, gate, attn, x, lnx_g, lnx_b, _bf(w_out[0]), m2,
                                       norm2_g, w_router[0].T, ctx_len, tr)
    gates_t = _router(logits_t, expert_bias[0][:, None])
    gates = _pad_to(gates_t.T, 1, -(-ne // LANES) * LANES)
    n_tok = b * t
    tm = 1024 if n_tok % 1024 == 0 else tr
    routed = _moe_routed(h2.reshape(n_tok, d), gates, e_gate[0], e_up[0], e_down[0], tm)
    return _final(h2, x1, routed.reshape(b, t, d), _bf(s_gate[0]), _bf(s_up[0]), _bf(s_down[0]),
                  m2, final_g[None], tr)
```

```python
import functools
import math

import jax
import jax.numpy as jnp
from jax import lax
from jax.experimental import pallas as pl
from jax.experimental.pallas import tpu as pltpu

GRID_W = 64
WINDOW = 128
ROPE_BASE = 10000.0
N_GROUPS = 8
TOPK_GROUPS = 4
TOP_K = 8
ROUTED_SCALE = 2.5
NORM_EPS = 1e-6
LNX_EPS = 64e-5
KK_EPS = 1e-12

LANES = 128
SUBLANES = 8
VMEM_LIMIT_BYTES = 56 * 1024 * 1024

HEAD_DIM = 64
CHUNK = 64
QBLOCK = 128
EXPERT_ROWS = 256

F32 = jnp.float32
BF16 = jnp.bfloat16
NEG_BIG = -1e30


def _bf(x):
    return x.astype(BF16)


def _dot(a, b, precision=None):
    return jnp.dot(a, b, preferred_element_type=F32, precision=precision)


def _dot_nt(a, b, precision=None):
    return lax.dot_general(a, b, (((1,), (1,)), ((), ())), preferred_element_type=F32,
                           precision=precision)


def _dot_tn(a, b):
    return lax.dot_general(a, b, (((0,), (0,)), ((), ())), preferred_element_type=F32)


def _params(*sem):
    return pltpu.CompilerParams(dimension_semantics=sem, vmem_limit_bytes=VMEM_LIMIT_BYTES)


def _mod_kernel(c_ref, w_ref, b_ref, o_ref):
    c = c_ref[...]
    s = c * jax.nn.sigmoid(c)
    hi = _bf(s)
    lo = _bf(s - hi.astype(F32))
    r = _dot(jnp.concatenate([hi, lo], axis=0), _bf(w_ref[...]))
    o_ref[...] = r[:SUBLANES] + r[SUBLANES:] + b_ref[...]


def _modulation(c8, w_mod, b_mod):
    d, n = w_mod.shape
    tn = 1024 if n % 1024 == 0 else n
    return pl.pallas_call(
        _mod_kernel,
        out_shape=jax.ShapeDtypeStruct((SUBLANES, n), F32),
        grid=(n // tn,),
        in_specs=[pl.BlockSpec((SUBLANES, d), lambda j: (0, 0)),
                  pl.BlockSpec((d, tn), lambda j: (0, j)),
                  pl.BlockSpec((1, tn), lambda j: (0, j))],
        out_specs=pl.BlockSpec((SUBLANES, tn), lambda j: (0, j)),
        compiler_params=_params("parallel"),
        name="adaln_modulation",
    )(c8, w_mod, b_mod)


def _inproj_kernel(x_ref, g_ref, m_ref, w_ref, o_ref):
    x = x_ref[0]
    ms = jnp.mean(x * x, axis=-1, keepdims=True)
    y = x * lax.rsqrt(ms + NORM_EPS) * g_ref[...]
    m = m_ref[0, 0]
    h = y * (1.0 + m[1:2]) + m[0:1]
    o_ref[0] = _dot(_bf(h), w_ref[...])


def _in_projection(h_all, g1, m1, w_in_p, ctx_len, tr):
    b, ta, d = h_all.shape
    ncols = w_in_p.shape[1]
    tn = ncols // 4 if (ncols // 4) % LANES == 0 and ncols % 4 == 0 else ncols
    nct = ctx_len // tr
    return pl.pallas_call(
        _inproj_kernel,
        out_shape=jax.ShapeDtypeStruct((b, ta, ncols), F32),
        grid=(ncols // tn, b, ta // tr),
        in_specs=[pl.BlockSpec((1, tr, d), lambda n, bb, i: (bb, i, 0)),
                  pl.BlockSpec((1, d), lambda n, bb, i: (0, 0)),
                  pl.BlockSpec((1, 1, 2, d), lambda n, bb, i: (bb, jnp.where(i >= nct, 1, 0), 0, 0)),
                  pl.BlockSpec((d, tn), lambda n, bb, i: (0, n))],
        out_specs=pl.BlockSpec((1, tr, tn), lambda n, bb, i: (bb, i, n)),
        compiler_params=_params("parallel", "parallel", "parallel"),
        name="norm_modulate_inproj",
    )(h_all, g1, m1, w_in_p)


def _wkv_prep_kernel(r_c, r_p, r_n, k_c, k_p, k_n, v_c, v_p, v_n, w_c, w_p, w_n, a_c, a_p, a_n,
                     g_c, g_p, g_n, mu_r, mu_k, mu_v, mu_w, mu_a, mu_g, kk_ref, ka_ref, rk_ref,
                     db_ref, du_ref, ib_ref, iu_ref, gu_ref,
                     rp_out, y0_out, m_out, n_out, bonus_out, gate_out,
                     *, nctx_chunks, n_chunks, prec_t):
    L = CHUNK
    c = pl.program_id(1)
    has_prev = jnp.logical_and(c != 0, c != nctx_chunks)
    has_next = jnp.logical_and(c != nctx_chunks - 1, c != n_chunks - 1)

    def shifted(cur, prv, nxt, mu):
        x = cur[0]
        rows = lax.broadcasted_iota(jnp.int32, (L, 1), 0)
        pr = jnp.where(has_prev, prv[0, SUBLANES - 1:SUBLANES, :], 0.0)
        nx = jnp.where(has_next, nxt[0, 0:1, :], 0.0)
        up = jnp.where(rows == 0, pr, pltpu.roll(x, 1, 0))
        dn = jnp.where(rows == L - 1, nx, pltpu.roll(x, L - 1, 0))
        return x + mu[...] * (0.5 * (up + dn) - x)

    r = shifted(r_c, r_p, r_n, mu_r)
    k = shifted(k_c, k_p, k_n, mu_k)
    v = shifted(v_c, v_p, v_n, mu_v)
    xw = shifted(w_c, w_p, w_n, mu_w)
    xa = shifted(a_c, a_p, a_n, mu_a)
    xg = shifted(g_c, g_p, g_n, mu_g)

    lane = lax.broadcasted_iota(jnp.int32, (1, LANES), 1)
    m0 = lane < HEAD_DIM

    def head_sum(x):
        s0 = jnp.sum(jnp.where(m0, x, 0.0), axis=-1, keepdims=True)
        s1 = jnp.sum(jnp.where(m0, 0.0, x), axis=-1, keepdims=True)
        return jnp.where(m0, s0, s1)

    def stack2(x):
        return jnp.concatenate([jnp.where(m0, x, 0.0), jnp.where(m0, 0.0, x)], axis=0)

    tw = _bf(jnp.tanh(xw))
    xab = _bf(xa)
    lw_all = [(-math.exp(-0.5)) * jax.nn.sigmoid(db_ref[d:d + 1, :] + _dot(tw, du_ref[d]))
              for d in range(2)]
    sa_all = [jax.nn.sigmoid(ib_ref[d:d + 1, :] + _dot(xab, iu_ref[d])) for d in range(2)]
    gate_out[0] = _dot(_bf(jax.nn.sigmoid(xg)), gu_ref[...])

    ri = lax.broadcasted_iota(jnp.int32, (L, L), 0)
    ci = lax.broadcasted_iota(jnp.int32, (L, L), 1)
    ri2 = lax.broadcasted_iota(jnp.int32, (2 * L, 2 * L), 0)
    ci2 = lax.broadcasted_iota(jnp.int32, (2 * L, 2 * L), 1)
    rt = jnp.bitwise_and(ri2, L - 1)
    ct = jnp.bitwise_and(ci2, L - 1)
    eye = ri2 == ci2
    n_sq = int(math.log2(L)) - 1

    tri = [(ci <= ri).astype(F32), (ci >= ri).astype(F32)]
    strict = [ct < rt, ct > rt]
    incl = [ct <= rt, ct >= rt]
    cum = [_dot(tri[d], lw_all[d], precision=lax.Precision.HIGHEST) for d in range(2)]
    cend = [jnp.sum(lw_all[d], axis=0, keepdims=True) for d in range(2)]
    e_in = [jnp.exp(cum[d]) for d in range(2)]
    e_ex = [jnp.exp(cum[d] - lw_all[d]) for d in range(2)]
    e_neg = [jnp.exp(-cum[d]) for d in range(2)]
    e_end = [jnp.exp(cend[d] - cum[d]) for d in range(2)]
    p_end = [jnp.exp(cend[d]) for d in range(2)]
    kd_all = [k * (1.0 + (sa_all[d] - 1.0) * ka_ref[...]) for d in range(2)]

    npp = r.shape[1] // LANES
    chains = [(p, d) for p in range(npp) for d in range(2)]
    sls = [slice(p * LANES, (p + 1) * LANES) for p in range(npp)]
    kk_p, vs_p, vsb_p = [], [], []
    for p in range(npp):
        kx = k[:, sls[p]] * kk_ref[:, sls[p]]
        kk_p.append(kx * lax.rsqrt(head_sum(kx * kx) + KK_EPS))
        vs_p.append(stack2(v[:, sls[p]]))
        vsb_p.append(_bf(vs_p[p]))
        ksum = kd_all[0][:, sls[p]] + kd_all[1][:, sls[p]]
        bonus_out[0, :, sls[p]] = head_sum(r[:, sls[p]] * ksum * rk_ref[:, sls[p]]) * v[:, sls[p]]

    a_s, r_s, bh_s, kh_s, g = [], [], [], [], []
    for p, d in chains:
        sl = sls[p]
        kd = kd_all[d][:, sl]
        bvec = kk_p[p] * sa_all[d][:, sl]
        a_s.append(stack2(-kk_p[p] * e_ex[d][:, sl]))
        r_s.append(stack2(r[:, sl] * e_in[d][:, sl]))
        b_s = stack2(bvec * e_neg[d][:, sl])
        k_s = stack2(kd * e_neg[d][:, sl])
        bh_s.append(stack2(bvec * e_end[d][:, sl]))
        kh_s.append(stack2(kd * e_end[d][:, sl]))
        g.append(_dot_nt(_bf(jnp.concatenate([a_s[-1], r_s[-1]], axis=0)),
                         _bf(jnp.concatenate([b_s, k_s], axis=0))))
    a_ab = [jnp.where(strict[d], g[i][:2 * L, :2 * L], 0.0) for i, (p, d) in enumerate(chains)]
    a_ak = [jnp.where(strict[d], g[i][:2 * L, 2 * L:], 0.0) for i, (p, d) in enumerate(chains)]
    a_rb = [jnp.where(incl[d], g[i][2 * L:, :2 * L], 0.0) for i, (p, d) in enumerate(chains)]
    a_rk = [jnp.where(incl[d], g[i][2 * L:, 2 * L:], 0.0) for i, (p, d) in enumerate(chains)]

    ident = jnp.where(eye, 1.0, 0.0)
    tmat = [ident + a for a in a_ab]
    apow = a_ab
    for _ in range(n_sq):
        if prec_t is None:
            apow = [_dot(_bf(a), _bf(a)) for a in apow]
            tmat = [t + _dot(_bf(a), _bf(t)) for a, t in zip(apow, tmat)]
        else:
            apow = [_dot(a, a, precision=prec_t) for a in apow]
            tmat = [t + _dot(a, t, precision=prec_t) for a, t in zip(apow, tmat)]

    x1 = [_dot(_bf(a_ak[i]), vsb_p[p]) for i, (p, d) in enumerate(chains)]
    y0v = [_dot(_bf(a_rk[i]), vsb_p[p]) for i, (p, d) in enumerate(chains)]
    tu = [_dot(_bf(tmat[i]), _bf(jnp.concatenate([a_s[i], x1[i]], axis=1)))
          for i in range(len(chains))]
    ry = [_dot(_bf(a_rb[i]), _bf(tu[i])) for i in range(len(chains))]
    for i, (p, d) in enumerate(chains):
        rp_out[d, 0, 0, p] = _bf(r_s[i] + ry[i][:, :LANES])
        y0s = ry[i][:, LANES:] + y0v[i]
        y0_out[d, 0, 0, p] = y0s[:L] + y0s[L:]
    for i, (p, d) in enumerate(chains):
        m_out[d, 0, 0, p] = _bf(jnp.where(eye, p_end[d][:, sls[p]], 0.0)
                                + _dot_tn(_bf(tu[i][:, :LANES]), _bf(bh_s[i])))
    for i, (p, d) in enumerate(chains):
        n_out[d, 0, 0, p] = _dot_tn(_bf(jnp.concatenate([tu[i][:, LANES:], vs_p[p]], axis=0)),
                                    _bf(jnp.concatenate([bh_s[i], kh_s[i]], axis=0)))


def _wkv_prep(p_all, lay, prm, ctx_len, prec_t, pp):
    b, ta, _ = p_all.shape
    L = CHUNK
    nc = ta // L
    nctx = ctx_len // L
    npair = lay["rd"] // LANES
    gl = lay["gl"]
    rows8 = ta // SUBLANES

    def cur(colfn, width=LANES):
        return pl.BlockSpec((1, L, width), lambda bb, c, p: (bb, c, colfn(p)))

    def prv(colfn, width=LANES):
        return pl.BlockSpec((1, SUBLANES, width),
                            lambda bb, c, p: (bb, jnp.maximum(c * (L // SUBLANES) - 1, 0), colfn(p)))

    def nxt(colfn, width=LANES):
        return pl.BlockSpec((1, SUBLANES, width),
                            lambda bb, c, p: (bb, jnp.minimum((c + 1) * (L // SUBLANES), rows8 - 1), colfn(p)))

    def trio(colfn, width=LANES):
        return [cur(colfn, width), prv(colfn, width), nxt(colfn, width)]

    pw = pp * LANES
    assert npair % pp == 0 and all(lay[n] % pw == 0 for n in ("r", "k", "v"))
    rb, kb, vb = lay["r"] // pw, lay["k"] // pw, lay["v"] // pw
    wb, ab = lay["xw"] // LANES, lay["xa"] // LANES
    assert lay["xg"] % gl == 0
    gb = lay["xg"] // gl

    def vec(width=pw):
        return pl.BlockSpec((1, width), lambda bb, c, p: (0, p))

    def vec0(width=LANES):
        return pl.BlockSpec((1, width), lambda bb, c, p: (0, 0))

    in_specs = (trio(lambda p: rb + p, pw) + trio(lambda p: kb + p, pw) + trio(lambda p: vb + p, pw)
                + trio(lambda p: wb) + trio(lambda p: ab) + trio(lambda p: gb, gl)
                + [vec(), vec(), vec(), vec0(), vec0(), vec0(gl),
                   vec(), vec(), vec(),
                   pl.BlockSpec((2, pw), lambda bb, c, p: (0, p)),
                   pl.BlockSpec((2, LANES, pw), lambda bb, c, p: (0, 0, p)),
                   pl.BlockSpec((2, pw), lambda bb, c, p: (0, p)),
                   pl.BlockSpec((2, LANES, pw), lambda bb, c, p: (0, 0, p)),
                   pl.BlockSpec((gl, pw), lambda bb, c, p: (0, p))])

    def opspec(rows):
        return pl.BlockSpec((2, 1, 1, pp, rows, LANES), lambda bb, c, p: (0, c, bb, p, 0, 0))

    out_specs = [opspec(2 * L), opspec(L), opspec(LANES), opspec(LANES),
                 pl.BlockSpec((1, L, pw), lambda bb, c, p: (bb, c, p)),
                 pl.BlockSpec((1, L, pw), lambda bb, c, p: (bb, c, p))]
    out_shape = [jax.ShapeDtypeStruct((2, nc, b, npair, 2 * L, LANES), BF16),
                 jax.ShapeDtypeStruct((2, nc, b, npair, L, LANES), F32),
                 jax.ShapeDtypeStruct((2, nc, b, npair, LANES, LANES), BF16),
                 jax.ShapeDtypeStruct((2, nc, b, npair, LANES, LANES), F32),
                 jax.ShapeDtypeStruct((b, ta, lay["rd"]), F32),
                 jax.ShapeDtypeStruct((b, ta, lay["rd"]), F32)]
    kern = functools.partial(_wkv_prep_kernel, nctx_chunks=nctx, n_chunks=nc, prec_t=prec_t)
    args = [p_all] * 18 + [prm["mu_r"], prm["mu_k"], prm["mu_v"], prm["mu_w"], prm["mu_a"], prm["mu_g"],
                           prm["k_k"], prm["k_a"], prm["r_k"], prm["decay_bias"], prm["decay_up"],
                           prm["iclr_bias"], prm["iclr_up"], prm["gate_up"]]
    return pl.pallas_call(
        kern, out_shape=out_shape, grid=(b, nc, npair // pp), in_specs=in_specs, out_specs=out_specs,
        compiler_params=_params("parallel", "parallel", "parallel"),
        name="wkv_chunk_operators",
    )(*args)


def _wkv_seq_kernel(rp_ref, y0_ref, m_ref, n_ref, y_ref, s_ref, *, nb, npair):
    L = CHUNK

    @pl.when(pl.program_id(1) == 0)
    def _():
        s_ref[...] = jnp.zeros_like(s_ref)

    for bb in range(nb):
        for p in range(npair):
            i = bb * npair + p
            s0 = _bf(s_ref[i])
            ys = _dot_nt(rp_ref[0, 0, bb, p], s0)
            y_ref[0, bb, :, p * LANES:(p + 1) * LANES] = ys[:L] + ys[L:] + y0_ref[0, 0, bb, p]
            s_ref[i] = _dot(s0, m_ref[0, 0, bb, p]) + n_ref[0, 0, bb, p]


def _wkv_sweep(rp, y0, mm, nn, ctx_len):
    _, nc, b, npair, _, _ = rp.shape
    L = CHUNK
    nctx = ctx_len // L

    def chunk_of(d, c):
        rev = jnp.where(c < nctx, nctx - 1 - c, nc - 1 - (c - nctx))
        return jnp.where(d == 0, c, rev)

    def opspec(rows):
        return pl.BlockSpec((1, 1, b, npair, rows, LANES), lambda d, c: (d, chunk_of(d, c), 0, 0, 0, 0))

    return pl.pallas_call(
        functools.partial(_wkv_seq_kernel, nb=b, npair=npair),
        out_shape=jax.ShapeDtypeStruct((2, b, nc * L, npair * LANES), F32),
        grid=(2, nc),
        in_specs=[opspec(2 * L), opspec(L), opspec(LANES), opspec(LANES)],
        out_specs=pl.BlockSpec((1, b, L, npair * LANES), lambda d, c: (d, 0, chunk_of(d, c), 0)),
        scratch_shapes=[pltpu.VMEM((b * npair, LANES, LANES), F32)],
        compiler_params=_params("parallel", "arbitrary"),
        name="wkv_state_sweep",
    )(rp, y0, mm, nn)


def _attn_kernel(q_ref, kp_ref, kc_ref, kn_ref, vp_ref, vc_ref, vn_ref, kx_ref, vx_ref,
                 cq_ref, sq_ref, cp_ref, sp_ref, cn_ref, sn_ref, sink_ref, o_ref,
                 *, n_blocks, kv_heads, group):
    n = pl.program_id(1)
    qb = QBLOCK

    def rope(x, cos, sin):
        w = x.shape[1]
        reps = w // LANES
        cs = jnp.concatenate([cos] * reps, axis=1) if reps > 1 else cos
        sn = jnp.concatenate([sin] * reps, axis=1) if reps > 1 else sin
        lane = lax.broadcasted_iota(jnp.int32, (1, w), 1)
        sw = jnp.where(jnp.bitwise_and(lane, 16) == 0, pltpu.roll(x, w - 16, 1), pltpu.roll(x, 16, 1))
        return x * cs + sw * sn

    q = rope(q_ref[0], cq_ref[...], sq_ref[...]) * (HEAD_DIM ** -0.5)
    kwin = jnp.concatenate([rope(kp_ref[0], cp_ref[...], sp_ref[...]),
                            rope(kc_ref[0], cq_ref[...], sq_ref[...]),
                            rope(kn_ref[0], cn_ref[...], sn_ref[...]),
                            kx_ref[0]], axis=0)
    vwin = jnp.concatenate([vp_ref[0], vc_ref[0], vn_ref[0], vx_ref[0]], axis=0)
    nkeys = kwin.shape[0]

    rows = group * qb
    qi = jnp.bitwise_and(lax.broadcasted_iota(jnp.int32, (rows, nkeys), 0), qb - 1)
    s_i = lax.broadcasted_iota(jnp.int32, (rows, nkeys), 1)
    blk = s_i // qb + (n - 1)
    in_win = (s_i >= qi) & (s_i <= qi + 2 * WINDOW) & (blk >= 0) & (blk < n_blocks)
    valid = in_win | (s_i >= 3 * qb)
    rgrp = lax.broadcasted_iota(jnp.int32, (rows, 1), 0) // qb

    scores, sinks = [], []
    for kh in range(kv_heads):
        kk_ = _bf(kwin[:, kh * HEAD_DIM:(kh + 1) * HEAD_DIM])
        qs = jnp.concatenate([q[:, (kh * group + g) * HEAD_DIM:(kh * group + g + 1) * HEAD_DIM]
                              for g in range(group)], axis=0)
        sk = jnp.zeros((rows, 1), F32)
        for g in range(group):
            h = kh * group + g
            sk = jnp.where(rgrp == g, sink_ref[0:1, h:h + 1], sk)
        sinks.append(sk)
        scores.append(jnp.where(valid, _dot_nt(_bf(qs), kk_), NEG_BIG))
    probs, denoms = [], []
    for s, sk in zip(scores, sinks):
        m = jnp.maximum(jnp.max(s, axis=-1, keepdims=True), sk)
        e = jnp.exp(s - m)
        denoms.append(jnp.sum(e, axis=-1, keepdims=True) + jnp.exp(sk - m))
        probs.append(_bf(e))
    pieces = []
    for kh in range(kv_heads):
        vv_ = _bf(vwin[:, kh * HEAD_DIM:(kh + 1) * HEAD_DIM])
        o = _dot(probs[kh], vv_) / denoms[kh]
        for g in range(group):
            pieces.append(o[g * qb:(g + 1) * qb])
    o_ref[0] = _bf(jnp.concatenate(pieces, axis=1))


def _attention(p_all, lay, cos_t, sin_t, sink, ctx_len, seq):
    b = p_all.shape[0]
    qb = QBLOCK
    nb = seq // qb
    cb = ctx_len // qb
    ad, kvd = lay["ad"], lay["kvd"]
    kv_heads = kvd // HEAD_DIM
    group = (ad // HEAD_DIM) // kv_heads
    assert lay["q"] % ad == 0 and lay["ka"] % kvd == 0 and lay["va"] % kvd == 0
    qc, kc, vc = lay["q"] // ad, lay["ka"] // kvd, lay["va"] // kvd

    def clampb(n, off):
        return jnp.clip(n + off, 0, nb - 1)

    def win(col, off):
        return pl.BlockSpec((1, qb, kvd), lambda bb, n: (bb, cb + clampb(n, off), col))

    def tab(off):
        return pl.BlockSpec((qb, LANES), lambda bb, n: (clampb(n, off), 0))

    assert ctx_len % qb == 0
    ctxspec = lambda col: pl.BlockSpec((1, ctx_len, kvd), lambda bb, n: (bb, 0, col))
    in_specs = [pl.BlockSpec((1, qb, ad), lambda bb, n: (bb, cb + n, qc)),
                win(kc, -1), win(kc, 0), win(kc, 1), win(vc, -1), win(vc, 0), win(vc, 1),
                ctxspec(kc), ctxspec(vc),
                tab(0), tab(0), tab(-1), tab(-1), tab(1), tab(1),
                pl.BlockSpec((1, sink.shape[1]), lambda bb, n: (0, 0))]
    return pl.pallas_call(
        functools.partial(_attn_kernel, n_blocks=nb, kv_heads=kv_heads, group=group),
        out_shape=jax.ShapeDtypeStruct((b, seq, ad), BF16),
        grid=(b, nb),
        in_specs=in_specs,
        out_specs=pl.BlockSpec((1, qb, ad), lambda bb, n: (bb, n, 0)),
        compiler_params=_params("parallel", "parallel"),
        name="window_ctx_attention",
    )(p_all, p_all, p_all, p_all, p_all, p_all, p_all, p_all, p_all,
      cos_t, sin_t, cos_t, sin_t, cos_t, sin_t, sink)


def _outproj_kernel(yf_ref, yr_ref, bon_ref, gat_ref, att_ref, x_ref, lg_ref, lb_ref, w_ref,
                    m_ref, g2_ref, wr_ref, x1_ref, h2_ref, lt_ref):
    ysum = yf_ref[0, 0] + yr_ref[0, 0]
    ri = lax.broadcasted_iota(jnp.int32, (LANES, LANES), 0) // HEAD_DIM
    ci = lax.broadcasted_iota(jnp.int32, (LANES, LANES), 1) // HEAD_DIM
    hm = _bf(jnp.where(ri == ci, 1.0 / HEAD_DIM, 0.0))

    def head_mean(z):
        hi = _bf(z)
        lo = _bf(z - hi.astype(F32))
        return _dot(hi, hm) + _dot(lo, hm)

    blocks = []
    for j in range(ysum.shape[1] // LANES):
        yb = ysum[:, j * LANES:(j + 1) * LANES]
        dev = yb - head_mean(yb)
        var = head_mean(dev * dev)
        blocks.append(dev * lax.rsqrt(var + LNX_EPS))
    yn = jnp.concatenate(blocks, axis=1) if len(blocks) > 1 else blocks[0]
    rw = (yn * lg_ref[...] + lb_ref[...] + bon_ref[0]) * gat_ref[0]
    mix = jnp.concatenate([_bf(rw), att_ref[0]], axis=1)
    m = m_ref[0]
    x1 = x_ref[0] + m[0:1] * _dot(mix, w_ref[...])
    x1_ref[0] = x1
    ms = jnp.mean(x1 * x1, axis=-1, keepdims=True)
    h2 = x1 * lax.rsqrt(ms + NORM_EPS) * g2_ref[...] * (1.0 + m[2:3]) + m[1:2]
    h2_ref[0] = h2
    lt_ref[...] = _dot_nt(wr_ref[...], h2, precision=lax.Precision.HIGHEST)


def _out_projection(y, bonus, gate, attn, x, lnx_g, lnx_b, w_out_b, m2, g2, w_router_t, ctx_len, tr):
    b, t, d = x.shape
    rd = y.shape[3]
    ad = attn.shape[2]
    ne = w_router_t.shape[0]
    co = ctx_len // tr
    nt = t // tr
    return pl.pallas_call(
        _outproj_kernel,
        out_shape=[jax.ShapeDtypeStruct((b, t, d), F32),
                   jax.ShapeDtypeStruct((b, t, d), F32),
                   jax.ShapeDtypeStruct((ne, b * t), F32)],
        grid=(b, nt),
        in_specs=[pl.BlockSpec((1, 1, tr, rd), lambda bb, i: (0, bb, co + i, 0)),
                  pl.BlockSpec((1, 1, tr, rd), lambda bb, i: (1, bb, co + i, 0)),
                  pl.BlockSpec((1, tr, rd), lambda bb, i: (bb, co + i, 0)),
                  pl.BlockSpec((1, tr, rd), lambda bb, i: (bb, co + i, 0)),
                  pl.BlockSpec((1, tr, ad), lambda bb, i: (bb, i, 0)),
                  pl.BlockSpec((1, tr, d), lambda bb, i: (bb, i, 0)),
                  pl.BlockSpec((1, rd), lambda bb, i: (0, 0)),
                  pl.BlockSpec((1, rd), lambda bb, i: (0, 0)),
                  pl.BlockSpec((rd + ad, d), lambda bb, i: (0, 0)),
                  pl.BlockSpec((1, 4, d), lambda bb, i: (bb, 0, 0)),
                  pl.BlockSpec((1, d), lambda bb, i: (0, 0)),
                  pl.BlockSpec((ne, d), lambda bb, i: (0, 0))],
        out_specs=[pl.BlockSpec((1, tr, d), lambda bb, i: (bb, i, 0)),
                   pl.BlockSpec((1, tr, d), lambda bb, i: (bb, i, 0)),
                   pl.BlockSpec((ne, tr), lambda bb, i: (0, bb * nt + i))],
        compiler_params=_params("parallel", "parallel"),
        name="rwkv_out_outproj_norm2_router",
    )(y, y, bonus, gate, attn, x, lnx_g, lnx_b, w_out_b, m2, g2, w_router_t)


def _router_kernel(l_ref, b_ref, idx_ref, gate_ref, rank_ref, cnt_ref, carry_ref):
    @pl.when(pl.program_id(0) == 0)
    def _():
        carry_ref[...] = jnp.zeros_like(carry_ref)

    sc = jax.nn.sigmoid(l_ref[...])
    ne, tn = sc.shape
    eg = ne // N_GROUPS
    bz = sc + b_ref[...]
    g3 = bz.reshape(N_GROUPS, eg, tn)
    io = lax.broadcasted_iota(jnp.int32, (N_GROUPS, eg, tn), 1).astype(F32)
    m1 = jnp.max(g3, axis=1, keepdims=True)
    i1 = jnp.min(jnp.where(g3 == m1, io, float(eg)), axis=1, keepdims=True)
    m2 = jnp.max(jnp.where(io == i1, -jnp.inf, g3), axis=1, keepdims=True)
    gs = (m1 + m2).reshape(N_GROUPS, tn)

    def topk_hits(vals, k):
        n = vals.shape[0]
        idx = lax.broadcasted_iota(jnp.int32, vals.shape, 0).astype(F32)
        hits = []
        cur = vals
        for _ in range(k):
            mx = jnp.max(cur, axis=0, keepdims=True)
            ix = jnp.min(jnp.where(cur == mx, idx, float(n)), axis=0, keepdims=True)
            hit = idx == ix
            hits.append(hit)
            cur = jnp.where(hit, -jnp.inf, cur)
        return hits

    gsel = sum(jnp.where(h, 1.0, 0.0) for h in topk_hits(gs, TOPK_GROUPS))
    masked = jnp.where(gsel.reshape(N_GROUPS, 1, tn) > 0.0, g3, -jnp.inf).reshape(ne, tn)
    hits = topk_hits(masked, TOP_K)
    chosen = sum(jnp.where(h, 1.0, 0.0) for h in hits)
    denom = jnp.sum(chosen * sc, axis=0, keepdims=True)

    si = lax.broadcasted_iota(jnp.int32, (tn, tn), 0)
    ti = lax.broadcasted_iota(jnp.int32, (tn, tn), 1)
    before = _bf(jnp.where(si < ti, 1.0, 0.0))
    rank_full = _dot(_bf(chosen), before) + carry_ref[...]
    carry_ref[...] += jnp.sum(chosen, axis=1, keepdims=True)
    cnt_ref[...] = carry_ref[...]

    eidx = lax.broadcasted_iota(jnp.int32, (ne, tn), 0).astype(F32)
    for j, hit in enumerate(hits):
        idx_ref[j:j + 1, :] = jnp.sum(jnp.where(hit, eidx, 0.0), axis=0, keepdims=True).astype(jnp.int32)
        gate_ref[j:j + 1, :] = (jnp.sum(jnp.where(hit, sc, 0.0), axis=0, keepdims=True)
                                / denom * ROUTED_SCALE)
        rank_ref[j:j + 1, :] = jnp.sum(jnp.where(hit, rank_full, 0.0), axis=0,
                                       keepdims=True).astype(jnp.int32)


def _router(logits_t, bias_col):
    ne, n = logits_t.shape
    tn = 1024 if n % 1024 == 0 else n
    choice = lambda: pl.BlockSpec((TOP_K, tn), lambda i: (0, i))
    return pl.pallas_call(
        _router_kernel,
        out_shape=[jax.ShapeDtypeStruct((TOP_K, n), jnp.int32),
                   jax.ShapeDtypeStruct((TOP_K, n), F32),
                   jax.ShapeDtypeStruct((TOP_K, n), jnp.int32),
                   jax.ShapeDtypeStruct((ne, 1), F32)],
        grid=(n // tn,),
        in_specs=[pl.BlockSpec((ne, tn), lambda i: (0, i)),
                  pl.BlockSpec((ne, 1), lambda i: (0, 0))],
        out_specs=[choice(), choice(), choice(), pl.BlockSpec((ne, 1), lambda i: (0, 0))],
        scratch_shapes=[pltpu.VMEM((ne, 1), F32)],
        compiler_params=_params("arbitrary"),
        name="group_topk_router",
    )(logits_t, bias_col)


def _dispatch_kernel(pad_start_ref, npad_ref, nused_ref, pos_ref, h_ref, xs_ref, zero_ref, sem,
                     *, n_experts):
    tt = h_ref.shape[0]
    bm = zero_ref.shape[0]

    def issue(t, carry):
        for k in range(TOP_K):
            pltpu.make_async_copy(h_ref.at[pl.ds(t, 1)], xs_ref.at[pl.ds(pos_ref[k, t], 1)], sem).start()
        return carry

    lax.fori_loop(0, tt, issue, 0)
    for _ in range(TOP_K):
        pltpu.make_async_copy(h_ref, xs_ref.at[pl.ds(0, tt)], sem).wait()

    @pl.when(pl.program_id(0) == pl.num_programs(0) - 1)
    def _():
        zero_ref[...] = jnp.zeros_like(zero_ref)

        zero_row_src = zero_ref.at[pl.ds(0, 1)]

        def per_expert(e, carry):
            def zero_row(i, c2):
                pltpu.make_async_copy(zero_row_src, xs_ref.at[pl.ds(pad_start_ref[e] + i, 1)], sem).start()
                return c2

            def zero_wait(i, c2):
                pltpu.make_async_copy(zero_row_src, xs_ref.at[pl.ds(0, 1)], sem).wait()
                return c2

            lax.fori_loop(0, npad_ref[e], zero_row, 0)
            lax.fori_loop(0, npad_ref[e], zero_wait, 0)
            return carry

        lax.fori_loop(0, n_experts, per_expert, 0)

        def zero_block(j, carry):
            blk = pltpu.make_async_copy(zero_ref, xs_ref.at[pl.ds(j * bm, bm)], sem)
            blk.start()
            blk.wait()
            return carry

        lax.fori_loop(nused_ref[0], xs_ref.shape[0] // bm, zero_block, 0)


def _dispatch(h2, pos8, pad_start, npad, n_used, n_slots, tt, bm):
    n, d = h2.shape
    ne = pad_start.shape[0]
    grid_spec = pltpu.PrefetchScalarGridSpec(
        num_scalar_prefetch=3,
        grid=(n // tt,),
        in_specs=[pl.BlockSpec((TOP_K, tt), lambda i, ps, npd, nu: (0, i), memory_space=pltpu.SMEM),
                  pl.BlockSpec((tt, d), lambda i, ps, npd, nu: (i, 0))],
        out_specs=pl.BlockSpec(memory_space=pl.ANY),
        scratch_shapes=[pltpu.VMEM((bm, d), F32), pltpu.SemaphoreType.DMA(())])
    return pl.pallas_call(
        functools.partial(_dispatch_kernel, n_experts=ne),
        out_shape=jax.ShapeDtypeStruct((n_slots, d), F32),
        grid_spec=grid_spec,
        compiler_params=_params("arbitrary"),
        name="expert_dispatch",
    )(pad_start, npad, n_used, pos8, h2)


def _expert_ffn_kernel(be_ref, nu_ref, x_ref, wg_ref, wu_ref, wd_ref, y_ref, wgb, wub, wdb):
    j = pl.program_id(0)
    used = j < nu_ref[0]
    fresh = jnp.logical_or(j == 0, be_ref[j] != be_ref[jnp.maximum(j - 1, 0)])

    @pl.when(jnp.logical_and(used, fresh))
    def _():
        wgb[...] = _bf(wg_ref[0])
        wub[...] = _bf(wu_ref[0])
        wdb[...] = _bf(wd_ref[0])

    @pl.when(used)
    def _():
        x = _bf(x_ref[...])
        a = _dot(x, wgb[...])
        u = _dot(x, wub[...])
        y_ref[...] = _dot(_bf(a * jax.nn.sigmoid(a) * u), wdb[...])

    @pl.when(jnp.logical_not(used))
    def _():
        y_ref[...] = jnp.zeros_like(y_ref)


def _expert_ffn(xs, blk_expert, n_used, e_gate, e_up, e_down, bm):
    n_slots, d = xs.shape
    ne, _, ff = e_gate.shape
    grid_spec = pltpu.PrefetchScalarGridSpec(
        num_scalar_prefetch=2,
        grid=(n_slots // bm,),
        in_specs=[pl.BlockSpec((bm, d), lambda j, be, nu: (jnp.minimum(j, nu[0] - 1), 0)),
                  pl.BlockSpec((1, d, ff), lambda j, be, nu: (be[j], 0, 0)),
                  pl.BlockSpec((1, d, ff), lambda j, be, nu: (be[j], 0, 0)),
                  pl.BlockSpec((1, ff, d), lambda j, be, nu: (be[j], 0, 0))],
        out_specs=pl.BlockSpec((bm, d), lambda j, be, nu: (j, 0)),
        scratch_shapes=[pltpu.VMEM((d, ff), BF16), pltpu.VMEM((d, ff), BF16), pltpu.VMEM((ff, d), BF16)])
    return pl.pallas_call(
        _expert_ffn_kernel,
        out_shape=jax.ShapeDtypeStruct((n_slots, d), F32),
        grid_spec=grid_spec,
        compiler_params=_params("arbitrary"),
        name="routed_expert_ffn",
    )(blk_expert, n_used, xs, e_gate, e_up, e_down)


def _final_kernel(pos_ref, h_ref, x1_ref, g_ref, ys_ref, wg_ref, wu_ref, wd_ref, m_ref, fg_ref,
                  o_ref, buf, sem):
    tr = h_ref.shape[1]

    def issue(t, carry):
        for k in range(TOP_K):
            pltpu.make_async_copy(ys_ref.at[pl.ds(pos_ref[k, t], 1)], buf.at[k, pl.ds(t, 1)], sem).start()
        return carry

    lax.fori_loop(0, tr, issue, 0)
    h = _bf(h_ref[0])
    a = _dot(h, wg_ref[...])
    u = _dot(h, wu_ref[...])
    moe = _dot(_bf(a * jax.nn.sigmoid(a) * u), wd_ref[...])
    g = g_ref[...]
    for k in range(TOP_K):
        pltpu.make_async_copy(ys_ref.at[pl.ds(0, tr)], buf.at[k], sem).wait()
    for k in range(TOP_K):
        moe = moe + g[:, k:k + 1] * buf[k]
    x2 = x1_ref[0] + m_ref[0, 3:4, :] * moe
    ms = jnp.mean(x2 * x2, axis=-1, keepdims=True)
    o_ref[0] = x2 * lax.rsqrt(ms + NORM_EPS) * fg_ref[...]


def _final(h2, x1, pos8, gate_rows, ys, s_gate_b, s_up_b, s_down_b, m2, final_g, tr):
    b, t, d = x1.shape
    ff = s_gate_b.shape[1]
    nt = t // tr
    row = lambda: pl.BlockSpec((1, tr, d), lambda bb, i: (bb, i, 0))
    return pl.pallas_call(
        _final_kernel,
        out_shape=jax.ShapeDtypeStruct((b, t, d), F32),
        grid=(b, nt),
        in_specs=[pl.BlockSpec((TOP_K, tr), lambda bb, i: (0, bb * nt + i), memory_space=pltpu.SMEM),
                  row(), row(),
                  pl.BlockSpec((tr, TOP_K), lambda bb, i: (bb * nt + i, 0)),
                  pl.BlockSpec(memory_space=pl.ANY),
                  pl.BlockSpec((d, ff), lambda bb, i: (0, 0)),
                  pl.BlockSpec((d, ff), lambda bb, i: (0, 0)),
                  pl.BlockSpec((ff, d), lambda bb, i: (0, 0)),
                  pl.BlockSpec((1, 4, d), lambda bb, i: (bb, 0, 0)),
                  pl.BlockSpec((1, d), lambda bb, i: (0, 0))],
        out_specs=row(),
        scratch_shapes=[pltpu.VMEM((TOP_K, tr, d), F32), pltpu.SemaphoreType.DMA(())],
        compiler_params=_params("parallel", "parallel"),
        name="combine_shared_expert_final_norm",
    )(pos8, h2, x1, gate_rows, ys, s_gate_b, s_up_b, s_down_b, m2, final_g)


def _pad_to(a, axis, n):
    pad = [(0, 0)] * a.ndim
    pad[axis] = (0, n - a.shape[axis])
    return jnp.pad(a, pad)


def _layout(rd, wl, al, gl, ad, kvd):
    assert wl <= LANES and al <= LANES
    lay = {"rd": rd, "gl": gl, "ad": ad, "kvd": kvd}
    off = 0
    for name, width in (("q", ad), ("r", rd), ("k", rd), ("v", rd), ("ka", kvd), ("va", kvd),
                        ("xw", LANES), ("xa", LANES), ("xg", gl)):
        assert width % LANES == 0
        lay[name] = off
        off += width
    lay["ncols"] = off
    return lay


def kernel(x, c, ctx, c_ctx, w_mod, b_mod, norm1_g, norm2_g, w_in, shift_mu, k_k, k_a, r_k, decay_bias, decay_up, iclr_bias, iclr_up, gate_up, lnx_g, lnx_b, attn_sink, w_out, w_router, expert_bias, e_gate, e_up, e_down, s_gate, s_up, s_down, final_g):
    assert w_mod.shape[0] == 1, "single-layer block only"
    b, t, d = x.shape
    ctx_len = ctx.shape[1]
    rh, hd = r_k.shape[1:]
    assert hd == HEAD_DIM and b + 1 <= SUBLANES
    rd = rh * hd
    wl, al, gl = decay_up.shape[2], iclr_up.shape[2], gate_up.shape[1]
    ad = attn_sink.shape[1] * hd
    kvd = (w_in.shape[2] - (3 * rd + wl + al + gl) - ad) // 2
    ne = w_router.shape[2]
    lay = _layout(rd, wl, al, gl, ad, kvd)
    tr = 256 if ctx_len % 256 == 0 and t % 256 == 0 else CHUNK
    assert ctx_len % CHUNK == 0 and t % CHUNK == 0 and t % QBLOCK == 0 and WINDOW == QBLOCK

    c8 = _pad_to(jnp.concatenate([c, c_ctx[None]], axis=0), 0, SUBLANES)
    mod = _modulation(c8, w_mod[0], b_mod[0][None])
    mod6 = mod.reshape(SUBLANES, 6, d)
    mx, mc = mod6[:b], mod6[b]
    m1 = jnp.stack([jnp.broadcast_to(mc[None, 0:2], (b, 2, d)), mx[:, 0:2]], axis=1)
    m2 = jnp.stack([mx[:, 2], mx[:, 3], mx[:, 4], mx[:, 5]], axis=1)

    wi = w_in[0]
    o_w, o_a, o_g, o_q = 3 * rd, 3 * rd + wl, 3 * rd + wl + al, 3 * rd + wl + al + gl
    w_in_p = _bf(jnp.concatenate([
        wi[:, o_q:o_q + ad], wi[:, :3 * rd], wi[:, o_q + ad:],
        _pad_to(wi[:, o_w:o_a], 1, LANES), _pad_to(wi[:, o_a:o_g], 1, LANES), wi[:, o_g:o_q]], axis=1))
    mu = shift_mu[0]
    prm = {
        "mu_r": mu[None, :rd], "mu_k": mu[None, rd:2 * rd], "mu_v": mu[None, 2 * rd:3 * rd],
        "mu_w": _pad_to(mu[None, o_w:o_a], 1, LANES), "mu_a": _pad_to(mu[None, o_a:o_g], 1, LANES),
        "mu_g": mu[None, o_g:o_q],
        "k_k": k_k, "k_a": k_a, "r_k": r_k.reshape(1, rd),
        "decay_bias": decay_bias[0], "decay_up": _bf(_pad_to(decay_up[0], 1, LANES)),
        "iclr_bias": iclr_bias[0], "iclr_up": _bf(_pad_to(iclr_up[0], 1, LANES)),
        "gate_up": _bf(gate_up[0]),
    }

    h_all = jnp.concatenate([ctx, x], axis=1)
    p_all = _in_projection(h_all, norm1_g, m1, w_in_p, ctx_len, tr)

    npair = rd // LANES
    pp = max(n for n in (1, 2, 4) if npair % n == 0
             and all(lay[key] % (n * LANES) == 0 for key in ("r", "k", "v")))
    rp, y0, mm, nn, bonus, gate = _wkv_prep(p_all, lay, prm, ctx_len, None, pp)
    y = _wkv_sweep(rp, y0, mm, nn, ctx_len)

    nf = HEAD_DIM // 4
    freqs = ROPE_BASE ** (-jnp.arange(nf, dtype=F32) / nf)
    pos = jnp.arange(t, dtype=jnp.int32)
    ang_r = (pos // GRID_W).astype(F32)[:, None] * freqs
    ang_c = (pos % GRID_W).astype(F32)[:, None] * freqs
    cos_h = jnp.concatenate([jnp.cos(ang_r)] * 2 + [jnp.cos(ang_c)] * 2, axis=1)
    sin_h = jnp.concatenate([-jnp.sin(ang_r), jnp.sin(ang_r), -jnp.sin(ang_c), jnp.sin(ang_c)], axis=1)
    cos_t = jnp.concatenate([cos_h, cos_h], axis=1)
    sin_t = jnp.concatenate([sin_h, sin_h], axis=1)
    attn = _attention(p_all, lay, cos_t, sin_t, attn_sink, ctx_len, t)

    x1, h2, logits_t = _out_projection(y, bonus, gate, attn, x, lnx_g, lnx_b, _bf(w_out[0]), m2,
                                       norm2_g, w_router[0].T, ctx_len, tr)
    idx8, gate8, rank8, counts = _router(logits_t, expert_bias[0][:, None])

    n_tok = b * t
    bm = EXPERT_ROWS
    assert (n_tok * TOP_K) % bm == 0
    n_blk = n_tok * TOP_K // bm + ne
    cnt = counts[:, 0].astype(jnp.int32)
    padded = (cnt + bm - 1) // bm * bm
    pend = jnp.cumsum(padded)
    pstart = pend - padded
    pos8 = pstart[idx8] + rank8
    blk_expert = jnp.minimum(jnp.searchsorted(pend, jnp.arange(n_blk, dtype=jnp.int32) * bm, side="right"),
                             ne - 1).astype(jnp.int32)
    n_used = (pend[-1:] // bm).astype(jnp.int32)

    xs = _dispatch(h2.reshape(n_tok, d), pos8, pstart + cnt, padded - cnt, n_used, n_blk * bm, tr, bm)
    ys = _expert_ffn(xs, blk_expert, n_used, e_gate[0], e_up[0], e_down[0], bm)
    return _final(h2, x1, pos8, gate8.T, ys, _bf(s_gate[0]), _bf(s_up[0]), _bf(s_down[0]),
                  m2, final_g[None], tr)
```

```python
import functools
import math

import jax
import jax.numpy as jnp
from jax import lax
from jax.experimental import pallas as pl
from jax.experimental.pallas import tpu as pltpu

GRID_W = 64
WINDOW = 128
ROPE_BASE = 10000.0
N_GROUPS = 8
TOPK_GROUPS = 4
TOP_K = 8
ROUTED_SCALE = 2.5
NORM_EPS = 1e-6
LNX_EPS = 64e-5
KK_EPS = 1e-12

LANES = 128
SUBLANES = 8
VMEM_LIMIT_BYTES = 56 * 1024 * 1024

HEAD_DIM = 64
CHUNK = 64
QBLOCK = 128
EXPERT_ROWS = 256

F32 = jnp.float32
BF16 = jnp.bfloat16
NEG_BIG = -1e30


def _bf(x):
    return x.astype(BF16)


def _dot(a, b, precision=None):
    return jnp.dot(a, b, preferred_element_type=F32, precision=precision)


def _dot_nt(a, b, precision=None):
    return lax.dot_general(a, b, (((1,), (1,)), ((), ())), preferred_element_type=F32,
                           precision=precision)


def _dot_tn(a, b):
    return lax.dot_general(a, b, (((0,), (0,)), ((), ())), preferred_element_type=F32)


def _params(*sem):
    return pltpu.CompilerParams(dimension_semantics=sem, vmem_limit_bytes=VMEM_LIMIT_BYTES)


def _mod_kernel(c_ref, w_ref, b_ref, o_ref):
    c = c_ref[...]
    s = c * jax.nn.sigmoid(c)
    hi = _bf(s)
    lo = _bf(s - hi.astype(F32))
    r = _dot(jnp.concatenate([hi, lo], axis=0), _bf(w_ref[...]))
    o_ref[...] = r[:SUBLANES] + r[SUBLANES:] + b_ref[...]


def _modulation(c8, w_mod, b_mod):
    d, n = w_mod.shape
    tn = 1024 if n % 1024 == 0 else n
    return pl.pallas_call(
        _mod_kernel,
        out_shape=jax.ShapeDtypeStruct((SUBLANES, n), F32),
        grid=(n // tn,),
        in_specs=[pl.BlockSpec((SUBLANES, d), lambda j: (0, 0)),
                  pl.BlockSpec((d, tn), lambda j: (0, j)),
                  pl.BlockSpec((1, tn), lambda j: (0, j))],
        out_specs=pl.BlockSpec((SUBLANES, tn), lambda j: (0, j)),
        compiler_params=_params("parallel"),
        name="adaln_modulation",
    )(c8, w_mod, b_mod)


def _inproj_kernel(c_ref, x_ref, g_ref, m_ref, w_ref, o_ref, *, nct):
    x = jnp.where(pl.program_id(2) < nct, c_ref[0], x_ref[0])
    ms = jnp.mean(x * x, axis=-1, keepdims=True)
    y = x * lax.rsqrt(ms + NORM_EPS) * g_ref[...]
    m = m_ref[0, 0]
    h = y * (1.0 + m[1:2]) + m[0:1]
    o_ref[0] = _dot(_bf(h), w_ref[...])


def _in_projection(ctx, x, g1, m1, w_in_p, tr):
    b, t, d = x.shape
    ctx_len = ctx.shape[1]
    ta = ctx_len + t
    ncols = w_in_p.shape[1]
    tn = ncols // 4 if (ncols // 4) % LANES == 0 and ncols % 4 == 0 else ncols
    nct = ctx_len // tr
    return pl.pallas_call(
        functools.partial(_inproj_kernel, nct=nct),
        out_shape=jax.ShapeDtypeStruct((b, ta, ncols), F32),
        grid=(ncols // tn, b, ta // tr),
        in_specs=[pl.BlockSpec((1, tr, d), lambda n, bb, i: (bb, jnp.minimum(i, nct - 1), 0)),
                  pl.BlockSpec((1, tr, d), lambda n, bb, i: (bb, jnp.maximum(i - nct, 0), 0)),
                  pl.BlockSpec((1, d), lambda n, bb, i: (0, 0)),
                  pl.BlockSpec((1, 1, 2, d), lambda n, bb, i: (bb, jnp.where(i >= nct, 1, 0), 0, 0)),
                  pl.BlockSpec((d, tn), lambda n, bb, i: (0, n))],
        out_specs=pl.BlockSpec((1, tr, tn), lambda n, bb, i: (bb, i, n)),
        compiler_params=_params("parallel", "parallel", "parallel"),
        name="norm_modulate_inproj",
    )(ctx, x, g1, m1, w_in_p)


def _wkv_prep_kernel(r_c, r_p, r_n, k_c, k_p, k_n, v_c, v_p, v_n, w_c, w_p, w_n, a_c, a_p, a_n,
                     g_c, g_p, g_n, mu_r, mu_k, mu_v, mu_w, mu_a, mu_g, kk_ref, ka_ref, rk_ref,
                     db_ref, du_ref, ib_ref, iu_ref, gu_ref,
                     rp_out, y0_out, m_out, n_out, bonus_out, gate_out,
                     *, nctx_chunks, n_chunks, prec_t):
    L = CHUNK
    c = pl.program_id(1)
    has_prev = jnp.logical_and(c != 0, c != nctx_chunks)
    has_next = jnp.logical_and(c != nctx_chunks - 1, c != n_chunks - 1)

    def shifted(cur, prv, nxt, mu):
        x = cur[0]
        rows = lax.broadcasted_iota(jnp.int32, (L, 1), 0)
        pr = jnp.where(has_prev, prv[0, SUBLANES - 1:SUBLANES, :], 0.0)
        nx = jnp.where(has_next, nxt[0, 0:1, :], 0.0)
        up = jnp.where(rows == 0, pr, pltpu.roll(x, 1, 0))
        dn = jnp.where(rows == L - 1, nx, pltpu.roll(x, L - 1, 0))
        return x + mu[...] * (0.5 * (up + dn) - x)

    r = shifted(r_c, r_p, r_n, mu_r)
    k = shifted(k_c, k_p, k_n, mu_k)
    v = shifted(v_c, v_p, v_n, mu_v)
    xw = shifted(w_c, w_p, w_n, mu_w)
    xa = shifted(a_c, a_p, a_n, mu_a)
    xg = shifted(g_c, g_p, g_n, mu_g)

    lane = lax.broadcasted_iota(jnp.int32, (1, LANES), 1)
    m0 = lane < HEAD_DIM

    def head_sum(x):
        s0 = jnp.sum(jnp.where(m0, x, 0.0), axis=-1, keepdims=True)
        s1 = jnp.sum(jnp.where(m0, 0.0, x), axis=-1, keepdims=True)
        return jnp.where(m0, s0, s1)

    def stack2(x):
        return jnp.concatenate([jnp.where(m0, x, 0.0), jnp.where(m0, 0.0, x)], axis=0)

    tw = _bf(jnp.tanh(xw))
    xab = _bf(xa)
    lw_all = [(-math.exp(-0.5)) * jax.nn.sigmoid(db_ref[d:d + 1, :] + _dot(tw, du_ref[d]))
              for d in range(2)]
    sa_all = [jax.nn.sigmoid(ib_ref[d:d + 1, :] + _dot(xab, iu_ref[d])) for d in range(2)]
    gate_out[0] = _dot(_bf(jax.nn.sigmoid(xg)), gu_ref[...])

    ri = lax.broadcasted_iota(jnp.int32, (L, L), 0)
    ci = lax.broadcasted_iota(jnp.int32, (L, L), 1)
    ri2 = lax.broadcasted_iota(jnp.int32, (2 * L, 2 * L), 0)
    ci2 = lax.broadcasted_iota(jnp.int32, (2 * L, 2 * L), 1)
    rt = jnp.bitwise_and(ri2, L - 1)
    ct = jnp.bitwise_and(ci2, L - 1)
    eye = ri2 == ci2
    n_sq = int(math.log2(L)) - 1

    tri = [(ci <= ri).astype(F32), (ci >= ri).astype(F32)]
    strict = [ct < rt, ct > rt]
    incl = [ct <= rt, ct >= rt]
    cum = [_dot(tri[d], lw_all[d], precision=lax.Precision.HIGHEST) for d in range(2)]
    cend = [jnp.sum(lw_all[d], axis=0, keepdims=True) for d in range(2)]
    e_in = [jnp.exp(cum[d]) for d in range(2)]
    e_ex = [jnp.exp(cum[d] - lw_all[d]) for d in range(2)]
    e_neg = [jnp.exp(-cum[d]) for d in range(2)]
    e_end = [jnp.exp(cend[d] - cum[d]) for d in range(2)]
    p_end = [jnp.exp(cend[d]) for d in range(2)]
    kd_all = [k * (1.0 + (sa_all[d] - 1.0) * ka_ref[...]) for d in range(2)]

    npp = r.shape[1] // LANES
    chains = [(p, d) for p in range(npp) for d in range(2)]
    sls = [slice(p * LANES, (p + 1) * LANES) for p in range(npp)]
    kk_p, vs_p, vsb_p = [], [], []
    for p in range(npp):
        kx = k[:, sls[p]] * kk_ref[:, sls[p]]
        kk_p.append(kx * lax.rsqrt(head_sum(kx * kx) + KK_EPS))
        vs_p.append(stack2(v[:, sls[p]]))
        vsb_p.append(_bf(vs_p[p]))
        ksum = kd_all[0][:, sls[p]] + kd_all[1][:, sls[p]]
        bonus_out[0, :, sls[p]] = head_sum(r[:, sls[p]] * ksum * rk_ref[:, sls[p]]) * v[:, sls[p]]

    a_s, r_s, bh_s, kh_s, g = [], [], [], [], []
    for p, d in chains:
        sl = sls[p]
        kd = kd_all[d][:, sl]
        bvec = kk_p[p] * sa_all[d][:, sl]
        a_s.append(stack2(-kk_p[p] * e_ex[d][:, sl]))
        r_s.append(stack2(r[:, sl] * e_in[d][:, sl]))
        b_s = stack2(bvec * e_neg[d][:, sl])
        k_s = stack2(kd * e_neg[d][:, sl])
        bh_s.append(stack2(bvec * e_end[d][:, sl]))
        kh_s.append(stack2(kd * e_end[d][:, sl]))
        g.append(_dot_nt(_bf(jnp.concatenate([a_s[-1], r_s[-1]], axis=0)),
                         _bf(jnp.concatenate([b_s, k_s], axis=0))))
    a_ab = [jnp.where(strict[d], g[i][:2 * L, :2 * L], 0.0) for i, (p, d) in enumerate(chains)]
    a_ak = [jnp.where(strict[d], g[i][:2 * L, 2 * L:], 0.0) for i, (p, d) in enumerate(chains)]
    a_rb = [jnp.where(incl[d], g[i][2 * L:, :2 * L], 0.0) for i, (p, d) in enumerate(chains)]
    a_rk = [jnp.where(incl[d], g[i][2 * L:, 2 * L:], 0.0) for i, (p, d) in enumerate(chains)]

    ident = jnp.where(eye, 1.0, 0.0)
    tmat = [ident + a for a in a_ab]
    apow = a_ab
    for _ in range(n_sq):
        if prec_t is None:
            apow = [_dot(_bf(a), _bf(a)) for a in apow]
            tmat = [t + _dot(_bf(a), _bf(t)) for a, t in zip(apow, tmat)]
        else:
            apow = [_dot(a, a, precision=prec_t) for a in apow]
            tmat = [t + _dot(a, t, precision=prec_t) for a, t in zip(apow, tmat)]

    x1 = [_dot(_bf(a_ak[i]), vsb_p[p]) for i, (p, d) in enumerate(chains)]
    y0v = [_dot(_bf(a_rk[i]), vsb_p[p]) for i, (p, d) in enumerate(chains)]
    tu = [_dot(_bf(tmat[i]), _bf(jnp.concatenate([a_s[i], x1[i]], axis=1)))
          for i in range(len(chains))]
    ry = [_dot(_bf(a_rb[i]), _bf(tu[i])) for i in range(len(chains))]
    for i, (p, d) in enumerate(chains):
        rp_out[d, 0, 0, p] = _bf(r_s[i] + ry[i][:, :LANES])
        y0s = ry[i][:, LANES:] + y0v[i]
        y0_out[d, 0, 0, p] = y0s[:L] + y0s[L:]
    for i, (p, d) in enumerate(chains):
        m_out[d, 0, 0, p] = _bf(jnp.where(eye, p_end[d][:, sls[p]], 0.0)
                                + _dot_tn(_bf(tu[i][:, :LANES]), _bf(bh_s[i])))
    for i, (p, d) in enumerate(chains):
        n_out[d, 0, 0, p] = _dot_tn(_bf(jnp.concatenate([tu[i][:, LANES:], vs_p[p]], axis=0)),
                                    _bf(jnp.concatenate([bh_s[i], kh_s[i]], axis=0)))


def _wkv_prep(p_all, lay, prm, ctx_len, prec_t, pp):
    b, ta, _ = p_all.shape
    L = CHUNK
    nc = ta // L
    nctx = ctx_len // L
    npair = lay["rd"] // LANES
    gl = lay["gl"]
    rows8 = ta // SUBLANES

    def cur(colfn, width=LANES):
        return pl.BlockSpec((1, L, width), lambda bb, c, p: (bb, c, colfn(p)))

    def prv(colfn, width=LANES):
        return pl.BlockSpec((1, SUBLANES, width),
                            lambda bb, c, p: (bb, jnp.maximum(c * (L // SUBLANES) - 1, 0), colfn(p)))

    def nxt(colfn, width=LANES):
        return pl.BlockSpec((1, SUBLANES, width),
                            lambda bb, c, p: (bb, jnp.minimum((c + 1) * (L // SUBLANES), rows8 - 1), colfn(p)))

    def trio(colfn, width=LANES):
        return [cur(colfn, width), prv(colfn, width), nxt(colfn, width)]

    pw = pp * LANES
    assert npair % pp == 0 and all(lay[n] % pw == 0 for n in ("r", "k", "v"))
    rb, kb, vb = lay["r"] // pw, lay["k"] // pw, lay["v"] // pw
    wb, ab = lay["xw"] // LANES, lay["xa"] // LANES
    assert lay["xg"] % gl == 0
    gb = lay["xg"] // gl

    def vec(width=pw):
        return pl.BlockSpec((1, width), lambda bb, c, p: (0, p))

    def vec0(width=LANES):
        return pl.BlockSpec((1, width), lambda bb, c, p: (0, 0))

    in_specs = (trio(lambda p: rb + p, pw) + trio(lambda p: kb + p, pw) + trio(lambda p: vb + p, pw)
                + trio(lambda p: wb) + trio(lambda p: ab) + trio(lambda p: gb, gl)
                + [vec(), vec(), vec(), vec0(), vec0(), vec0(gl),
                   vec(), vec(), vec(),
                   pl.BlockSpec((2, pw), lambda bb, c, p: (0, p)),
                   pl.BlockSpec((2, LANES, pw), lambda bb, c, p: (0, 0, p)),
                   pl.BlockSpec((2, pw), lambda bb, c, p: (0, p)),
                   pl.BlockSpec((2, LANES, pw), lambda bb, c, p: (0, 0, p)),
                   pl.BlockSpec((gl, pw), lambda bb, c, p: (0, p))])

    def opspec(rows):
        return pl.BlockSpec((2, 1, 1, pp, rows, LANES), lambda bb, c, p: (0, c, bb, p, 0, 0))

    out_specs = [opspec(2 * L), opspec(L), opspec(LANES), opspec(LANES),
                 pl.BlockSpec((1, L, pw), lambda bb, c, p: (bb, c, p)),
                 pl.BlockSpec((1, L, pw), lambda bb, c, p: (bb, c, p))]
    out_shape = [jax.ShapeDtypeStruct((2, nc, b, npair, 2 * L, LANES), BF16),
                 jax.ShapeDtypeStruct((2, nc, b, npair, L, LANES), F32),
                 jax.ShapeDtypeStruct((2, nc, b, npair, LANES, LANES), BF16),
                 jax.ShapeDtypeStruct((2, nc, b, npair, LANES, LANES), F32),
                 jax.ShapeDtypeStruct((b, ta, lay["rd"]), F32),
                 jax.ShapeDtypeStruct((b, ta, lay["rd"]), F32)]
    kern = functools.partial(_wkv_prep_kernel, nctx_chunks=nctx, n_chunks=nc, prec_t=prec_t)
    args = [p_all] * 18 + [prm["mu_r"], prm["mu_k"], prm["mu_v"], prm["mu_w"], prm["mu_a"], prm["mu_g"],
                           prm["k_k"], prm["k_a"], prm["r_k"], prm["decay_bias"], prm["decay_up"],
                           prm["iclr_bias"], prm["iclr_up"], prm["gate_up"]]
    return pl.pallas_call(
        kern, out_shape=out_shape, grid=(b, nc, npair // pp), in_specs=in_specs, out_specs=out_specs,
        compiler_params=_params("parallel", "parallel", "parallel"),
        name="wkv_chunk_operators",
    )(*args)


def _wkv_seq_kernel(rp_ref, y0_ref, m_ref, n_ref, y_ref, s_ref, *, nb, npair):
    L = CHUNK

    @pl.when(pl.program_id(1) == 0)
    def _():
        s_ref[...] = jnp.zeros_like(s_ref)

    for bb in range(nb):
        for p in range(npair):
            i = bb * npair + p
            s0 = _bf(s_ref[i])
            ys = _dot_nt(rp_ref[0, 0, bb, p], s0)
            y_ref[0, bb, :, p * LANES:(p + 1) * LANES] = ys[:L] + ys[L:] + y0_ref[0, 0, bb, p]
            s_ref[i] = _dot(s0, m_ref[0, 0, bb, p]) + n_ref[0, 0, bb, p]


def _wkv_sweep(rp, y0, mm, nn, ctx_len):
    _, nc, b, npair, _, _ = rp.shape
    L = CHUNK
    nctx = ctx_len // L

    def chunk_of(d, c):
        rev = jnp.where(c < nctx, nctx - 1 - c, nc - 1 - (c - nctx))
        return jnp.where(d == 0, c, rev)

    def opspec(rows):
        return pl.BlockSpec((1, 1, b, npair, rows, LANES), lambda d, c: (d, chunk_of(d, c), 0, 0, 0, 0))

    return pl.pallas_call(
        functools.partial(_wkv_seq_kernel, nb=b, npair=npair),
        out_shape=jax.ShapeDtypeStruct((2, b, nc * L, npair * LANES), F32),
        grid=(2, nc),
        in_specs=[opspec(2 * L), opspec(L), opspec(LANES), opspec(LANES)],
        out_specs=pl.BlockSpec((1, b, L, npair * LANES), lambda d, c: (d, 0, chunk_of(d, c), 0)),
        scratch_shapes=[pltpu.VMEM((b * npair, LANES, LANES), F32)],
        compiler_params=_params("parallel", "arbitrary"),
        name="wkv_state_sweep",
    )(rp, y0, mm, nn)


def _attn_kernel(q_ref, kp_ref, kc_ref, kn_ref, vp_ref, vc_ref, vn_ref, kx_ref, vx_ref,
                 cq_ref, sq_ref, cp_ref, sp_ref, cn_ref, sn_ref, sink_ref, o_ref,
                 *, n_blocks, kv_heads, group):
    n = pl.program_id(1)
    qb = QBLOCK

    def rope(x, cos, sin):
        w = x.shape[1]
        reps = w // LANES
        cs = jnp.concatenate([cos] * reps, axis=1) if reps > 1 else cos
        sn = jnp.concatenate([sin] * reps, axis=1) if reps > 1 else sin
        lane = lax.broadcasted_iota(jnp.int32, (1, w), 1)
        sw = jnp.where(jnp.bitwise_and(lane, 16) == 0, pltpu.roll(x, w - 16, 1), pltpu.roll(x, 16, 1))
        return x * cs + sw * sn

    q = rope(q_ref[0], cq_ref[...], sq_ref[...]) * (HEAD_DIM ** -0.5)
    kwin = jnp.concatenate([rope(kp_ref[0], cp_ref[...], sp_ref[...]),
                            rope(kc_ref[0], cq_ref[...], sq_ref[...]),
                            rope(kn_ref[0], cn_ref[...], sn_ref[...]),
                            kx_ref[0]], axis=0)
    vwin = jnp.concatenate([vp_ref[0], vc_ref[0], vn_ref[0], vx_ref[0]], axis=0)
    nkeys = kwin.shape[0]

    rows = group * qb
    qi = jnp.bitwise_and(lax.broadcasted_iota(jnp.int32, (rows, nkeys), 0), qb - 1)
    s_i = lax.broadcasted_iota(jnp.int32, (rows, nkeys), 1)
    blk = s_i // qb + (n - 1)
    in_win = (s_i >= qi) & (s_i <= qi + 2 * WINDOW) & (blk >= 0) & (blk < n_blocks)
    valid = in_win | (s_i >= 3 * qb)
    rgrp = lax.broadcasted_iota(jnp.int32, (rows, 1), 0) // qb

    scores, sinks = [], []
    for kh in range(kv_heads):
        kk_ = _bf(kwin[:, kh * HEAD_DIM:(kh + 1) * HEAD_DIM])
        qs = jnp.concatenate([q[:, (kh * group + g) * HEAD_DIM:(kh * group + g + 1) * HEAD_DIM]
                              for g in range(group)], axis=0)
        sk = jnp.zeros((rows, 1), F32)
        for g in range(group):
            h = kh * group + g
            sk = jnp.where(rgrp == g, sink_ref[0:1, h:h + 1], sk)
        sinks.append(sk)
        scores.append(jnp.where(valid, _dot_nt(_bf(qs), kk_), NEG_BIG))
    probs, denoms = [], []
    for s, sk in zip(scores, sinks):
        m = jnp.maximum(jnp.max(s, axis=-1, keepdims=True), sk)
        e = jnp.exp(s - m)
        denoms.append(jnp.sum(e, axis=-1, keepdims=True) + jnp.exp(sk - m))
        probs.append(_bf(e))
    pieces = []
    for kh in range(kv_heads):
        vv_ = _bf(vwin[:, kh * HEAD_DIM:(kh + 1) * HEAD_DIM])
        o = _dot(probs[kh], vv_) / denoms[kh]
        for g in range(group):
            pieces.append(o[g * qb:(g + 1) * qb])
    o_ref[0] = _bf(jnp.concatenate(pieces, axis=1))


def _attention(p_all, lay, cos_t, sin_t, sink, ctx_len, seq):
    b = p_all.shape[0]
    qb = QBLOCK
    nb = seq // qb
    cb = ctx_len // qb
    ad, kvd = lay["ad"], lay["kvd"]
    kv_heads = kvd // HEAD_DIM
    group = (ad // HEAD_DIM) // kv_heads
    assert lay["q"] % ad == 0 and lay["ka"] % kvd == 0 and lay["va"] % kvd == 0
    qc, kc, vc = lay["q"] // ad, lay["ka"] // kvd, lay["va"] // kvd

    def clampb(n, off):
        return jnp.clip(n + off, 0, nb - 1)

    def win(col, off):
        return pl.BlockSpec((1, qb, kvd), lambda bb, n: (bb, cb + clampb(n, off), col))

    def tab(off):
        return pl.BlockSpec((qb, LANES), lambda bb, n: (clampb(n, off), 0))

    assert ctx_len % qb == 0
    ctxspec = lambda col: pl.BlockSpec((1, ctx_len, kvd), lambda bb, n: (bb, 0, col))
    in_specs = [pl.BlockSpec((1, qb, ad), lambda bb, n: (bb, cb + n, qc)),
                win(kc, -1), win(kc, 0), win(kc, 1), win(vc, -1), win(vc, 0), win(vc, 1),
                ctxspec(kc), ctxspec(vc),
                tab(0), tab(0), tab(-1), tab(-1), tab(1), tab(1),
                pl.BlockSpec((1, sink.shape[1]), lambda bb, n: (0, 0))]
    return pl.pallas_call(
        functools.partial(_attn_kernel, n_blocks=nb, kv_heads=kv_heads, group=group),
        out_shape=jax.ShapeDtypeStruct((b, seq, ad), BF16),
        grid=(b, nb),
        in_specs=in_specs,
        out_specs=pl.BlockSpec((1, qb, ad), lambda bb, n: (bb, n, 0)),
        compiler_params=_params("parallel", "parallel"),
        name="window_ctx_attention",
    )(p_all, p_all, p_all, p_all, p_all, p_all, p_all, p_all, p_all,
      cos_t, sin_t, cos_t, sin_t, cos_t, sin_t, sink)


def _outproj_kernel(yf_ref, yr_ref, bon_ref, gat_ref, att_ref, x_ref, lg_ref, lb_ref, w_ref,
                    m_ref, g2_ref, wr_ref, x1_ref, h2_ref, lt_ref):
    ysum = yf_ref[0, 0] + yr_ref[0, 0]
    ri = lax.broadcasted_iota(jnp.int32, (LANES, LANES), 0) // HEAD_DIM
    ci = lax.broadcasted_iota(jnp.int32, (LANES, LANES), 1) // HEAD_DIM
    hm = _bf(jnp.where(ri == ci, 1.0 / HEAD_DIM, 0.0))

    def head_mean(z):
        hi = _bf(z)
        lo = _bf(z - hi.astype(F32))
        return _dot(hi, hm) + _dot(lo, hm)

    blocks = []
    for j in range(ysum.shape[1] // LANES):
        yb = ysum[:, j * LANES:(j + 1) * LANES]
        dev = yb - head_mean(yb)
        var = head_mean(dev * dev)
        blocks.append(dev * lax.rsqrt(var + LNX_EPS))
    yn = jnp.concatenate(blocks, axis=1) if len(blocks) > 1 else blocks[0]
    rw = (yn * lg_ref[...] + lb_ref[...] + bon_ref[0]) * gat_ref[0]
    mix = jnp.concatenate([_bf(rw), att_ref[0]], axis=1)
    m = m_ref[0]
    x1 = x_ref[0] + m[0:1] * _dot(mix, w_ref[...])
    x1_ref[0] = x1
    ms = jnp.mean(x1 * x1, axis=-1, keepdims=True)
    h2 = x1 * lax.rsqrt(ms + NORM_EPS) * g2_ref[...] * (1.0 + m[2:3]) + m[1:2]
    h2_ref[0] = h2
    lt_ref[...] = _dot_nt(wr_ref[...], h2, precision=lax.Precision.HIGHEST)


def _out_projection(y, bonus, gate, attn, x, lnx_g, lnx_b, w_out_b, m2, g2, w_router_t, ctx_len, tr):
    b, t, d = x.shape
    rd = y.shape[3]
    ad = attn.shape[2]
    ne = w_router_t.shape[0]
    co = ctx_len // tr
    nt = t // tr
    return pl.pallas_call(
        _outproj_kernel,
        out_shape=[jax.ShapeDtypeStruct((b, t, d), F32),
                   jax.ShapeDtypeStruct((b, t, d), F32),
                   jax.ShapeDtypeStruct((ne, b * t), F32)],
        grid=(b, nt),
        in_specs=[pl.BlockSpec((1, 1, tr, rd), lambda bb, i: (0, bb, co + i, 0)),
                  pl.BlockSpec((1, 1, tr, rd), lambda bb, i: (1, bb, co + i, 0)),
                  pl.BlockSpec((1, tr, rd), lambda bb, i: (bb, co + i, 0)),
                  pl.BlockSpec((1, tr, rd), lambda bb, i: (bb, co + i, 0)),
                  pl.BlockSpec((1, tr, ad), lambda bb, i: (bb, i, 0)),
                  pl.BlockSpec((1, tr, d), lambda bb, i: (bb, i, 0)),
                  pl.BlockSpec((1, rd), lambda bb, i: (0, 0)),
                  pl.BlockSpec((1, rd), lambda bb, i: (0, 0)),
                  pl.BlockSpec((rd + ad, d), lambda bb, i: (0, 0)),
                  pl.BlockSpec((1, 4, d), lambda bb, i: (bb, 0, 0)),
                  pl.BlockSpec((1, d), lambda bb, i: (0, 0)),
                  pl.BlockSpec((ne, d), lambda bb, i: (0, 0))],
        out_specs=[pl.BlockSpec((1, tr, d), lambda bb, i: (bb, i, 0)),
                   pl.BlockSpec((1, tr, d), lambda bb, i: (bb, i, 0)),
                   pl.BlockSpec((ne, tr), lambda bb, i: (0, bb * nt + i))],
        compiler_params=_params("parallel", "parallel"),
        name="rwkv_out_outproj_norm2_router",
    )(y, y, bonus, gate, attn, x, lnx_g, lnx_b, w_out_b, m2, g2, w_router_t)


def _router_kernel(l_ref, b_ref, idx_ref, gate_ref, rank_ref, cnt_ref, carry_ref):
    @pl.when(pl.program_id(0) == 0)
    def _():
        carry_ref[...] = jnp.zeros_like(carry_ref)

    sc = jax.nn.sigmoid(l_ref[...])
    ne, tn = sc.shape
    eg = ne // N_GROUPS
    bz = sc + b_ref[...]
    g3 = bz.reshape(N_GROUPS, eg, tn)
    io = lax.broadcasted_iota(jnp.int32, (N_GROUPS, eg, tn), 1).astype(F32)
    m1 = jnp.max(g3, axis=1, keepdims=True)
    i1 = jnp.min(jnp.where(g3 == m1, io, float(eg)), axis=1, keepdims=True)
    m2 = jnp.max(jnp.where(io == i1, -jnp.inf, g3), axis=1, keepdims=True)
    gs = (m1 + m2).reshape(N_GROUPS, tn)

    def topk_hits(vals, k):
        n = vals.shape[0]
        idx = lax.broadcasted_iota(jnp.int32, vals.shape, 0).astype(F32)
        hits = []
        cur = vals
        for _ in range(k):
            mx = jnp.max(cur, axis=0, keepdims=True)
            ix = jnp.min(jnp.where(cur == mx, idx, float(n)), axis=0, keepdims=True)
            hit = idx == ix
            hits.append(hit)
            cur = jnp.where(hit, -jnp.inf, cur)
        return hits

    gsel = sum(jnp.where(h, 1.0, 0.0) for h in topk_hits(gs, TOPK_GROUPS))
    masked = jnp.where(gsel.reshape(N_GROUPS, 1, tn) > 0.0, g3, -jnp.inf).reshape(ne, tn)
    hits = topk_hits(masked, TOP_K)
    chosen = sum(jnp.where(h, 1.0, 0.0) for h in hits)
    denom = jnp.sum(chosen * sc, axis=0, keepdims=True)

    si = lax.broadcasted_iota(jnp.int32, (tn, tn), 0)
    ti = lax.broadcasted_iota(jnp.int32, (tn, tn), 1)
    before = _bf(jnp.where(si < ti, 1.0, 0.0))
    rank_full = _dot(_bf(chosen), before) + carry_ref[...]
    carry_ref[...] += jnp.sum(chosen, axis=1, keepdims=True)
    cnt_ref[...] = carry_ref[...]

    eidx = lax.broadcasted_iota(jnp.int32, (ne, tn), 0).astype(F32)
    for j, hit in enumerate(hits):
        idx_ref[j:j + 1, :] = jnp.sum(jnp.where(hit, eidx, 0.0), axis=0, keepdims=True).astype(jnp.int32)
        gate_ref[j:j + 1, :] = (jnp.sum(jnp.where(hit, sc, 0.0), axis=0, keepdims=True)
                                / denom * ROUTED_SCALE)
        rank_ref[j:j + 1, :] = jnp.sum(jnp.where(hit, rank_full, 0.0), axis=0,
                                       keepdims=True).astype(jnp.int32)


def _router(logits_t, bias_col):
    ne, n = logits_t.shape
    tn = 1024 if n % 1024 == 0 else n
    choice = lambda: pl.BlockSpec((TOP_K, tn), lambda i: (0, i))
    return pl.pallas_call(
        _router_kernel,
        out_shape=[jax.ShapeDtypeStruct((TOP_K, n), jnp.int32),
                   jax.ShapeDtypeStruct((TOP_K, n), F32),
                   jax.ShapeDtypeStruct((TOP_K, n), jnp.int32),
                   jax.ShapeDtypeStruct((ne, 1), F32)],
        grid=(n // tn,),
        in_specs=[pl.BlockSpec((ne, tn), lambda i: (0, i)),
                  pl.BlockSpec((ne, 1), lambda i: (0, 0))],
        out_specs=[choice(), choice(), choice(), pl.BlockSpec((ne, 1), lambda i: (0, 0))],
        scratch_shapes=[pltpu.VMEM((ne, 1), F32)],
        compiler_params=_params("arbitrary"),
        name="group_topk_router",
    )(logits_t, bias_col)


def _slots_kernel(idx_ref, rank_ref, ps_ref, o_ref):
    ne, tn = ps_ref.shape[0], idx_ref.shape[1]
    eidx = lax.broadcasted_iota(jnp.int32, (ne, tn), 0)
    for j in range(TOP_K):
        start = jnp.sum(jnp.where(eidx == idx_ref[j:j + 1, :], ps_ref[...], 0.0), axis=0, keepdims=True)
        o_ref[j:j + 1, :] = start.astype(jnp.int32) + rank_ref[j:j + 1, :]


def _slot_positions(idx8, rank8, pstart_col):
    _, n = idx8.shape
    ne = pstart_col.shape[0]
    tn = 1024 if n % 1024 == 0 else n
    choice = lambda: pl.BlockSpec((TOP_K, tn), lambda i: (0, i))
    return pl.pallas_call(
        _slots_kernel,
        out_shape=jax.ShapeDtypeStruct((TOP_K, n), jnp.int32),
        grid=(n // tn,),
        in_specs=[choice(), choice(), pl.BlockSpec((ne, 1), lambda i: (0, 0))],
        out_specs=choice(),
        compiler_params=_params("parallel"),
        name="expert_slot_positions",
    )(idx8, rank8, pstart_col)


def _dispatch_kernel(pad_start_ref, npad_ref, nused_ref, pos_ref, h_ref, xs_ref, zero_ref, sem,
                     *, n_experts):
    tt = h_ref.shape[0]
    bm = zero_ref.shape[0]

    def issue(t, carry):
        for k in range(TOP_K):
            pltpu.make_async_copy(h_ref.at[pl.ds(t, 1)], xs_ref.at[pl.ds(pos_ref[k, t], 1)], sem).start()
        return carry

    lax.fori_loop(0, tt, issue, 0)
    for _ in range(TOP_K):
        pltpu.make_async_copy(h_ref, xs_ref.at[pl.ds(0, tt)], sem).wait()

    @pl.when(pl.program_id(0) == pl.num_programs(0) - 1)
    def _():
        zero_ref[...] = jnp.zeros_like(zero_ref)

        zero_row_src = zero_ref.at[pl.ds(0, 1)]

        def per_expert(e, carry):
            def zero_row(i, c2):
                pltpu.make_async_copy(zero_row_src, xs_ref.at[pl.ds(pad_start_ref[e] + i, 1)], sem).start()
                return c2

            def zero_wait(i, c2):
                pltpu.make_async_copy(zero_row_src, xs_ref.at[pl.ds(0, 1)], sem).wait()
                return c2

            lax.fori_loop(0, npad_ref[e], zero_row, 0)
            lax.fori_loop(0, npad_ref[e], zero_wait, 0)
            return carry

        lax.fori_loop(0, n_experts, per_expert, 0)

        def zero_block(j, carry):
            blk = pltpu.make_async_copy(zero_ref, xs_ref.at[pl.ds(j * bm, bm)], sem)
            blk.start()
            blk.wait()
            return carry

        lax.fori_loop(nused_ref[0], xs_ref.shape[0] // bm, zero_block, 0)


def _dispatch(h2, pos8, pad_start, npad, n_used, n_slots, tt, bm):
    n, d = h2.shape
    ne = pad_start.shape[0]
    grid_spec = pltpu.PrefetchScalarGridSpec(
        num_scalar_prefetch=3,
        grid=(n // tt,),
        in_specs=[pl.BlockSpec((TOP_K, tt), lambda i, ps, npd, nu: (0, i), memory_space=pltpu.SMEM),
                  pl.BlockSpec((tt, d), lambda i, ps, npd, nu: (i, 0))],
        out_specs=pl.BlockSpec(memory_space=pl.ANY),
        scratch_shapes=[pltpu.VMEM((bm, d), F32), pltpu.SemaphoreType.DMA(())])
    return pl.pallas_call(
        functools.partial(_dispatch_kernel, n_experts=ne),
        out_shape=jax.ShapeDtypeStruct((n_slots, d), F32),
        grid_spec=grid_spec,
        compiler_params=_params("arbitrary"),
        name="expert_dispatch",
    )(pad_start, npad, n_used, pos8, h2)


def _expert_ffn_kernel(be_ref, nu_ref, x_ref, wg_ref, wu_ref, wd_ref, y_ref, wgb, wub, wdb):
    j = pl.program_id(0)
    used = j < nu_ref[0]
    fresh = jnp.logical_or(j == 0, be_ref[j] != be_ref[jnp.maximum(j - 1, 0)])

    @pl.when(jnp.logical_and(used, fresh))
    def _():
        wgb[...] = _bf(wg_ref[0])
        wub[...] = _bf(wu_ref[0])
        wdb[...] = _bf(wd_ref[0])

    @pl.when(used)
    def _():
        x = _bf(x_ref[...])
        a = _dot(x, wgb[...])
        u = _dot(x, wub[...])
        y_ref[...] = _dot(_bf(a * jax.nn.sigmoid(a) * u), wdb[...])

    @pl.when(jnp.logical_not(used))
    def _():
        y_ref[...] = jnp.zeros_like(y_ref)


def _expert_ffn(xs, blk_expert, n_used, e_gate, e_up, e_down, bm):
    n_slots, d = xs.shape
    ne, _, ff = e_gate.shape
    grid_spec = pltpu.PrefetchScalarGridSpec(
        num_scalar_prefetch=2,
        grid=(n_slots // bm,),
        in_specs=[pl.BlockSpec((bm, d), lambda j, be, nu: (jnp.minimum(j, nu[0] - 1), 0)),
                  pl.BlockSpec((1, d, ff), lambda j, be, nu: (be[j], 0, 0)),
                  pl.BlockSpec((1, d, ff), lambda j, be, nu: (be[j], 0, 0)),
                  pl.BlockSpec((1, ff, d), lambda j, be, nu: (be[j], 0, 0))],
        out_specs=pl.BlockSpec((bm, d), lambda j, be, nu: (j, 0)),
        scratch_shapes=[pltpu.VMEM((d, ff), BF16), pltpu.VMEM((d, ff), BF16), pltpu.VMEM((ff, d), BF16)])
    return pl.pallas_call(
        _expert_ffn_kernel,
        out_shape=jax.ShapeDtypeStruct((n_slots, d), F32),
        grid_spec=grid_spec,
        compiler_params=_params("arbitrary"),
        name="routed_expert_ffn",
    )(blk_expert, n_used, xs, e_gate, e_up, e_down)


def _final_kernel(pos_ref, h_ref, x1_ref, g_ref, ys_ref, wg_ref, wu_ref, wd_ref, m_ref, fg_ref,
                  o_ref, buf, sem):
    tr = h_ref.shape[1]

    def issue(t, carry):
        for k in range(TOP_K):
            pltpu.make_async_copy(ys_ref.at[pl.ds(pos_ref[k, t], 1)], buf.at[k, pl.ds(t, 1)], sem).start()
        return carry

    lax.fori_loop(0, tr, issue, 0)
    h = _bf(h_ref[0])
    a = _dot(h, wg_ref[...])
    u = _dot(h, wu_ref[...])
    moe = _dot(_bf(a * jax.nn.sigmoid(a) * u), wd_ref[...])
    g = g_ref[...]
    for k in range(TOP_K):
        pltpu.make_async_copy(ys_ref.at[pl.ds(0, tr)], buf.at[k], sem).wait()
    for k in range(TOP_K):
        moe = moe + g[:, k:k + 1] * buf[k]
    x2 = x1_ref[0] + m_ref[0, 3:4, :] * moe
    ms = jnp.mean(x2 * x2, axis=-1, keepdims=True)
    o_ref[0] = x2 * lax.rsqrt(ms + NORM_EPS) * fg_ref[...]


def _final(h2, x1, pos8, gate_rows, ys, s_gate_b, s_up_b, s_down_b, m2, final_g, tr):
    b, t, d = x1.shape
    ff = s_gate_b.shape[1]
    nt = t // tr
    row = lambda: pl.BlockSpec((1, tr, d), lambda bb, i: (bb, i, 0))
    return pl.pallas_call(
        _final_kernel,
        out_shape=jax.ShapeDtypeStruct((b, t, d), F32),
        grid=(b, nt),
        in_specs=[pl.BlockSpec((TOP_K, tr), lambda bb, i: (0, bb * nt + i), memory_space=pltpu.SMEM),
                  row(), row(),
                  pl.BlockSpec((tr, TOP_K), lambda bb, i: (bb * nt + i, 0)),
                  pl.BlockSpec(memory_space=pl.ANY),
                  pl.BlockSpec((d, ff), lambda bb, i: (0, 0)),
                  pl.BlockSpec((d, ff), lambda bb, i: (0, 0)),
                  pl.BlockSpec((ff, d), lambda bb, i: (0, 0)),
                  pl.BlockSpec((1, 4, d), lambda bb, i: (bb, 0, 0)),
                  pl.BlockSpec((1, d), lambda bb, i: (0, 0))],
        out_specs=row(),
        scratch_shapes=[pltpu.VMEM((TOP_K, tr, d), F32), pltpu.SemaphoreType.DMA(())],
        compiler_params=_params("parallel", "parallel"),
        name="combine_shared_expert_final_norm",
    )(pos8, h2, x1, gate_rows, ys, s_gate_b, s_up_b, s_down_b, m2, final_g)


def _pad_to(a, axis, n):
    pad = [(0, 0)] * a.ndim
    pad[axis] = (0, n - a.shape[axis])
    return jnp.pad(a, pad)


def _layout(rd, wl, al, gl, ad, kvd):
    assert wl <= LANES and al <= LANES
    lay = {"rd": rd, "gl": gl, "ad": ad, "kvd": kvd}
    off = 0
    for name, width in (("q", ad), ("r", rd), ("k", rd), ("v", rd), ("ka", kvd), ("va", kvd),
                        ("xw", LANES), ("xa", LANES), ("xg", gl)):
        assert width % LANES == 0
        lay[name] = off
        off += width
    lay["ncols"] = off
    return lay


def kernel(x, c, ctx, c_ctx, w_mod, b_mod, norm1_g, norm2_g, w_in, shift_mu, k_k, k_a, r_k, decay_bias, decay_up, iclr_bias, iclr_up, gate_up, lnx_g, lnx_b, attn_sink, w_out, w_router, expert_bias, e_gate, e_up, e_down, s_gate, s_up, s_down, final_g):
    assert w_mod.shape[0] == 1, "single-layer block only"
    b, t, d = x.shape
    ctx_len = ctx.shape[1]
    rh, hd = r_k.shape[1:]
    assert hd == HEAD_DIM and b + 1 <= SUBLANES
    rd = rh * hd
    wl, al, gl = decay_up.shape[2], iclr_up.shape[2], gate_up.shape[1]
    ad = attn_sink.shape[1] * hd
    kvd = (w_in.shape[2] - (3 * rd + wl + al + gl) - ad) // 2
    ne = w_router.shape[2]
    lay = _layout(rd, wl, al, gl, ad, kvd)
    tr = 256 if ctx_len % 256 == 0 and t % 256 == 0 else CHUNK
    assert ctx_len % CHUNK == 0 and t % CHUNK == 0 and t % QBLOCK == 0 and WINDOW == QBLOCK

    c8 = _pad_to(jnp.concatenate([c, c_ctx[None]], axis=0), 0, SUBLANES)
    mod = _modulation(c8, w_mod[0], b_mod[0][None])
    mod6 = mod.reshape(SUBLANES, 6, d)
    mx, mc = mod6[:b], mod6[b]
    m1 = jnp.stack([jnp.broadcast_to(mc[None, 0:2], (b, 2, d)), mx[:, 0:2]], axis=1)
    m2 = jnp.stack([mx[:, 2], mx[:, 3], mx[:, 4], mx[:, 5]], axis=1)

    wi = w_in[0]
    o_w, o_a, o_g, o_q = 3 * rd, 3 * rd + wl, 3 * rd + wl + al, 3 * rd + wl + al + gl
    w_in_p = _bf(jnp.concatenate([
        wi[:, o_q:o_q + ad], wi[:, :3 * rd], wi[:, o_q + ad:],
        _pad_to(wi[:, o_w:o_a], 1, LANES), _pad_to(wi[:, o_a:o_g], 1, LANES), wi[:, o_g:o_q]], axis=1))
    mu = shift_mu[0]
    prm = {
        "mu_r": mu[None, :rd], "mu_k": mu[None, rd:2 * rd], "mu_v": mu[None, 2 * rd:3 * rd],
        "mu_w": _pad_to(mu[None, o_w:o_a], 1, LANES), "mu_a": _pad_to(mu[None, o_a:o_g], 1, LANES),
        "mu_g": mu[None, o_g:o_q],
        "k_k": k_k, "k_a": k_a, "r_k": r_k.reshape(1, rd),
        "decay_bias": decay_bias[0], "decay_up": _bf(_pad_to(decay_up[0], 1, LANES)),
        "iclr_bias": iclr_bias[0], "iclr_up": _bf(_pad_to(iclr_up[0], 1, LANES)),
        "gate_up": _bf(gate_up[0]),
    }

    p_all = _in_projection(ctx, x, norm1_g, m1, w_in_p, tr)

    npair = rd // LANES
    pp = max(n for n in (1, 2, 4) if npair % n == 0
             and all(lay[key] % (n * LANES) == 0 for key in ("r", "k", "v")))
    rp, y0, mm, nn, bonus, gate = _wkv_prep(p_all, lay, prm, ctx_len, None, pp)
    y = _wkv_sweep(rp, y0, mm, nn, ctx_len)

    nf = HEAD_DIM // 4
    freqs = ROPE_BASE ** (-jnp.arange(nf, dtype=F32) / nf)
    pos = jnp.arange(t, dtype=jnp.int32)
    ang_r = (pos // GRID_W).astype(F32)[:, None] * freqs
    ang_c = (pos % GRID_W).astype(F32)[:, None] * freqs
    cos_h = jnp.concatenate([jnp.cos(ang_r)] * 2 + [jnp.cos(ang_c)] * 2, axis=1)
    sin_h = jnp.concatenate([-jnp.sin(ang_r), jnp.sin(ang_r), -jnp.sin(ang_c), jnp.sin(ang_c)], axis=1)
    cos_t = jnp.concatenate([cos_h, cos_h], axis=1)
    sin_t = jnp.concatenate([sin_h, sin_h], axis=1)
    attn = _attention(p_all, lay, cos_t, sin_t, attn_sink, ctx_len, t)

    x1, h2, logits_t = _out_projection(y, bonus, gate, attn, x, lnx_g, lnx_b, _bf(w_out[0]), m2,
                                       norm2_g, w_router[0].T, ctx_len, tr)
    idx8, gate8, rank8, counts = _router(logits_t, expert_bias[0][:, None])

    n_tok = b * t
    bm = EXPERT_ROWS
    assert (n_tok * TOP_K) % bm == 0
    n_blk = n_tok * TOP_K // bm + ne
    cnt = counts[:, 0].astype(jnp.int32)
    padded = (cnt + bm - 1) // bm * bm
    pend = jnp.cumsum(padded)
    pstart = pend - padded
    pos8 = _slot_positions(idx8, rank8, pstart.astype(F32)[:, None])
    blk_row0 = jnp.arange(n_blk, dtype=jnp.int32) * bm
    blk_expert = jnp.minimum(jnp.sum((pend[None, :] <= blk_row0[:, None]).astype(jnp.int32), axis=1), ne - 1)
    n_used = (pend[-1:] // bm).astype(jnp.int32)

    xs = _dispatch(h2.reshape(n_tok, d), pos8, pstart + cnt, padded - cnt, n_used, n_blk * bm, tr, bm)
    ys = _expert_ffn(xs, blk_expert, n_used, e_gate[0], e_up[0], e_down[0], bm)
    return _final(h2, x1, pos8, gate8.T, ys, _bf(s_gate[0]), _bf(s_up[0]), _bf(s_down[0]),
                  m2, final_g[None], tr)
```

```python
import functools
import math

import jax
import jax.numpy as jnp
from jax import lax
from jax.experimental import pallas as pl
from jax.experimental.pallas import tpu as pltpu

GRID_W = 64
WINDOW = 128
ROPE_BASE = 10000.0
N_GROUPS = 8
TOPK_GROUPS = 4
TOP_K = 8
ROUTED_SCALE = 2.5
NORM_EPS = 1e-6
LNX_EPS = 64e-5
KK_EPS = 1e-12

LANES = 128
SUBLANES = 8
VMEM_LIMIT_BYTES = 56 * 1024 * 1024

HEAD_DIM = 64
CHUNK = 64
QBLOCK = 128
EXPERT_ROWS = 256

F32 = jnp.float32
BF16 = jnp.bfloat16
NEG_BIG = -1e30


def _bf(x):
    return x.astype(BF16)


def _dot(a, b, precision=None):
    return jnp.dot(a, b, preferred_element_type=F32, precision=precision)


def _dot_nt(a, b, precision=None):
    return lax.dot_general(a, b, (((1,), (1,)), ((), ())), preferred_element_type=F32,
                           precision=precision)


def _dot_tn(a, b):
    return lax.dot_general(a, b, (((0,), (0,)), ((), ())), preferred_element_type=F32)


def _params(*sem):
    return pltpu.CompilerParams(dimension_semantics=sem, vmem_limit_bytes=VMEM_LIMIT_BYTES)


def _mod_kernel(c_ref, w_ref, b_ref, o_ref):
    c = c_ref[...]
    s = c * jax.nn.sigmoid(c)
    hi = _bf(s)
    lo = _bf(s - hi.astype(F32))
    r = _dot(jnp.concatenate([hi, lo], axis=0), _bf(w_ref[...]))
    o_ref[...] = r[:SUBLANES] + r[SUBLANES:] + b_ref[...]


def _modulation(c8, w_mod, b_mod):
    d, n = w_mod.shape
    tn = 1024 if n % 1024 == 0 else n
    return pl.pallas_call(
        _mod_kernel,
        out_shape=jax.ShapeDtypeStruct((SUBLANES, n), F32),
        grid=(n // tn,),
        in_specs=[pl.BlockSpec((SUBLANES, d), lambda j: (0, 0)),
                  pl.BlockSpec((d, tn), lambda j: (0, j)),
                  pl.BlockSpec((1, tn), lambda j: (0, j))],
        out_specs=pl.BlockSpec((SUBLANES, tn), lambda j: (0, j)),
        compiler_params=_params("parallel"),
        name="adaln_modulation",
    )(c8, w_mod, b_mod)


def _inproj_kernel(c_ref, x_ref, g_ref, m_ref, w_ref, o_ref, *, nct):
    x = jnp.where(pl.program_id(2) < nct, c_ref[0], x_ref[0])
    ms = jnp.mean(x * x, axis=-1, keepdims=True)
    y = x * lax.rsqrt(ms + NORM_EPS) * g_ref[...]
    m = m_ref[0, 0]
    h = y * (1.0 + m[1:2]) + m[0:1]
    o_ref[0] = _dot(_bf(h), w_ref[...])


def _in_projection(ctx, x, g1, m1, w_in_p, tr):
    b, t, d = x.shape
    ctx_len = ctx.shape[1]
    ta = ctx_len + t
    ncols = w_in_p.shape[1]
    tn = ncols // 4 if (ncols // 4) % LANES == 0 and ncols % 4 == 0 else ncols
    nct = ctx_len // tr
    return pl.pallas_call(
        functools.partial(_inproj_kernel, nct=nct),
        out_shape=jax.ShapeDtypeStruct((b, ta, ncols), F32),
        grid=(ncols // tn, b, ta // tr),
        in_specs=[pl.BlockSpec((1, tr, d), lambda n, bb, i: (bb, jnp.minimum(i, nct - 1), 0)),
                  pl.BlockSpec((1, tr, d), lambda n, bb, i: (bb, jnp.maximum(i - nct, 0), 0)),
                  pl.BlockSpec((1, d), lambda n, bb, i: (0, 0)),
                  pl.BlockSpec((1, 1, 2, d), lambda n, bb, i: (bb, jnp.where(i >= nct, 1, 0), 0, 0)),
                  pl.BlockSpec((d, tn), lambda n, bb, i: (0, n))],
        out_specs=pl.BlockSpec((1, tr, tn), lambda n, bb, i: (bb, i, n)),
        compiler_params=_params("parallel", "parallel", "parallel"),
        name="norm_modulate_inproj",
    )(ctx, x, g1, m1, w_in_p)


def _wkv_prep_kernel(r_c, r_p, r_n, k_c, k_p, k_n, v_c, v_p, v_n, w_c, w_p, w_n, a_c, a_p, a_n,
                     g_c, g_p, g_n, mu_r, mu_k, mu_v, mu_w, mu_a, mu_g, kk_ref, ka_ref, rk_ref,
                     db_ref, du_ref, ib_ref, iu_ref, gu_ref,
                     rp_out, y0_out, m_out, n_out, bonus_out, gate_out,
                     *, nctx_chunks, n_chunks, prec_t):
    L = CHUNK
    c = pl.program_id(1)
    has_prev = jnp.logical_and(c != 0, c != nctx_chunks)
    has_next = jnp.logical_and(c != nctx_chunks - 1, c != n_chunks - 1)

    def shifted(cur, prv, nxt, mu):
        x = cur[0]
        rows = lax.broadcasted_iota(jnp.int32, (L, 1), 0)
        pr = jnp.where(has_prev, prv[0, SUBLANES - 1:SUBLANES, :], 0.0)
        nx = jnp.where(has_next, nxt[0, 0:1, :], 0.0)
        up = jnp.where(rows == 0, pr, pltpu.roll(x, 1, 0))
        dn = jnp.where(rows == L - 1, nx, pltpu.roll(x, L - 1, 0))
        return x + mu[...] * (0.5 * (up + dn) - x)

    r = shifted(r_c, r_p, r_n, mu_r)
    k = shifted(k_c, k_p, k_n, mu_k)
    v = shifted(v_c, v_p, v_n, mu_v)
    xw = shifted(w_c, w_p, w_n, mu_w)
    xa = shifted(a_c, a_p, a_n, mu_a)
    xg = shifted(g_c, g_p, g_n, mu_g)

    lane = lax.broadcasted_iota(jnp.int32, (1, LANES), 1)
    m0 = lane < HEAD_DIM

    def head_sum(x):
        s0 = jnp.sum(jnp.where(m0, x, 0.0), axis=-1, keepdims=True)
        s1 = jnp.sum(jnp.where(m0, 0.0, x), axis=-1, keepdims=True)
        return jnp.where(m0, s0, s1)

    def stack2(x):
        return jnp.concatenate([jnp.where(m0, x, 0.0), jnp.where(m0, 0.0, x)], axis=0)

    tw = _bf(jnp.tanh(xw))
    xab = _bf(xa)
    lw_all = [(-math.exp(-0.5)) * jax.nn.sigmoid(db_ref[d:d + 1, :] + _dot(tw, du_ref[d]))
              for d in range(2)]
    sa_all = [jax.nn.sigmoid(ib_ref[d:d + 1, :] + _dot(xab, iu_ref[d])) for d in range(2)]
    gate_out[0] = _dot(_bf(jax.nn.sigmoid(xg)), gu_ref[...])

    ri = lax.broadcasted_iota(jnp.int32, (L, L), 0)
    ci = lax.broadcasted_iota(jnp.int32, (L, L), 1)
    ri2 = lax.broadcasted_iota(jnp.int32, (2 * L, 2 * L), 0)
    ci2 = lax.broadcasted_iota(jnp.int32, (2 * L, 2 * L), 1)
    rt = jnp.bitwise_and(ri2, L - 1)
    ct = jnp.bitwise_and(ci2, L - 1)
    eye = ri2 == ci2
    n_sq = int(math.log2(L)) - 1

    tri = [(ci <= ri).astype(F32), (ci >= ri).astype(F32)]
    strict = [ct < rt, ct > rt]
    incl = [ct <= rt, ct >= rt]
    cum = [_dot(tri[d], lw_all[d], precision=lax.Precision.HIGHEST) for d in range(2)]
    cend = [jnp.sum(lw_all[d], axis=0, keepdims=True) for d in range(2)]
    e_in = [jnp.exp(cum[d]) for d in range(2)]
    e_ex = [jnp.exp(cum[d] - lw_all[d]) for d in range(2)]
    e_neg = [jnp.exp(-cum[d]) for d in range(2)]
    e_end = [jnp.exp(cend[d] - cum[d]) for d in range(2)]
    p_end = [jnp.exp(cend[d]) for d in range(2)]
    kd_all = [k * (1.0 + (sa_all[d] - 1.0) * ka_ref[...]) for d in range(2)]

    npp = r.shape[1] // LANES
    chains = [(p, d) for p in range(npp) for d in range(2)]
    sls = [slice(p * LANES, (p + 1) * LANES) for p in range(npp)]
    kk_p, vs_p, vsb_p = [], [], []
    for p in range(npp):
        kx = k[:, sls[p]] * kk_ref[:, sls[p]]
        kk_p.append(kx * lax.rsqrt(head_sum(kx * kx) + KK_EPS))
        vs_p.append(stack2(v[:, sls[p]]))
        vsb_p.append(_bf(vs_p[p]))
        ksum = kd_all[0][:, sls[p]] + kd_all[1][:, sls[p]]
        bonus_out[0, :, sls[p]] = head_sum(r[:, sls[p]] * ksum * rk_ref[:, sls[p]]) * v[:, sls[p]]

    a_s, r_s, bh_s, kh_s, g = [], [], [], [], []
    for p, d in chains:
        sl = sls[p]
        kd = kd_all[d][:, sl]
        bvec = kk_p[p] * sa_all[d][:, sl]
        a_s.append(stack2(-kk_p[p] * e_ex[d][:, sl]))
        r_s.append(stack2(r[:, sl] * e_in[d][:, sl]))
        b_s = stack2(bvec * e_neg[d][:, sl])
        k_s = stack2(kd * e_neg[d][:, sl])
        bh_s.append(stack2(bvec * e_end[d][:, sl]))
        kh_s.append(stack2(kd * e_end[d][:, sl]))
        g.append(_dot_nt(_bf(jnp.concatenate([a_s[-1], r_s[-1]], axis=0)),
                         _bf(jnp.concatenate([b_s, k_s], axis=0))))
    a_ab = [jnp.where(strict[d], g[i][:2 * L, :2 * L], 0.0) for i, (p, d) in enumerate(chains)]
    a_ak = [jnp.where(strict[d], g[i][:2 * L, 2 * L:], 0.0) for i, (p, d) in enumerate(chains)]
    a_rb = [jnp.where(incl[d], g[i][2 * L:, :2 * L], 0.0) for i, (p, d) in enumerate(chains)]
    a_rk = [jnp.where(incl[d], g[i][2 * L:, 2 * L:], 0.0) for i, (p, d) in enumerate(chains)]

    ident = jnp.where(eye, 1.0, 0.0)
    tmat = [ident + a for a in a_ab]
    apow = a_ab
    for _ in range(n_sq):
        if prec_t is None:
            apow = [_dot(_bf(a), _bf(a)) for a in apow]
            tmat = [t + _dot(_bf(a), _bf(t)) for a, t in zip(apow, tmat)]
        else:
            apow = [_dot(a, a, precision=prec_t) for a in apow]
            tmat = [t + _dot(a, t, precision=prec_t) for a, t in zip(apow, tmat)]

    x1 = [_dot(_bf(a_ak[i]), vsb_p[p]) for i, (p, d) in enumerate(chains)]
    y0v = [_dot(_bf(a_rk[i]), vsb_p[p]) for i, (p, d) in enumerate(chains)]
    tu = [_dot(_bf(tmat[i]), _bf(jnp.concatenate([a_s[i], x1[i]], axis=1)))
          for i in range(len(chains))]
    ry = [_dot(_bf(a_rb[i]), _bf(tu[i])) for i in range(len(chains))]
    def fold(z):
        h = z.shape[0] // 2
        return z[:h] + z[h:]

    for i, (p, d) in enumerate(chains):
        rp_out[d, 0, 0, p] = _bf(fold(r_s[i] + ry[i][:, :LANES]))
        y0_out[d, 0, 0, p] = fold(ry[i][:, LANES:] + y0v[i])
    for i, (p, d) in enumerate(chains):
        m_out[d, 0, 0, p] = _bf(fold(jnp.where(eye, p_end[d][:, sls[p]], 0.0)
                                     + _dot_tn(_bf(tu[i][:, :LANES]), _bf(bh_s[i]))))
    for i, (p, d) in enumerate(chains):
        n_out[d, 0, 0, p] = fold(_dot_tn(_bf(jnp.concatenate([tu[i][:, LANES:], vs_p[p]], axis=0)),
                                         _bf(jnp.concatenate([bh_s[i], kh_s[i]], axis=0))))


def _wkv_prep(p_all, lay, prm, ctx_len, prec_t, pp):
    b, ta, _ = p_all.shape
    L = CHUNK
    nc = ta // L
    nctx = ctx_len // L
    npair = lay["rd"] // LANES
    gl = lay["gl"]
    rows8 = ta // SUBLANES

    def cur(colfn, width=LANES):
        return pl.BlockSpec((1, L, width), lambda bb, c, p: (bb, c, colfn(p)))

    def prv(colfn, width=LANES):
        return pl.BlockSpec((1, SUBLANES, width),
                            lambda bb, c, p: (bb, jnp.maximum(c * (L // SUBLANES) - 1, 0), colfn(p)))

    def nxt(colfn, width=LANES):
        return pl.BlockSpec((1, SUBLANES, width),
                            lambda bb, c, p: (bb, jnp.minimum((c + 1) * (L // SUBLANES), rows8 - 1), colfn(p)))

    def trio(colfn, width=LANES):
        return [cur(colfn, width), prv(colfn, width), nxt(colfn, width)]

    pw = pp * LANES
    assert npair % pp == 0 and all(lay[n] % pw == 0 for n in ("r", "k", "v"))
    rb, kb, vb = lay["r"] // pw, lay["k"] // pw, lay["v"] // pw
    wb, ab = lay["xw"] // LANES, lay["xa"] // LANES
    assert lay["xg"] % gl == 0
    gb = lay["xg"] // gl

    def vec(width=pw):
        return pl.BlockSpec((1, width), lambda bb, c, p: (0, p))

    def vec0(width=LANES):
        return pl.BlockSpec((1, width), lambda bb, c, p: (0, 0))

    in_specs = (trio(lambda p: rb + p, pw) + trio(lambda p: kb + p, pw) + trio(lambda p: vb + p, pw)
                + trio(lambda p: wb) + trio(lambda p: ab) + trio(lambda p: gb, gl)
                + [vec(), vec(), vec(), vec0(), vec0(), vec0(gl),
                   vec(), vec(), vec(),
                   pl.BlockSpec((2, pw), lambda bb, c, p: (0, p)),
                   pl.BlockSpec((2, LANES, pw), lambda bb, c, p: (0, 0, p)),
                   pl.BlockSpec((2, pw), lambda bb, c, p: (0, p)),
                   pl.BlockSpec((2, LANES, pw), lambda bb, c, p: (0, 0, p)),
                   pl.BlockSpec((gl, pw), lambda bb, c, p: (0, p))])

    def opspec(rows):
        return pl.BlockSpec((2, 1, 1, pp, rows, LANES), lambda bb, c, p: (0, c, bb, p, 0, 0))

    out_specs = [opspec(L), opspec(L), opspec(HEAD_DIM), opspec(HEAD_DIM),
                 pl.BlockSpec((1, L, pw), lambda bb, c, p: (bb, c, p)),
                 pl.BlockSpec((1, L, pw), lambda bb, c, p: (bb, c, p))]
    out_shape = [jax.ShapeDtypeStruct((2, nc, b, npair, L, LANES), BF16),
                 jax.ShapeDtypeStruct((2, nc, b, npair, L, LANES), F32),
                 jax.ShapeDtypeStruct((2, nc, b, npair, HEAD_DIM, LANES), BF16),
                 jax.ShapeDtypeStruct((2, nc, b, npair, HEAD_DIM, LANES), F32),
                 jax.ShapeDtypeStruct((b, ta, lay["rd"]), F32),
                 jax.ShapeDtypeStruct((b, ta, lay["rd"]), F32)]
    kern = functools.partial(_wkv_prep_kernel, nctx_chunks=nctx, n_chunks=nc, prec_t=prec_t)
    args = [p_all] * 18 + [prm["mu_r"], prm["mu_k"], prm["mu_v"], prm["mu_w"], prm["mu_a"], prm["mu_g"],
                           prm["k_k"], prm["k_a"], prm["r_k"], prm["decay_bias"], prm["decay_up"],
                           prm["iclr_bias"], prm["iclr_up"], prm["gate_up"]]
    return pl.pallas_call(
        kern, out_shape=out_shape, grid=(b, nc, npair // pp), in_specs=in_specs, out_specs=out_specs,
        compiler_params=_params("parallel", "parallel", "parallel"),
        name="wkv_chunk_operators",
    )(*args)


def _wkv_seq_kernel(rp_ref, y0_ref, m_ref, n_ref, y_ref, s_ref, *, nb, npair):
    L = CHUNK

    @pl.when(pl.program_id(1) == 0)
    def _():
        s_ref[...] = jnp.zeros_like(s_ref)

    m0 = lax.broadcasted_iota(jnp.int32, (1, LANES), 1) < HEAD_DIM

    def unfold(z):
        zero = jnp.zeros_like(z)
        return jnp.concatenate([jnp.where(m0, z, zero), jnp.where(m0, zero, z)], axis=0)

    for bb in range(nb):
        for p in range(npair):
            i = bb * npair + p
            s0 = _bf(s_ref[i])
            ys = _dot_nt(unfold(rp_ref[0, 0, bb, p]), s0)
            y_ref[0, bb, :, p * LANES:(p + 1) * LANES] = ys[:L] + ys[L:] + y0_ref[0, 0, bb, p]
            s_ref[i] = _dot(s0, unfold(m_ref[0, 0, bb, p])) + unfold(n_ref[0, 0, bb, p])


def _wkv_sweep(rp, y0, mm, nn, ctx_len):
    _, nc, b, npair, _, _ = rp.shape
    L = CHUNK
    nctx = ctx_len // L

    def chunk_of(d, c):
        rev = jnp.where(c < nctx, nctx - 1 - c, nc - 1 - (c - nctx))
        return jnp.where(d == 0, c, rev)

    def opspec(rows):
        return pl.BlockSpec((1, 1, b, npair, rows, LANES), lambda d, c: (d, chunk_of(d, c), 0, 0, 0, 0))

    return pl.pallas_call(
        functools.partial(_wkv_seq_kernel, nb=b, npair=npair),
        out_shape=jax.ShapeDtypeStruct((2, b, nc * L, npair * LANES), F32),
        grid=(2, nc),
        in_specs=[opspec(L), opspec(L), opspec(HEAD_DIM), opspec(HEAD_DIM)],
        out_specs=pl.BlockSpec((1, b, L, npair * LANES), lambda d, c: (d, 0, chunk_of(d, c), 0)),
        scratch_shapes=[pltpu.VMEM((b * npair, LANES, LANES), F32)],
        compiler_params=_params("parallel", "arbitrary"),
        name="wkv_state_sweep",
    )(rp, y0, mm, nn)


def _attn_kernel(q_ref, kp_ref, kc_ref, kn_ref, vp_ref, vc_ref, vn_ref, kx_ref, vx_ref,
                 cq_ref, sq_ref, cp_ref, sp_ref, cn_ref, sn_ref, sink_ref, o_ref,
                 *, n_blocks, kv_heads, group):
    n = pl.program_id(1)
    qb = QBLOCK

    def rope(x, cos, sin):
        w = x.shape[1]
        reps = w // LANES
        cs = jnp.concatenate([cos] * reps, axis=1) if reps > 1 else cos
        sn = jnp.concatenate([sin] * reps, axis=1) if reps > 1 else sin
        lane = lax.broadcasted_iota(jnp.int32, (1, w), 1)
        sw = jnp.where(jnp.bitwise_and(lane, 16) == 0, pltpu.roll(x, w - 16, 1), pltpu.roll(x, 16, 1))
        return x * cs + sw * sn

    q = rope(q_ref[0], cq_ref[...], sq_ref[...]) * (HEAD_DIM ** -0.5)
    kwin = jnp.concatenate([rope(kp_ref[0], cp_ref[...], sp_ref[...]),
                            rope(kc_ref[0], cq_ref[...], sq_ref[...]),
                            rope(kn_ref[0], cn_ref[...], sn_ref[...]),
                            kx_ref[0]], axis=0)
    vwin = jnp.concatenate([vp_ref[0], vc_ref[0], vn_ref[0], vx_ref[0]], axis=0)
    nkeys = kwin.shape[0]

    rows = group * qb
    qi = jnp.bitwise_and(lax.broadcasted_iota(jnp.int32, (rows, nkeys), 0), qb - 1)
    s_i = lax.broadcasted_iota(jnp.int32, (rows, nkeys), 1)
    blk = s_i // qb + (n - 1)
    in_win = (s_i >= qi) & (s_i <= qi + 2 * WINDOW) & (blk >= 0) & (blk < n_blocks)
    valid = in_win | (s_i >= 3 * qb)
    rgrp = lax.broadcasted_iota(jnp.int32, (rows, 1), 0) // qb

    scores, sinks = [], []
    for kh in range(kv_heads):
        kk_ = _bf(kwin[:, kh * HEAD_DIM:(kh + 1) * HEAD_DIM])
        qs = jnp.concatenate([q[:, (kh * group + g) * HEAD_DIM:(kh * group + g + 1) * HEAD_DIM]
                              for g in range(group)], axis=0)
        sk = jnp.zeros((rows, 1), F32)
        for g in range(group):
            h = kh * group + g
            sk = jnp.where(rgrp == g, sink_ref[0:1, h:h + 1], sk)
        sinks.append(sk)
        scores.append(jnp.where(valid, _dot_nt(_bf(qs), kk_), NEG_BIG))
    probs, denoms = [], []
    for s, sk in zip(scores, sinks):
        m = jnp.maximum(jnp.max(s, axis=-1, keepdims=True), sk)
        e = jnp.exp(s - m)
        denoms.append(jnp.sum(e, axis=-1, keepdims=True) + jnp.exp(sk - m))
        probs.append(_bf(e))
    pieces = []
    for kh in range(kv_heads):
        vv_ = _bf(vwin[:, kh * HEAD_DIM:(kh + 1) * HEAD_DIM])
        o = _dot(probs[kh], vv_) / denoms[kh]
        for g in range(group):
            pieces.append(o[g * qb:(g + 1) * qb])
    o_ref[0] = _bf(jnp.concatenate(pieces, axis=1))


def _attention(p_all, lay, cos_t, sin_t, sink, ctx_len, seq):
    b = p_all.shape[0]
    qb = QBLOCK
    nb = seq // qb
    cb = ctx_len // qb
    ad, kvd = lay["ad"], lay["kvd"]
    kv_heads = kvd // HEAD_DIM
    group = (ad // HEAD_DIM) // kv_heads
    assert lay["q"] % ad == 0 and lay["ka"] % kvd == 0 and lay["va"] % kvd == 0
    qc, kc, vc = lay["q"] // ad, lay["ka"] // kvd, lay["va"] // kvd

    def clampb(n, off):
        return jnp.clip(n + off, 0, nb - 1)

    def win(col, off):
        return pl.BlockSpec((1, qb, kvd), lambda bb, n: (bb, cb + clampb(n, off), col))

    def tab(off):
        return pl.BlockSpec((qb, LANES), lambda bb, n: (clampb(n, off), 0))

    assert ctx_len % qb == 0
    ctxspec = lambda col: pl.BlockSpec((1, ctx_len, kvd), lambda bb, n: (bb, 0, col))
    in_specs = [pl.BlockSpec((1, qb, ad), lambda bb, n: (bb, cb + n, qc)),
                win(kc, -1), win(kc, 0), win(kc, 1), win(vc, -1), win(vc, 0), win(vc, 1),
                ctxspec(kc), ctxspec(vc),
                tab(0), tab(0), tab(-1), tab(-1), tab(1), tab(1),
                pl.BlockSpec((1, sink.shape[1]), lambda bb, n: (0, 0))]
    return pl.pallas_call(
        functools.partial(_attn_kernel, n_blocks=nb, kv_heads=kv_heads, group=group),
        out_shape=jax.ShapeDtypeStruct((b, seq, ad), BF16),
        grid=(b, nb),
        in_specs=in_specs,
        out_specs=pl.BlockSpec((1, qb, ad), lambda bb, n: (bb, n, 0)),
        compiler_params=_params("parallel", "parallel"),
        name="window_ctx_attention",
    )(p_all, p_all, p_all, p_all, p_all, p_all, p_all, p_all, p_all,
      cos_t, sin_t, cos_t, sin_t, cos_t, sin_t, sink)


def _outproj_kernel(yf_ref, yr_ref, bon_ref, gat_ref, att_ref, x_ref, lg_ref, lb_ref, w_ref,
                    m_ref, g2_ref, wr_ref, x1_ref, h2_ref, lt_ref):
    ysum = yf_ref[0, 0] + yr_ref[0, 0]
    ri = lax.broadcasted_iota(jnp.int32, (LANES, LANES), 0) // HEAD_DIM
    ci = lax.broadcasted_iota(jnp.int32, (LANES, LANES), 1) // HEAD_DIM
    hm = _bf(jnp.where(ri == ci, 1.0 / HEAD_DIM, 0.0))

    def head_mean(z):
        hi = _bf(z)
        lo = _bf(z - hi.astype(F32))
        return _dot(hi, hm) + _dot(lo, hm)

    ybs = [ysum[:, j * LANES:(j + 1) * LANES] for j in range(ysum.shape[1] // LANES)]
    devs = [yb - mu for yb, mu in zip(ybs, [head_mean(yb) for yb in ybs])]
    blocks = [dev * lax.rsqrt(var + LNX_EPS)
              for dev, var in zip(devs, [head_mean(dev * dev) for dev in devs])]
    yn = jnp.concatenate(blocks, axis=1) if len(blocks) > 1 else blocks[0]
    rw = (yn * lg_ref[...] + lb_ref[...] + bon_ref[0]) * gat_ref[0]
    mix = jnp.concatenate([_bf(rw), att_ref[0]], axis=1)
    m = m_ref[0]
    x1 = x_ref[0] + m[0:1] * _dot(mix, w_ref[...])
    x1_ref[0] = x1
    ms = jnp.mean(x1 * x1, axis=-1, keepdims=True)
    h2 = x1 * lax.rsqrt(ms + NORM_EPS) * g2_ref[...] * (1.0 + m[2:3]) + m[1:2]
    h2_ref[0] = h2
    hi = _bf(h2)
    lo = _bf(h2 - hi.astype(F32))
    wr = wr_ref[...]
    wr_hi = _bf(wr)
    wr_lo = _bf(wr - wr_hi.astype(F32))
    lg = _dot(hi, wr_hi) + (_dot(hi, wr_lo) + _dot(lo, wr_hi))
    lt_ref[...] = lg.T[:lt_ref.shape[0]]


def _out_projection(y, bonus, gate, attn, x, lnx_g, lnx_b, w_out_b, m2, g2, w_router, ctx_len, tr):
    b, t, d = x.shape
    rd = y.shape[3]
    ad = attn.shape[2]
    ne = w_router.shape[1]
    assert ne <= LANES
    wr = _pad_to(w_router, 1, LANES)
    co = ctx_len // tr
    nt = t // tr
    return pl.pallas_call(
        _outproj_kernel,
        out_shape=[jax.ShapeDtypeStruct((b, t, d), F32),
                   jax.ShapeDtypeStruct((b, t, d), F32),
                   jax.ShapeDtypeStruct((ne, b * t), F32)],
        grid=(b, nt),
        in_specs=[pl.BlockSpec((1, 1, tr, rd), lambda bb, i: (0, bb, co + i, 0)),
                  pl.BlockSpec((1, 1, tr, rd), lambda bb, i: (1, bb, co + i, 0)),
                  pl.BlockSpec((1, tr, rd), lambda bb, i: (bb, co + i, 0)),
                  pl.BlockSpec((1, tr, rd), lambda bb, i: (bb, co + i, 0)),
                  pl.BlockSpec((1, tr, ad), lambda bb, i: (bb, i, 0)),
                  pl.BlockSpec((1, tr, d), lambda bb, i: (bb, i, 0)),
                  pl.BlockSpec((1, rd), lambda bb, i: (0, 0)),
                  pl.BlockSpec((1, rd), lambda bb, i: (0, 0)),
                  pl.BlockSpec((rd + ad, d), lambda bb, i: (0, 0)),
                  pl.BlockSpec((1, 4, d), lambda bb, i: (bb, 0, 0)),
                  pl.BlockSpec((1, d), lambda bb, i: (0, 0)),
                  pl.BlockSpec((d, LANES), lambda bb, i: (0, 0))],
        out_specs=[pl.BlockSpec((1, tr, d), lambda bb, i: (bb, i, 0)),
                   pl.BlockSpec((1, tr, d), lambda bb, i: (bb, i, 0)),
                   pl.BlockSpec((ne, tr), lambda bb, i: (0, bb * nt + i))],
        compiler_params=_params("parallel", "parallel"),
        name="rwkv_out_outproj_norm2_router",
    )(y, y, bonus, gate, attn, x, lnx_g, lnx_b, w_out_b, m2, g2, wr)


def _router_kernel(l_ref, b_ref, idx_ref, gate_ref, rank_ref, cnt_ref, carry_ref):
    @pl.when(pl.program_id(0) == 0)
    def _():
        carry_ref[...] = jnp.zeros_like(carry_ref)

    sc = jax.nn.sigmoid(l_ref[...])
    ne, tn = sc.shape
    eg = ne // N_GROUPS
    bz = sc + b_ref[...]
    g3 = bz.reshape(N_GROUPS, eg, tn)
    io = lax.broadcasted_iota(jnp.int32, (N_GROUPS, eg, tn), 1).astype(F32)
    m1 = jnp.max(g3, axis=1, keepdims=True)
    i1 = jnp.min(jnp.where(g3 == m1, io, float(eg)), axis=1, keepdims=True)
    m2 = jnp.max(jnp.where(io == i1, -jnp.inf, g3), axis=1, keepdims=True)
    gs = (m1 + m2).reshape(N_GROUPS, tn)

    def topk_hits(vals, k):
        n = vals.shape[0]
        idx = lax.broadcasted_iota(jnp.int32, vals.shape, 0).astype(F32)
        hits = []
        cur = vals
        for _ in range(k):
            mx = jnp.max(cur, axis=0, keepdims=True)
            ix = jnp.min(jnp.where(cur == mx, idx, float(n)), axis=0, keepdims=True)
            hit = idx == ix
            hits.append(hit)
            cur = jnp.where(hit, -jnp.inf, cur)
        return hits

    gsel = sum(jnp.where(h, 1.0, 0.0) for h in topk_hits(gs, TOPK_GROUPS))
    masked = jnp.where(gsel.reshape(N_GROUPS, 1, tn) > 0.0, g3, -jnp.inf).reshape(ne, tn)
    hits = topk_hits(masked, TOP_K)
    chosen = sum(jnp.where(h, 1.0, 0.0) for h in hits)
    denom = jnp.sum(chosen * sc, axis=0, keepdims=True)

    si = lax.broadcasted_iota(jnp.int32, (tn, tn), 0)
    ti = lax.broadcasted_iota(jnp.int32, (tn, tn), 1)
    before = _bf(jnp.where(si < ti, 1.0, 0.0))
    rank_full = _dot(_bf(chosen), before) + carry_ref[...]
    carry_ref[...] += jnp.sum(chosen, axis=1, keepdims=True)
    cnt_ref[...] = carry_ref[...]

    eidx = lax.broadcasted_iota(jnp.int32, (ne, tn), 0).astype(F32)
    for j, hit in enumerate(hits):
        idx_ref[j:j + 1, :] = jnp.sum(jnp.where(hit, eidx, 0.0), axis=0, keepdims=True).astype(jnp.int32)
        gate_ref[j:j + 1, :] = (jnp.sum(jnp.where(hit, sc, 0.0), axis=0, keepdims=True)
                                / denom * ROUTED_SCALE)
        rank_ref[j:j + 1, :] = jnp.sum(jnp.where(hit, rank_full, 0.0), axis=0,
                                       keepdims=True).astype(jnp.int32)


def _router(logits_t, bias_col):
    ne, n = logits_t.shape
    tn = 1024 if n % 1024 == 0 else n
    choice = lambda: pl.BlockSpec((TOP_K, tn), lambda i: (0, i))
    return pl.pallas_call(
        _router_kernel,
        out_shape=[jax.ShapeDtypeStruct((TOP_K, n), jnp.int32),
                   jax.ShapeDtypeStruct((TOP_K, n), F32),
                   jax.ShapeDtypeStruct((TOP_K, n), jnp.int32),
                   jax.ShapeDtypeStruct((ne, 1), F32)],
        grid=(n // tn,),
        in_specs=[pl.BlockSpec((ne, tn), lambda i: (0, i)),
                  pl.BlockSpec((ne, 1), lambda i: (0, 0))],
        out_specs=[choice(), choice(), choice(), pl.BlockSpec((ne, 1), lambda i: (0, 0))],
        scratch_shapes=[pltpu.VMEM((ne, 1), F32)],
        compiler_params=_params("arbitrary"),
        name="group_topk_router",
    )(logits_t, bias_col)


def _slots_kernel(idx_ref, rank_ref, ps_ref, o_ref):
    ne, tn = ps_ref.shape[0], idx_ref.shape[1]
    eidx = lax.broadcasted_iota(jnp.int32, (ne, tn), 0)
    for j in range(TOP_K):
        start = jnp.sum(jnp.where(eidx == idx_ref[j:j + 1, :], ps_ref[...], 0.0), axis=0, keepdims=True)
        o_ref[j:j + 1, :] = start.astype(jnp.int32) + rank_ref[j:j + 1, :]


def _slot_positions(idx8, rank8, pstart_col):
    _, n = idx8.shape
    ne = pstart_col.shape[0]
    tn = 1024 if n % 1024 == 0 else n
    choice = lambda: pl.BlockSpec((TOP_K, tn), lambda i: (0, i))
    return pl.pallas_call(
        _slots_kernel,
        out_shape=jax.ShapeDtypeStruct((TOP_K, n), jnp.int32),
        grid=(n // tn,),
        in_specs=[choice(), choice(), pl.BlockSpec((ne, 1), lambda i: (0, 0))],
        out_specs=choice(),
        compiler_params=_params("parallel"),
        name="expert_slot_positions",
    )(idx8, rank8, pstart_col)


def _dispatch_kernel(pad_start_ref, npad_ref, nused_ref, pos_ref, h_ref, xs_ref, zero_ref, sem,
                     *, n_experts):
    tt = h_ref.shape[0]
    bm = zero_ref.shape[0]

    def issue(t, carry):
        for k in range(TOP_K):
            pltpu.make_async_copy(h_ref.at[pl.ds(t, 1)], xs_ref.at[pl.ds(pos_ref[k, t], 1)],
                                  sem).start(priority=k % 2)
        return carry

    lax.fori_loop(0, tt, issue, 0)
    for _ in range(TOP_K):
        pltpu.make_async_copy(h_ref, xs_ref.at[pl.ds(0, tt)], sem).wait()

    @pl.when(pl.program_id(0) == pl.num_programs(0) - 1)
    def _():
        zero_ref[...] = jnp.zeros_like(zero_ref)

        zero_row_src = zero_ref.at[pl.ds(0, 1)]

        def per_expert(e, carry):
            def zero_row(i, c2):
                pltpu.make_async_copy(zero_row_src, xs_ref.at[pl.ds(pad_start_ref[e] + i, 1)], sem).start()
                return c2

            def zero_wait(i, c2):
                pltpu.make_async_copy(zero_row_src, xs_ref.at[pl.ds(0, 1)], sem).wait()
                return c2

            lax.fori_loop(0, npad_ref[e], zero_row, 0)
            lax.fori_loop(0, npad_ref[e], zero_wait, 0)
            return carry

        lax.fori_loop(0, n_experts, per_expert, 0)

        def zero_block(j, carry):
            blk = pltpu.make_async_copy(zero_ref, xs_ref.at[pl.ds(j * bm, bm)], sem)
            blk.start()
            blk.wait()
            return carry

        lax.fori_loop(nused_ref[0], xs_ref.shape[0] // bm, zero_block, 0)


def _dispatch(h2, pos8, pad_start, npad, n_used, n_slots, tt, bm):
    n, d = h2.shape
    ne = pad_start.shape[0]
    grid_spec = pltpu.PrefetchScalarGridSpec(
        num_scalar_prefetch=3,
        grid=(n // tt,),
        in_specs=[pl.BlockSpec((TOP_K, tt), lambda i, ps, npd, nu: (0, i), memory_space=pltpu.SMEM),
                  pl.BlockSpec((tt, d), lambda i, ps, npd, nu: (i, 0))],
        out_specs=pl.BlockSpec(memory_space=pl.ANY),
        scratch_shapes=[pltpu.VMEM((bm, d), F32), pltpu.SemaphoreType.DMA(())])
    return pl.pallas_call(
        functools.partial(_dispatch_kernel, n_experts=ne),
        out_shape=jax.ShapeDtypeStruct((n_slots, d), F32),
        grid_spec=grid_spec,
        compiler_params=_params("arbitrary"),
        name="expert_dispatch",
    )(pad_start, npad, n_used, pos8, h2)


def _expert_ffn_kernel(be_ref, nu_ref, nx_ref, sl_ref, x_ref, wg_hbm, wu_hbm, wd_hbm, y_ref,
                       wgf, wuf, wdf, wgb, wub, wdb, sems):
    j = pl.program_id(0)
    e = be_ref[j]
    used = j < nu_ref[0]
    fresh = jnp.logical_or(j == 0, e != be_ref[jnp.maximum(j - 1, 0)])

    def weight_copies(expert, slot):
        return [pltpu.make_async_copy(src.at[expert], dst.at[slot], sems.at[slot])
                for src, dst in ((wg_hbm, wgf), (wu_hbm, wuf), (wd_hbm, wdf))]

    @pl.when(j == 0)
    def _():
        for cp in weight_copies(e, sl_ref[e]):
            cp.start()

    @pl.when(jnp.logical_and(used, fresh))
    def _():
        slot = sl_ref[e]
        for cp in weight_copies(e, slot):
            cp.wait()
        nxt = nx_ref[e]

        @pl.when(nxt >= 0)
        def _():
            for cp in weight_copies(nxt, 1 - slot):
                cp.start()

        wgb[...] = _bf(wgf[slot])
        wub[...] = _bf(wuf[slot])
        wdb[...] = _bf(wdf[slot])

    @pl.when(used)
    def _():
        x = _bf(x_ref[...])
        a = _dot(x, wgb[...])
        u = _dot(x, wub[...])
        y_ref[...] = _dot(_bf(a * jax.nn.sigmoid(a) * u), wdb[...])

    @pl.when(jnp.logical_not(used))
    def _():
        y_ref[...] = jnp.zeros_like(y_ref)


def _expert_ffn(xs, blk_expert, n_used, next_expert, expert_slot, e_gate, e_up, e_down, bm):
    n_slots, d = xs.shape
    ne, _, ff = e_gate.shape
    grid_spec = pltpu.PrefetchScalarGridSpec(
        num_scalar_prefetch=4,
        grid=(n_slots // bm,),
        in_specs=[pl.BlockSpec((bm, d), lambda j, be, nu, nx, sl: (jnp.minimum(j, nu[0] - 1), 0)),
                  pl.BlockSpec(memory_space=pl.ANY),
                  pl.BlockSpec(memory_space=pl.ANY),
                  pl.BlockSpec(memory_space=pl.ANY)],
        out_specs=pl.BlockSpec((bm, d), lambda j, be, nu, nx, sl: (j, 0)),
        scratch_shapes=[pltpu.VMEM((2, d, ff), F32), pltpu.VMEM((2, d, ff), F32), pltpu.VMEM((2, ff, d), F32),
                        pltpu.VMEM((d, ff), BF16), pltpu.VMEM((d, ff), BF16), pltpu.VMEM((ff, d), BF16),
                        pltpu.SemaphoreType.DMA((2,))])
    return pl.pallas_call(
        _expert_ffn_kernel,
        out_shape=jax.ShapeDtypeStruct((n_slots, d), F32),
        grid_spec=grid_spec,
        compiler_params=_params("arbitrary"),
        name="routed_expert_ffn",
    )(blk_expert, n_used, next_expert, expert_slot, xs, e_gate, e_up, e_down)


def _final_kernel(pos_ref, h_ref, x1_ref, g_ref, ys_ref, wg_ref, wu_ref, wd_ref, m_ref, fg_ref,
                  o_ref, buf, sem):
    tr = h_ref.shape[1]

    def issue(t, carry):
        for k in range(TOP_K):
            pltpu.make_async_copy(ys_ref.at[pl.ds(pos_ref[k, t], 1)], buf.at[k, pl.ds(t, 1)],
                                  sem).start(priority=k % 2)
        return carry

    lax.fori_loop(0, tr, issue, 0)
    h = _bf(h_ref[0])
    a = _dot(h, wg_ref[...])
    u = _dot(h, wu_ref[...])
    moe = _dot(_bf(a * jax.nn.sigmoid(a) * u), wd_ref[...])
    g = g_ref[...]
    for k in range(TOP_K):
        pltpu.make_async_copy(ys_ref.at[pl.ds(0, tr)], buf.at[k], sem).wait()
    for k in range(TOP_K):
        moe = moe + g[:, k:k + 1] * buf[k]
    x2 = x1_ref[0] + m_ref[0, 3:4, :] * moe
    ms = jnp.mean(x2 * x2, axis=-1, keepdims=True)
    o_ref[0] = x2 * lax.rsqrt(ms + NORM_EPS) * fg_ref[...]


def _final(h2, x1, pos8, gate_rows, ys, s_gate_b, s_up_b, s_down_b, m2, final_g, tr):
    b, t, d = x1.shape
    ff = s_gate_b.shape[1]
    nt = t // tr
    row = lambda: pl.BlockSpec((1, tr, d), lambda bb, i: (bb, i, 0))
    return pl.pallas_call(
        _final_kernel,
        out_shape=jax.ShapeDtypeStruct((b, t, d), F32),
        grid=(b, nt),
        in_specs=[pl.BlockSpec((TOP_K, tr), lambda bb, i: (0, bb * nt + i), memory_space=pltpu.SMEM),
                  row(), row(),
                  pl.BlockSpec((tr, TOP_K), lambda bb, i: (bb * nt + i, 0)),
                  pl.BlockSpec(memory_space=pl.ANY),
                  pl.BlockSpec((d, ff), lambda bb, i: (0, 0)),
                  pl.BlockSpec((d, ff), lambda bb, i: (0, 0)),
                  pl.BlockSpec((ff, d), lambda bb, i: (0, 0)),
                  pl.BlockSpec((1, 4, d), lambda bb, i: (bb, 0, 0)),
                  pl.BlockSpec((1, d), lambda bb, i: (0, 0))],
        out_specs=row(),
        scratch_shapes=[pltpu.VMEM((TOP_K, tr, d), F32), pltpu.SemaphoreType.DMA(())],
        compiler_params=_params("parallel", "parallel"),
        name="combine_shared_expert_final_norm",
    )(pos8, h2, x1, gate_rows, ys, s_gate_b, s_up_b, s_down_b, m2, final_g)


def _pad_to(a, axis, n):
    pad = [(0, 0)] * a.ndim
    pad[axis] = (0, n - a.shape[axis])
    return jnp.pad(a, pad)


def _layout(rd, wl, al, gl, ad, kvd):
    assert wl <= LANES and al <= LANES
    lay = {"rd": rd, "gl": gl, "ad": ad, "kvd": kvd}
    off = 0
    for name, width in (("q", ad), ("r", rd), ("k", rd), ("v", rd), ("ka", kvd), ("va", kvd),
                        ("xw", LANES), ("xa", LANES), ("xg", gl)):
        assert width % LANES == 0
        lay[name] = off
        off += width
    lay["ncols"] = off
    return lay


def kernel(x, c, ctx, c_ctx, w_mod, b_mod, norm1_g, norm2_g, w_in, shift_mu, k_k, k_a, r_k, decay_bias, decay_up, iclr_bias, iclr_up, gate_up, lnx_g, lnx_b, attn_sink, w_out, w_router, expert_bias, e_gate, e_up, e_down, s_gate, s_up, s_down, final_g):
    assert w_mod.shape[0] == 1, "single-layer block only"
    b, t, d = x.shape
    ctx_len = ctx.shape[1]
    rh, hd = r_k.shape[1:]
    assert hd == HEAD_DIM and b + 1 <= SUBLANES
    rd = rh * hd
    wl, al, gl = decay_up.shape[2], iclr_up.shape[2], gate_up.shape[1]
    ad = attn_sink.shape[1] * hd
    kvd = (w_in.shape[2] - (3 * rd + wl + al + gl) - ad) // 2
    ne = w_router.shape[2]
    lay = _layout(rd, wl, al, gl, ad, kvd)
    tr = 256 if ctx_len % 256 == 0 and t % 256 == 0 else CHUNK
    assert ctx_len % CHUNK == 0 and t % CHUNK == 0 and t % QBLOCK == 0 and WINDOW == QBLOCK

    c8 = _pad_to(jnp.concatenate([c, c_ctx[None]], axis=0), 0, SUBLANES)
    mod = _modulation(c8, w_mod[0], b_mod[0][None])
    mod6 = mod.reshape(SUBLANES, 6, d)
    mx, mc = mod6[:b], mod6[b]
    m1 = jnp.stack([jnp.broadcast_to(mc[None, 0:2], (b, 2, d)), mx[:, 0:2]], axis=1)
    m2 = jnp.stack([mx[:, 2], mx[:, 3], mx[:, 4], mx[:, 5]], axis=1)

    wi = w_in[0]
    o_w, o_a, o_g, o_q = 3 * rd, 3 * rd + wl, 3 * rd + wl + al, 3 * rd + wl + al + gl
    w_in_p = _bf(jnp.concatenate([
        wi[:, o_q:o_q + ad], wi[:, :3 * rd], wi[:, o_q + ad:],
        _pad_to(wi[:, o_w:o_a], 1, LANES), _pad_to(wi[:, o_a:o_g], 1, LANES), wi[:, o_g:o_q]], axis=1))
    mu = shift_mu[0]
    prm = {
        "mu_r": mu[None, :rd], "mu_k": mu[None, rd:2 * rd], "mu_v": mu[None, 2 * rd:3 * rd],
        "mu_w": _pad_to(mu[None, o_w:o_a], 1, LANES), "mu_a": _pad_to(mu[None, o_a:o_g], 1, LANES),
        "mu_g": mu[None, o_g:o_q],
        "k_k": k_k, "k_a": k_a, "r_k": r_k.reshape(1, rd),
        "decay_bias": decay_bias[0], "decay_up": _bf(_pad_to(decay_up[0], 1, LANES)),
        "iclr_bias": iclr_bias[0], "iclr_up": _bf(_pad_to(iclr_up[0], 1, LANES)),
        "gate_up": _bf(gate_up[0]),
    }

    p_all = _in_projection(ctx, x, norm1_g, m1, w_in_p, tr)

    npair = rd // LANES
    pp = max(n for n in (1, 2, 4, 8) if npair % n == 0
             and all(lay[key] % (n * LANES) == 0 for key in ("r", "k", "v")))
    rp, y0, mm, nn, bonus, gate = _wkv_prep(p_all, lay, prm, ctx_len, None, pp)
    y = _wkv_sweep(rp, y0, mm, nn, ctx_len)

    nf = HEAD_DIM // 4
    freqs = ROPE_BASE ** (-jnp.arange(nf, dtype=F32) / nf)
    pos = jnp.arange(t, dtype=jnp.int32)
    ang_r = (pos // GRID_W).astype(F32)[:, None] * freqs
    ang_c = (pos % GRID_W).astype(F32)[:, None] * freqs
    cos_h = jnp.concatenate([jnp.cos(ang_r)] * 2 + [jnp.cos(ang_c)] * 2, axis=1)
    sin_h = jnp.concatenate([-jnp.sin(ang_r), jnp.sin(ang_r), -jnp.sin(ang_c), jnp.sin(ang_c)], axis=1)
    cos_t = jnp.concatenate([cos_h, cos_h], axis=1)
    sin_t = jnp.concatenate([sin_h, sin_h], axis=1)
    attn = _attention(p_all, lay, cos_t, sin_t, attn_sink, ctx_len, t)

    x1, h2, logits_t = _out_projection(y, bonus, gate, attn, x, lnx_g, lnx_b, _bf(w_out[0]), m2,
                                       norm2_g, w_router[0], ctx_len, tr)
    idx8, gate8, rank8, counts = _router(logits_t, expert_bias[0][:, None])

    n_tok = b * t
    bm = EXPERT_ROWS
    assert (n_tok * TOP_K) % bm == 0
    n_blk = n_tok * TOP_K // bm + ne
    cnt = counts[:, 0].astype(jnp.int32)
    padded = (cnt + bm - 1) // bm * bm
    pend = jnp.cumsum(padded)
    pstart = pend - padded
    pos8 = _slot_positions(idx8, rank8, pstart.astype(F32)[:, None])
    blk_row0 = jnp.arange(n_blk, dtype=jnp.int32) * bm
    blk_expert = jnp.minimum(jnp.sum((pend[None, :] <= blk_row0[:, None]).astype(jnp.int32), axis=1), ne - 1)
    n_used = (pend[-1:] // bm).astype(jnp.int32)

    xs = _dispatch(h2.reshape(n_tok, d), pos8, pstart + cnt, padded - cnt, n_used, n_blk * bm, tr, bm)
    has_rows = cnt > 0
    eids = jnp.arange(ne, dtype=jnp.int32)
    later = jnp.where(has_rows[None, :] & (eids[None, :] > eids[:, None]), eids[None, :], ne)
    next_expert = jnp.min(later, axis=1)
    next_expert = jnp.where(next_expert < ne, next_expert, -1).astype(jnp.int32)
    expert_slot = ((jnp.cumsum(has_rows.astype(jnp.int32)) - 1) % 2).astype(jnp.int32)
    ys = _expert_ffn(xs, blk_expert, n_used, next_expert, expert_slot, e_gate[0], e_up[0], e_down[0], bm)
    return _final(h2, x1, pos8, gate8.T, ys, _bf(s_gate[0]), _bf(s_up[0]), _bf(s_down[0]),
                  m2, final_g[None], tr)
```

```python
import functools
import math

import jax
import jax.numpy as jnp
from jax import lax
from jax.experimental import pallas as pl
from jax.experimental.pallas import tpu as pltpu

GRID_W = 64
WINDOW = 128
ROPE_BASE = 10000.0
N_GROUPS = 8
TOPK_GROUPS = 4
TOP_K = 8
ROUTED_SCALE = 2.5
NORM_EPS = 1e-6
LNX_EPS = 64e-5
KK_EPS = 1e-12

LANES = 128
SUBLANES = 8
VMEM_LIMIT_BYTES = 56 * 1024 * 1024

HEAD_DIM = 64
CHUNK = 64
QBLOCK = 128
EXPERT_ROWS = 256

F32 = jnp.float32
BF16 = jnp.bfloat16
NEG_BIG = -1e30


def _bf(x):
    return x.astype(BF16)


def _dot(a, b, precision=None):
    return jnp.dot(a, b, preferred_element_type=F32, precision=precision)


def _dot_nt(a, b, precision=None):
    return lax.dot_general(a, b, (((1,), (1,)), ((), ())), preferred_element_type=F32,
                           precision=precision)


def _dot_tn(a, b):
    return lax.dot_general(a, b, (((0,), (0,)), ((), ())), preferred_element_type=F32)


def _params(*sem):
    return pltpu.CompilerParams(dimension_semantics=sem, vmem_limit_bytes=VMEM_LIMIT_BYTES)


def _mod_kernel(c_ref, w_ref, b_ref, o_ref):
    c = c_ref[...]
    s = c * jax.nn.sigmoid(c)
    hi = _bf(s)
    lo = _bf(s - hi.astype(F32))
    r = _dot(jnp.concatenate([hi, lo], axis=0), _bf(w_ref[...]))
    o_ref[...] = r[:SUBLANES] + r[SUBLANES:] + b_ref[...]


def _modulation(c8, w_mod, b_mod):
    d, n = w_mod.shape
    tn = 1024 if n % 1024 == 0 else n
    return pl.pallas_call(
        _mod_kernel,
        out_shape=jax.ShapeDtypeStruct((SUBLANES, n), F32),
        grid=(n // tn,),
        in_specs=[pl.BlockSpec((SUBLANES, d), lambda j: (0, 0)),
                  pl.BlockSpec((d, tn), lambda j: (0, j)),
                  pl.BlockSpec((1, tn), lambda j: (0, j))],
        out_specs=pl.BlockSpec((SUBLANES, tn), lambda j: (0, j)),
        compiler_params=_params("parallel"),
        name="adaln_modulation",
    )(c8, w_mod, b_mod)


def _inproj_kernel(c_ref, x_ref, g_ref, m_ref, w_ref, o_ref, *, nct):
    x = jnp.where(pl.program_id(2) < nct, c_ref[0], x_ref[0])
    ms = jnp.mean(x * x, axis=-1, keepdims=True)
    y = x * lax.rsqrt(ms + NORM_EPS) * g_ref[...]
    m = m_ref[0, 0]
    h = y * (1.0 + m[1:2]) + m[0:1]
    o_ref[0] = _dot(_bf(h), w_ref[...])


def _in_projection(ctx, x, g1, m1, w_in_p, tr):
    b, t, d = x.shape
    ctx_len = ctx.shape[1]
    ta = ctx_len + t
    ncols = w_in_p.shape[1]
    tn = ncols // 4 if (ncols // 4) % LANES == 0 and ncols % 4 == 0 else ncols
    nct = ctx_len // tr
    return pl.pallas_call(
        functools.partial(_inproj_kernel, nct=nct),
        out_shape=jax.ShapeDtypeStruct((b, ta, ncols), F32),
        grid=(ncols // tn, b, ta // tr),
        in_specs=[pl.BlockSpec((1, tr, d), lambda n, bb, i: (bb, jnp.minimum(i, nct - 1), 0)),
                  pl.BlockSpec((1, tr, d), lambda n, bb, i: (bb, jnp.maximum(i - nct, 0), 0)),
                  pl.BlockSpec((1, d), lambda n, bb, i: (0, 0)),
                  pl.BlockSpec((1, 1, 2, d), lambda n, bb, i: (bb, jnp.where(i >= nct, 1, 0), 0, 0)),
                  pl.BlockSpec((d, tn), lambda n, bb, i: (0, n))],
        out_specs=pl.BlockSpec((1, tr, tn), lambda n, bb, i: (bb, i, n)),
        compiler_params=_params("parallel", "parallel", "parallel"),
        name="norm_modulate_inproj",
    )(ctx, x, g1, m1, w_in_p)


def _wkv_prep_kernel(r_c, r_p, r_n, k_c, k_p, k_n, v_c, v_p, v_n, w_c, w_p, w_n, a_c, a_p, a_n,
                     g_c, g_p, g_n, mu_r, mu_k, mu_v, mu_w, mu_a, mu_g, kk_ref, ka_ref, rk_ref,
                     db_ref, du_ref, ib_ref, iu_ref, gu_ref,
                     rp_out, y0_out, m_out, n_out, bonus_out, gate_out,
                     *, nctx_chunks, n_chunks, prec_t):
    L = CHUNK
    c = pl.program_id(1)
    has_prev = jnp.logical_and(c != 0, c != nctx_chunks)
    has_next = jnp.logical_and(c != nctx_chunks - 1, c != n_chunks - 1)

    def shifted(cur, prv, nxt, mu):
        x = cur[0]
        rows = lax.broadcasted_iota(jnp.int32, (L, 1), 0)
        pr = jnp.where(has_prev, prv[0, SUBLANES - 1:SUBLANES, :], 0.0)
        nx = jnp.where(has_next, nxt[0, 0:1, :], 0.0)
        up = jnp.where(rows == 0, pr, pltpu.roll(x, 1, 0))
        dn = jnp.where(rows == L - 1, nx, pltpu.roll(x, L - 1, 0))
        return x + mu[...] * (0.5 * (up + dn) - x)

    r = shifted(r_c, r_p, r_n, mu_r)
    k = shifted(k_c, k_p, k_n, mu_k)
    v = shifted(v_c, v_p, v_n, mu_v)
    xw = shifted(w_c, w_p, w_n, mu_w)
    xa = shifted(a_c, a_p, a_n, mu_a)
    xg = shifted(g_c, g_p, g_n, mu_g)

    lane = lax.broadcasted_iota(jnp.int32, (1, LANES), 1)
    m0 = lane < HEAD_DIM

    def head_sum(x):
        s0 = jnp.sum(jnp.where(m0, x, 0.0), axis=-1, keepdims=True)
        s1 = jnp.sum(jnp.where(m0, 0.0, x), axis=-1, keepdims=True)
        return jnp.where(m0, s0, s1)

    def stack2(x):
        return jnp.concatenate([jnp.where(m0, x, 0.0), jnp.where(m0, 0.0, x)], axis=0)

    tw = _bf(jnp.tanh(xw))
    xab = _bf(xa)
    lw_all = [(-math.exp(-0.5)) * jax.nn.sigmoid(db_ref[d:d + 1, :] + _dot(tw, du_ref[d]))
              for d in range(2)]
    sa_all = [jax.nn.sigmoid(ib_ref[d:d + 1, :] + _dot(xab, iu_ref[d])) for d in range(2)]
    gate_out[0] = _dot(_bf(jax.nn.sigmoid(xg)), gu_ref[...])

    ri = lax.broadcasted_iota(jnp.int32, (L, L), 0)
    ci = lax.broadcasted_iota(jnp.int32, (L, L), 1)
    ri2 = lax.broadcasted_iota(jnp.int32, (2 * L, 2 * L), 0)
    ci2 = lax.broadcasted_iota(jnp.int32, (2 * L, 2 * L), 1)
    rt = jnp.bitwise_and(ri2, L - 1)
    ct = jnp.bitwise_and(ci2, L - 1)
    eye = ri2 == ci2
    n_sq = int(math.log2(L)) - 1

    tri = [(ci <= ri).astype(F32), (ci >= ri).astype(F32)]
    strict = [ct < rt, ct > rt]
    incl = [ct <= rt, ct >= rt]
    cum = [_dot(tri[d], lw_all[d], precision=lax.Precision.HIGHEST) for d in range(2)]
    cend = [jnp.sum(lw_all[d], axis=0, keepdims=True) for d in range(2)]
    e_in = [jnp.exp(cum[d]) for d in range(2)]
    e_ex = [jnp.exp(cum[d] - lw_all[d]) for d in range(2)]
    e_neg = [jnp.exp(-cum[d]) for d in range(2)]
    e_end = [jnp.exp(cend[d] - cum[d]) for d in range(2)]
    p_end = [jnp.exp(cend[d]) for d in range(2)]
    kd_all = [k * (1.0 + (sa_all[d] - 1.0) * ka_ref[...]) for d in range(2)]

    npp = r.shape[1] // LANES
    chains = [(p, d) for p in range(npp) for d in range(2)]
    sls = [slice(p * LANES, (p + 1) * LANES) for p in range(npp)]
    kk_p, vs_p, vsb_p = [], [], []
    for p in range(npp):
        kx = k[:, sls[p]] * kk_ref[:, sls[p]]
        kk_p.append(kx * lax.rsqrt(head_sum(kx * kx) + KK_EPS))
        vs_p.append(stack2(v[:, sls[p]]))
        vsb_p.append(_bf(vs_p[p]))
        ksum = kd_all[0][:, sls[p]] + kd_all[1][:, sls[p]]
        bonus_out[0, :, sls[p]] = head_sum(r[:, sls[p]] * ksum * rk_ref[:, sls[p]]) * v[:, sls[p]]

    a_s, r_s, bh_s, kh_s, g = [], [], [], [], []
    for p, d in chains:
        sl = sls[p]
        kd = kd_all[d][:, sl]
        bvec = kk_p[p] * sa_all[d][:, sl]
        a_s.append(stack2(-kk_p[p] * e_ex[d][:, sl]))
        r_s.append(stack2(r[:, sl] * e_in[d][:, sl]))
        b_s = stack2(bvec * e_neg[d][:, sl])
        k_s = stack2(kd * e_neg[d][:, sl])
        bh_s.append(stack2(bvec * e_end[d][:, sl]))
        kh_s.append(stack2(kd * e_end[d][:, sl]))
        g.append(_dot_nt(_bf(jnp.concatenate([a_s[-1], r_s[-1]], axis=0)),
                         _bf(jnp.concatenate([b_s, k_s], axis=0))))
    a_ab = [jnp.where(strict[d], g[i][:2 * L, :2 * L], 0.0) for i, (p, d) in enumerate(chains)]
    a_ak = [jnp.where(strict[d], g[i][:2 * L, 2 * L:], 0.0) for i, (p, d) in enumerate(chains)]
    a_rb = [jnp.where(incl[d], g[i][2 * L:, :2 * L], 0.0) for i, (p, d) in enumerate(chains)]
    a_rk = [jnp.where(incl[d], g[i][2 * L:, 2 * L:], 0.0) for i, (p, d) in enumerate(chains)]

    ident = jnp.where(eye, 1.0, 0.0)
    tmat = [ident + a for a in a_ab]
    apow = a_ab
    for _ in range(n_sq):
        if prec_t is None:
            apow = [_dot(_bf(a), _bf(a)) for a in apow]
            tmat = [t + _dot(_bf(a), _bf(t)) for a, t in zip(apow, tmat)]
        else:
            apow = [_dot(a, a, precision=prec_t) for a in apow]
            tmat = [t + _dot(a, t, precision=prec_t) for a, t in zip(apow, tmat)]

    x1 = [_dot(_bf(a_ak[i]), vsb_p[p]) for i, (p, d) in enumerate(chains)]
    y0v = [_dot(_bf(a_rk[i]), vsb_p[p]) for i, (p, d) in enumerate(chains)]
    tu = [_dot(_bf(tmat[i]), _bf(jnp.concatenate([a_s[i], x1[i]], axis=1)))
          for i in range(len(chains))]
    ry = [_dot(_bf(a_rb[i]), _bf(tu[i])) for i in range(len(chains))]
    def fold(z):
        h = z.shape[0] // 2
        return z[:h] + z[h:]

    for i, (p, d) in enumerate(chains):
        rp_out[d, 0, 0, p] = _bf(fold(r_s[i] + ry[i][:, :LANES]))
        y0_out[d, 0, 0, p] = fold(ry[i][:, LANES:] + y0v[i])
    for i, (p, d) in enumerate(chains):
        m_out[d, 0, 0, p] = _bf(fold(jnp.where(eye, p_end[d][:, sls[p]], 0.0)
                                     + _dot_tn(_bf(tu[i][:, :LANES]), _bf(bh_s[i]))))
    for i, (p, d) in enumerate(chains):
        n_out[d, 0, 0, p] = fold(_dot_tn(_bf(jnp.concatenate([tu[i][:, LANES:], vs_p[p]], axis=0)),
                                         _bf(jnp.concatenate([bh_s[i], kh_s[i]], axis=0))))


def _wkv_prep(p_all, lay, prm, ctx_len, prec_t, pp):
    b, ta, _ = p_all.shape
    L = CHUNK
    nc = ta // L
    nctx = ctx_len // L
    npair = lay["rd"] // LANES
    gl = lay["gl"]
    rows8 = ta // SUBLANES

    def cur(colfn, width=LANES):
        return pl.BlockSpec((1, L, width), lambda bb, c, p: (bb, c, colfn(p)))

    def prv(colfn, width=LANES):
        return pl.BlockSpec((1, SUBLANES, width),
                            lambda bb, c, p: (bb, jnp.maximum(c * (L // SUBLANES) - 1, 0), colfn(p)))

    def nxt(colfn, width=LANES):
        return pl.BlockSpec((1, SUBLANES, width),
                            lambda bb, c, p: (bb, jnp.minimum((c + 1) * (L // SUBLANES), rows8 - 1), colfn(p)))

    def trio(colfn, width=LANES):
        return [cur(colfn, width), prv(colfn, width), nxt(colfn, width)]

    pw = pp * LANES
    assert npair % pp == 0 and all(lay[n] % pw == 0 for n in ("r", "k", "v"))
    rb, kb, vb = lay["r"] // pw, lay["k"] // pw, lay["v"] // pw
    wb, ab = lay["xw"] // LANES, lay["xa"] // LANES
    assert lay["xg"] % gl == 0
    gb = lay["xg"] // gl

    def vec(width=pw):
        return pl.BlockSpec((1, width), lambda bb, c, p: (0, p))

    def vec0(width=LANES):
        return pl.BlockSpec((1, width), lambda bb, c, p: (0, 0))

    in_specs = (trio(lambda p: rb + p, pw) + trio(lambda p: kb + p, pw) + trio(lambda p: vb + p, pw)
                + trio(lambda p: wb) + trio(lambda p: ab) + trio(lambda p: gb, gl)
                + [vec(), vec(), vec(), vec0(), vec0(), vec0(gl),
                   vec(), vec(), vec(),
                   pl.BlockSpec((2, pw), lambda bb, c, p: (0, p)),
                   pl.BlockSpec((2, LANES, pw), lambda bb, c, p: (0, 0, p)),
                   pl.BlockSpec((2, pw), lambda bb, c, p: (0, p)),
                   pl.BlockSpec((2, LANES, pw), lambda bb, c, p: (0, 0, p)),
                   pl.BlockSpec((gl, pw), lambda bb, c, p: (0, p))])

    def opspec(rows):
        return pl.BlockSpec((2, 1, 1, pp, rows, LANES), lambda bb, c, p: (0, c, bb, p, 0, 0))

    out_specs = [opspec(L), opspec(L), opspec(HEAD_DIM), opspec(HEAD_DIM),
                 pl.BlockSpec((1, L, pw), lambda bb, c, p: (bb, c, p)),
                 pl.BlockSpec((1, L, pw), lambda bb, c, p: (bb, c, p))]
    out_shape = [jax.ShapeDtypeStruct((2, nc, b, npair, L, LANES), BF16),
                 jax.ShapeDtypeStruct((2, nc, b, npair, L, LANES), F32),
                 jax.ShapeDtypeStruct((2, nc, b, npair, HEAD_DIM, LANES), BF16),
                 jax.ShapeDtypeStruct((2, nc, b, npair, HEAD_DIM, LANES), F32),
                 jax.ShapeDtypeStruct((b, ta, lay["rd"]), F32),
                 jax.ShapeDtypeStruct((b, ta, lay["rd"]), F32)]
    kern = functools.partial(_wkv_prep_kernel, nctx_chunks=nctx, n_chunks=nc, prec_t=prec_t)
    args = [p_all] * 18 + [prm["mu_r"], prm["mu_k"], prm["mu_v"], prm["mu_w"], prm["mu_a"], prm["mu_g"],
                           prm["k_k"], prm["k_a"], prm["r_k"], prm["decay_bias"], prm["decay_up"],
                           prm["iclr_bias"], prm["iclr_up"], prm["gate_up"]]
    return pl.pallas_call(
        kern, out_shape=out_shape, grid=(b, nc, npair // pp), in_specs=in_specs, out_specs=out_specs,
        compiler_params=_params("parallel", "parallel", "parallel"),
        name="wkv_chunk_operators",
    )(*args)


def _wkv_seq_kernel(rp_ref, y0_ref, m_ref, n_ref, y_ref, s_ref, *, nb, npair):
    L = CHUNK

    @pl.when(pl.program_id(1) == 0)
    def _():
        s_ref[...] = jnp.zeros_like(s_ref)

    m0 = lax.broadcasted_iota(jnp.int32, (1, LANES), 1) < HEAD_DIM

    def unfold(z):
        zero = jnp.zeros_like(z)
        return jnp.concatenate([jnp.where(m0, z, zero), jnp.where(m0, zero, z)], axis=0)

    for bb in range(nb):
        for p in range(npair):
            i = bb * npair + p
            s0 = _bf(s_ref[i])
            ys = _dot_nt(unfold(rp_ref[0, 0, bb, p]), s0)
            y_ref[0, bb, :, p * LANES:(p + 1) * LANES] = ys[:L] + ys[L:] + y0_ref[0, 0, bb, p]
            s_ref[i] = _dot(s0, unfold(m_ref[0, 0, bb, p])) + unfold(n_ref[0, 0, bb, p])


def _wkv_sweep(rp, y0, mm, nn, ctx_len):
    _, nc, b, npair, _, _ = rp.shape
    L = CHUNK
    nctx = ctx_len // L

    def chunk_of(d, c):
        rev = jnp.where(c < nctx, nctx - 1 - c, nc - 1 - (c - nctx))
        return jnp.where(d == 0, c, rev)

    def opspec(rows):
        return pl.BlockSpec((1, 1, b, npair, rows, LANES), lambda d, c: (d, chunk_of(d, c), 0, 0, 0, 0))

    return pl.pallas_call(
        functools.partial(_wkv_seq_kernel, nb=b, npair=npair),
        out_shape=jax.ShapeDtypeStruct((2, b, nc * L, npair * LANES), F32),
        grid=(2, nc),
        in_specs=[opspec(L), opspec(L), opspec(HEAD_DIM), opspec(HEAD_DIM)],
        out_specs=pl.BlockSpec((1, b, L, npair * LANES), lambda d, c: (d, 0, chunk_of(d, c), 0)),
        scratch_shapes=[pltpu.VMEM((b * npair, LANES, LANES), F32)],
        compiler_params=_params("parallel", "arbitrary"),
        name="wkv_state_sweep",
    )(rp, y0, mm, nn)


def _attn_kernel(q_ref, kp_ref, kc_ref, kn_ref, vp_ref, vc_ref, vn_ref, kx_ref, vx_ref,
                 cq_ref, sq_ref, cp_ref, sp_ref, cn_ref, sn_ref, sink_ref, o_ref,
                 *, n_blocks, kv_heads, group):
    n = pl.program_id(1)
    qb = QBLOCK

    def rope(x, cos, sin):
        w = x.shape[1]
        reps = w // LANES
        cs = jnp.concatenate([cos] * reps, axis=1) if reps > 1 else cos
        sn = jnp.concatenate([sin] * reps, axis=1) if reps > 1 else sin
        lane = lax.broadcasted_iota(jnp.int32, (1, w), 1)
        sw = jnp.where(jnp.bitwise_and(lane, 16) == 0, pltpu.roll(x, w - 16, 1), pltpu.roll(x, 16, 1))
        return x * cs + sw * sn

    q = rope(q_ref[0], cq_ref[...], sq_ref[...]) * (HEAD_DIM ** -0.5)
    kwin = jnp.concatenate([rope(kp_ref[0], cp_ref[...], sp_ref[...]),
                            rope(kc_ref[0], cq_ref[...], sq_ref[...]),
                            rope(kn_ref[0], cn_ref[...], sn_ref[...]),
                            kx_ref[0]], axis=0)
    vwin = jnp.concatenate([vp_ref[0], vc_ref[0], vn_ref[0], vx_ref[0]], axis=0)
    nkeys = kwin.shape[0]

    rows = group * qb
    qi = jnp.bitwise_and(lax.broadcasted_iota(jnp.int32, (rows, nkeys), 0), qb - 1)
    s_i = lax.broadcasted_iota(jnp.int32, (rows, nkeys), 1)
    blk = s_i // qb + (n - 1)
    in_win = (s_i >= qi) & (s_i <= qi + 2 * WINDOW) & (blk >= 0) & (blk < n_blocks)
    valid = in_win | (s_i >= 3 * qb)
    rgrp = lax.broadcasted_iota(jnp.int32, (rows, 1), 0) // qb

    scores, sinks = [], []
    for kh in range(kv_heads):
        kk_ = _bf(kwin[:, kh * HEAD_DIM:(kh + 1) * HEAD_DIM])
        qs = jnp.concatenate([q[:, (kh * group + g) * HEAD_DIM:(kh * group + g + 1) * HEAD_DIM]
                              for g in range(group)], axis=0)
        sk = jnp.zeros((rows, 1), F32)
        for g in range(group):
            h = kh * group + g
            sk = jnp.where(rgrp == g, sink_ref[0:1, h:h + 1], sk)
        sinks.append(sk)
        scores.append(jnp.where(valid, _dot_nt(_bf(qs), kk_), NEG_BIG))
    probs, denoms = [], []
    for s, sk in zip(scores, sinks):
        m = jnp.maximum(jnp.max(s, axis=-1, keepdims=True), sk)
        e = jnp.exp(s - m)
        denoms.append(jnp.sum(e, axis=-1, keepdims=True) + jnp.exp(sk - m))
        probs.append(_bf(e))
    pieces = []
    for kh in range(kv_heads):
        vv_ = _bf(vwin[:, kh * HEAD_DIM:(kh + 1) * HEAD_DIM])
        o = _dot(probs[kh], vv_) / denoms[kh]
        for g in range(group):
            pieces.append(o[g * qb:(g + 1) * qb])
    o_ref[0] = _bf(jnp.concatenate(pieces, axis=1))


def _attention(p_all, lay, cos_t, sin_t, sink, ctx_len, seq):
    b = p_all.shape[0]
    qb = QBLOCK
    nb = seq // qb
    cb = ctx_len // qb
    ad, kvd = lay["ad"], lay["kvd"]
    kv_heads = kvd // HEAD_DIM
    group = (ad // HEAD_DIM) // kv_heads
    assert lay["q"] % ad == 0 and lay["ka"] % kvd == 0 and lay["va"] % kvd == 0
    qc, kc, vc = lay["q"] // ad, lay["ka"] // kvd, lay["va"] // kvd

    def clampb(n, off):
        return jnp.clip(n + off, 0, nb - 1)

    def win(col, off):
        return pl.BlockSpec((1, qb, kvd), lambda bb, n: (bb, cb + clampb(n, off), col))

    def tab(off):
        return pl.BlockSpec((qb, LANES), lambda bb, n: (clampb(n, off), 0))

    assert ctx_len % qb == 0
    ctxspec = lambda col: pl.BlockSpec((1, ctx_len, kvd), lambda bb, n: (bb, 0, col))
    in_specs = [pl.BlockSpec((1, qb, ad), lambda bb, n: (bb, cb + n, qc)),
                win(kc, -1), win(kc, 0), win(kc, 1), win(vc, -1), win(vc, 0), win(vc, 1),
                ctxspec(kc), ctxspec(vc),
                tab(0), tab(0), tab(-1), tab(-1), tab(1), tab(1),
                pl.BlockSpec((1, sink.shape[1]), lambda bb, n: (0, 0))]
    return pl.pallas_call(
        functools.partial(_attn_kernel, n_blocks=nb, kv_heads=kv_heads, group=group),
        out_shape=jax.ShapeDtypeStruct((b, seq, ad), BF16),
        grid=(b, nb),
        in_specs=in_specs,
        out_specs=pl.BlockSpec((1, qb, ad), lambda bb, n: (bb, n, 0)),
        compiler_params=_params("parallel", "parallel"),
        name="window_ctx_attention",
    )(p_all, p_all, p_all, p_all, p_all, p_all, p_all, p_all, p_all,
      cos_t, sin_t, cos_t, sin_t, cos_t, sin_t, sink)


def _outproj_kernel(yf_ref, yr_ref, bon_ref, gat_ref, att_ref, x_ref, lg_ref, lb_ref, w_ref,
                    m_ref, g2_ref, wr_ref, x1_ref, h2_ref, lt_ref):
    ysum = yf_ref[0, 0] + yr_ref[0, 0]
    ri = lax.broadcasted_iota(jnp.int32, (LANES, LANES), 0) // HEAD_DIM
    ci = lax.broadcasted_iota(jnp.int32, (LANES, LANES), 1) // HEAD_DIM
    hm = _bf(jnp.where(ri == ci, 1.0 / HEAD_DIM, 0.0))

    def head_mean(z):
        hi = _bf(z)
        lo = _bf(z - hi.astype(F32))
        return _dot(hi, hm) + _dot(lo, hm)

    ybs = [ysum[:, j * LANES:(j + 1) * LANES] for j in range(ysum.shape[1] // LANES)]
    devs = [yb - mu for yb, mu in zip(ybs, [head_mean(yb) for yb in ybs])]
    blocks = [dev * lax.rsqrt(var + LNX_EPS)
              for dev, var in zip(devs, [head_mean(dev * dev) for dev in devs])]
    yn = jnp.concatenate(blocks, axis=1) if len(blocks) > 1 else blocks[0]
    rw = (yn * lg_ref[...] + lb_ref[...] + bon_ref[0]) * gat_ref[0]
    mix = jnp.concatenate([_bf(rw), att_ref[0]], axis=1)
    m = m_ref[0]
    x1 = x_ref[0] + m[0:1] * _dot(mix, w_ref[...])
    x1_ref[0] = x1
    ms = jnp.mean(x1 * x1, axis=-1, keepdims=True)
    h2 = x1 * lax.rsqrt(ms + NORM_EPS) * g2_ref[...] * (1.0 + m[2:3]) + m[1:2]
    h2_ref[0] = h2
    hi = _bf(h2)
    lo = _bf(h2 - hi.astype(F32))
    wr = wr_ref[...]
    wr_hi = _bf(wr)
    wr_lo = _bf(wr - wr_hi.astype(F32))
    lg = _dot(hi, wr_hi) + (_dot(hi, wr_lo) + _dot(lo, wr_hi))
    lt_ref[...] = lg.T[:lt_ref.shape[0]]


def _out_projection(y, bonus, gate, attn, x, lnx_g, lnx_b, w_out_b, m2, g2, w_router, ctx_len, tr):
    b, t, d = x.shape
    rd = y.shape[3]
    ad = attn.shape[2]
    ne = w_router.shape[1]
    assert ne <= LANES
    wr = _pad_to(w_router, 1, LANES)
    co = ctx_len // tr
    nt = t // tr
    return pl.pallas_call(
        _outproj_kernel,
        out_shape=[jax.ShapeDtypeStruct((b, t, d), F32),
                   jax.ShapeDtypeStruct((b, t, d), F32),
                   jax.ShapeDtypeStruct((ne, b * t), F32)],
        grid=(b, nt),
        in_specs=[pl.BlockSpec((1, 1, tr, rd), lambda bb, i: (0, bb, co + i, 0)),
                  pl.BlockSpec((1, 1, tr, rd), lambda bb, i: (1, bb, co + i, 0)),
                  pl.BlockSpec((1, tr, rd), lambda bb, i: (bb, co + i, 0)),
                  pl.BlockSpec((1, tr, rd), lambda bb, i: (bb, co + i, 0)),
                  pl.BlockSpec((1, tr, ad), lambda bb, i: (bb, i, 0)),
                  pl.BlockSpec((1, tr, d), lambda bb, i: (bb, i, 0)),
                  pl.BlockSpec((1, rd), lambda bb, i: (0, 0)),
                  pl.BlockSpec((1, rd), lambda bb, i: (0, 0)),
                  pl.BlockSpec((rd + ad, d), lambda bb, i: (0, 0)),
                  pl.BlockSpec((1, 4, d), lambda bb, i: (bb, 0, 0)),
                  pl.BlockSpec((1, d), lambda bb, i: (0, 0)),
                  pl.BlockSpec((d, LANES), lambda bb, i: (0, 0))],
        out_specs=[pl.BlockSpec((1, tr, d), lambda bb, i: (bb, i, 0)),
                   pl.BlockSpec((1, tr, d), lambda bb, i: (bb, i, 0)),
                   pl.BlockSpec((ne, tr), lambda bb, i: (0, bb * nt + i))],
        compiler_params=_params("parallel", "parallel"),
        name="rwkv_out_outproj_norm2_router",
    )(y, y, bonus, gate, attn, x, lnx_g, lnx_b, w_out_b, m2, g2, wr)


def _router_kernel(l_ref, b_ref, idx_ref, gate_ref, rank_ref, cnt_ref, carry_ref):
    @pl.when(pl.program_id(0) == 0)
    def _():
        carry_ref[...] = jnp.zeros_like(carry_ref)

    sc = jax.nn.sigmoid(l_ref[...])
    ne, tn = sc.shape
    eg = ne // N_GROUPS
    bz = sc + b_ref[...]
    g3 = bz.reshape(N_GROUPS, eg, tn)
    io = lax.broadcasted_iota(jnp.int32, (N_GROUPS, eg, tn), 1).astype(F32)
    m1 = jnp.max(g3, axis=1, keepdims=True)
    i1 = jnp.min(jnp.where(g3 == m1, io, float(eg)), axis=1, keepdims=True)
    m2 = jnp.max(jnp.where(io == i1, -jnp.inf, g3), axis=1, keepdims=True)
    gs = (m1 + m2).reshape(N_GROUPS, tn)

    def topk_hits(vals, k):
        n = vals.shape[0]
        idx = lax.broadcasted_iota(jnp.int32, vals.shape, 0).astype(F32)
        hits = []
        cur = vals
        for _ in range(k):
            mx = jnp.max(cur, axis=0, keepdims=True)
            ix = jnp.min(jnp.where(cur == mx, idx, float(n)), axis=0, keepdims=True)
            hit = idx == ix
            hits.append(hit)
            cur = jnp.where(hit, -jnp.inf, cur)
        return hits

    gsel = sum(jnp.where(h, 1.0, 0.0) for h in topk_hits(gs, TOPK_GROUPS))
    masked = jnp.where(gsel.reshape(N_GROUPS, 1, tn) > 0.0, g3, -jnp.inf).reshape(ne, tn)
    hits = topk_hits(masked, TOP_K)
    chosen = sum(jnp.where(h, 1.0, 0.0) for h in hits)
    denom = jnp.sum(chosen * sc, axis=0, keepdims=True)

    si = lax.broadcasted_iota(jnp.int32, (tn, tn), 0)
    ti = lax.broadcasted_iota(jnp.int32, (tn, tn), 1)
    before = _bf(jnp.where(si < ti, 1.0, 0.0))
    rank_full = _dot(_bf(chosen), before) + carry_ref[...]
    carry_ref[...] += jnp.sum(chosen, axis=1, keepdims=True)
    cnt_ref[...] = carry_ref[...]

    eidx = lax.broadcasted_iota(jnp.int32, (ne, tn), 0).astype(F32)
    for j, hit in enumerate(hits):
        idx_ref[j:j + 1, :] = jnp.sum(jnp.where(hit, eidx, 0.0), axis=0, keepdims=True).astype(jnp.int32)
        gate_ref[j:j + 1, :] = (jnp.sum(jnp.where(hit, sc, 0.0), axis=0, keepdims=True)
                                / denom * ROUTED_SCALE)
        rank_ref[j:j + 1, :] = jnp.sum(jnp.where(hit, rank_full, 0.0), axis=0,
                                       keepdims=True).astype(jnp.int32)


def _router(logits_t, bias_col):
    ne, n = logits_t.shape
    tn = 1024 if n % 1024 == 0 else n
    choice = lambda: pl.BlockSpec((TOP_K, tn), lambda i: (0, i))
    return pl.pallas_call(
        _router_kernel,
        out_shape=[jax.ShapeDtypeStruct((TOP_K, n), jnp.int32),
                   jax.ShapeDtypeStruct((TOP_K, n), F32),
                   jax.ShapeDtypeStruct((TOP_K, n), jnp.int32),
                   jax.ShapeDtypeStruct((ne, 1), F32)],
        grid=(n // tn,),
        in_specs=[pl.BlockSpec((ne, tn), lambda i: (0, i)),
                  pl.BlockSpec((ne, 1), lambda i: (0, 0))],
        out_specs=[choice(), choice(), choice(), pl.BlockSpec((ne, 1), lambda i: (0, 0))],
        scratch_shapes=[pltpu.VMEM((ne, 1), F32)],
        compiler_params=_params("arbitrary"),
        name="group_topk_router",
    )(logits_t, bias_col)


def _slots_kernel(idx_ref, rank_ref, ps_ref, o_ref):
    ne, tn = ps_ref.shape[0], idx_ref.shape[1]
    eidx = lax.broadcasted_iota(jnp.int32, (ne, tn), 0)
    for j in range(TOP_K):
        start = jnp.sum(jnp.where(eidx == idx_ref[j:j + 1, :], ps_ref[...], 0.0), axis=0, keepdims=True)
        o_ref[j:j + 1, :] = start.astype(jnp.int32) + rank_ref[j:j + 1, :]


def _slot_positions(idx8, rank8, pstart_col):
    _, n = idx8.shape
    ne = pstart_col.shape[0]
    tn = 1024 if n % 1024 == 0 else n
    choice = lambda: pl.BlockSpec((TOP_K, tn), lambda i: (0, i))
    return pl.pallas_call(
        _slots_kernel,
        out_shape=jax.ShapeDtypeStruct((TOP_K, n), jnp.int32),
        grid=(n // tn,),
        in_specs=[choice(), choice(), pl.BlockSpec((ne, 1), lambda i: (0, 0))],
        out_specs=choice(),
        compiler_params=_params("parallel"),
        name="expert_slot_positions",
    )(idx8, rank8, pstart_col)


def _slot_table_kernel(pos_ref, last_blk_ref, tail_ref, tok_ref, *, bm):
    def clear_group(base):
        for i in range(SUBLANES):
            tok_ref[base + i] = 0

    def per_expert(e, carry):
        def clear(g, c2):
            clear_group(last_blk_ref[e] + g * SUBLANES)
            return c2

        lax.fori_loop(0, bm // SUBLANES, clear, 0)
        return carry

    lax.fori_loop(0, last_blk_ref.shape[0], per_expert, 0)

    tail = tail_ref[0]

    def clear_tail(g, carry):
        clear_group(tail + g * SUBLANES)
        return carry

    lax.fori_loop(0, (tok_ref.shape[0] - tail) // SUBLANES, clear_tail, 0)

    def fill(t, carry):
        for k in range(TOP_K):
            tok_ref[pos_ref[k, t]] = t
        return carry

    lax.fori_loop(0, pos_ref.shape[1], fill, 0, unroll=4)


def _slot_table(pos8, last_blk_start, tail_start, n_slots, bm):
    smem = lambda: pl.BlockSpec(memory_space=pltpu.SMEM)
    return pl.pallas_call(
        functools.partial(_slot_table_kernel, bm=bm),
        out_shape=jax.ShapeDtypeStruct((n_slots,), jnp.int32),
        in_specs=[smem(), smem(), smem()],
        out_specs=smem(),
        name="expert_slot_tokens",
    )(pos8, last_blk_start, tail_start)


def _expert_ffn_kernel(be_ref, nu_ref, nx_ref, sl_ref, tok_ref, tokn_ref, h_hbm, wg_hbm, wu_hbm, wd_hbm,
                       y_ref, xbuf0, xbuf1, wgf, wuf, wdf, wgb, wub, wdb, sems, xsems):
    j = pl.program_id(0)
    e = be_ref[j]
    n_used = nu_ref[0]
    used = j < n_used
    fresh = jnp.logical_or(j == 0, e != be_ref[jnp.maximum(j - 1, 0)])
    bm = xbuf0.shape[0]
    xbufs = (xbuf0, xbuf1)

    def gather_rows(toks, slot):
        for r in range(bm):
            pltpu.make_async_copy(h_hbm.at[pl.ds(toks[r], 1)], xbufs[slot].at[pl.ds(r, 1)],
                                  xsems.at[slot]).start(priority=r % 2)

    def wait_rows(slot):
        pltpu.make_async_copy(h_hbm.at[pl.ds(0, bm)], xbufs[slot], xsems.at[slot]).wait()

    @pl.when(j == 0)
    def _():
        gather_rows(tok_ref, 0)

    def weight_copies(expert, slot):
        return [pltpu.make_async_copy(src.at[expert], dst.at[slot], sems.at[slot])
                for src, dst in ((wg_hbm, wgf), (wu_hbm, wuf), (wd_hbm, wdf))]

    @pl.when(j == 0)
    def _():
        for cp in weight_copies(e, sl_ref[e]):
            cp.start()

    @pl.when(jnp.logical_and(used, fresh))
    def _():
        slot = sl_ref[e]
        for cp in weight_copies(e, slot):
            cp.wait()
        nxt = nx_ref[e]

        @pl.when(nxt >= 0)
        def _():
            for cp in weight_copies(nxt, 1 - slot):
                cp.start()

        wgb[...] = _bf(wgf[slot])
        wub[...] = _bf(wuf[slot])
        wdb[...] = _bf(wdf[slot])

    for parity in range(2):
        @pl.when(jnp.logical_and(used, j % 2 == parity))
        def _(parity=parity):
            wait_rows(parity)
            gather_rows(tokn_ref, 1 - parity)
            x = _bf(xbufs[parity][...])
            a = _dot(x, wgb[...])
            u = _dot(x, wub[...])
            y_ref[...] = _dot(_bf(a * jax.nn.sigmoid(a) * u), wdb[...])

        @pl.when(jnp.logical_and(j == n_used - 1, j % 2 == parity))
        def _(parity=parity):
            wait_rows(1 - parity)

    @pl.when(jnp.logical_not(used))
    def _():
        y_ref[...] = jnp.zeros_like(y_ref)


def _expert_ffn(h2, slot_tok, blk_expert, n_used, next_expert, expert_slot, e_gate, e_up, e_down, bm):
    n_slots = slot_tok.shape[0]
    _, d = h2.shape
    ne, _, ff = e_gate.shape
    n_blk = n_slots // bm
    grid_spec = pltpu.PrefetchScalarGridSpec(
        num_scalar_prefetch=4,
        grid=(n_blk,),
        in_specs=[pl.BlockSpec((bm,), lambda j, be, nu, nx, sl: (jnp.minimum(j, nu[0] - 1),),
                               memory_space=pltpu.SMEM),
                  pl.BlockSpec((bm,), lambda j, be, nu, nx, sl: (jnp.minimum(j + 1, nu[0] - 1),),
                               memory_space=pltpu.SMEM),
                  pl.BlockSpec(memory_space=pl.ANY),
                  pl.BlockSpec(memory_space=pl.ANY),
                  pl.BlockSpec(memory_space=pl.ANY),
                  pl.BlockSpec(memory_space=pl.ANY)],
        out_specs=pl.BlockSpec((bm, d), lambda j, be, nu, nx, sl: (j, 0)),
        scratch_shapes=[pltpu.VMEM((bm, d), F32), pltpu.VMEM((bm, d), F32),
                        pltpu.VMEM((2, d, ff), F32), pltpu.VMEM((2, d, ff), F32), pltpu.VMEM((2, ff, d), F32),
                        pltpu.VMEM((d, ff), BF16), pltpu.VMEM((d, ff), BF16), pltpu.VMEM((ff, d), BF16),
                        pltpu.SemaphoreType.DMA((2,)), pltpu.SemaphoreType.DMA((2,))])
    return pl.pallas_call(
        _expert_ffn_kernel,
        out_shape=jax.ShapeDtypeStruct((n_slots, d), F32),
        grid_spec=grid_spec,
        compiler_params=_params("arbitrary"),
        name="routed_expert_ffn",
    )(blk_expert, n_used, next_expert, expert_slot, slot_tok, slot_tok, h2, e_gate, e_up, e_down)


def _final_kernel(pos_ref, h_ref, x1_ref, g_ref, ys_ref, wg_ref, wu_ref, wd_ref, m_ref, fg_ref,
                  o_ref, buf, sem):
    tr = h_ref.shape[1]

    def issue(t, carry):
        for k in range(TOP_K):
            pltpu.make_async_copy(ys_ref.at[pl.ds(pos_ref[k, t], 1)], buf.at[k, pl.ds(t, 1)],
                                  sem).start(priority=k % 2)
        return carry

    lax.fori_loop(0, tr, issue, 0)
    h = _bf(h_ref[0])
    a = _dot(h, wg_ref[...])
    u = _dot(h, wu_ref[...])
    moe = _dot(_bf(a * jax.nn.sigmoid(a) * u), wd_ref[...])
    g = g_ref[...]
    for k in range(TOP_K):
        pltpu.make_async_copy(ys_ref.at[pl.ds(0, tr)], buf.at[k], sem).wait()
    for k in range(TOP_K):
        moe = moe + g[:, k:k + 1] * buf[k]
    x2 = x1_ref[0] + m_ref[0, 3:4, :] * moe
    ms = jnp.mean(x2 * x2, axis=-1, keepdims=True)
    o_ref[0] = x2 * lax.rsqrt(ms + NORM_EPS) * fg_ref[...]


def _final(h2, x1, pos8, gate_rows, ys, s_gate_b, s_up_b, s_down_b, m2, final_g, tr):
    b, t, d = x1.shape
    ff = s_gate_b.shape[1]
    nt = t // tr
    row = lambda: pl.BlockSpec((1, tr, d), lambda bb, i: (bb, i, 0))
    return pl.pallas_call(
        _final_kernel,
        out_shape=jax.ShapeDtypeStruct((b, t, d), F32),
        grid=(b, nt),
        in_specs=[pl.BlockSpec((TOP_K, tr), lambda bb, i: (0, bb * nt + i), memory_space=pltpu.SMEM),
                  row(), row(),
                  pl.BlockSpec((tr, TOP_K), lambda bb, i: (bb * nt + i, 0)),
                  pl.BlockSpec(memory_space=pl.ANY),
                  pl.BlockSpec((d, ff), lambda bb, i: (0, 0)),
                  pl.BlockSpec((d, ff), lambda bb, i: (0, 0)),
                  pl.BlockSpec((ff, d), lambda bb, i: (0, 0)),
                  pl.BlockSpec((1, 4, d), lambda bb, i: (bb, 0, 0)),
                  pl.BlockSpec((1, d), lambda bb, i: (0, 0))],
        out_specs=row(),
        scratch_shapes=[pltpu.VMEM((TOP_K, tr, d), F32), pltpu.SemaphoreType.DMA(())],
        compiler_params=_params("parallel", "parallel"),
        name="combine_shared_expert_final_norm",
    )(pos8, h2, x1, gate_rows, ys, s_gate_b, s_up_b, s_down_b, m2, final_g)


def _pad_to(a, axis, n):
    pad = [(0, 0)] * a.ndim
    pad[axis] = (0, n - a.shape[axis])
    return jnp.pad(a, pad)


def _layout(rd, wl, al, gl, ad, kvd):
    assert wl <= LANES and al <= LANES
    lay = {"rd": rd, "gl": gl, "ad": ad, "kvd": kvd}
    off = 0
    for name, width in (("q", ad), ("r", rd), ("k", rd), ("v", rd), ("ka", kvd), ("va", kvd),
                        ("xw", LANES), ("xa", LANES), ("xg", gl)):
        assert width % LANES == 0
        lay[name] = off
        off += width
    lay["ncols"] = off
    return lay


def kernel(x, c, ctx, c_ctx, w_mod, b_mod, norm1_g, norm2_g, w_in, shift_mu, k_k, k_a, r_k, decay_bias, decay_up, iclr_bias, iclr_up, gate_up, lnx_g, lnx_b, attn_sink, w_out, w_router, expert_bias, e_gate, e_up, e_down, s_gate, s_up, s_down, final_g):
    assert w_mod.shape[0] == 1, "single-layer block only"
    b, t, d = x.shape
    ctx_len = ctx.shape[1]
    rh, hd = r_k.shape[1:]
    assert hd == HEAD_DIM and b + 1 <= SUBLANES
    rd = rh * hd
    wl, al, gl = decay_up.shape[2], iclr_up.shape[2], gate_up.shape[1]
    ad = attn_sink.shape[1] * hd
    kvd = (w_in.shape[2] - (3 * rd + wl + al + gl) - ad) // 2
    ne = w_router.shape[2]
    lay = _layout(rd, wl, al, gl, ad, kvd)
    tr = 256 if ctx_len % 256 == 0 and t % 256 == 0 else CHUNK
    assert ctx_len % CHUNK == 0 and t % CHUNK == 0 and t % QBLOCK == 0 and WINDOW == QBLOCK

    c8 = _pad_to(jnp.concatenate([c, c_ctx[None]], axis=0), 0, SUBLANES)
    mod = _modulation(c8, w_mod[0], b_mod[0][None])
    mod6 = mod.reshape(SUBLANES, 6, d)
    mx, mc = mod6[:b], mod6[b]
    m1 = jnp.stack([jnp.broadcast_to(mc[None, 0:2], (b, 2, d)), mx[:, 0:2]], axis=1)
    m2 = jnp.stack([mx[:, 2], mx[:, 3], mx[:, 4], mx[:, 5]], axis=1)

    wi = w_in[0]
    o_w, o_a, o_g, o_q = 3 * rd, 3 * rd + wl, 3 * rd + wl + al, 3 * rd + wl + al + gl
    w_in_p = _bf(jnp.concatenate([
        wi[:, o_q:o_q + ad], wi[:, :3 * rd], wi[:, o_q + ad:],
        _pad_to(wi[:, o_w:o_a], 1, LANES), _pad_to(wi[:, o_a:o_g], 1, LANES), wi[:, o_g:o_q]], axis=1))
    mu = shift_mu[0]
    prm = {
        "mu_r": mu[None, :rd], "mu_k": mu[None, rd:2 * rd], "mu_v": mu[None, 2 * rd:3 * rd],
        "mu_w": _pad_to(mu[None, o_w:o_a], 1, LANES), "mu_a": _pad_to(mu[None, o_a:o_g], 1, LANES),
        "mu_g": mu[None, o_g:o_q],
        "k_k": k_k, "k_a": k_a, "r_k": r_k.reshape(1, rd),
        "decay_bias": decay_bias[0], "decay_up": _bf(_pad_to(decay_up[0], 1, LANES)),
        "iclr_bias": iclr_bias[0], "iclr_up": _bf(_pad_to(iclr_up[0], 1, LANES)),
        "gate_up": _bf(gate_up[0]),
    }

    p_all = _in_projection(ctx, x, norm1_g, m1, w_in_p, tr)

    npair = rd // LANES
    pp = max(n for n in (1, 2, 4, 8) if npair % n == 0
             and all(lay[key] % (n * LANES) == 0 for key in ("r", "k", "v")))
    rp, y0, mm, nn, bonus, gate = _wkv_prep(p_all, lay, prm, ctx_len, None, pp)
    y = _wkv_sweep(rp, y0, mm, nn, ctx_len)

    nf = HEAD_DIM // 4
    freqs = ROPE_BASE ** (-jnp.arange(nf, dtype=F32) / nf)
    pos = jnp.arange(t, dtype=jnp.int32)
    ang_r = (pos // GRID_W).astype(F32)[:, None] * freqs
    ang_c = (pos % GRID_W).astype(F32)[:, None] * freqs
    cos_h = jnp.concatenate([jnp.cos(ang_r)] * 2 + [jnp.cos(ang_c)] * 2, axis=1)
    sin_h = jnp.concatenate([-jnp.sin(ang_r), jnp.sin(ang_r), -jnp.sin(ang_c), jnp.sin(ang_c)], axis=1)
    cos_t = jnp.concatenate([cos_h, cos_h], axis=1)
    sin_t = jnp.concatenate([sin_h, sin_h], axis=1)
    attn = _attention(p_all, lay, cos_t, sin_t, attn_sink, ctx_len, t)

    x1, h2, logits_t = _out_projection(y, bonus, gate, attn, x, lnx_g, lnx_b, _bf(w_out[0]), m2,
                                       norm2_g, w_router[0], ctx_len, tr)
    idx8, gate8, rank8, counts = _router(logits_t, expert_bias[0][:, None])

    n_tok = b * t
    bm = EXPERT_ROWS
    assert (n_tok * TOP_K) % bm == 0
    n_blk = n_tok * TOP_K // bm + ne
    cnt = counts[:, 0].astype(jnp.int32)
    padded = (cnt + bm - 1) // bm * bm
    pend = jnp.cumsum(padded)
    pstart = pend - padded
    pos8 = _slot_positions(idx8, rank8, pstart.astype(F32)[:, None])
    blk_row0 = jnp.arange(n_blk, dtype=jnp.int32) * bm
    blk_expert = jnp.minimum(jnp.sum((pend[None, :] <= blk_row0[:, None]).astype(jnp.int32), axis=1), ne - 1)
    n_used = (pend[-1:] // bm).astype(jnp.int32)

    slot_tok = _slot_table(pos8, jnp.maximum(pend - bm, 0), pend[-1:], n_blk * bm, bm)
    has_rows = cnt > 0
    eids = jnp.arange(ne, dtype=jnp.int32)
    later = jnp.where(has_rows[None, :] & (eids[None, :] > eids[:, None]), eids[None, :], ne)
    next_expert = jnp.min(later, axis=1)
    next_expert = jnp.where(next_expert < ne, next_expert, -1).astype(jnp.int32)
    expert_slot = ((jnp.cumsum(has_rows.astype(jnp.int32)) - 1) % 2).astype(jnp.int32)
    ys = _expert_ffn(h2.reshape(n_tok, d), slot_tok, blk_expert, n_used, next_expert, expert_slot,
                     e_gate[0], e_up[0], e_down[0], bm)
    return _final(h2, x1, pos8, gate8.T, ys, _bf(s_gate[0]), _bf(s_up[0]), _bf(s_down[0]),
                  m2, final_g[None], tr)
```

```python
import functools
import math

import jax
import jax.numpy as jnp
from jax import lax
from jax.experimental import pallas as pl
from jax.experimental.pallas import tpu as pltpu

GRID_W = 64
WINDOW = 128
ROPE_BASE = 10000.0
N_GROUPS = 8
TOPK_GROUPS = 4
TOP_K = 8
ROUTED_SCALE = 2.5
NORM_EPS = 1e-6
LNX_EPS = 64e-5
KK_EPS = 1e-12

LANES = 128
SUBLANES = 8
VMEM_LIMIT_BYTES = 56 * 1024 * 1024

HEAD_DIM = 64
CHUNK = 64
QBLOCK = 128
EXPERT_ROWS = 256
ROW_DMA_PRIORITY = 0
WEIGHT_DMA_PRIORITY = 1

F32 = jnp.float32
BF16 = jnp.bfloat16
NEG_BIG = -1e30


def _bf(x):
    return x.astype(BF16)


def _dot(a, b, precision=None):
    return jnp.dot(a, b, preferred_element_type=F32, precision=precision)


def _dot_nt(a, b, precision=None):
    return lax.dot_general(a, b, (((1,), (1,)), ((), ())), preferred_element_type=F32,
                           precision=precision)


def _dot_tn(a, b):
    return lax.dot_general(a, b, (((0,), (0,)), ((), ())), preferred_element_type=F32)


def _params(*sem):
    return pltpu.CompilerParams(dimension_semantics=sem, vmem_limit_bytes=VMEM_LIMIT_BYTES)


def _mod_kernel(c_ref, w_ref, b_ref, o_ref):
    c = c_ref[...]
    s = c * jax.nn.sigmoid(c)
    hi = _bf(s)
    lo = _bf(s - hi.astype(F32))
    r = _dot(jnp.concatenate([hi, lo], axis=0), _bf(w_ref[...]))
    o_ref[...] = r[:SUBLANES] + r[SUBLANES:] + b_ref[...]


def _modulation(c8, w_mod, b_mod):
    d, n = w_mod.shape
    tn = 1024 if n % 1024 == 0 else n
    return pl.pallas_call(
        _mod_kernel,
        out_shape=jax.ShapeDtypeStruct((SUBLANES, n), F32),
        grid=(n // tn,),
        in_specs=[pl.BlockSpec((SUBLANES, d), lambda j: (0, 0)),
                  pl.BlockSpec((d, tn), lambda j: (0, j)),
                  pl.BlockSpec((1, tn), lambda j: (0, j))],
        out_specs=pl.BlockSpec((SUBLANES, tn), lambda j: (0, j)),
        compiler_params=_params("parallel"),
        name="adaln_modulation",
    )(c8, w_mod, b_mod)


def _inproj_kernel(c_ref, x_ref, g_ref, m_ref, w_ref, o_ref, *, nct):
    x = jnp.where(pl.program_id(2) < nct, c_ref[0], x_ref[0])
    ms = jnp.mean(x * x, axis=-1, keepdims=True)
    y = x * lax.rsqrt(ms + NORM_EPS) * g_ref[...]
    m = m_ref[0, 0]
    h = y * (1.0 + m[1:2]) + m[0:1]
    o_ref[0] = _dot(_bf(h), w_ref[...])


def _in_projection(ctx, x, g1, m1, w_in_p, tr):
    b, t, d = x.shape
    ctx_len = ctx.shape[1]
    ta = ctx_len + t
    ncols = w_in_p.shape[1]
    tn = ncols // 4 if (ncols // 4) % LANES == 0 and ncols % 4 == 0 else ncols
    nct = ctx_len // tr
    return pl.pallas_call(
        functools.partial(_inproj_kernel, nct=nct),
        out_shape=jax.ShapeDtypeStruct((b, ta, ncols), F32),
        grid=(ncols // tn, b, ta // tr),
        in_specs=[pl.BlockSpec((1, tr, d), lambda n, bb, i: (bb, jnp.minimum(i, nct - 1), 0)),
                  pl.BlockSpec((1, tr, d), lambda n, bb, i: (bb, jnp.maximum(i - nct, 0), 0)),
                  pl.BlockSpec((1, d), lambda n, bb, i: (0, 0)),
                  pl.BlockSpec((1, 1, 2, d), lambda n, bb, i: (bb, jnp.where(i >= nct, 1, 0), 0, 0)),
                  pl.BlockSpec((d, tn), lambda n, bb, i: (0, n))],
        out_specs=pl.BlockSpec((1, tr, tn), lambda n, bb, i: (bb, i, n)),
        compiler_params=_params("parallel", "parallel", "parallel"),
        name="norm_modulate_inproj",
    )(ctx, x, g1, m1, w_in_p)


def _wkv_prep_kernel(r_c, r_p, r_n, k_c, k_p, k_n, v_c, v_p, v_n, w_c, w_p, w_n, a_c, a_p, a_n,
                     g_c, g_p, g_n, mu_r, mu_k, mu_v, mu_w, mu_a, mu_g, kk_ref, ka_ref, rk_ref,
                     db_ref, du_ref, ib_ref, iu_ref, gu_ref,
                     rp_out, y0_out, m_out, n_out, bonus_out, gate_out,
                     *, nctx_chunks, n_chunks, prec_t):
    L = CHUNK
    c = pl.program_id(1)
    has_prev = jnp.logical_and(c != 0, c != nctx_chunks)
    has_next = jnp.logical_and(c != nctx_chunks - 1, c != n_chunks - 1)

    def shifted(cur, prv, nxt, mu):
        x = cur[0]
        rows = lax.broadcasted_iota(jnp.int32, (L, 1), 0)
        pr = jnp.where(has_prev, prv[0, SUBLANES - 1:SUBLANES, :], 0.0)
        nx = jnp.where(has_next, nxt[0, 0:1, :], 0.0)
        up = jnp.where(rows == 0, pr, pltpu.roll(x, 1, 0))
        dn = jnp.where(rows == L - 1, nx, pltpu.roll(x, L - 1, 0))
        return x + mu[...] * (0.5 * (up + dn) - x)

    r = shifted(r_c, r_p, r_n, mu_r)
    k = shifted(k_c, k_p, k_n, mu_k)
    v = shifted(v_c, v_p, v_n, mu_v)
    xw = shifted(w_c, w_p, w_n, mu_w)
    xa = shifted(a_c, a_p, a_n, mu_a)
    xg = shifted(g_c, g_p, g_n, mu_g)

    lane = lax.broadcasted_iota(jnp.int32, (1, LANES), 1)
    m0 = lane < HEAD_DIM

    def head_sum(x):
        s0 = jnp.sum(jnp.where(m0, x, 0.0), axis=-1, keepdims=True)
        s1 = jnp.sum(jnp.where(m0, 0.0, x), axis=-1, keepdims=True)
        return jnp.where(m0, s0, s1)

    def stack2(x):
        return jnp.concatenate([jnp.where(m0, x, 0.0), jnp.where(m0, 0.0, x)], axis=0)

    tw = _bf(jnp.tanh(xw))
    xab = _bf(xa)
    lw_all = [(-math.exp(-0.5)) * jax.nn.sigmoid(db_ref[d:d + 1, :] + _dot(tw, du_ref[d]))
              for d in range(2)]
    sa_all = [jax.nn.sigmoid(ib_ref[d:d + 1, :] + _dot(xab, iu_ref[d])) for d in range(2)]
    gate_out[0] = _dot(_bf(jax.nn.sigmoid(xg)), gu_ref[...])

    ri = lax.broadcasted_iota(jnp.int32, (L, L), 0)
    ci = lax.broadcasted_iota(jnp.int32, (L, L), 1)
    ri2 = lax.broadcasted_iota(jnp.int32, (2 * L, 2 * L), 0)
    ci2 = lax.broadcasted_iota(jnp.int32, (2 * L, 2 * L), 1)
    rt = jnp.bitwise_and(ri2, L - 1)
    ct = jnp.bitwise_and(ci2, L - 1)
    eye = ri2 == ci2
    n_sq = int(math.log2(L)) - 1

    tri = [(ci <= ri).astype(F32), (ci >= ri).astype(F32)]
    strict = [ct < rt, ct > rt]
    incl = [ct <= rt, ct >= rt]
    cum = [_dot(tri[d], lw_all[d], precision=lax.Precision.HIGHEST) for d in range(2)]
    cend = [jnp.sum(lw_all[d], axis=0, keepdims=True) for d in range(2)]
    e_in = [jnp.exp(cum[d]) for d in range(2)]
    e_ex = [jnp.exp(cum[d] - lw_all[d]) for d in range(2)]
    e_neg = [jnp.exp(-cum[d]) for d in range(2)]
    e_end = [jnp.exp(cend[d] - cum[d]) for d in range(2)]
    p_end = [jnp.exp(cend[d]) for d in range(2)]
    kd_all = [k * (1.0 + (sa_all[d] - 1.0) * ka_ref[...]) for d in range(2)]

    npp = r.shape[1] // LANES
    chains = [(p, d) for p in range(npp) for d in range(2)]
    sls = [slice(p * LANES, (p + 1) * LANES) for p in range(npp)]
    kk_p, vs_p, vsb_p = [], [], []
    for p in range(npp):
        kx = k[:, sls[p]] * kk_ref[:, sls[p]]
        kk_p.append(kx * lax.rsqrt(head_sum(kx * kx) + KK_EPS))
        vs_p.append(stack2(v[:, sls[p]]))
        vsb_p.append(_bf(vs_p[p]))
        ksum = kd_all[0][:, sls[p]] + kd_all[1][:, sls[p]]
        bonus_out[0, :, sls[p]] = head_sum(r[:, sls[p]] * ksum * rk_ref[:, sls[p]]) * v[:, sls[p]]

    a_s, r_s, bh_s, kh_s, g = [], [], [], [], []
    for p, d in chains:
        sl = sls[p]
        kd = kd_all[d][:, sl]
        bvec = kk_p[p] * sa_all[d][:, sl]
        a_s.append(stack2(-kk_p[p] * e_ex[d][:, sl]))
        r_s.append(stack2(r[:, sl] * e_in[d][:, sl]))
        b_s = stack2(bvec * e_neg[d][:, sl])
        k_s = stack2(kd * e_neg[d][:, sl])
        bh_s.append(stack2(bvec * e_end[d][:, sl]))
        kh_s.append(stack2(kd * e_end[d][:, sl]))
        g.append(_dot_nt(_bf(jnp.concatenate([a_s[-1], r_s[-1]], axis=0)),
                         _bf(jnp.concatenate([b_s, k_s], axis=0))))
    a_ab = [jnp.where(strict[d], g[i][:2 * L, :2 * L], 0.0) for i, (p, d) in enumerate(chains)]
    a_ak = [jnp.where(strict[d], g[i][:2 * L, 2 * L:], 0.0) for i, (p, d) in enumerate(chains)]
    a_rb = [jnp.where(incl[d], g[i][2 * L:, :2 * L], 0.0) for i, (p, d) in enumerate(chains)]
    a_rk = [jnp.where(incl[d], g[i][2 * L:, 2 * L:], 0.0) for i, (p, d) in enumerate(chains)]

    ident = jnp.where(eye, 1.0, 0.0)
    tmat = [ident + a for a in a_ab]
    apow = a_ab
    for _ in range(n_sq):
        if prec_t is None:
            apow = [_dot(_bf(a), _bf(a)) for a in apow]
            tmat = [t + _dot(_bf(a), _bf(t)) for a, t in zip(apow, tmat)]
        else:
            apow = [_dot(a, a, precision=prec_t) for a in apow]
            tmat = [t + _dot(a, t, precision=prec_t) for a, t in zip(apow, tmat)]

    x1 = [_dot(_bf(a_ak[i]), vsb_p[p]) for i, (p, d) in enumerate(chains)]
    y0v = [_dot(_bf(a_rk[i]), vsb_p[p]) for i, (p, d) in enumerate(chains)]
    tu = [_dot(_bf(tmat[i]), _bf(jnp.concatenate([a_s[i], x1[i]], axis=1)))
          for i in range(len(chains))]
    ry = [_dot(_bf(a_rb[i]), _bf(tu[i])) for i in range(len(chains))]
    def fold(z):
        h = z.shape[0] // 2
        return z[:h] + z[h:]

    for i, (p, d) in enumerate(chains):
        rp_out[d, 0, 0, p] = _bf(fold(r_s[i] + ry[i][:, :LANES]))
        y0_out[d, 0, 0, p] = fold(ry[i][:, LANES:] + y0v[i])
    for i, (p, d) in enumerate(chains):
        m_out[d, 0, 0, p] = _bf(fold(jnp.where(eye, p_end[d][:, sls[p]], 0.0)
                                     + _dot_tn(_bf(tu[i][:, :LANES]), _bf(bh_s[i]))))
    for i, (p, d) in enumerate(chains):
        n_out[d, 0, 0, p] = fold(_dot_tn(_bf(jnp.concatenate([tu[i][:, LANES:], vs_p[p]], axis=0)),
                                         _bf(jnp.concatenate([bh_s[i], kh_s[i]], axis=0))))


def _wkv_prep(p_all, lay, prm, ctx_len, prec_t, pp):
    b, ta, _ = p_all.shape
    L = CHUNK
    nc = ta // L
    nctx = ctx_len // L
    npair = lay["rd"] // LANES
    gl = lay["gl"]
    rows8 = ta // SUBLANES

    def cur(colfn, width=LANES):
        return pl.BlockSpec((1, L, width), lambda bb, c, p: (bb, c, colfn(p)))

    def prv(colfn, width=LANES):
        return pl.BlockSpec((1, SUBLANES, width),
                            lambda bb, c, p: (bb, jnp.maximum(c * (L // SUBLANES) - 1, 0), colfn(p)))

    def nxt(colfn, width=LANES):
        return pl.BlockSpec((1, SUBLANES, width),
                            lambda bb, c, p: (bb, jnp.minimum((c + 1) * (L // SUBLANES), rows8 - 1), colfn(p)))

    def trio(colfn, width=LANES):
        return [cur(colfn, width), prv(colfn, width), nxt(colfn, width)]

    pw = pp * LANES
    assert npair % pp == 0 and all(lay[n] % pw == 0 for n in ("r", "k", "v"))
    rb, kb, vb = lay["r"] // pw, lay["k"] // pw, lay["v"] // pw
    wb, ab = lay["xw"] // LANES, lay["xa"] // LANES
    assert lay["xg"] % gl == 0
    gb = lay["xg"] // gl

    def vec(width=pw):
        return pl.BlockSpec((1, width), lambda bb, c, p: (0, p))

    def vec0(width=LANES):
        return pl.BlockSpec((1, width), lambda bb, c, p: (0, 0))

    in_specs = (trio(lambda p: rb + p, pw) + trio(lambda p: kb + p, pw) + trio(lambda p: vb + p, pw)
                + trio(lambda p: wb) + trio(lambda p: ab) + trio(lambda p: gb, gl)
                + [vec(), vec(), vec(), vec0(), vec0(), vec0(gl),
                   vec(), vec(), vec(),
                   pl.BlockSpec((2, pw), lambda bb, c, p: (0, p)),
                   pl.BlockSpec((2, LANES, pw), lambda bb, c, p: (0, 0, p)),
                   pl.BlockSpec((2, pw), lambda bb, c, p: (0, p)),
                   pl.BlockSpec((2, LANES, pw), lambda bb, c, p: (0, 0, p)),
                   pl.BlockSpec((gl, pw), lambda bb, c, p: (0, p))])

    def opspec(rows):
        return pl.BlockSpec((2, 1, 1, pp, rows, LANES), lambda bb, c, p: (0, c, bb, p, 0, 0))

    out_specs = [opspec(L), opspec(L), opspec(HEAD_DIM), opspec(HEAD_DIM),
                 pl.BlockSpec((1, L, pw), lambda bb, c, p: (bb, c, p)),
                 pl.BlockSpec((1, L, pw), lambda bb, c, p: (bb, c, p))]
    out_shape = [jax.ShapeDtypeStruct((2, nc, b, npair, L, LANES), BF16),
                 jax.ShapeDtypeStruct((2, nc, b, npair, L, LANES), F32),
                 jax.ShapeDtypeStruct((2, nc, b, npair, HEAD_DIM, LANES), BF16),
                 jax.ShapeDtypeStruct((2, nc, b, npair, HEAD_DIM, LANES), F32),
                 jax.ShapeDtypeStruct((b, ta, lay["rd"]), F32),
                 jax.ShapeDtypeStruct((b, ta, lay["rd"]), F32)]
    kern = functools.partial(_wkv_prep_kernel, nctx_chunks=nctx, n_chunks=nc, prec_t=prec_t)
    args = [p_all] * 18 + [prm["mu_r"], prm["mu_k"], prm["mu_v"], prm["mu_w"], prm["mu_a"], prm["mu_g"],
                           prm["k_k"], prm["k_a"], prm["r_k"], prm["decay_bias"], prm["decay_up"],
                           prm["iclr_bias"], prm["iclr_up"], prm["gate_up"]]
    return pl.pallas_call(
        kern, out_shape=out_shape, grid=(b, nc, npair // pp), in_specs=in_specs, out_specs=out_specs,
        compiler_params=_params("parallel", "parallel", "parallel"),
        name="wkv_chunk_operators",
    )(*args)


def _wkv_seq_kernel(rp_ref, y0_ref, m_ref, n_ref, y_ref, s_ref, *, nb, npair):
    L = CHUNK

    @pl.when(pl.program_id(1) == 0)
    def _():
        s_ref[...] = jnp.zeros_like(s_ref)

    m0 = lax.broadcasted_iota(jnp.int32, (1, LANES), 1) < HEAD_DIM

    def unfold(z):
        zero = jnp.zeros_like(z)
        return jnp.concatenate([jnp.where(m0, z, zero), jnp.where(m0, zero, z)], axis=0)

    for bb in range(nb):
        for p in range(npair):
            i = bb * npair + p
            s0 = _bf(s_ref[i])
            ys = _dot_nt(unfold(rp_ref[0, 0, bb, p]), s0)
            y_ref[0, bb, :, p * LANES:(p + 1) * LANES] = ys[:L] + ys[L:] + y0_ref[0, 0, bb, p]
            s_ref[i] = _dot(s0, unfold(m_ref[0, 0, bb, p])) + unfold(n_ref[0, 0, bb, p])


def _wkv_sweep(rp, y0, mm, nn, ctx_len):
    _, nc, b, npair, _, _ = rp.shape
    L = CHUNK
    nctx = ctx_len // L

    def chunk_of(d, c):
        rev = jnp.where(c < nctx, nctx - 1 - c, nc - 1 - (c - nctx))
        return jnp.where(d == 0, c, rev)

    def opspec(rows):
        return pl.BlockSpec((1, 1, b, npair, rows, LANES), lambda d, c: (d, chunk_of(d, c), 0, 0, 0, 0))

    return pl.pallas_call(
        functools.partial(_wkv_seq_kernel, nb=b, npair=npair),
        out_shape=jax.ShapeDtypeStruct((2, b, nc * L, npair * LANES), F32),
        grid=(2, nc),
        in_specs=[opspec(L), opspec(L), opspec(HEAD_DIM), opspec(HEAD_DIM)],
        out_specs=pl.BlockSpec((1, b, L, npair * LANES), lambda d, c: (d, 0, chunk_of(d, c), 0)),
        scratch_shapes=[pltpu.VMEM((b * npair, LANES, LANES), F32)],
        compiler_params=_params("parallel", "arbitrary"),
        name="wkv_state_sweep",
    )(rp, y0, mm, nn)


def _attn_kernel(q_ref, kp_ref, kc_ref, kn_ref, vp_ref, vc_ref, vn_ref, kx_ref, vx_ref,
                 cq_ref, sq_ref, cp_ref, sp_ref, cn_ref, sn_ref, sink_ref, o_ref,
                 *, n_blocks, kv_heads, group):
    n = pl.program_id(1)
    qb = QBLOCK

    def rope(x, cos, sin):
        w = x.shape[1]
        reps = w // LANES
        cs = jnp.concatenate([cos] * reps, axis=1) if reps > 1 else cos
        sn = jnp.concatenate([sin] * reps, axis=1) if reps > 1 else sin
        lane = lax.broadcasted_iota(jnp.int32, (1, w), 1)
        sw = jnp.where(jnp.bitwise_and(lane, 16) == 0, pltpu.roll(x, w - 16, 1), pltpu.roll(x, 16, 1))
        return x * cs + sw * sn

    q = rope(q_ref[0], cq_ref[...], sq_ref[...]) * (HEAD_DIM ** -0.5)
    kwin = jnp.concatenate([rope(kp_ref[0], cp_ref[...], sp_ref[...]),
                            rope(kc_ref[0], cq_ref[...], sq_ref[...]),
                            rope(kn_ref[0], cn_ref[...], sn_ref[...]),
                            kx_ref[0]], axis=0)
    vwin = jnp.concatenate([vp_ref[0], vc_ref[0], vn_ref[0], vx_ref[0]], axis=0)
    nkeys = kwin.shape[0]

    rows = group * qb
    qi = jnp.bitwise_and(lax.broadcasted_iota(jnp.int32, (rows, nkeys), 0), qb - 1)
    s_i = lax.broadcasted_iota(jnp.int32, (rows, nkeys), 1)
    blk = s_i // qb + (n - 1)
    in_win = (s_i >= qi) & (s_i <= qi + 2 * WINDOW) & (blk >= 0) & (blk < n_blocks)
    valid = in_win | (s_i >= 3 * qb)
    rgrp = lax.broadcasted_iota(jnp.int32, (rows, 1), 0) // qb

    scores, sinks = [], []
    for kh in range(kv_heads):
        kk_ = _bf(kwin[:, kh * HEAD_DIM:(kh + 1) * HEAD_DIM])
        qs = jnp.concatenate([q[:, (kh * group + g) * HEAD_DIM:(kh * group + g + 1) * HEAD_DIM]
                              for g in range(group)], axis=0)
        sk = jnp.zeros((rows, 1), F32)
        for g in range(group):
            h = kh * group + g
            sk = jnp.where(rgrp == g, sink_ref[0:1, h:h + 1], sk)
        sinks.append(sk)
        scores.append(jnp.where(valid, _dot_nt(_bf(qs), kk_), NEG_BIG))
    probs, denoms = [], []
    for s, sk in zip(scores, sinks):
        m = jnp.maximum(jnp.max(s, axis=-1, keepdims=True), sk)
        e = jnp.exp(s - m)
        denoms.append(jnp.sum(e, axis=-1, keepdims=True) + jnp.exp(sk - m))
        probs.append(_bf(e))
    pieces = []
    for kh in range(kv_heads):
        vv_ = _bf(vwin[:, kh * HEAD_DIM:(kh + 1) * HEAD_DIM])
        o = _dot(probs[kh], vv_) / denoms[kh]
        for g in range(group):
            pieces.append(o[g * qb:(g + 1) * qb])
    o_ref[0] = _bf(jnp.concatenate(pieces, axis=1))


def _attention(p_all, lay, cos_t, sin_t, sink, ctx_len, seq):
    b = p_all.shape[0]
    qb = QBLOCK
    nb = seq // qb
    cb = ctx_len // qb
    ad, kvd = lay["ad"], lay["kvd"]
    kv_heads = kvd // HEAD_DIM
    group = (ad // HEAD_DIM) // kv_heads
    assert lay["q"] % ad == 0 and lay["ka"] % kvd == 0 and lay["va"] % kvd == 0
    qc, kc, vc = lay["q"] // ad, lay["ka"] // kvd, lay["va"] // kvd

    def clampb(n, off):
        return jnp.clip(n + off, 0, nb - 1)

    def win(col, off):
        return pl.BlockSpec((1, qb, kvd), lambda bb, n: (bb, cb + clampb(n, off), col))

    def tab(off):
        return pl.BlockSpec((qb, LANES), lambda bb, n: (clampb(n, off), 0))

    assert ctx_len % qb == 0
    ctxspec = lambda col: pl.BlockSpec((1, ctx_len, kvd), lambda bb, n: (bb, 0, col))
    in_specs = [pl.BlockSpec((1, qb, ad), lambda bb, n: (bb, cb + n, qc)),
                win(kc, -1), win(kc, 0), win(kc, 1), win(vc, -1), win(vc, 0), win(vc, 1),
                ctxspec(kc), ctxspec(vc),
                tab(0), tab(0), tab(-1), tab(-1), tab(1), tab(1),
                pl.BlockSpec((1, sink.shape[1]), lambda bb, n: (0, 0))]
    return pl.pallas_call(
        functools.partial(_attn_kernel, n_blocks=nb, kv_heads=kv_heads, group=group),
        out_shape=jax.ShapeDtypeStruct((b, seq, ad), BF16),
        grid=(b, nb),
        in_specs=in_specs,
        out_specs=pl.BlockSpec((1, qb, ad), lambda bb, n: (bb, n, 0)),
        compiler_params=_params("parallel", "parallel"),
        name="window_ctx_attention",
    )(p_all, p_all, p_all, p_all, p_all, p_all, p_all, p_all, p_all,
      cos_t, sin_t, cos_t, sin_t, cos_t, sin_t, sink)


def _outproj_kernel(yf_ref, yr_ref, bon_ref, gat_ref, att_ref, x_ref, lg_ref, lb_ref, w_ref,
                    m_ref, g2_ref, wr_ref, x1_ref, h2_ref, lt_ref):
    ysum = yf_ref[0, 0] + yr_ref[0, 0]
    ri = lax.broadcasted_iota(jnp.int32, (LANES, LANES), 0) // HEAD_DIM
    ci = lax.broadcasted_iota(jnp.int32, (LANES, LANES), 1) // HEAD_DIM
    hm = _bf(jnp.where(ri == ci, 1.0 / HEAD_DIM, 0.0))

    def head_mean(z):
        hi = _bf(z)
        lo = _bf(z - hi.astype(F32))
        return _dot(hi, hm) + _dot(lo, hm)

    ybs = [ysum[:, j * LANES:(j + 1) * LANES] for j in range(ysum.shape[1] // LANES)]
    devs = [yb - mu for yb, mu in zip(ybs, [head_mean(yb) for yb in ybs])]
    blocks = [dev * lax.rsqrt(var + LNX_EPS)
              for dev, var in zip(devs, [head_mean(dev * dev) for dev in devs])]
    yn = jnp.concatenate(blocks, axis=1) if len(blocks) > 1 else blocks[0]
    rw = (yn * lg_ref[...] + lb_ref[...] + bon_ref[0]) * gat_ref[0]
    mix = jnp.concatenate([_bf(rw), att_ref[0]], axis=1)
    m = m_ref[0]
    x1 = x_ref[0] + m[0:1] * _dot(mix, w_ref[...])
    x1_ref[0] = x1
    ms = jnp.mean(x1 * x1, axis=-1, keepdims=True)
    h2 = x1 * lax.rsqrt(ms + NORM_EPS) * g2_ref[...] * (1.0 + m[2:3]) + m[1:2]
    h2_ref[0] = h2
    hi = _bf(h2)
    lo = _bf(h2 - hi.astype(F32))
    wr = wr_ref[...]
    wr_hi = _bf(wr)
    wr_lo = _bf(wr - wr_hi.astype(F32))
    lg = _dot(hi, wr_hi) + (_dot(hi, wr_lo) + _dot(lo, wr_hi))
    lt_ref[...] = lg.T[:lt_ref.shape[0]]


def _out_projection(y, bonus, gate, attn, x, lnx_g, lnx_b, w_out_b, m2, g2, w_router, ctx_len, tr):
    b, t, d = x.shape
    rd = y.shape[3]
    ad = attn.shape[2]
    ne = w_router.shape[1]
    assert ne <= LANES
    wr = _pad_to(w_router, 1, LANES)
    co = ctx_len // tr
    nt = t // tr
    return pl.pallas_call(
        _outproj_kernel,
        out_shape=[jax.ShapeDtypeStruct((b, t, d), F32),
                   jax.ShapeDtypeStruct((b, t, d), F32),
                   jax.ShapeDtypeStruct((ne, b * t), F32)],
        grid=(b, nt),
        in_specs=[pl.BlockSpec((1, 1, tr, rd), lambda bb, i: (0, bb, co + i, 0)),
                  pl.BlockSpec((1, 1, tr, rd), lambda bb, i: (1, bb, co + i, 0)),
                  pl.BlockSpec((1, tr, rd), lambda bb, i: (bb, co + i, 0)),
                  pl.BlockSpec((1, tr, rd), lambda bb, i: (bb, co + i, 0)),
                  pl.BlockSpec((1, tr, ad), lambda bb, i: (bb, i, 0)),
                  pl.BlockSpec((1, tr, d), lambda bb, i: (bb, i, 0)),
                  pl.BlockSpec((1, rd), lambda bb, i: (0, 0)),
                  pl.BlockSpec((1, rd), lambda bb, i: (0, 0)),
                  pl.BlockSpec((rd + ad, d), lambda bb, i: (0, 0)),
                  pl.BlockSpec((1, 4, d), lambda bb, i: (bb, 0, 0)),
                  pl.BlockSpec((1, d), lambda bb, i: (0, 0)),
                  pl.BlockSpec((d, LANES), lambda bb, i: (0, 0))],
        out_specs=[pl.BlockSpec((1, tr, d), lambda bb, i: (bb, i, 0)),
                   pl.BlockSpec((1, tr, d), lambda bb, i: (bb, i, 0)),
                   pl.BlockSpec((ne, tr), lambda bb, i: (0, bb * nt + i))],
        compiler_params=_params("parallel", "parallel"),
        name="rwkv_out_outproj_norm2_router",
    )(y, y, bonus, gate, attn, x, lnx_g, lnx_b, w_out_b, m2, g2, wr)


def _router_kernel(l_ref, b_ref, idx_ref, gate_ref, rank_ref, cnt_ref, carry_ref):
    @pl.when(pl.program_id(0) == 0)
    def _():
        carry_ref[...] = jnp.zeros_like(carry_ref)

    sc = jax.nn.sigmoid(l_ref[...])
    ne, tn = sc.shape
    eg = ne // N_GROUPS
    bz = sc + b_ref[...]
    g3 = bz.reshape(N_GROUPS, eg, tn)
    io = lax.broadcasted_iota(jnp.int32, (N_GROUPS, eg, tn), 1).astype(F32)
    m1 = jnp.max(g3, axis=1, keepdims=True)
    i1 = jnp.min(jnp.where(g3 == m1, io, float(eg)), axis=1, keepdims=True)
    m2 = jnp.max(jnp.where(io == i1, -jnp.inf, g3), axis=1, keepdims=True)
    gs = (m1 + m2).reshape(N_GROUPS, tn)

    def topk_hits(vals, k):
        n = vals.shape[0]
        idx = lax.broadcasted_iota(jnp.int32, vals.shape, 0).astype(F32)
        hits = []
        cur = vals
        for _ in range(k):
            mx = jnp.max(cur, axis=0, keepdims=True)
            ix = jnp.min(jnp.where(cur == mx, idx, float(n)), axis=0, keepdims=True)
            hit = idx == ix
            hits.append(hit)
            cur = jnp.where(hit, -jnp.inf, cur)
        return hits

    gsel = sum(jnp.where(h, 1.0, 0.0) for h in topk_hits(gs, TOPK_GROUPS))
    masked = jnp.where(gsel.reshape(N_GROUPS, 1, tn) > 0.0, g3, -jnp.inf).reshape(ne, tn)
    hits = topk_hits(masked, TOP_K)
    chosen = sum(jnp.where(h, 1.0, 0.0) for h in hits)
    denom = jnp.sum(chosen * sc, axis=0, keepdims=True)

    si = lax.broadcasted_iota(jnp.int32, (tn, tn), 0)
    ti = lax.broadcasted_iota(jnp.int32, (tn, tn), 1)
    before = _bf(jnp.where(si < ti, 1.0, 0.0))
    rank_full = _dot(_bf(chosen), before) + carry_ref[...]
    carry_ref[...] += jnp.sum(chosen, axis=1, keepdims=True)
    cnt_ref[...] = carry_ref[...]

    eidx = lax.broadcasted_iota(jnp.int32, (ne, tn), 0).astype(F32)
    for j, hit in enumerate(hits):
        idx_ref[j:j + 1, :] = jnp.sum(jnp.where(hit, eidx, 0.0), axis=0, keepdims=True).astype(jnp.int32)
        gate_ref[j:j + 1, :] = (jnp.sum(jnp.where(hit, sc, 0.0), axis=0, keepdims=True)
                                / denom * ROUTED_SCALE)
        rank_ref[j:j + 1, :] = jnp.sum(jnp.where(hit, rank_full, 0.0), axis=0,
                                       keepdims=True).astype(jnp.int32)


def _router(logits_t, bias_col):
    ne, n = logits_t.shape
    tn = 1024 if n % 1024 == 0 else n
    choice = lambda: pl.BlockSpec((TOP_K, tn), lambda i: (0, i))
    return pl.pallas_call(
        _router_kernel,
        out_shape=[jax.ShapeDtypeStruct((TOP_K, n), jnp.int32),
                   jax.ShapeDtypeStruct((TOP_K, n), F32),
                   jax.ShapeDtypeStruct((TOP_K, n), jnp.int32),
                   jax.ShapeDtypeStruct((ne, 1), F32)],
        grid=(n // tn,),
        in_specs=[pl.BlockSpec((ne, tn), lambda i: (0, i)),
                  pl.BlockSpec((ne, 1), lambda i: (0, 0))],
        out_specs=[choice(), choice(), choice(), pl.BlockSpec((ne, 1), lambda i: (0, 0))],
        scratch_shapes=[pltpu.VMEM((ne, 1), F32)],
        compiler_params=_params("arbitrary"),
        name="group_topk_router",
    )(logits_t, bias_col)


def _slots_kernel(idx_ref, rank_ref, ps_ref, o_ref):
    ne, tn = ps_ref.shape[0], idx_ref.shape[1]
    eidx = lax.broadcasted_iota(jnp.int32, (ne, tn), 0)
    for j in range(TOP_K):
        start = jnp.sum(jnp.where(eidx == idx_ref[j:j + 1, :], ps_ref[...], 0.0), axis=0, keepdims=True)
        o_ref[j:j + 1, :] = start.astype(jnp.int32) + rank_ref[j:j + 1, :]


def _slot_positions(idx8, rank8, pstart_col):
    _, n = idx8.shape
    ne = pstart_col.shape[0]
    tn = 1024 if n % 1024 == 0 else n
    choice = lambda: pl.BlockSpec((TOP_K, tn), lambda i: (0, i))
    return pl.pallas_call(
        _slots_kernel,
        out_shape=jax.ShapeDtypeStruct((TOP_K, n), jnp.int32),
        grid=(n // tn,),
        in_specs=[choice(), choice(), pl.BlockSpec((ne, 1), lambda i: (0, 0))],
        out_specs=choice(),
        compiler_params=_params("parallel"),
        name="expert_slot_positions",
    )(idx8, rank8, pstart_col)


def _slot_table_kernel(pos_ref, last_blk_ref, tail_ref, tok_ref, *, bm):
    def clear_group(base):
        for i in range(SUBLANES):
            tok_ref[base + i] = 0

    def per_expert(e, carry):
        def clear(g, c2):
            clear_group(last_blk_ref[e] + g * SUBLANES)
            return c2

        lax.fori_loop(0, bm // SUBLANES, clear, 0)
        return carry

    lax.fori_loop(0, last_blk_ref.shape[0], per_expert, 0)

    tail = tail_ref[0]

    def clear_tail(g, carry):
        clear_group(tail + g * SUBLANES)
        return carry

    lax.fori_loop(0, (tok_ref.shape[0] - tail) // SUBLANES, clear_tail, 0)

    def fill(t, carry):
        for k in range(TOP_K):
            tok_ref[pos_ref[k, t]] = t
        return carry

    lax.fori_loop(0, pos_ref.shape[1], fill, 0, unroll=4)


def _slot_table(pos8, last_blk_start, tail_start, n_slots, bm):
    smem = lambda: pl.BlockSpec(memory_space=pltpu.SMEM)
    return pl.pallas_call(
        functools.partial(_slot_table_kernel, bm=bm),
        out_shape=jax.ShapeDtypeStruct((n_slots,), jnp.int32),
        in_specs=[smem(), smem(), smem()],
        out_specs=smem(),
        name="expert_slot_tokens",
    )(pos8, last_blk_start, tail_start)


def _expert_ffn_kernel(be_ref, nu_ref, nx_ref, sl_ref, tok_ref, tokn_ref, h_hbm, wg_hbm, wu_hbm, wd_hbm,
                       y_ref, xbuf0, xbuf1, wgf, wuf, wdf, wgb, wub, wdb, sems, xsems):
    j = pl.program_id(0)
    e = be_ref[j]
    n_used = nu_ref[0]
    used = j < n_used
    fresh = jnp.logical_or(j == 0, e != be_ref[jnp.maximum(j - 1, 0)])
    bm = xbuf0.shape[0]
    xbufs = (xbuf0, xbuf1)

    def gather_rows(toks, slot):
        for r in range(bm):
            pltpu.make_async_copy(h_hbm.at[pl.ds(toks[r], 1)], xbufs[slot].at[pl.ds(r, 1)],
                                  xsems.at[slot]).start(priority=ROW_DMA_PRIORITY)

    def wait_rows(slot):
        pltpu.make_async_copy(h_hbm.at[pl.ds(0, bm)], xbufs[slot], xsems.at[slot]).wait()

    @pl.when(j == 0)
    def _():
        gather_rows(tok_ref, 0)

    def weight_copies(expert, slot):
        return [pltpu.make_async_copy(src.at[expert], dst.at[slot], sems.at[slot])
                for src, dst in ((wg_hbm, wgf), (wu_hbm, wuf), (wd_hbm, wdf))]

    @pl.when(j == 0)
    def _():
        for cp in weight_copies(e, sl_ref[e]):
            cp.start(priority=WEIGHT_DMA_PRIORITY)

    @pl.when(jnp.logical_and(used, fresh))
    def _():
        slot = sl_ref[e]
        for cp in weight_copies(e, slot):
            cp.wait()
        nxt = nx_ref[e]

        @pl.when(nxt >= 0)
        def _():
            for cp in weight_copies(nxt, 1 - slot):
                cp.start(priority=WEIGHT_DMA_PRIORITY)

        wgb[...] = _bf(wgf[slot])
        wub[...] = _bf(wuf[slot])
        wdb[...] = _bf(wdf[slot])

    for parity in range(2):
        @pl.when(jnp.logical_and(used, j % 2 == parity))
        def _(parity=parity):
            wait_rows(parity)
            gather_rows(tokn_ref, 1 - parity)
            x = _bf(xbufs[parity][...])
            a = _dot(x, wgb[...])
            u = _dot(x, wub[...])
            y_ref[...] = _dot(_bf(a * jax.nn.sigmoid(a) * u), wdb[...])

        @pl.when(jnp.logical_and(j == n_used - 1, j % 2 == parity))
        def _(parity=parity):
            wait_rows(1 - parity)

    @pl.when(jnp.logical_not(used))
    def _():
        y_ref[...] = jnp.zeros_like(y_ref)


def _expert_ffn(h2, slot_tok, blk_expert, n_used, next_expert, expert_slot, e_gate, e_up, e_down, bm):
    n_slots = slot_tok.shape[0]
    _, d = h2.shape
    ne, _, ff = e_gate.shape
    n_blk = n_slots // bm
    grid_spec = pltpu.PrefetchScalarGridSpec(
        num_scalar_prefetch=4,
        grid=(n_blk,),
        in_specs=[pl.BlockSpec((bm,), lambda j, be, nu, nx, sl: (jnp.minimum(j, nu[0] - 1),),
                               memory_space=pltpu.SMEM),
                  pl.BlockSpec((bm,), lambda j, be, nu, nx, sl: (jnp.minimum(j + 1, nu[0] - 1),),
                               memory_space=pltpu.SMEM),
                  pl.BlockSpec(memory_space=pl.ANY),
                  pl.BlockSpec(memory_space=pl.ANY),
                  pl.BlockSpec(memory_space=pl.ANY),
                  pl.BlockSpec(memory_space=pl.ANY)],
        out_specs=pl.BlockSpec((bm, d), lambda j, be, nu, nx, sl: (j, 0)),
        scratch_shapes=[pltpu.VMEM((bm, d), F32), pltpu.VMEM((bm, d), F32),
                        pltpu.VMEM((2, d, ff), F32), pltpu.VMEM((2, d, ff), F32), pltpu.VMEM((2, ff, d), F32),
                        pltpu.VMEM((d, ff), BF16), pltpu.VMEM((d, ff), BF16), pltpu.VMEM((ff, d), BF16),
                        pltpu.SemaphoreType.DMA((2,)), pltpu.SemaphoreType.DMA((2,))])
    return pl.pallas_call(
        _expert_ffn_kernel,
        out_shape=jax.ShapeDtypeStruct((n_slots, d), F32),
        grid_spec=grid_spec,
        compiler_params=_params("arbitrary"),
        name="routed_expert_ffn",
    )(blk_expert, n_used, next_expert, expert_slot, slot_tok, slot_tok, h2, e_gate, e_up, e_down)


def _final_kernel(pos_ref, h_ref, x1_ref, g_ref, ys_ref, wg_ref, wu_ref, wd_ref, m_ref, fg_ref,
                  o_ref, buf, sem):
    tr = h_ref.shape[1]

    def issue(t, carry):
        for k in range(TOP_K):
            pltpu.make_async_copy(ys_ref.at[pl.ds(pos_ref[k, t], 1)], buf.at[k, pl.ds(t, 1)],
                                  sem).start(priority=1)
        return carry

    lax.fori_loop(0, tr, issue, 0)
    h = _bf(h_ref[0])
    a = _dot(h, wg_ref[...])
    u = _dot(h, wu_ref[...])
    moe = _dot(_bf(a * jax.nn.sigmoid(a) * u), wd_ref[...])
    g = g_ref[...]
    for k in range(TOP_K):
        pltpu.make_async_copy(ys_ref.at[pl.ds(0, tr)], buf.at[k], sem).wait()
    for k in range(TOP_K):
        moe = moe + g[:, k:k + 1] * buf[k]
    x2 = x1_ref[0] + m_ref[0, 3:4, :] * moe
    ms = jnp.mean(x2 * x2, axis=-1, keepdims=True)
    o_ref[0] = x2 * lax.rsqrt(ms + NORM_EPS) * fg_ref[...]


def _final(h2, x1, pos8, gate_rows, ys, s_gate_b, s_up_b, s_down_b, m2, final_g, tr):
    b, t, d = x1.shape
    ff = s_gate_b.shape[1]
    nt = t // tr
    row = lambda: pl.BlockSpec((1, tr, d), lambda bb, i: (bb, i, 0))
    return pl.pallas_call(
        _final_kernel,
        out_shape=jax.ShapeDtypeStruct((b, t, d), F32),
        grid=(b, nt),
        in_specs=[pl.BlockSpec((TOP_K, tr), lambda bb, i: (0, bb * nt + i), memory_space=pltpu.SMEM),
                  row(), row(),
                  pl.BlockSpec((tr, TOP_K), lambda bb, i: (bb * nt + i, 0)),
                  pl.BlockSpec(memory_space=pl.ANY),
                  pl.BlockSpec((d, ff), lambda bb, i: (0, 0)),
                  pl.BlockSpec((d, ff), lambda bb, i: (0, 0)),
                  pl.BlockSpec((ff, d), lambda bb, i: (0, 0)),
                  pl.BlockSpec((1, 4, d), lambda bb, i: (bb, 0, 0)),
                  pl.BlockSpec((1, d), lambda bb, i: (0, 0))],
        out_specs=row(),
        scratch_shapes=[pltpu.VMEM((TOP_K, tr, d), F32), pltpu.SemaphoreType.DMA(())],
        compiler_params=_params("parallel", "parallel"),
        name="combine_shared_expert_final_norm",
    )(pos8, h2, x1, gate_rows, ys, s_gate_b, s_up_b, s_down_b, m2, final_g)


def _pad_to(a, axis, n):
    pad = [(0, 0)] * a.ndim
    pad[axis] = (0, n - a.shape[axis])
    return jnp.pad(a, pad)


def _layout(rd, wl, al, gl, ad, kvd):
    assert wl <= LANES and al <= LANES
    lay = {"rd": rd, "gl": gl, "ad": ad, "kvd": kvd}
    off = 0
    for name, width in (("q", ad), ("r", rd), ("k", rd), ("v", rd), ("ka", kvd), ("va", kvd),
                        ("xw", LANES), ("xa", LANES), ("xg", gl)):
        assert width % LANES == 0
        lay[name] = off
        off += width
    lay["ncols"] = off
    return lay


def kernel(x, c, ctx, c_ctx, w_mod, b_mod, norm1_g, norm2_g, w_in, shift_mu, k_k, k_a, r_k, decay_bias, decay_up, iclr_bias, iclr_up, gate_up, lnx_g, lnx_b, attn_sink, w_out, w_router, expert_bias, e_gate, e_up, e_down, s_gate, s_up, s_down, final_g):
    assert w_mod.shape[0] == 1, "single-layer block only"
    b, t, d = x.shape
    ctx_len = ctx.shape[1]
    rh, hd = r_k.shape[1:]
    assert hd == HEAD_DIM and b + 1 <= SUBLANES
    rd = rh * hd
    wl, al, gl = decay_up.shape[2], iclr_up.shape[2], gate_up.shape[1]
    ad = attn_sink.shape[1] * hd
    kvd = (w_in.shape[2] - (3 * rd + wl + al + gl) - ad) // 2
    ne = w_router.shape[2]
    lay = _layout(rd, wl, al, gl, ad, kvd)
    tr = 256 if ctx_len % 256 == 0 and t % 256 == 0 else CHUNK
    assert ctx_len % CHUNK == 0 and t % CHUNK == 0 and t % QBLOCK == 0 and WINDOW == QBLOCK

    c8 = _pad_to(jnp.concatenate([c, c_ctx[None]], axis=0), 0, SUBLANES)
    mod = _modulation(c8, w_mod[0], b_mod[0][None])
    mod6 = mod.reshape(SUBLANES, 6, d)
    mx, mc = mod6[:b], mod6[b]
    m1 = jnp.stack([jnp.broadcast_to(mc[None, 0:2], (b, 2, d)), mx[:, 0:2]], axis=1)
    m2 = jnp.stack([mx[:, 2], mx[:, 3], mx[:, 4], mx[:, 5]], axis=1)

    wi = w_in[0]
    o_w, o_a, o_g, o_q = 3 * rd, 3 * rd + wl, 3 * rd + wl + al, 3 * rd + wl + al + gl
    w_in_p = _bf(jnp.concatenate([
        wi[:, o_q:o_q + ad], wi[:, :3 * rd], wi[:, o_q + ad:],
        _pad_to(wi[:, o_w:o_a], 1, LANES), _pad_to(wi[:, o_a:o_g], 1, LANES), wi[:, o_g:o_q]], axis=1))
    mu = shift_mu[0]
    prm = {
        "mu_r": mu[None, :rd], "mu_k": mu[None, rd:2 * rd], "mu_v": mu[None, 2 * rd:3 * rd],
        "mu_w": _pad_to(mu[None, o_w:o_a], 1, LANES), "mu_a": _pad_to(mu[None, o_a:o_g], 1, LANES),
        "mu_g": mu[None, o_g:o_q],
        "k_k": k_k, "k_a": k_a, "r_k": r_k.reshape(1, rd),
        "decay_bias": decay_bias[0], "decay_up": _bf(_pad_to(decay_up[0], 1, LANES)),
        "iclr_bias": iclr_bias[0], "iclr_up": _bf(_pad_to(iclr_up[0], 1, LANES)),
        "gate_up": _bf(gate_up[0]),
    }

    p_all = _in_projection(ctx, x, norm1_g, m1, w_in_p, tr)

    npair = rd // LANES
    pp = max(n for n in (1, 2, 4, 8) if npair % n == 0
             and all(lay[key] % (n * LANES) == 0 for key in ("r", "k", "v")))
    rp, y0, mm, nn, bonus, gate = _wkv_prep(p_all, lay, prm, ctx_len, None, pp)
    y = _wkv_sweep(rp, y0, mm, nn, ctx_len)

    nf = HEAD_DIM // 4
    freqs = ROPE_BASE ** (-jnp.arange(nf, dtype=F32) / nf)
    pos = jnp.arange(t, dtype=jnp.int32)
    ang_r = (pos // GRID_W).astype(F32)[:, None] * freqs
    ang_c = (pos % GRID_W).astype(F32)[:, None] * freqs
    cos_h = jnp.concatenate([jnp.cos(ang_r)] * 2 + [jnp.cos(ang_c)] * 2, axis=1)
    sin_h = jnp.concatenate([-jnp.sin(ang_r), jnp.sin(ang_r), -jnp.sin(ang_c), jnp.sin(ang_c)], axis=1)
    cos_t = jnp.concatenate([cos_h, cos_h], axis=1)
    sin_t = jnp.concatenate([sin_h, sin_h], axis=1)
    attn = _attention(p_all, lay, cos_t, sin_t, attn_sink, ctx_len, t)

    x1, h2, logits_t = _out_projection(y, bonus, gate, attn, x, lnx_g, lnx_b, _bf(w_out[0]), m2,
                                       norm2_g, w_router[0], ctx_len, tr)
    idx8, gate8, rank8, counts = _router(logits_t, expert_bias[0][:, None])

    n_tok = b * t
    bm = EXPERT_ROWS
    assert (n_tok * TOP_K) % bm == 0
    n_blk = n_tok * TOP_K // bm + ne
    cnt = counts[:, 0].astype(jnp.int32)
    padded = (cnt + bm - 1) // bm * bm
    pend = jnp.cumsum(padded)
    pstart = pend - padded
    pos8 = _slot_positions(idx8, rank8, pstart.astype(F32)[:, None])
    blk_row0 = jnp.arange(n_blk, dtype=jnp.int32) * bm
    blk_expert = jnp.minimum(jnp.sum((pend[None, :] <= blk_row0[:, None]).astype(jnp.int32), axis=1), ne - 1)
    n_used = (pend[-1:] // bm).astype(jnp.int32)

    slot_tok = _slot_table(pos8, jnp.maximum(pend - bm, 0), pend[-1:], n_blk * bm, bm)
    has_rows = cnt > 0
    eids = jnp.arange(ne, dtype=jnp.int32)
    later = jnp.where(has_rows[None, :] & (eids[None, :] > eids[:, None]), eids[None, :], ne)
    next_expert = jnp.min(later, axis=1)
    next_expert = jnp.where(next_expert < ne, next_expert, -1).astype(jnp.int32)
    expert_slot = ((jnp.cumsum(has_rows.astype(jnp.int32)) - 1) % 2).astype(jnp.int32)
    ys = _expert_ffn(h2.reshape(n_tok, d), slot_tok, blk_expert, n_used, next_expert, expert_slot,
                     e_gate[0], e_up[0], e_down[0], bm)
    return _final(h2, x1, pos8, gate8.T, ys, _bf(s_gate[0]), _bf(s_up[0]), _bf(s_down[0]),
                  m2, final_g[None], tr)
```

```python
import functools
import math

import jax
import jax.numpy as jnp
from jax import lax
from jax.experimental import pallas as pl
from jax.experimental.pallas import tpu as pltpu

GRID_W = 64
WINDOW = 128
ROPE_BASE = 10000.0
N_GROUPS = 8
TOPK_GROUPS = 4
TOP_K = 8
ROUTED_SCALE = 2.5
NORM_EPS = 1e-6
LNX_EPS = 64e-5
KK_EPS = 1e-12

LANES = 128
SUBLANES = 8
VMEM_LIMIT_BYTES = 56 * 1024 * 1024

HEAD_DIM = 64
CHUNK = 64
QBLOCK = 128
EXPERT_ROWS = 256
ROW_DMA_PRIORITY = 1
WEIGHT_DMA_PRIORITY = 0

F32 = jnp.float32
BF16 = jnp.bfloat16
NEG_BIG = -1e30


def _bf(x):
    return x.astype(BF16)


def _dot(a, b, precision=None):
    return jnp.dot(a, b, preferred_element_type=F32, precision=precision)


def _dot_nt(a, b, precision=None):
    return lax.dot_general(a, b, (((1,), (1,)), ((), ())), preferred_element_type=F32,
                           precision=precision)


def _dot_tn(a, b):
    return lax.dot_general(a, b, (((0,), (0,)), ((), ())), preferred_element_type=F32)


def _params(*sem):
    return pltpu.CompilerParams(dimension_semantics=sem, vmem_limit_bytes=VMEM_LIMIT_BYTES)


def _mod_kernel(c_ref, w_ref, b_ref, o_ref):
    c = c_ref[...]
    s = c * jax.nn.sigmoid(c)
    hi = _bf(s)
    lo = _bf(s - hi.astype(F32))
    r = _dot(jnp.concatenate([hi, lo], axis=0), _bf(w_ref[...]))
    o_ref[...] = r[:SUBLANES] + r[SUBLANES:] + b_ref[...]


def _modulation(c8, w_mod, b_mod):
    d, n = w_mod.shape
    tn = 1024 if n % 1024 == 0 else n
    return pl.pallas_call(
        _mod_kernel,
        out_shape=jax.ShapeDtypeStruct((SUBLANES, n), F32),
        grid=(n // tn,),
        in_specs=[pl.BlockSpec((SUBLANES, d), lambda j: (0, 0)),
                  pl.BlockSpec((d, tn), lambda j: (0, j)),
                  pl.BlockSpec((1, tn), lambda j: (0, j))],
        out_specs=pl.BlockSpec((SUBLANES, tn), lambda j: (0, j)),
        compiler_params=_params("parallel"),
        name="adaln_modulation",
    )(c8, w_mod, b_mod)


def _inproj_kernel(c_ref, x_ref, g_ref, m_ref, w_ref, o_ref, *, nct):
    x = jnp.where(pl.program_id(2) < nct, c_ref[0], x_ref[0])
    ms = jnp.mean(x * x, axis=-1, keepdims=True)
    y = x * lax.rsqrt(ms + NORM_EPS) * g_ref[...]
    m = m_ref[0, 0]
    h = y * (1.0 + m[1:2]) + m[0:1]
    o_ref[0] = _dot(_bf(h), w_ref[...])


def _in_projection(ctx, x, g1, m1, w_in_p, tr):
    b, t, d = x.shape
    ctx_len = ctx.shape[1]
    ta = ctx_len + t
    ncols = w_in_p.shape[1]
    tn = ncols // 2 if (ncols // 2) % LANES == 0 and ncols % 2 == 0 else ncols
    nct = ctx_len // tr
    return pl.pallas_call(
        functools.partial(_inproj_kernel, nct=nct),
        out_shape=jax.ShapeDtypeStruct((b, ta, ncols), F32),
        grid=(ncols // tn, b, ta // tr),
        in_specs=[pl.BlockSpec((1, tr, d), lambda n, bb, i: (bb, jnp.minimum(i, nct - 1), 0)),
                  pl.BlockSpec((1, tr, d), lambda n, bb, i: (bb, jnp.maximum(i - nct, 0), 0)),
                  pl.BlockSpec((1, d), lambda n, bb, i: (0, 0)),
                  pl.BlockSpec((1, 1, 2, d), lambda n, bb, i: (bb, jnp.where(i >= nct, 1, 0), 0, 0)),
                  pl.BlockSpec((d, tn), lambda n, bb, i: (0, n))],
        out_specs=pl.BlockSpec((1, tr, tn), lambda n, bb, i: (bb, i, n)),
        compiler_params=_params("parallel", "parallel", "parallel"),
        name="norm_modulate_inproj",
    )(ctx, x, g1, m1, w_in_p)


def _wkv_prep_kernel(r_c, r_p, r_n, k_c, k_p, k_n, v_c, v_p, v_n, w_c, w_p, w_n, a_c, a_p, a_n,
                     g_c, g_p, g_n, mu_r, mu_k, mu_v, mu_w, mu_a, mu_g, kk_ref, ka_ref, rk_ref,
                     db_ref, du_ref, ib_ref, iu_ref, gu_ref,
                     rp_out, y0_out, m_out, n_out, bonus_out, gate_out,
                     *, nctx_chunks, n_chunks, prec_t):
    L = CHUNK
    c = pl.program_id(1)
    has_prev = jnp.logical_and(c != 0, c != nctx_chunks)
    has_next = jnp.logical_and(c != nctx_chunks - 1, c != n_chunks - 1)

    def shifted(cur, prv, nxt, mu):
        x = cur[0]
        rows = lax.broadcasted_iota(jnp.int32, (L, 1), 0)
        pr = jnp.where(has_prev, prv[0, SUBLANES - 1:SUBLANES, :], 0.0)
        nx = jnp.where(has_next, nxt[0, 0:1, :], 0.0)
        up = jnp.where(rows == 0, pr, pltpu.roll(x, 1, 0))
        dn = jnp.where(rows == L - 1, nx, pltpu.roll(x, L - 1, 0))
        return x + mu[...] * (0.5 * (up + dn) - x)

    r = shifted(r_c, r_p, r_n, mu_r)
    k = shifted(k_c, k_p, k_n, mu_k)
    v = shifted(v_c, v_p, v_n, mu_v)
    xw = shifted(w_c, w_p, w_n, mu_w)
    xa = shifted(a_c, a_p, a_n, mu_a)
    xg = shifted(g_c, g_p, g_n, mu_g)

    lane = lax.broadcasted_iota(jnp.int32, (1, LANES), 1)
    m0 = lane < HEAD_DIM

    def head_sum(x):
        s0 = jnp.sum(jnp.where(m0, x, 0.0), axis=-1, keepdims=True)
        s1 = jnp.sum(jnp.where(m0, 0.0, x), axis=-1, keepdims=True)
        return jnp.where(m0, s0, s1)

    def stack2(x):
        return jnp.concatenate([jnp.where(m0, x, 0.0), jnp.where(m0, 0.0, x)], axis=0)

    tw = _bf(jnp.tanh(xw))
    xab = _bf(xa)
    lw_all = [(-math.exp(-0.5)) * jax.nn.sigmoid(db_ref[d:d + 1, :] + _dot(tw, du_ref[d]))
              for d in range(2)]
    sa_all = [jax.nn.sigmoid(ib_ref[d:d + 1, :] + _dot(xab, iu_ref[d])) for d in range(2)]
    gate_out[0] = _dot(_bf(jax.nn.sigmoid(xg)), gu_ref[...])

    ri = lax.broadcasted_iota(jnp.int32, (L, L), 0)
    ci = lax.broadcasted_iota(jnp.int32, (L, L), 1)
    ri2 = lax.broadcasted_iota(jnp.int32, (2 * L, 2 * L), 0)
    ci2 = lax.broadcasted_iota(jnp.int32, (2 * L, 2 * L), 1)
    rt = jnp.bitwise_and(ri2, L - 1)
    ct = jnp.bitwise_and(ci2, L - 1)
    eye = ri2 == ci2
    n_sq = int(math.log2(L)) - 1

    tri = [(ci <= ri).astype(F32), (ci >= ri).astype(F32)]
    strict = [ct < rt, ct > rt]
    incl = [ct <= rt, ct >= rt]
    cum = [_dot(tri[d], lw_all[d], precision=lax.Precision.HIGHEST) for d in range(2)]
    cend = [jnp.sum(lw_all[d], axis=0, keepdims=True) for d in range(2)]
    e_in = [jnp.exp(cum[d]) for d in range(2)]
    e_ex = [jnp.exp(cum[d] - lw_all[d]) for d in range(2)]
    e_neg = [jnp.exp(-cum[d]) for d in range(2)]
    e_end = [jnp.exp(cend[d] - cum[d]) for d in range(2)]
    p_end = [jnp.exp(cend[d]) for d in range(2)]
    kd_all = [k * (1.0 + (sa_all[d] - 1.0) * ka_ref[...]) for d in range(2)]

    npp = r.shape[1] // LANES
    chains = [(p, d) for p in range(npp) for d in range(2)]
    sls = [slice(p * LANES, (p + 1) * LANES) for p in range(npp)]
    kk_p, vs_p, vsb_p = [], [], []
    for p in range(npp):
        kx = k[:, sls[p]] * kk_ref[:, sls[p]]
        kk_p.append(kx * lax.rsqrt(head_sum(kx * kx) + KK_EPS))
        vs_p.append(stack2(v[:, sls[p]]))
        vsb_p.append(_bf(vs_p[p]))
        ksum = kd_all[0][:, sls[p]] + kd_all[1][:, sls[p]]
        bonus_out[0, :, sls[p]] = head_sum(r[:, sls[p]] * ksum * rk_ref[:, sls[p]]) * v[:, sls[p]]

    a_s, r_s, bh_s, kh_s, g = [], [], [], [], []
    for p, d in chains:
        sl = sls[p]
        kd = kd_all[d][:, sl]
        bvec = kk_p[p] * sa_all[d][:, sl]
        a_s.append(stack2(-kk_p[p] * e_ex[d][:, sl]))
        r_s.append(stack2(r[:, sl] * e_in[d][:, sl]))
        b_s = stack2(bvec * e_neg[d][:, sl])
        k_s = stack2(kd * e_neg[d][:, sl])
        bh_s.append(stack2(bvec * e_end[d][:, sl]))
        kh_s.append(stack2(kd * e_end[d][:, sl]))
        g.append(_dot_nt(_bf(jnp.concatenate([a_s[-1], r_s[-1]], axis=0)),
                         _bf(jnp.concatenate([b_s, k_s], axis=0))))
    a_ab = [jnp.where(strict[d], g[i][:2 * L, :2 * L], 0.0) for i, (p, d) in enumerate(chains)]
    a_ak = [jnp.where(strict[d], g[i][:2 * L, 2 * L:], 0.0) for i, (p, d) in enumerate(chains)]
    a_rb = [jnp.where(incl[d], g[i][2 * L:, :2 * L], 0.0) for i, (p, d) in enumerate(chains)]
    a_rk = [jnp.where(incl[d], g[i][2 * L:, 2 * L:], 0.0) for i, (p, d) in enumerate(chains)]

    ident = jnp.where(eye, 1.0, 0.0)
    tmat = [ident + a for a in a_ab]
    apow = a_ab
    for _ in range(n_sq):
        if prec_t is None:
            apow = [_dot(_bf(a), _bf(a)) for a in apow]
            tmat = [t + _dot(_bf(a), _bf(t)) for a, t in zip(apow, tmat)]
        else:
            apow = [_dot(a, a, precision=prec_t) for a in apow]
            tmat = [t + _dot(a, t, precision=prec_t) for a, t in zip(apow, tmat)]

    x1 = [_dot(_bf(a_ak[i]), vsb_p[p]) for i, (p, d) in enumerate(chains)]
    y0v = [_dot(_bf(a_rk[i]), vsb_p[p]) for i, (p, d) in enumerate(chains)]
    tu = [_dot(_bf(tmat[i]), _bf(jnp.concatenate([a_s[i], x1[i]], axis=1)))
          for i in range(len(chains))]
    ry = [_dot(_bf(a_rb[i]), _bf(tu[i])) for i in range(len(chains))]
    def fold(z):
        h = z.shape[0] // 2
        return z[:h] + z[h:]

    for i, (p, d) in enumerate(chains):
        rp_out[d, 0, 0, p] = _bf(fold(r_s[i] + ry[i][:, :LANES]))
        y0_out[d, 0, 0, p] = fold(ry[i][:, LANES:] + y0v[i])
    for i, (p, d) in enumerate(chains):
        m_out[d, 0, 0, p] = _bf(fold(jnp.where(eye, p_end[d][:, sls[p]], 0.0)
                                     + _dot_tn(_bf(tu[i][:, :LANES]), _bf(bh_s[i]))))
    for i, (p, d) in enumerate(chains):
        n_out[d, 0, 0, p] = fold(_dot_tn(_bf(jnp.concatenate([tu[i][:, LANES:], vs_p[p]], axis=0)),
                                         _bf(jnp.concatenate([bh_s[i], kh_s[i]], axis=0))))


def _wkv_prep(p_all, lay, prm, ctx_len, prec_t, pp):
    b, ta, _ = p_all.shape
    L = CHUNK
    nc = ta // L
    nctx = ctx_len // L
    npair = lay["rd"] // LANES
    gl = lay["gl"]
    rows8 = ta // SUBLANES

    def cur(colfn, width=LANES):
        return pl.BlockSpec((1, L, width), lambda bb, c, p: (bb, c, colfn(p)))

    def prv(colfn, width=LANES):
        return pl.BlockSpec((1, SUBLANES, width),
                            lambda bb, c, p: (bb, jnp.maximum(c * (L // SUBLANES) - 1, 0), colfn(p)))

    def nxt(colfn, width=LANES):
        return pl.BlockSpec((1, SUBLANES, width),
                            lambda bb, c, p: (bb, jnp.minimum((c + 1) * (L // SUBLANES), rows8 - 1), colfn(p)))

    def trio(colfn, width=LANES):
        return [cur(colfn, width), prv(colfn, width), nxt(colfn, width)]

    pw = pp * LANES
    assert npair % pp == 0 and all(lay[n] % pw == 0 for n in ("r", "k", "v"))
    rb, kb, vb = lay["r"] // pw, lay["k"] // pw, lay["v"] // pw
    wb, ab = lay["xw"] // LANES, lay["xa"] // LANES
    assert lay["xg"] % gl == 0
    gb = lay["xg"] // gl

    def vec(width=pw):
        return pl.BlockSpec((1, width), lambda bb, c, p: (0, p))

    def vec0(width=LANES):
        return pl.BlockSpec((1, width), lambda bb, c, p: (0, 0))

    in_specs = (trio(lambda p: rb + p, pw) + trio(lambda p: kb + p, pw) + trio(lambda p: vb + p, pw)
                + trio(lambda p: wb) + trio(lambda p: ab) + trio(lambda p: gb, gl)
                + [vec(), vec(), vec(), vec0(), vec0(), vec0(gl),
                   vec(), vec(), vec(),
                   pl.BlockSpec((2, pw), lambda bb, c, p: (0, p)),
                   pl.BlockSpec((2, LANES, pw), lambda bb, c, p: (0, 0, p)),
                   pl.BlockSpec((2, pw), lambda bb, c, p: (0, p)),
                   pl.BlockSpec((2, LANES, pw), lambda bb, c, p: (0, 0, p)),
                   pl.BlockSpec((gl, pw), lambda bb, c, p: (0, p))])

    def opspec(rows):
        return pl.BlockSpec((2, 1, 1, pp, rows, LANES), lambda bb, c, p: (0, c, bb, p, 0, 0))

    out_specs = [opspec(L), opspec(L), opspec(HEAD_DIM), opspec(HEAD_DIM),
                 pl.BlockSpec((1, L, pw), lambda bb, c, p: (bb, c, p)),
                 pl.BlockSpec((1, L, pw), lambda bb, c, p: (bb, c, p))]
    out_shape = [jax.ShapeDtypeStruct((2, nc, b, npair, L, LANES), BF16),
                 jax.ShapeDtypeStruct((2, nc, b, npair, L, LANES), F32),
                 jax.ShapeDtypeStruct((2, nc, b, npair, HEAD_DIM, LANES), BF16),
                 jax.ShapeDtypeStruct((2, nc, b, npair, HEAD_DIM, LANES), F32),
                 jax.ShapeDtypeStruct((b, ta, lay["rd"]), F32),
                 jax.ShapeDtypeStruct((b, ta, lay["rd"]), F32)]
    kern = functools.partial(_wkv_prep_kernel, nctx_chunks=nctx, n_chunks=nc, prec_t=prec_t)
    args = [p_all] * 18 + [prm["mu_r"], prm["mu_k"], prm["mu_v"], prm["mu_w"], prm["mu_a"], prm["mu_g"],
                           prm["k_k"], prm["k_a"], prm["r_k"], prm["decay_bias"], prm["decay_up"],
                           prm["iclr_bias"], prm["iclr_up"], prm["gate_up"]]
    return pl.pallas_call(
        kern, out_shape=out_shape, grid=(b, nc, npair // pp), in_specs=in_specs, out_specs=out_specs,
        compiler_params=_params("parallel", "parallel", "parallel"),
        name="wkv_chunk_operators",
    )(*args)


def _wkv_seq_kernel(rpf, y0f, mf, nf, rpr, y0r, mr, nr, yf_ref, yr_ref, s_ref, *, nb, npair):
    L = CHUNK

    @pl.when(pl.program_id(0) == 0)
    def _():
        s_ref[...] = jnp.zeros_like(s_ref)

    m0 = lax.broadcasted_iota(jnp.int32, (1, LANES), 1) < HEAD_DIM

    def unfold(z):
        zero = jnp.zeros_like(z)
        return jnp.concatenate([jnp.where(m0, z, zero), jnp.where(m0, zero, z)], axis=0)

    for d, (rp_ref, y0_ref, m_ref, n_ref, y_ref) in enumerate(((rpf, y0f, mf, nf, yf_ref),
                                                               (rpr, y0r, mr, nr, yr_ref))):
        for bb in range(nb):
            for p in range(npair):
                i = (d * nb + bb) * npair + p
                s0 = _bf(s_ref[i])
                ys = _dot_nt(unfold(rp_ref[0, 0, bb, p]), s0)
                y_ref[bb, :, p * LANES:(p + 1) * LANES] = ys[:L] + ys[L:] + y0_ref[0, 0, bb, p]
                s_ref[i] = _dot(s0, unfold(m_ref[0, 0, bb, p])) + unfold(n_ref[0, 0, bb, p])


def _wkv_sweep(rp, y0, mm, nn, ctx_len):
    _, nc, b, npair, _, _ = rp.shape
    L = CHUNK
    nctx = ctx_len // L

    def chunk_of(d, c):
        return c if d == 0 else jnp.where(c < nctx, nctx - 1 - c, nc - 1 - (c - nctx))

    def opspec(d, rows):
        return pl.BlockSpec((1, 1, b, npair, rows, LANES), lambda c: (d, chunk_of(d, c), 0, 0, 0, 0))

    def yspec(d):
        return pl.BlockSpec((b, L, npair * LANES), lambda c: (0, chunk_of(d, c), 0))

    rows = (L, L, HEAD_DIM, HEAD_DIM)
    y_shape = jax.ShapeDtypeStruct((b, nc * L, npair * LANES), F32)
    return pl.pallas_call(
        functools.partial(_wkv_seq_kernel, nb=b, npair=npair),
        out_shape=[y_shape, y_shape],
        grid=(nc,),
        in_specs=[opspec(d, r) for d in range(2) for r in rows],
        out_specs=[yspec(0), yspec(1)],
        scratch_shapes=[pltpu.VMEM((2 * b * npair, LANES, LANES), F32)],
        compiler_params=_params("arbitrary"),
        name="wkv_state_sweep",
    )(rp, y0, mm, nn, rp, y0, mm, nn)


def _attn_kernel(q_ref, kp_ref, kc_ref, kn_ref, vp_ref, vc_ref, vn_ref, kx_ref, vx_ref,
                 cq_ref, sq_ref, cp_ref, sp_ref, cn_ref, sn_ref, sink_ref, o_ref,
                 *, n_blocks, kv_heads, group):
    n = pl.program_id(1)
    qb = QBLOCK

    def rope(x, cos, sin):
        w = x.shape[1]
        reps = w // LANES
        cs = jnp.concatenate([cos] * reps, axis=1) if reps > 1 else cos
        sn = jnp.concatenate([sin] * reps, axis=1) if reps > 1 else sin
        lane = lax.broadcasted_iota(jnp.int32, (1, w), 1)
        sw = jnp.where(jnp.bitwise_and(lane, 16) == 0, pltpu.roll(x, w - 16, 1), pltpu.roll(x, 16, 1))
        return x * cs + sw * sn

    q = rope(q_ref[0], cq_ref[...], sq_ref[...]) * (HEAD_DIM ** -0.5)
    kwin = jnp.concatenate([rope(kp_ref[0], cp_ref[...], sp_ref[...]),
                            rope(kc_ref[0], cq_ref[...], sq_ref[...]),
                            rope(kn_ref[0], cn_ref[...], sn_ref[...]),
                            kx_ref[0]], axis=0)
    vwin = jnp.concatenate([vp_ref[0], vc_ref[0], vn_ref[0], vx_ref[0]], axis=0)
    nkeys = kwin.shape[0]

    rows = group * qb
    qi = jnp.bitwise_and(lax.broadcasted_iota(jnp.int32, (rows, nkeys), 0), qb - 1)
    s_i = lax.broadcasted_iota(jnp.int32, (rows, nkeys), 1)
    blk = s_i // qb + (n - 1)
    in_win = (s_i >= qi) & (s_i <= qi + 2 * WINDOW) & (blk >= 0) & (blk < n_blocks)
    valid = in_win | (s_i >= 3 * qb)
    rgrp = lax.broadcasted_iota(jnp.int32, (rows, 1), 0) // qb

    scores, sinks = [], []
    for kh in range(kv_heads):
        kk_ = _bf(kwin[:, kh * HEAD_DIM:(kh + 1) * HEAD_DIM])
        qs = jnp.concatenate([q[:, (kh * group + g) * HEAD_DIM:(kh * group + g + 1) * HEAD_DIM]
                              for g in range(group)], axis=0)
        sk = jnp.zeros((rows, 1), F32)
        for g in range(group):
            h = kh * group + g
            sk = jnp.where(rgrp == g, sink_ref[0:1, h:h + 1], sk)
        sinks.append(sk)
        scores.append(jnp.where(valid, _dot_nt(_bf(qs), kk_), NEG_BIG))
    probs, denoms = [], []
    for s, sk in zip(scores, sinks):
        m = jnp.maximum(jnp.max(s, axis=-1, keepdims=True), sk)
        e = jnp.exp(s - m)
        denoms.append(jnp.sum(e, axis=-1, keepdims=True) + jnp.exp(sk - m))
        probs.append(_bf(e))
    pieces = []
    for kh in range(kv_heads):
        vv_ = _bf(vwin[:, kh * HEAD_DIM:(kh + 1) * HEAD_DIM])
        o = _dot(probs[kh], vv_) / denoms[kh]
        for g in range(group):
            pieces.append(o[g * qb:(g + 1) * qb])
    o_ref[0] = _bf(jnp.concatenate(pieces, axis=1))


def _attention(p_all, lay, cos_t, sin_t, sink, ctx_len, seq):
    b = p_all.shape[0]
    qb = QBLOCK
    nb = seq // qb
    cb = ctx_len // qb
    ad, kvd = lay["ad"], lay["kvd"]
    kv_heads = kvd // HEAD_DIM
    group = (ad // HEAD_DIM) // kv_heads
    assert lay["q"] % ad == 0 and lay["ka"] % kvd == 0 and lay["va"] % kvd == 0
    qc, kc, vc = lay["q"] // ad, lay["ka"] // kvd, lay["va"] // kvd

    def clampb(n, off):
        return jnp.clip(n + off, 0, nb - 1)

    def win(col, off):
        return pl.BlockSpec((1, qb, kvd), lambda bb, n: (bb, cb + clampb(n, off), col))

    def tab(off):
        return pl.BlockSpec((qb, LANES), lambda bb, n: (clampb(n, off), 0))

    assert ctx_len % qb == 0
    ctxspec = lambda col: pl.BlockSpec((1, ctx_len, kvd), lambda bb, n: (bb, 0, col))
    in_specs = [pl.BlockSpec((1, qb, ad), lambda bb, n: (bb, cb + n, qc)),
                win(kc, -1), win(kc, 0), win(kc, 1), win(vc, -1), win(vc, 0), win(vc, 1),
                ctxspec(kc), ctxspec(vc),
                tab(0), tab(0), tab(-1), tab(-1), tab(1), tab(1),
                pl.BlockSpec((1, sink.shape[1]), lambda bb, n: (0, 0))]
    return pl.pallas_call(
        functools.partial(_attn_kernel, n_blocks=nb, kv_heads=kv_heads, group=group),
        out_shape=jax.ShapeDtypeStruct((b, seq, ad), BF16),
        grid=(b, nb),
        in_specs=in_specs,
        out_specs=pl.BlockSpec((1, qb, ad), lambda bb, n: (bb, n, 0)),
        compiler_params=_params("parallel", "parallel"),
        name="window_ctx_attention",
    )(p_all, p_all, p_all, p_all, p_all, p_all, p_all, p_all, p_all,
      cos_t, sin_t, cos_t, sin_t, cos_t, sin_t, sink)


def _outproj_kernel(yf_ref, yr_ref, bon_ref, gat_ref, att_ref, x_ref, lg_ref, lb_ref, w_ref,
                    m_ref, g2_ref, wr_ref, x1_ref, h2_ref, lt_ref):
    ysum = yf_ref[0] + yr_ref[0]
    ri = lax.broadcasted_iota(jnp.int32, (LANES, LANES), 0) // HEAD_DIM
    ci = lax.broadcasted_iota(jnp.int32, (LANES, LANES), 1) // HEAD_DIM
    hm = _bf(jnp.where(ri == ci, 1.0 / HEAD_DIM, 0.0))

    def head_mean(z):
        hi = _bf(z)
        lo = _bf(z - hi.astype(F32))
        return _dot(hi, hm) + _dot(lo, hm)

    ybs = [ysum[:, j * LANES:(j + 1) * LANES] for j in range(ysum.shape[1] // LANES)]
    devs = [yb - mu for yb, mu in zip(ybs, [head_mean(yb) for yb in ybs])]
    blocks = [dev * lax.rsqrt(var + LNX_EPS)
              for dev, var in zip(devs, [head_mean(dev * dev) for dev in devs])]
    yn = jnp.concatenate(blocks, axis=1) if len(blocks) > 1 else blocks[0]
    rw = (yn * lg_ref[...] + lb_ref[...] + bon_ref[0]) * gat_ref[0]
    mix = jnp.concatenate([_bf(rw), att_ref[0]], axis=1)
    m = m_ref[0]
    x1 = x_ref[0] + m[0:1] * _dot(mix, w_ref[...])
    x1_ref[0] = x1
    ms = jnp.mean(x1 * x1, axis=-1, keepdims=True)
    h2 = x1 * lax.rsqrt(ms + NORM_EPS) * g2_ref[...] * (1.0 + m[2:3]) + m[1:2]
    h2_ref[0] = h2
    hi = _bf(h2)
    lo = _bf(h2 - hi.astype(F32))
    wr = wr_ref[...]
    wr_hi = _bf(wr)
    wr_lo = _bf(wr - wr_hi.astype(F32))
    lg = _dot(hi, wr_hi) + (_dot(hi, wr_lo) + _dot(lo, wr_hi))
    lt_ref[...] = lg.T[:lt_ref.shape[0]]


def _out_projection(y_f, y_r, bonus, gate, attn, x, lnx_g, lnx_b, w_out_b, m2, g2, w_router, ctx_len, tr):
    b, t, d = x.shape
    rd = y_f.shape[2]
    ad = attn.shape[2]
    ne = w_router.shape[1]
    assert ne <= LANES
    wr = _pad_to(w_router, 1, LANES)
    co = ctx_len // tr
    nt = t // tr
    return pl.pallas_call(
        _outproj_kernel,
        out_shape=[jax.ShapeDtypeStruct((b, t, d), F32),
                   jax.ShapeDtypeStruct((b, t, d), F32),
                   jax.ShapeDtypeStruct((ne, b * t), F32)],
        grid=(b, nt),
        in_specs=[pl.BlockSpec((1, tr, rd), lambda bb, i: (bb, co + i, 0)),
                  pl.BlockSpec((1, tr, rd), lambda bb, i: (bb, co + i, 0)),
                  pl.BlockSpec((1, tr, rd), lambda bb, i: (bb, co + i, 0)),
                  pl.BlockSpec((1, tr, rd), lambda bb, i: (bb, co + i, 0)),
                  pl.BlockSpec((1, tr, ad), lambda bb, i: (bb, i, 0)),
                  pl.BlockSpec((1, tr, d), lambda bb, i: (bb, i, 0)),
                  pl.BlockSpec((1, rd), lambda bb, i: (0, 0)),
                  pl.BlockSpec((1, rd), lambda bb, i: (0, 0)),
                  pl.BlockSpec((rd + ad, d), lambda bb, i: (0, 0)),
                  pl.BlockSpec((1, 4, d), lambda bb, i: (bb, 0, 0)),
                  pl.BlockSpec((1, d), lambda bb, i: (0, 0)),
                  pl.BlockSpec((d, LANES), lambda bb, i: (0, 0))],
        out_specs=[pl.BlockSpec((1, tr, d), lambda bb, i: (bb, i, 0)),
                   pl.BlockSpec((1, tr, d), lambda bb, i: (bb, i, 0)),
                   pl.BlockSpec((ne, tr), lambda bb, i: (0, bb * nt + i))],
        compiler_params=_params("parallel", "parallel"),
        name="rwkv_out_outproj_norm2_router",
    )(y_f, y_r, bonus, gate, attn, x, lnx_g, lnx_b, w_out_b, m2, g2, wr)


def _router_kernel(l_ref, b_ref, idx_ref, gate_ref, rank_ref, cnt_ref, carry_ref):
    @pl.when(pl.program_id(0) == 0)
    def _():
        carry_ref[...] = jnp.zeros_like(carry_ref)

    sc = jax.nn.sigmoid(l_ref[...])
    ne, tn = sc.shape
    eg = ne // N_GROUPS
    bz = sc + b_ref[...]
    g3 = bz.reshape(N_GROUPS, eg, tn)
    io = lax.broadcasted_iota(jnp.int32, (N_GROUPS, eg, tn), 1).astype(F32)
    m1 = jnp.max(g3, axis=1, keepdims=True)
    i1 = jnp.min(jnp.where(g3 == m1, io, float(eg)), axis=1, keepdims=True)
    m2 = jnp.max(jnp.where(io == i1, -jnp.inf, g3), axis=1, keepdims=True)
    gs = (m1 + m2).reshape(N_GROUPS, tn)

    def topk_hits(vals, k):
        n = vals.shape[0]
        idx = lax.broadcasted_iota(jnp.int32, vals.shape, 0).astype(F32)
        hits = []
        cur = vals
        for _ in range(k):
            mx = jnp.max(cur, axis=0, keepdims=True)
            ix = jnp.min(jnp.where(cur == mx, idx, float(n)), axis=0, keepdims=True)
            hit = idx == ix
            hits.append(hit)
            cur = jnp.where(hit, -jnp.inf, cur)
        return hits

    gsel = sum(jnp.where(h, 1.0, 0.0) for h in topk_hits(gs, TOPK_GROUPS))
    masked = jnp.where(gsel.reshape(N_GROUPS, 1, tn) > 0.0, g3, -jnp.inf).reshape(ne, tn)
    hits = topk_hits(masked, TOP_K)
    chosen = sum(jnp.where(h, 1.0, 0.0) for h in hits)
    denom = jnp.sum(chosen * sc, axis=0, keepdims=True)

    si = lax.broadcasted_iota(jnp.int32, (tn, tn), 0)
    ti = lax.broadcasted_iota(jnp.int32, (tn, tn), 1)
    before = _bf(jnp.where(si < ti, 1.0, 0.0))
    rank_full = _dot(_bf(chosen), before) + carry_ref[...]
    carry_ref[...] += jnp.sum(chosen, axis=1, keepdims=True)
    cnt_ref[...] = carry_ref[...]

    eidx = lax.broadcasted_iota(jnp.int32, (ne, tn), 0).astype(F32)
    for j, hit in enumerate(hits):
        idx_ref[j:j + 1, :] = jnp.sum(jnp.where(hit, eidx, 0.0), axis=0, keepdims=True).astype(jnp.int32)
        gate_ref[j:j + 1, :] = (jnp.sum(jnp.where(hit, sc, 0.0), axis=0, keepdims=True)
                                / denom * ROUTED_SCALE)
        rank_ref[j:j + 1, :] = jnp.sum(jnp.where(hit, rank_full, 0.0), axis=0,
                                       keepdims=True).astype(jnp.int32)


def _router(logits_t, bias_col):
    ne, n = logits_t.shape
    tn = 1024 if n % 1024 == 0 else n
    choice = lambda: pl.BlockSpec((TOP_K, tn), lambda i: (0, i))
    return pl.pallas_call(
        _router_kernel,
        out_shape=[jax.ShapeDtypeStruct((TOP_K, n), jnp.int32),
                   jax.ShapeDtypeStruct((TOP_K, n), F32),
                   jax.ShapeDtypeStruct((TOP_K, n), jnp.int32),
                   jax.ShapeDtypeStruct((ne, 1), F32)],
        grid=(n // tn,),
        in_specs=[pl.BlockSpec((ne, tn), lambda i: (0, i)),
                  pl.BlockSpec((ne, 1), lambda i: (0, 0))],
        out_specs=[choice(), choice(), choice(), pl.BlockSpec((ne, 1), lambda i: (0, 0))],
        scratch_shapes=[pltpu.VMEM((ne, 1), F32)],
        compiler_params=_params("arbitrary"),
        name="group_topk_router",
    )(logits_t, bias_col)


def _slots_kernel(idx_ref, rank_ref, ps_ref, o_ref):
    ne, tn = ps_ref.shape[0], idx_ref.shape[1]
    eidx = lax.broadcasted_iota(jnp.int32, (ne, tn), 0)
    for j in range(TOP_K):
        start = jnp.sum(jnp.where(eidx == idx_ref[j:j + 1, :], ps_ref[...], 0.0), axis=0, keepdims=True)
        o_ref[j:j + 1, :] = start.astype(jnp.int32) + rank_ref[j:j + 1, :]


def _slot_positions(idx8, rank8, pstart_col):
    _, n = idx8.shape
    ne = pstart_col.shape[0]
    tn = 1024 if n % 1024 == 0 else n
    choice = lambda: pl.BlockSpec((TOP_K, tn), lambda i: (0, i))
    return pl.pallas_call(
        _slots_kernel,
        out_shape=jax.ShapeDtypeStruct((TOP_K, n), jnp.int32),
        grid=(n // tn,),
        in_specs=[choice(), choice(), pl.BlockSpec((ne, 1), lambda i: (0, 0))],
        out_specs=choice(),
        compiler_params=_params("parallel"),
        name="expert_slot_positions",
    )(idx8, rank8, pstart_col)


def _slot_table_kernel(pos_ref, last_blk_ref, tail_ref, tok_ref, *, bm):
    def clear_group(base):
        for i in range(SUBLANES):
            tok_ref[base + i] = 0

    def per_expert(e, carry):
        def clear(g, c2):
            clear_group(last_blk_ref[e] + g * SUBLANES)
            return c2

        lax.fori_loop(0, bm // SUBLANES, clear, 0)
        return carry

    lax.fori_loop(0, last_blk_ref.shape[0], per_expert, 0)

    tail = tail_ref[0]

    def clear_tail(g, carry):
        clear_group(tail + g * SUBLANES)
        return carry

    lax.fori_loop(0, (tok_ref.shape[0] - tail) // SUBLANES, clear_tail, 0)

    def fill(t, carry):
        for k in range(TOP_K):
            tok_ref[pos_ref[k, t]] = t
        return carry

    lax.fori_loop(0, pos_ref.shape[1], fill, 0, unroll=4)


def _slot_table(pos8, last_blk_start, tail_start, n_slots, bm):
    smem = lambda: pl.BlockSpec(memory_space=pltpu.SMEM)
    return pl.pallas_call(
        functools.partial(_slot_table_kernel, bm=bm),
        out_shape=jax.ShapeDtypeStruct((n_slots,), jnp.int32),
        in_specs=[smem(), smem(), smem()],
        out_specs=smem(),
        name="expert_slot_tokens",
    )(pos8, last_blk_start, tail_start)


def _expert_ffn_kernel(be_ref, nu_ref, nx_ref, sl_ref, tok_ref, tokn_ref, h_hbm, wg_hbm, wu_hbm, wd_hbm,
                       y_ref, xbuf0, xbuf1, wgf, wuf, wdf, wgb, wub, wdb, sems, xsems):
    j = pl.program_id(0)
    e = be_ref[j]
    n_used = nu_ref[0]
    used = j < n_used
    fresh = jnp.logical_or(j == 0, e != be_ref[jnp.maximum(j - 1, 0)])
    bm = xbuf0.shape[0]
    xbufs = (xbuf0, xbuf1)

    def gather_rows(toks, slot):
        for r in range(bm):
            pltpu.make_async_copy(h_hbm.at[pl.ds(toks[r], 1)], xbufs[slot].at[pl.ds(r, 1)],
                                  xsems.at[slot]).start(priority=ROW_DMA_PRIORITY)

    def wait_rows(slot):
        pltpu.make_async_copy(h_hbm.at[pl.ds(0, bm)], xbufs[slot], xsems.at[slot]).wait()

    @pl.when(j == 0)
    def _():
        gather_rows(tok_ref, 0)

    def weight_copies(expert, slot):
        return [pltpu.make_async_copy(src.at[expert], dst.at[slot], sems.at[slot])
                for src, dst in ((wg_hbm, wgf), (wu_hbm, wuf), (wd_hbm, wdf))]

    @pl.when(j == 0)
    def _():
        for cp in weight_copies(e, sl_ref[e]):
            cp.start(priority=WEIGHT_DMA_PRIORITY)

    @pl.when(jnp.logical_and(used, fresh))
    def _():
        slot = sl_ref[e]
        for cp in weight_copies(e, slot):
            cp.wait()
        nxt = nx_ref[e]

        @pl.when(nxt >= 0)
        def _():
            for cp in weight_copies(nxt, 1 - slot):
                cp.start(priority=WEIGHT_DMA_PRIORITY)

        wgb[...] = _bf(wgf[slot])
        wub[...] = _bf(wuf[slot])
        wdb[...] = _bf(wdf[slot])

    for parity in range(2):
        @pl.when(jnp.logical_and(used, j % 2 == parity))
        def _(parity=parity):
            wait_rows(parity)
            gather_rows(tokn_ref, 1 - parity)
            x = _bf(xbufs[parity][...])
            a = _dot(x, wgb[...])
            u = _dot(x, wub[...])
            y_ref[...] = _dot(_bf(a * jax.nn.sigmoid(a) * u), wdb[...])

        @pl.when(jnp.logical_and(j == n_used - 1, j % 2 == parity))
        def _(parity=parity):
            wait_rows(1 - parity)

    @pl.when(jnp.logical_not(used))
    def _():
        y_ref[...] = jnp.zeros_like(y_ref)


def _expert_ffn(h2, slot_tok, blk_expert, n_used, next_expert, expert_slot, e_gate, e_up, e_down, bm):
    n_slots = slot_tok.shape[0]
    _, d = h2.shape
    ne, _, ff = e_gate.shape
    n_blk = n_slots // bm
    grid_spec = pltpu.PrefetchScalarGridSpec(
        num_scalar_prefetch=4,
        grid=(n_blk,),
        in_specs=[pl.BlockSpec((bm,), lambda j, be, nu, nx, sl: (jnp.minimum(j, nu[0] - 1),),
                               memory_space=pltpu.SMEM),
                  pl.BlockSpec((bm,), lambda j, be, nu, nx, sl: (jnp.minimum(j + 1, nu[0] - 1),),
                               memory_space=pltpu.SMEM),
                  pl.BlockSpec(memory_space=pl.ANY),
                  pl.BlockSpec(memory_space=pl.ANY),
                  pl.BlockSpec(memory_space=pl.ANY),
                  pl.BlockSpec(memory_space=pl.ANY)],
        out_specs=pl.BlockSpec((bm, d), lambda j, be, nu, nx, sl: (j, 0)),
        scratch_shapes=[pltpu.VMEM((bm, d), F32), pltpu.VMEM((bm, d), F32),
                        pltpu.VMEM((2, d, ff), F32), pltpu.VMEM((2, d, ff), F32), pltpu.VMEM((2, ff, d), F32),
                        pltpu.VMEM((d, ff), BF16), pltpu.VMEM((d, ff), BF16), pltpu.VMEM((ff, d), BF16),
                        pltpu.SemaphoreType.DMA((2,)), pltpu.SemaphoreType.DMA((2,))])
    return pl.pallas_call(
        _expert_ffn_kernel,
        out_shape=jax.ShapeDtypeStruct((n_slots, d), F32),
        grid_spec=grid_spec,
        compiler_params=_params("arbitrary"),
        name="routed_expert_ffn",
    )(blk_expert, n_used, next_expert, expert_slot, slot_tok, slot_tok, h2, e_gate, e_up, e_down)


def _final_kernel(pos_ref, h_ref, x1_ref, g_ref, ys_ref, wg_ref, wu_ref, wd_ref, m_ref, fg_ref,
                  o_ref, buf, sem):
    tr = h_ref.shape[1]

    def issue(t, carry):
        for k in range(TOP_K):
            pltpu.make_async_copy(ys_ref.at[pl.ds(pos_ref[k, t], 1)], buf.at[k, pl.ds(t, 1)],
                                  sem).start(priority=1)
        return carry

    lax.fori_loop(0, tr, issue, 0)
    h = _bf(h_ref[0])
    a = _dot(h, wg_ref[...])
    u = _dot(h, wu_ref[...])
    moe = _dot(_bf(a * jax.nn.sigmoid(a) * u), wd_ref[...])
    g = g_ref[...]
    for k in range(TOP_K):
        pltpu.make_async_copy(ys_ref.at[pl.ds(0, tr)], buf.at[k], sem).wait()
    for k in range(TOP_K):
        moe = moe + g[:, k:k + 1] * buf[k]
    x2 = x1_ref[0] + m_ref[0, 3:4, :] * moe
    ms = jnp.mean(x2 * x2, axis=-1, keepdims=True)
    o_ref[0] = x2 * lax.rsqrt(ms + NORM_EPS) * fg_ref[...]


def _final(h2, x1, pos8, gate_rows, ys, s_gate_b, s_up_b, s_down_b, m2, final_g, tr):
    b, t, d = x1.shape
    ff = s_gate_b.shape[1]
    nt = t // tr
    row = lambda: pl.BlockSpec((1, tr, d), lambda bb, i: (bb, i, 0))
    return pl.pallas_call(
        _final_kernel,
        out_shape=jax.ShapeDtypeStruct((b, t, d), F32),
        grid=(b, nt),
        in_specs=[pl.BlockSpec((TOP_K, tr), lambda bb, i: (0, bb * nt + i), memory_space=pltpu.SMEM),
                  row(), row(),
                  pl.BlockSpec((tr, TOP_K), lambda bb, i: (bb * nt + i, 0)),
                  pl.BlockSpec(memory_space=pl.ANY),
                  pl.BlockSpec((d, ff), lambda bb, i: (0, 0)),
                  pl.BlockSpec((d, ff), lambda bb, i: (0, 0)),
                  pl.BlockSpec((ff, d), lambda bb, i: (0, 0)),
                  pl.BlockSpec((1, 4, d), lambda bb, i: (bb, 0, 0)),
                  pl.BlockSpec((1, d), lambda bb, i: (0, 0))],
        out_specs=row(),
        scratch_shapes=[pltpu.VMEM((TOP_K, tr, d), F32), pltpu.SemaphoreType.DMA(())],
        compiler_params=_params("parallel", "parallel"),
        name="combine_shared_expert_final_norm",
    )(pos8, h2, x1, gate_rows, ys, s_gate_b, s_up_b, s_down_b, m2, final_g)


def _pad_to(a, axis, n):
    pad = [(0, 0)] * a.ndim
    pad[axis] = (0, n - a.shape[axis])
    return jnp.pad(a, pad)


def _layout(rd, wl, al, gl, ad, kvd):
    assert wl <= LANES and al <= LANES
    lay = {"rd": rd, "gl": gl, "ad": ad, "kvd": kvd}
    off = 0
    for name, width in (("q", ad), ("r", rd), ("k", rd), ("v", rd), ("ka", kvd), ("va", kvd),
                        ("xw", LANES), ("xa", LANES), ("xg", gl)):
        assert width % LANES == 0
        lay[name] = off
        off += width
    lay["ncols"] = off
    return lay


def kernel(x, c, ctx, c_ctx, w_mod, b_mod, norm1_g, norm2_g, w_in, shift_mu, k_k, k_a, r_k, decay_bias, decay_up, iclr_bias, iclr_up, gate_up, lnx_g, lnx_b, attn_sink, w_out, w_router, expert_bias, e_gate, e_up, e_down, s_gate, s_up, s_down, final_g):
    assert w_mod.shape[0] == 1, "single-layer block only"
    b, t, d = x.shape
    ctx_len = ctx.shape[1]
    rh, hd = r_k.shape[1:]
    assert hd == HEAD_DIM and b + 1 <= SUBLANES
    rd = rh * hd
    wl, al, gl = decay_up.shape[2], iclr_up.shape[2], gate_up.shape[1]
    ad = attn_sink.shape[1] * hd
    kvd = (w_in.shape[2] - (3 * rd + wl + al + gl) - ad) // 2
    ne = w_router.shape[2]
    lay = _layout(rd, wl, al, gl, ad, kvd)
    tr = 256 if ctx_len % 256 == 0 and t % 256 == 0 else CHUNK
    assert ctx_len % CHUNK == 0 and t % CHUNK == 0 and t % QBLOCK == 0 and WINDOW == QBLOCK

    c8 = _pad_to(jnp.concatenate([c, c_ctx[None]], axis=0), 0, SUBLANES)
    mod = _modulation(c8, w_mod[0], b_mod[0][None])
    mod6 = mod.reshape(SUBLANES, 6, d)
    mx, mc = mod6[:b], mod6[b]
    m1 = jnp.stack([jnp.broadcast_to(mc[None, 0:2], (b, 2, d)), mx[:, 0:2]], axis=1)
    m2 = jnp.stack([mx[:, 2], mx[:, 3], mx[:, 4], mx[:, 5]], axis=1)

    wi = w_in[0]
    o_w, o_a, o_g, o_q = 3 * rd, 3 * rd + wl, 3 * rd + wl + al, 3 * rd + wl + al + gl
    w_in_p = _bf(jnp.concatenate([
        wi[:, o_q:o_q + ad], wi[:, :3 * rd], wi[:, o_q + ad:],
        _pad_to(wi[:, o_w:o_a], 1, LANES), _pad_to(wi[:, o_a:o_g], 1, LANES), wi[:, o_g:o_q]], axis=1))
    mu = shift_mu[0]
    prm = {
        "mu_r": mu[None, :rd], "mu_k": mu[None, rd:2 * rd], "mu_v": mu[None, 2 * rd:3 * rd],
        "mu_w": _pad_to(mu[None, o_w:o_a], 1, LANES), "mu_a": _pad_to(mu[None, o_a:o_g], 1, LANES),
        "mu_g": mu[None, o_g:o_q],
        "k_k": k_k, "k_a": k_a, "r_k": r_k.reshape(1, rd),
        "decay_bias": decay_bias[0], "decay_up": _bf(_pad_to(decay_up[0], 1, LANES)),
        "iclr_bias": iclr_bias[0], "iclr_up": _bf(_pad_to(iclr_up[0], 1, LANES)),
        "gate_up": _bf(gate_up[0]),
    }

    p_all = _in_projection(ctx, x, norm1_g, m1, w_in_p, tr)

    npair = rd // LANES
    pp = max(n for n in (1, 2, 4, 8) if npair % n == 0
             and all(lay[key] % (n * LANES) == 0 for key in ("r", "k", "v")))
    rp, y0, mm, nn, bonus, gate = _wkv_prep(p_all, lay, prm, ctx_len, None, pp)
    y_f, y_r = _wkv_sweep(rp, y0, mm, nn, ctx_len)

    nf = HEAD_DIM // 4
    freqs = ROPE_BASE ** (-jnp.arange(nf, dtype=F32) / nf)
    pos = jnp.arange(t, dtype=jnp.int32)
    ang_r = (pos // GRID_W).astype(F32)[:, None] * freqs
    ang_c = (pos % GRID_W).astype(F32)[:, None] * freqs
    cos_h = jnp.concatenate([jnp.cos(ang_r)] * 2 + [jnp.cos(ang_c)] * 2, axis=1)
    sin_h = jnp.concatenate([-jnp.sin(ang_r), jnp.sin(ang_r), -jnp.sin(ang_c), jnp.sin(ang_c)], axis=1)
    cos_t = jnp.concatenate([cos_h, cos_h], axis=1)
    sin_t = jnp.concatenate([sin_h, sin_h], axis=1)
    attn = _attention(p_all, lay, cos_t, sin_t, attn_sink, ctx_len, t)

    x1, h2, logits_t = _out_projection(y_f, y_r, bonus, gate, attn, x, lnx_g, lnx_b, _bf(w_out[0]), m2,
                                       norm2_g, w_router[0], ctx_len, tr)
    idx8, gate8, rank8, counts = _router(logits_t, expert_bias[0][:, None])

    n_tok = b * t
    bm = EXPERT_ROWS
    assert (n_tok * TOP_K) % bm == 0
    n_blk = n_tok * TOP_K // bm + ne
    cnt = counts[:, 0].astype(jnp.int32)
    padded = (cnt + bm - 1) // bm * bm
    pend = jnp.cumsum(padded)
    pstart = pend - padded
    pos8 = _slot_positions(idx8, rank8, pstart.astype(F32)[:, None])
    blk_row0 = jnp.arange(n_blk, dtype=jnp.int32) * bm
    blk_expert = jnp.minimum(jnp.sum((pend[None, :] <= blk_row0[:, None]).astype(jnp.int32), axis=1), ne - 1)
    n_used = (pend[-1:] // bm).astype(jnp.int32)

    slot_tok = _slot_table(pos8, jnp.maximum(pend - bm, 0), pend[-1:], n_blk * bm, bm)
    has_rows = cnt > 0
    eids = jnp.arange(ne, dtype=jnp.int32)
    later = jnp.where(has_rows[None, :] & (eids[None, :] > eids[:, None]), eids[None, :], ne)
    next_expert = jnp.min(later, axis=1)
    next_expert = jnp.where(next_expert < ne, next_expert, -1).astype(jnp.int32)
    expert_slot = ((jnp.cumsum(has_rows.astype(jnp.int32)) - 1) % 2).astype(jnp.int32)
    ys = _expert_ffn(h2.reshape(n_tok, d), slot_tok, blk_expert, n_used, next_expert, expert_slot,
                     e_gate[0], e_up[0], e_down[0], bm)
    return _final(h2, x1, pos8, gate8.T, ys, _bf(s_gate[0]), _bf(s_up[0]), _bf(s_down[0]),
                  m2, final_g[None], tr)
```

```python
import functools
import math

import jax
import jax.numpy as jnp
from jax import lax
from jax.experimental import pallas as pl
from jax.experimental.pallas import tpu as pltpu

GRID_W = 64
WINDOW = 128
ROPE_BASE = 10000.0
N_GROUPS = 8
TOPK_GROUPS = 4
TOP_K = 8
ROUTED_SCALE = 2.5
NORM_EPS = 1e-6
LNX_EPS = 64e-5
KK_EPS = 1e-12

LANES = 128
SUBLANES = 8
VMEM_LIMIT_BYTES = 56 * 1024 * 1024

HEAD_DIM = 64
CHUNK = 64
QBLOCK = 128
EXPERT_ROWS = 256

F32 = jnp.float32
BF16 = jnp.bfloat16
NEG_BIG = -1e30


def _bf(x):
    return x.astype(BF16)


def _dot(a, b, precision=None):
    return jnp.dot(a, b, preferred_element_type=F32, precision=precision)


def _dot_nt(a, b, precision=None):
    return lax.dot_general(a, b, (((1,), (1,)), ((), ())), preferred_element_type=F32,
                           precision=precision)


def _dot_tn(a, b):
    return lax.dot_general(a, b, (((0,), (0,)), ((), ())), preferred_element_type=F32)


def _params(*sem):
    return pltpu.CompilerParams(dimension_semantics=sem, vmem_limit_bytes=VMEM_LIMIT_BYTES)


def _mod_kernel(c_ref, w_ref, b_ref, o_ref):
    c = c_ref[...]
    s = c * jax.nn.sigmoid(c)
    hi = _bf(s)
    lo = _bf(s - hi.astype(F32))
    r = _dot(jnp.concatenate([hi, lo], axis=0), _bf(w_ref[...]))
    o_ref[...] = r[:SUBLANES] + r[SUBLANES:] + b_ref[...]


def _modulation(c8, w_mod, b_mod):
    d, n = w_mod.shape
    tn = 1024 if n % 1024 == 0 else n
    return pl.pallas_call(
        _mod_kernel,
        out_shape=jax.ShapeDtypeStruct((SUBLANES, n), F32),
        grid=(n // tn,),
        in_specs=[pl.BlockSpec((SUBLANES, d), lambda j: (0, 0)),
                  pl.BlockSpec((d, tn), lambda j: (0, j)),
                  pl.BlockSpec((1, tn), lambda j: (0, j))],
        out_specs=pl.BlockSpec((SUBLANES, tn), lambda j: (0, j)),
        compiler_params=_params("parallel"),
        name="adaln_modulation",
    )(c8, w_mod, b_mod)


def _inproj_kernel(c_ref, x_ref, g_ref, m_ref, w_ref, o_ref, *, nct):
    x = jnp.where(pl.program_id(2) < nct, c_ref[0], x_ref[0])
    ms = jnp.mean(x * x, axis=-1, keepdims=True)
    y = x * lax.rsqrt(ms + NORM_EPS) * g_ref[...]
    m = m_ref[0, 0]
    h = y * (1.0 + m[1:2]) + m[0:1]
    o_ref[0] = _dot(_bf(h), w_ref[...])


def _in_projection(ctx, x, g1, m1, w_in_p, tr):
    b, t, d = x.shape
    ctx_len = ctx.shape[1]
    ta = ctx_len + t
    ncols = w_in_p.shape[1]
    tn = ncols // 2 if (ncols // 2) % LANES == 0 and ncols % 2 == 0 else ncols
    nct = ctx_len // tr
    return pl.pallas_call(
        functools.partial(_inproj_kernel, nct=nct),
        out_shape=jax.ShapeDtypeStruct((b, ta, ncols), F32),
        grid=(ncols // tn, b, ta // tr),
        in_specs=[pl.BlockSpec((1, tr, d), lambda n, bb, i: (bb, jnp.minimum(i, nct - 1), 0)),
                  pl.BlockSpec((1, tr, d), lambda n, bb, i: (bb, jnp.maximum(i - nct, 0), 0)),
                  pl.BlockSpec((1, d), lambda n, bb, i: (0, 0)),
                  pl.BlockSpec((1, 1, 2, d), lambda n, bb, i: (bb, jnp.where(i >= nct, 1, 0), 0, 0)),
                  pl.BlockSpec((d, tn), lambda n, bb, i: (0, n))],
        out_specs=pl.BlockSpec((1, tr, tn), lambda n, bb, i: (bb, i, n)),
        compiler_params=_params("parallel", "parallel", "parallel"),
        name="norm_modulate_inproj",
    )(ctx, x, g1, m1, w_in_p)


def _wkv_prep_kernel(r_c, r_p, r_n, k_c, k_p, k_n, v_c, v_p, v_n, w_c, w_p, w_n, a_c, a_p, a_n,
                     g_c, g_p, g_n, mu_r, mu_k, mu_v, mu_w, mu_a, mu_g, kk_ref, ka_ref, rk_ref,
                     db_ref, du_ref, ib_ref, iu_ref, gu_ref,
                     rp_out, y0_out, m_out, n_out, bonus_out, gate_out,
                     *, nctx_chunks, n_chunks, prec_t):
    L = CHUNK
    c = pl.program_id(1)
    has_prev = jnp.logical_and(c != 0, c != nctx_chunks)
    has_next = jnp.logical_and(c != nctx_chunks - 1, c != n_chunks - 1)

    def shifted(cur, prv, nxt, mu):
        x = cur[0]
        rows = lax.broadcasted_iota(jnp.int32, (L, 1), 0)
        pr = jnp.where(has_prev, prv[0, SUBLANES - 1:SUBLANES, :], 0.0)
        nx = jnp.where(has_next, nxt[0, 0:1, :], 0.0)
        up = jnp.where(rows == 0, pr, pltpu.roll(x, 1, 0))
        dn = jnp.where(rows == L - 1, nx, pltpu.roll(x, L - 1, 0))
        return x + mu[...] * (0.5 * (up + dn) - x)

    r = shifted(r_c, r_p, r_n, mu_r)
    k = shifted(k_c, k_p, k_n, mu_k)
    v = shifted(v_c, v_p, v_n, mu_v)
    xw = shifted(w_c, w_p, w_n, mu_w)
    xa = shifted(a_c, a_p, a_n, mu_a)
    xg = shifted(g_c, g_p, g_n, mu_g)

    lane = lax.broadcasted_iota(jnp.int32, (1, LANES), 1)
    m0 = lane < HEAD_DIM

    def head_sum(x):
        s0 = jnp.sum(jnp.where(m0, x, 0.0), axis=-1, keepdims=True)
        s1 = jnp.sum(jnp.where(m0, 0.0, x), axis=-1, keepdims=True)
        return jnp.where(m0, s0, s1)

    def stack2(x):
        return jnp.concatenate([jnp.where(m0, x, 0.0), jnp.where(m0, 0.0, x)], axis=0)

    tw = _bf(jnp.tanh(xw))
    xab = _bf(xa)
    lw_all = [(-math.exp(-0.5)) * jax.nn.sigmoid(db_ref[d:d + 1, :] + _dot(tw, du_ref[d]))
              for d in range(2)]
    sa_all = [jax.nn.sigmoid(ib_ref[d:d + 1, :] + _dot(xab, iu_ref[d])) for d in range(2)]
    gate_out[0] = _dot(_bf(jax.nn.sigmoid(xg)), gu_ref[...])

    ri = lax.broadcasted_iota(jnp.int32, (L, L), 0)
    ci = lax.broadcasted_iota(jnp.int32, (L, L), 1)
    ri2 = lax.broadcasted_iota(jnp.int32, (2 * L, 2 * L), 0)
    ci2 = lax.broadcasted_iota(jnp.int32, (2 * L, 2 * L), 1)
    rt = jnp.bitwise_and(ri2, L - 1)
    ct = jnp.bitwise_and(ci2, L - 1)
    eye = ri2 == ci2
    n_sq = int(math.log2(L)) - 1

    tri = [(ci <= ri).astype(F32), (ci >= ri).astype(F32)]
    strict = [ct < rt, ct > rt]
    incl = [ct <= rt, ct >= rt]
    cum = [_dot(tri[d], lw_all[d], precision=lax.Precision.HIGHEST) for d in range(2)]
    cend = [jnp.sum(lw_all[d], axis=0, keepdims=True) for d in range(2)]
    e_in = [jnp.exp(cum[d]) for d in range(2)]
    e_ex = [jnp.exp(cum[d] - lw_all[d]) for d in range(2)]
    e_neg = [jnp.exp(-cum[d]) for d in range(2)]
    e_end = [jnp.exp(cend[d] - cum[d]) for d in range(2)]
    p_end = [jnp.exp(cend[d]) for d in range(2)]
    kd_all = [k * (1.0 + (sa_all[d] - 1.0) * ka_ref[...]) for d in range(2)]

    npp = r.shape[1] // LANES
    chains = [(p, d) for p in range(npp) for d in range(2)]
    sls = [slice(p * LANES, (p + 1) * LANES) for p in range(npp)]
    kk_p, vs_p, vsb_p = [], [], []
    for p in range(npp):
        kx = k[:, sls[p]] * kk_ref[:, sls[p]]
        kk_p.append(kx * lax.rsqrt(head_sum(kx * kx) + KK_EPS))
        vs_p.append(stack2(v[:, sls[p]]))
        vsb_p.append(_bf(vs_p[p]))
        ksum = kd_all[0][:, sls[p]] + kd_all[1][:, sls[p]]
        bonus_out[0, :, sls[p]] = head_sum(r[:, sls[p]] * ksum * rk_ref[:, sls[p]]) * v[:, sls[p]]

    a_s, r_s, bh_s, kh_s, g = [], [], [], [], []
    for p, d in chains:
        sl = sls[p]
        kd = kd_all[d][:, sl]
        bvec = kk_p[p] * sa_all[d][:, sl]
        a_s.append(stack2(-kk_p[p] * e_ex[d][:, sl]))
        r_s.append(stack2(r[:, sl] * e_in[d][:, sl]))
        b_s = stack2(bvec * e_neg[d][:, sl])
        k_s = stack2(kd * e_neg[d][:, sl])
        bh_s.append(stack2(bvec * e_end[d][:, sl]))
        kh_s.append(stack2(kd * e_end[d][:, sl]))
        g.append(_dot_nt(_bf(jnp.concatenate([a_s[-1], r_s[-1]], axis=0)),
                         _bf(jnp.concatenate([b_s, k_s], axis=0))))
    a_ab = [jnp.where(strict[d], g[i][:2 * L, :2 * L], 0.0) for i, (p, d) in enumerate(chains)]
    a_ak = [jnp.where(strict[d], g[i][:2 * L, 2 * L:], 0.0) for i, (p, d) in enumerate(chains)]
    a_rb = [jnp.where(incl[d], g[i][2 * L:, :2 * L], 0.0) for i, (p, d) in enumerate(chains)]
    a_rk = [jnp.where(incl[d], g[i][2 * L:, 2 * L:], 0.0) for i, (p, d) in enumerate(chains)]

    ident = jnp.where(eye, 1.0, 0.0)
    tmat = [ident + a for a in a_ab]
    apow = a_ab
    for _ in range(n_sq):
        if prec_t is None:
            apow = [_dot(_bf(a), _bf(a)) for a in apow]
            tmat = [t + _dot(_bf(a), _bf(t)) for a, t in zip(apow, tmat)]
        else:
            apow = [_dot(a, a, precision=prec_t) for a in apow]
            tmat = [t + _dot(a, t, precision=prec_t) for a, t in zip(apow, tmat)]

    x1 = [_dot(_bf(a_ak[i]), vsb_p[p]) for i, (p, d) in enumerate(chains)]
    y0v = [_dot(_bf(a_rk[i]), vsb_p[p]) for i, (p, d) in enumerate(chains)]
    tu = [_dot(_bf(tmat[i]), _bf(jnp.concatenate([a_s[i], x1[i]], axis=1)))
          for i in range(len(chains))]
    ry = [_dot(_bf(a_rb[i]), _bf(tu[i])) for i in range(len(chains))]
    def fold(z):
        h = z.shape[0] // 2
        return z[:h] + z[h:]

    for i, (p, d) in enumerate(chains):
        rp_out[d, 0, 0, p] = _bf(fold(r_s[i] + ry[i][:, :LANES]))
        y0_out[d, 0, 0, p] = fold(ry[i][:, LANES:] + y0v[i])
    for i, (p, d) in enumerate(chains):
        m_out[d, 0, 0, p] = _bf(fold(jnp.where(eye, p_end[d][:, sls[p]], 0.0)
                                     + _dot_tn(_bf(tu[i][:, :LANES]), _bf(bh_s[i]))))
    for i, (p, d) in enumerate(chains):
        n_out[d, 0, 0, p] = fold(_dot_tn(_bf(jnp.concatenate([tu[i][:, LANES:], vs_p[p]], axis=0)),
                                         _bf(jnp.concatenate([bh_s[i], kh_s[i]], axis=0))))


def _wkv_prep(p_all, lay, prm, ctx_len, prec_t, pp):
    b, ta, _ = p_all.shape
    L = CHUNK
    nc = ta // L
    nctx = ctx_len // L
    npair = lay["rd"] // LANES
    gl = lay["gl"]
    rows8 = ta // SUBLANES

    def cur(colfn, width=LANES):
        return pl.BlockSpec((1, L, width), lambda bb, c, p: (bb, c, colfn(p)))

    def prv(colfn, width=LANES):
        return pl.BlockSpec((1, SUBLANES, width),
                            lambda bb, c, p: (bb, jnp.maximum(c * (L // SUBLANES) - 1, 0), colfn(p)))

    def nxt(colfn, width=LANES):
        return pl.BlockSpec((1, SUBLANES, width),
                            lambda bb, c, p: (bb, jnp.minimum((c + 1) * (L // SUBLANES), rows8 - 1), colfn(p)))

    def trio(colfn, width=LANES):
        return [cur(colfn, width), prv(colfn, width), nxt(colfn, width)]

    pw = pp * LANES
    assert npair % pp == 0 and all(lay[n] % pw == 0 for n in ("r", "k", "v"))
    rb, kb, vb = lay["r"] // pw, lay["k"] // pw, lay["v"] // pw
    wb, ab = lay["xw"] // LANES, lay["xa"] // LANES
    assert lay["xg"] % gl == 0
    gb = lay["xg"] // gl

    def vec(width=pw):
        return pl.BlockSpec((1, width), lambda bb, c, p: (0, p))

    def vec0(width=LANES):
        return pl.BlockSpec((1, width), lambda bb, c, p: (0, 0))

    in_specs = (trio(lambda p: rb + p, pw) + trio(lambda p: kb + p, pw) + trio(lambda p: vb + p, pw)
                + trio(lambda p: wb) + trio(lambda p: ab) + trio(lambda p: gb, gl)
                + [vec(), vec(), vec(), vec0(), vec0(), vec0(gl),
                   vec(), vec(), vec(),
                   pl.BlockSpec((2, pw), lambda bb, c, p: (0, p)),
                   pl.BlockSpec((2, LANES, pw), lambda bb, c, p: (0, 0, p)),
                   pl.BlockSpec((2, pw), lambda bb, c, p: (0, p)),
                   pl.BlockSpec((2, LANES, pw), lambda bb, c, p: (0, 0, p)),
                   pl.BlockSpec((gl, pw), lambda bb, c, p: (0, p))])

    def opspec(rows):
        return pl.BlockSpec((2, 1, 1, pp, rows, LANES), lambda bb, c, p: (0, c, bb, p, 0, 0))

    out_specs = [opspec(L), opspec(L), opspec(HEAD_DIM), opspec(HEAD_DIM),
                 pl.BlockSpec((1, L, pw), lambda bb, c, p: (bb, c, p)),
                 pl.BlockSpec((1, L, pw), lambda bb, c, p: (bb, c, p))]
    out_shape = [jax.ShapeDtypeStruct((2, nc, b, npair, L, LANES), BF16),
                 jax.ShapeDtypeStruct((2, nc, b, npair, L, LANES), F32),
                 jax.ShapeDtypeStruct((2, nc, b, npair, HEAD_DIM, LANES), BF16),
                 jax.ShapeDtypeStruct((2, nc, b, npair, HEAD_DIM, LANES), F32),
                 jax.ShapeDtypeStruct((b, ta, lay["rd"]), F32),
                 jax.ShapeDtypeStruct((b, ta, lay["rd"]), F32)]
    kern = functools.partial(_wkv_prep_kernel, nctx_chunks=nctx, n_chunks=nc, prec_t=prec_t)
    args = [p_all] * 18 + [prm["mu_r"], prm["mu_k"], prm["mu_v"], prm["mu_w"], prm["mu_a"], prm["mu_g"],
                           prm["k_k"], prm["k_a"], prm["r_k"], prm["decay_bias"], prm["decay_up"],
                           prm["iclr_bias"], prm["iclr_up"], prm["gate_up"]]
    return pl.pallas_call(
        kern, out_shape=out_shape, grid=(b, nc, npair // pp), in_specs=in_specs, out_specs=out_specs,
        compiler_params=_params("parallel", "parallel", "parallel"),
        name="wkv_chunk_operators",
    )(*args)


def _wkv_seq_kernel(rpf, y0f, mf, nf, rpr, y0r, mr, nr, yf_ref, yr_ref, s_ref, *, nb, npair):
    L = CHUNK

    @pl.when(pl.program_id(0) == 0)
    def _():
        s_ref[...] = jnp.zeros_like(s_ref)

    m0 = lax.broadcasted_iota(jnp.int32, (1, LANES), 1) < HEAD_DIM

    def unfold(z):
        zero = jnp.zeros_like(z)
        return jnp.concatenate([jnp.where(m0, z, zero), jnp.where(m0, zero, z)], axis=0)

    for d, (rp_ref, y0_ref, m_ref, n_ref, y_ref) in enumerate(((rpf, y0f, mf, nf, yf_ref),
                                                               (rpr, y0r, mr, nr, yr_ref))):
        for bb in range(nb):
            for p in range(npair):
                i = (d * nb + bb) * npair + p
                s0 = _bf(s_ref[i])
                ys = _dot_nt(unfold(rp_ref[0, 0, bb, p]), s0)
                y_ref[bb, :, p * LANES:(p + 1) * LANES] = ys[:L] + ys[L:] + y0_ref[0, 0, bb, p]
                s_ref[i] = _dot(s0, unfold(m_ref[0, 0, bb, p])) + unfold(n_ref[0, 0, bb, p])


def _wkv_sweep(rp, y0, mm, nn, ctx_len):
    _, nc, b, npair, _, _ = rp.shape
    L = CHUNK
    nctx = ctx_len // L

    def chunk_of(d, c):
        return c if d == 0 else jnp.where(c < nctx, nctx - 1 - c, nc - 1 - (c - nctx))

    def opspec(d, rows):
        return pl.BlockSpec((1, 1, b, npair, rows, LANES), lambda c: (d, chunk_of(d, c), 0, 0, 0, 0))

    def yspec(d):
        return pl.BlockSpec((b, L, npair * LANES), lambda c: (0, chunk_of(d, c), 0))

    rows = (L, L, HEAD_DIM, HEAD_DIM)
    y_shape = jax.ShapeDtypeStruct((b, nc * L, npair * LANES), F32)
    return pl.pallas_call(
        functools.partial(_wkv_seq_kernel, nb=b, npair=npair),
        out_shape=[y_shape, y_shape],
        grid=(nc,),
        in_specs=[opspec(d, r) for d in range(2) for r in rows],
        out_specs=[yspec(0), yspec(1)],
        scratch_shapes=[pltpu.VMEM((2 * b * npair, LANES, LANES), F32)],
        compiler_params=_params("arbitrary"),
        name="wkv_state_sweep",
    )(rp, y0, mm, nn, rp, y0, mm, nn)


def _attn_kernel(q_ref, kp_ref, kc_ref, kn_ref, vp_ref, vc_ref, vn_ref, kx_ref, vx_ref,
                 cq_ref, sq_ref, cp_ref, sp_ref, cn_ref, sn_ref, sink_ref, o_ref,
                 *, n_blocks, kv_heads, group):
    n = pl.program_id(1)
    qb = QBLOCK

    def rope(x, cos, sin):
        w = x.shape[1]
        reps = w // LANES
        cs = jnp.concatenate([cos] * reps, axis=1) if reps > 1 else cos
        sn = jnp.concatenate([sin] * reps, axis=1) if reps > 1 else sin
        lane = lax.broadcasted_iota(jnp.int32, (1, w), 1)
        sw = jnp.where(jnp.bitwise_and(lane, 16) == 0, pltpu.roll(x, w - 16, 1), pltpu.roll(x, 16, 1))
        return x * cs + sw * sn

    q = rope(q_ref[0], cq_ref[...], sq_ref[...]) * (HEAD_DIM ** -0.5)
    kwin = jnp.concatenate([rope(kp_ref[0], cp_ref[...], sp_ref[...]),
                            rope(kc_ref[0], cq_ref[...], sq_ref[...]),
                            rope(kn_ref[0], cn_ref[...], sn_ref[...]),
                            kx_ref[0]], axis=0)
    vwin = jnp.concatenate([vp_ref[0], vc_ref[0], vn_ref[0], vx_ref[0]], axis=0)
    nkeys = kwin.shape[0]

    rows = group * qb
    qi = jnp.bitwise_and(lax.broadcasted_iota(jnp.int32, (rows, nkeys), 0), qb - 1)
    s_i = lax.broadcasted_iota(jnp.int32, (rows, nkeys), 1)
    blk = s_i // qb + (n - 1)
    in_win = (s_i >= qi) & (s_i <= qi + 2 * WINDOW) & (blk >= 0) & (blk < n_blocks)
    valid = in_win | (s_i >= 3 * qb)
    rgrp = lax.broadcasted_iota(jnp.int32, (rows, 1), 0) // qb

    scores, sinks = [], []
    for kh in range(kv_heads):
        kk_ = _bf(kwin[:, kh * HEAD_DIM:(kh + 1) * HEAD_DIM])
        qs = jnp.concatenate([q[:, (kh * group + g) * HEAD_DIM:(kh * group + g + 1) * HEAD_DIM]
                              for g in range(group)], axis=0)
        sk = jnp.zeros((rows, 1), F32)
        for g in range(group):
            h = kh * group + g
            sk = jnp.where(rgrp == g, sink_ref[0:1, h:h + 1], sk)
        sinks.append(sk)
        scores.append(jnp.where(valid, _dot_nt(_bf(qs), kk_), NEG_BIG))
    probs, denoms = [], []
    for s, sk in zip(scores, sinks):
        m = jnp.maximum(jnp.max(s, axis=-1, keepdims=True), sk)
        e = jnp.exp(s - m)
        denoms.append(jnp.sum(e, axis=-1, keepdims=True) + jnp.exp(sk - m))
        probs.append(_bf(e))
    pieces = []
    for kh in range(kv_heads):
        vv_ = _bf(vwin[:, kh * HEAD_DIM:(kh + 1) * HEAD_DIM])
        o = _dot(probs[kh], vv_) / denoms[kh]
        for g in range(group):
            pieces.append(o[g * qb:(g + 1) * qb])
    o_ref[0] = _bf(jnp.concatenate(pieces, axis=1))


def _attention(p_all, lay, cos_t, sin_t, sink, ctx_len, seq):
    b = p_all.shape[0]
    qb = QBLOCK
    nb = seq // qb
    cb = ctx_len // qb
    ad, kvd = lay["ad"], lay["kvd"]
    kv_heads = kvd // HEAD_DIM
    group = (ad // HEAD_DIM) // kv_heads
    assert lay["q"] % ad == 0 and lay["ka"] % kvd == 0 and lay["va"] % kvd == 0
    qc, kc, vc = lay["q"] // ad, lay["ka"] // kvd, lay["va"] // kvd

    def clampb(n, off):
        return jnp.clip(n + off, 0, nb - 1)

    def win(col, off):
        return pl.BlockSpec((1, qb, kvd), lambda bb, n: (bb, cb + clampb(n, off), col))

    def tab(off):
        return pl.BlockSpec((qb, LANES), lambda bb, n: (clampb(n, off), 0))

    assert ctx_len % qb == 0
    ctxspec = lambda col: pl.BlockSpec((1, ctx_len, kvd), lambda bb, n: (bb, 0, col))
    in_specs = [pl.BlockSpec((1, qb, ad), lambda bb, n: (bb, cb + n, qc)),
                win(kc, -1), win(kc, 0), win(kc, 1), win(vc, -1), win(vc, 0), win(vc, 1),
                ctxspec(kc), ctxspec(vc),
                tab(0), tab(0), tab(-1), tab(-1), tab(1), tab(1),
                pl.BlockSpec((1, sink.shape[1]), lambda bb, n: (0, 0))]
    return pl.pallas_call(
        functools.partial(_attn_kernel, n_blocks=nb, kv_heads=kv_heads, group=group),
        out_shape=jax.ShapeDtypeStruct((b, seq, ad), BF16),
        grid=(b, nb),
        in_specs=in_specs,
        out_specs=pl.BlockSpec((1, qb, ad), lambda bb, n: (bb, n, 0)),
        compiler_params=_params("parallel", "parallel"),
        name="window_ctx_attention",
    )(p_all, p_all, p_all, p_all, p_all, p_all, p_all, p_all, p_all,
      cos_t, sin_t, cos_t, sin_t, cos_t, sin_t, sink)


def _outproj_kernel(yf_ref, yr_ref, bon_ref, gat_ref, att_ref, x_ref, lg_ref, lb_ref, w_ref,
                    m_ref, g2_ref, wr_ref, x1_ref, h2_ref, lt_ref):
    ysum = yf_ref[0] + yr_ref[0]
    ri = lax.broadcasted_iota(jnp.int32, (LANES, LANES), 0) // HEAD_DIM
    ci = lax.broadcasted_iota(jnp.int32, (LANES, LANES), 1) // HEAD_DIM
    hm = _bf(jnp.where(ri == ci, 1.0 / HEAD_DIM, 0.0))

    def head_mean(z):
        hi = _bf(z)
        lo = _bf(z - hi.astype(F32))
        return _dot(hi, hm) + _dot(lo, hm)

    ybs = [ysum[:, j * LANES:(j + 1) * LANES] for j in range(ysum.shape[1] // LANES)]
    devs = [yb - mu for yb, mu in zip(ybs, [head_mean(yb) for yb in ybs])]
    blocks = [dev * lax.rsqrt(var + LNX_EPS)
              for dev, var in zip(devs, [head_mean(dev * dev) for dev in devs])]
    yn = jnp.concatenate(blocks, axis=1) if len(blocks) > 1 else blocks[0]
    rw = (yn * lg_ref[...] + lb_ref[...] + bon_ref[0]) * gat_ref[0]
    mix = jnp.concatenate([_bf(rw), att_ref[0]], axis=1)
    m = m_ref[0]
    x1 = x_ref[0] + m[0:1] * _dot(mix, w_ref[...])
    x1_ref[0] = x1
    ms = jnp.mean(x1 * x1, axis=-1, keepdims=True)
    h2 = x1 * lax.rsqrt(ms + NORM_EPS) * g2_ref[...] * (1.0 + m[2:3]) + m[1:2]
    h2_ref[0] = h2
    hi = _bf(h2)
    lo = _bf(h2 - hi.astype(F32))
    wr = wr_ref[...]
    wr_hi = _bf(wr)
    wr_lo = _bf(wr - wr_hi.astype(F32))
    lg = _dot(hi, wr_hi) + (_dot(hi, wr_lo) + _dot(lo, wr_hi))
    lt_ref[...] = lg.T[:lt_ref.shape[0]]


def _out_projection(y_f, y_r, bonus, gate, attn, x, lnx_g, lnx_b, w_out_b, m2, g2, w_router, ctx_len, tr):
    b, t, d = x.shape
    rd = y_f.shape[2]
    ad = attn.shape[2]
    ne = w_router.shape[1]
    assert ne <= LANES
    wr = _pad_to(w_router, 1, LANES)
    co = ctx_len // tr
    nt = t // tr
    return pl.pallas_call(
        _outproj_kernel,
        out_shape=[jax.ShapeDtypeStruct((b, t, d), F32),
                   jax.ShapeDtypeStruct((b, t, d), F32),
                   jax.ShapeDtypeStruct((ne, b * t), F32)],
        grid=(b, nt),
        in_specs=[pl.BlockSpec((1, tr, rd), lambda bb, i: (bb, co + i, 0)),
                  pl.BlockSpec((1, tr, rd), lambda bb, i: (bb, co + i, 0)),
                  pl.BlockSpec((1, tr, rd), lambda bb, i: (bb, co + i, 0)),
                  pl.BlockSpec((1, tr, rd), lambda bb, i: (bb, co + i, 0)),
                  pl.BlockSpec((1, tr, ad), lambda bb, i: (bb, i, 0)),
                  pl.BlockSpec((1, tr, d), lambda bb, i: (bb, i, 0)),
                  pl.BlockSpec((1, rd), lambda bb, i: (0, 0)),
                  pl.BlockSpec((1, rd), lambda bb, i: (0, 0)),
                  pl.BlockSpec((rd + ad, d), lambda bb, i: (0, 0)),
                  pl.BlockSpec((1, 4, d), lambda bb, i: (bb, 0, 0)),
                  pl.BlockSpec((1, d), lambda bb, i: (0, 0)),
                  pl.BlockSpec((d, LANES), lambda bb, i: (0, 0))],
        out_specs=[pl.BlockSpec((1, tr, d), lambda bb, i: (bb, i, 0)),
                   pl.BlockSpec((1, tr, d), lambda bb, i: (bb, i, 0)),
                   pl.BlockSpec((ne, tr), lambda bb, i: (0, bb * nt + i))],
        compiler_params=_params("parallel", "parallel"),
        name="rwkv_out_outproj_norm2_router",
    )(y_f, y_r, bonus, gate, attn, x, lnx_g, lnx_b, w_out_b, m2, g2, wr)


def _router_kernel(l_ref, b_ref, idx_ref, gate_ref, rank_ref, cnt_ref, carry_ref):
    @pl.when(pl.program_id(0) == 0)
    def _():
        carry_ref[...] = jnp.zeros_like(carry_ref)

    sc = jax.nn.sigmoid(l_ref[...])
    ne, tn = sc.shape
    eg = ne // N_GROUPS
    bz = sc + b_ref[...]
    g3 = bz.reshape(N_GROUPS, eg, tn)
    io = lax.broadcasted_iota(jnp.int32, (N_GROUPS, eg, tn), 1).astype(F32)
    m1 = jnp.max(g3, axis=1, keepdims=True)
    i1 = jnp.min(jnp.where(g3 == m1, io, float(eg)), axis=1, keepdims=True)
    m2 = jnp.max(jnp.where(io == i1, -jnp.inf, g3), axis=1, keepdims=True)
    gs = (m1 + m2).reshape(N_GROUPS, tn)

    def topk_hits(vals, k):
        n = vals.shape[0]
        idx = lax.broadcasted_iota(jnp.int32, vals.shape, 0).astype(F32)
        hits = []
        cur = vals
        for _ in range(k):
            mx = jnp.max(cur, axis=0, keepdims=True)
            ix = jnp.min(jnp.where(cur == mx, idx, float(n)), axis=0, keepdims=True)
            hit = idx == ix
            hits.append(hit)
            cur = jnp.where(hit, -jnp.inf, cur)
        return hits

    gsel = sum(jnp.where(h, 1.0, 0.0) for h in topk_hits(gs, TOPK_GROUPS))
    masked = jnp.where(gsel.reshape(N_GROUPS, 1, tn) > 0.0, g3, -jnp.inf).reshape(ne, tn)
    hits = topk_hits(masked, TOP_K)
    chosen = sum(jnp.where(h, 1.0, 0.0) for h in hits)
    denom = jnp.sum(chosen * sc, axis=0, keepdims=True)

    si = lax.broadcasted_iota(jnp.int32, (tn, tn), 0)
    ti = lax.broadcasted_iota(jnp.int32, (tn, tn), 1)
    before = _bf(jnp.where(si < ti, 1.0, 0.0))
    rank_full = _dot(_bf(chosen), before) + carry_ref[...]
    carry_ref[...] += jnp.sum(chosen, axis=1, keepdims=True)
    cnt_ref[...] = carry_ref[...]

    eidx = lax.broadcasted_iota(jnp.int32, (ne, tn), 0).astype(F32)
    for j, hit in enumerate(hits):
        idx_ref[j:j + 1, :] = jnp.sum(jnp.where(hit, eidx, 0.0), axis=0, keepdims=True).astype(jnp.int32)
        gate_ref[j:j + 1, :] = (jnp.sum(jnp.where(hit, sc, 0.0), axis=0, keepdims=True)
                                / denom * ROUTED_SCALE)
        rank_ref[j:j + 1, :] = jnp.sum(jnp.where(hit, rank_full, 0.0), axis=0,
                                       keepdims=True).astype(jnp.int32)


def _router(logits_t, bias_col):
    ne, n = logits_t.shape
    tn = 1024 if n % 1024 == 0 else n
    choice = lambda: pl.BlockSpec((TOP_K, tn), lambda i: (0, i))
    return pl.pallas_call(
        _router_kernel,
        out_shape=[jax.ShapeDtypeStruct((TOP_K, n), jnp.int32),
                   jax.ShapeDtypeStruct((TOP_K, n), F32),
                   jax.ShapeDtypeStruct((TOP_K, n), jnp.int32),
                   jax.ShapeDtypeStruct((ne, 1), F32)],
        grid=(n // tn,),
        in_specs=[pl.BlockSpec((ne, tn), lambda i: (0, i)),
                  pl.BlockSpec((ne, 1), lambda i: (0, 0))],
        out_specs=[choice(), choice(), choice(), pl.BlockSpec((ne, 1), lambda i: (0, 0))],
        scratch_shapes=[pltpu.VMEM((ne, 1), F32)],
        compiler_params=_params("arbitrary"),
        name="group_topk_router",
    )(logits_t, bias_col)


def _slots_kernel(idx_ref, rank_ref, ps_ref, o_ref):
    ne, tn = ps_ref.shape[0], idx_ref.shape[1]
    eidx = lax.broadcasted_iota(jnp.int32, (ne, tn), 0)
    for j in range(TOP_K):
        start = jnp.sum(jnp.where(eidx == idx_ref[j:j + 1, :], ps_ref[...], 0.0), axis=0, keepdims=True)
        o_ref[j:j + 1, :] = start.astype(jnp.int32) + rank_ref[j:j + 1, :]


def _slot_positions(idx8, rank8, pstart_col):
    _, n = idx8.shape
    ne = pstart_col.shape[0]
    tn = 1024 if n % 1024 == 0 else n
    choice = lambda: pl.BlockSpec((TOP_K, tn), lambda i: (0, i))
    return pl.pallas_call(
        _slots_kernel,
        out_shape=jax.ShapeDtypeStruct((TOP_K, n), jnp.int32),
        grid=(n // tn,),
        in_specs=[choice(), choice(), pl.BlockSpec((ne, 1), lambda i: (0, 0))],
        out_specs=choice(),
        compiler_params=_params("parallel"),
        name="expert_slot_positions",
    )(idx8, rank8, pstart_col)


def _dispatch_kernel(pad_start_ref, npad_ref, nused_ref, pos_ref, h_ref, xs_ref, zero_ref, sem,
                     *, n_experts):
    tt = h_ref.shape[0]
    bm = zero_ref.shape[0]

    def issue(t, carry):
        for k in range(TOP_K):
            pltpu.make_async_copy(h_ref.at[pl.ds(t, 1)], xs_ref.at[pl.ds(pos_ref[k, t], 1)], sem).start()
        return carry

    lax.fori_loop(0, tt, issue, 0)
    for _ in range(TOP_K):
        pltpu.make_async_copy(h_ref, xs_ref.at[pl.ds(0, tt)], sem).wait()

    @pl.when(pl.program_id(0) == pl.num_programs(0) - 1)
    def _():
        zero_ref[...] = jnp.zeros_like(zero_ref)

        zero_row_src = zero_ref.at[pl.ds(0, 1)]

        def per_expert(e, carry):
            def zero_row(i, c2):
                pltpu.make_async_copy(zero_row_src, xs_ref.at[pl.ds(pad_start_ref[e] + i, 1)], sem).start()
                return c2

            def zero_wait(i, c2):
                pltpu.make_async_copy(zero_row_src, xs_ref.at[pl.ds(0, 1)], sem).wait()
                return c2

            lax.fori_loop(0, npad_ref[e], zero_row, 0)
            lax.fori_loop(0, npad_ref[e], zero_wait, 0)
            return carry

        lax.fori_loop(0, n_experts, per_expert, 0)

        def zero_block(j, carry):
            blk = pltpu.make_async_copy(zero_ref, xs_ref.at[pl.ds(j * bm, bm)], sem)
            blk.start()
            blk.wait()
            return carry

        lax.fori_loop(nused_ref[0], xs_ref.shape[0] // bm, zero_block, 0)


def _dispatch(h2, pos8, pad_start, npad, n_used, n_slots, tt, bm):
    n, d = h2.shape
    ne = pad_start.shape[0]
    grid_spec = pltpu.PrefetchScalarGridSpec(
        num_scalar_prefetch=3,
        grid=(n // tt,),
        in_specs=[pl.BlockSpec((TOP_K, tt), lambda i, ps, npd, nu: (0, i), memory_space=pltpu.SMEM),
                  pl.BlockSpec((tt, d), lambda i, ps, npd, nu: (i, 0))],
        out_specs=pl.BlockSpec(memory_space=pl.ANY),
        scratch_shapes=[pltpu.VMEM((bm, d), F32), pltpu.SemaphoreType.DMA(())])
    return pl.pallas_call(
        functools.partial(_dispatch_kernel, n_experts=ne),
        out_shape=jax.ShapeDtypeStruct((n_slots, d), F32),
        grid_spec=grid_spec,
        compiler_params=_params("arbitrary"),
        name="expert_dispatch",
    )(pad_start, npad, n_used, pos8, h2)


def _expert_ffn_kernel(be_ref, nu_ref, nx_ref, sl_ref, x_ref, wg_hbm, wu_hbm, wd_hbm, y_ref,
                       wgf, wuf, wdf, wgb, wub, wdb, sems):
    j = pl.program_id(0)
    e = be_ref[j]
    used = j < nu_ref[0]
    fresh = jnp.logical_or(j == 0, e != be_ref[jnp.maximum(j - 1, 0)])

    def weight_copies(expert, slot):
        return [pltpu.make_async_copy(src.at[expert], dst.at[slot], sems.at[slot])
                for src, dst in ((wg_hbm, wgf), (wu_hbm, wuf), (wd_hbm, wdf))]

    @pl.when(j == 0)
    def _():
        for cp in weight_copies(e, sl_ref[e]):
            cp.start()

    @pl.when(jnp.logical_and(used, fresh))
    def _():
        slot = sl_ref[e]
        for cp in weight_copies(e, slot):
            cp.wait()
        nxt = nx_ref[e]

        @pl.when(nxt >= 0)
        def _():
            for cp in weight_copies(nxt, 1 - slot):
                cp.start()

        wgb[...] = _bf(wgf[slot])
        wub[...] = _bf(wuf[slot])
        wdb[...] = _bf(wdf[slot])

    @pl.when(used)
    def _():
        x = _bf(x_ref[...])
        a = _dot(x, wgb[...])
        u = _dot(x, wub[...])
        y_ref[...] = _dot(_bf(a * jax.nn.sigmoid(a) * u), wdb[...])

    @pl.when(jnp.logical_not(used))
    def _():
        y_ref[...] = jnp.zeros_like(y_ref)


def _expert_ffn(xs, blk_expert, n_used, next_expert, expert_slot, e_gate, e_up, e_down, bm):
    n_slots, d = xs.shape
    ne, _, ff = e_gate.shape
    grid_spec = pltpu.PrefetchScalarGridSpec(
        num_scalar_prefetch=4,
        grid=(n_slots // bm,),
        in_specs=[pl.BlockSpec((bm, d), lambda j, be, nu, nx, sl: (jnp.minimum(j, nu[0] - 1), 0)),
                  pl.BlockSpec(memory_space=pl.ANY),
                  pl.BlockSpec(memory_space=pl.ANY),
                  pl.BlockSpec(memory_space=pl.ANY)],
        out_specs=pl.BlockSpec((bm, d), lambda j, be, nu, nx, sl: (j, 0)),
        scratch_shapes=[pltpu.VMEM((2, d, ff), F32), pltpu.VMEM((2, d, ff), F32), pltpu.VMEM((2, ff, d), F32),
                        pltpu.VMEM((d, ff), BF16), pltpu.VMEM((d, ff), BF16), pltpu.VMEM((ff, d), BF16),
                        pltpu.SemaphoreType.DMA((2,))])
    return pl.pallas_call(
        _expert_ffn_kernel,
        out_shape=jax.ShapeDtypeStruct((n_slots, d), F32),
        grid_spec=grid_spec,
        compiler_params=_params("arbitrary"),
        name="routed_expert_ffn",
    )(blk_expert, n_used, next_expert, expert_slot, xs, e_gate, e_up, e_down)


def _final_kernel(pos_ref, h_ref, x1_ref, g_ref, ys_ref, wg_ref, wu_ref, wd_ref, m_ref, fg_ref,
                  o_ref, buf, sem):
    tr = h_ref.shape[1]

    def issue(t, carry):
        for k in range(TOP_K):
            pltpu.make_async_copy(ys_ref.at[pl.ds(pos_ref[k, t], 1)], buf.at[k, pl.ds(t, 1)], sem).start()
        return carry

    lax.fori_loop(0, tr, issue, 0)
    h = _bf(h_ref[0])
    a = _dot(h, wg_ref[...])
    u = _dot(h, wu_ref[...])
    moe = _dot(_bf(a * jax.nn.sigmoid(a) * u), wd_ref[...])
    g = g_ref[...]
    for k in range(TOP_K):
        pltpu.make_async_copy(ys_ref.at[pl.ds(0, tr)], buf.at[k], sem).wait()
    for k in range(TOP_K):
        moe = moe + g[:, k:k + 1] * buf[k]
    x2 = x1_ref[0] + m_ref[0, 3:4, :] * moe
    ms = jnp.mean(x2 * x2, axis=-1, keepdims=True)
    o_ref[0] = x2 * lax.rsqrt(ms + NORM_EPS) * fg_ref[...]


def _final(h2, x1, pos8, gate_rows, ys, s_gate_b, s_up_b, s_down_b, m2, final_g, tr):
    b, t, d = x1.shape
    ff = s_gate_b.shape[1]
    nt = t // tr
    row = lambda: pl.BlockSpec((1, tr, d), lambda bb, i: (bb, i, 0))
    return pl.pallas_call(
        _final_kernel,
        out_shape=jax.ShapeDtypeStruct((b, t, d), F32),
        grid=(b, nt),
        in_specs=[pl.BlockSpec((TOP_K, tr), lambda bb, i: (0, bb * nt + i), memory_space=pltpu.SMEM),
                  row(), row(),
                  pl.BlockSpec((tr, TOP_K), lambda bb, i: (bb * nt + i, 0)),
                  pl.BlockSpec(memory_space=pl.ANY),
                  pl.BlockSpec((d, ff), lambda bb, i: (0, 0)),
                  pl.BlockSpec((d, ff), lambda bb, i: (0, 0)),
                  pl.BlockSpec((ff, d), lambda bb, i: (0, 0)),
                  pl.BlockSpec((1, 4, d), lambda bb, i: (bb, 0, 0)),
                  pl.BlockSpec((1, d), lambda bb, i: (0, 0))],
        out_specs=row(),
        scratch_shapes=[pltpu.VMEM((TOP_K, tr, d), F32), pltpu.SemaphoreType.DMA(())],
        compiler_params=_params("parallel", "parallel"),
        name="combine_shared_expert_final_norm",
    )(pos8, h2, x1, gate_rows, ys, s_gate_b, s_up_b, s_down_b, m2, final_g)


def _pad_to(a, axis, n):
    pad = [(0, 0)] * a.ndim
    pad[axis] = (0, n - a.shape[axis])
    return jnp.pad(a, pad)


def _layout(rd, wl, al, gl, ad, kvd):
    assert wl <= LANES and al <= LANES
    lay = {"rd": rd, "gl": gl, "ad": ad, "kvd": kvd}
    off = 0
    for name, width in (("q", ad), ("r", rd), ("k", rd), ("v", rd), ("ka", kvd), ("va", kvd),
                        ("xw", LANES), ("xa", LANES), ("xg", gl)):
        assert width % LANES == 0
        lay[name] = off
        off += width
    lay["ncols"] = off
    return lay


def kernel(x, c, ctx, c_ctx, w_mod, b_mod, norm1_g, norm2_g, w_in, shift_mu, k_k, k_a, r_k, decay_bias, decay_up, iclr_bias, iclr_up, gate_up, lnx_g, lnx_b, attn_sink, w_out, w_router, expert_bias, e_gate, e_up, e_down, s_gate, s_up, s_down, final_g):
    assert w_mod.shape[0] == 1, "single-layer block only"
    b, t, d = x.shape
    ctx_len = ctx.shape[1]
    rh, hd = r_k.shape[1:]
    assert hd == HEAD_DIM and b + 1 <= SUBLANES
    rd = rh * hd
    wl, al, gl = decay_up.shape[2], iclr_up.shape[2], gate_up.shape[1]
    ad = attn_sink.shape[1] * hd
    kvd = (w_in.shape[2] - (3 * rd + wl + al + gl) - ad) // 2
    ne = w_router.shape[2]
    lay = _layout(rd, wl, al, gl, ad, kvd)
    tr = 256 if ctx_len % 256 == 0 and t % 256 == 0 else CHUNK
    assert ctx_len % CHUNK == 0 and t % CHUNK == 0 and t % QBLOCK == 0 and WINDOW == QBLOCK

    c8 = _pad_to(jnp.concatenate([c, c_ctx[None]], axis=0), 0, SUBLANES)
    mod = _modulation(c8, w_mod[0], b_mod[0][None])
    mod6 = mod.reshape(SUBLANES, 6, d)
    mx, mc = mod6[:b], mod6[b]
    m1 = jnp.stack([jnp.broadcast_to(mc[None, 0:2], (b, 2, d)), mx[:, 0:2]], axis=1)
    m2 = jnp.stack([mx[:, 2], mx[:, 3], mx[:, 4], mx[:, 5]], axis=1)

    wi = w_in[0]
    o_w, o_a, o_g, o_q = 3 * rd, 3 * rd + wl, 3 * rd + wl + al, 3 * rd + wl + al + gl
    w_in_p = _bf(jnp.concatenate([
        wi[:, o_q:o_q + ad], wi[:, :3 * rd], wi[:, o_q + ad:],
        _pad_to(wi[:, o_w:o_a], 1, LANES), _pad_to(wi[:, o_a:o_g], 1, LANES), wi[:, o_g:o_q]], axis=1))
    mu = shift_mu[0]
    prm = {
        "mu_r": mu[None, :rd], "mu_k": mu[None, rd:2 * rd], "mu_v": mu[None, 2 * rd:3 * rd],
        "mu_w": _pad_to(mu[None, o_w:o_a], 1, LANES), "mu_a": _pad_to(mu[None, o_a:o_g], 1, LANES),
        "mu_g": mu[None, o_g:o_q],
        "k_k": k_k, "k_a": k_a, "r_k": r_k.reshape(1, rd),
        "decay_bias": decay_bias[0], "decay_up": _bf(_pad_to(decay_up[0], 1, LANES)),
        "iclr_bias": iclr_bias[0], "iclr_up": _bf(_pad_to(iclr_up[0], 1, LANES)),
        "gate_up": _bf(gate_up[0]),
    }

    p_all = _in_projection(ctx, x, norm1_g, m1, w_in_p, tr)

    npair = rd // LANES
    pp = max(n for n in (1, 2, 4, 8) if npair % n == 0
             and all(lay[key] % (n * LANES) == 0 for key in ("r", "k", "v")))
    rp, y0, mm, nn, bonus, gate = _wkv_prep(p_all, lay, prm, ctx_len, None, pp)
    y_f, y_r = _wkv_sweep(rp, y0, mm, nn, ctx_len)

    nf = HEAD_DIM // 4
    freqs = ROPE_BASE ** (-jnp.arange(nf, dtype=F32) / nf)
    pos = jnp.arange(t, dtype=jnp.int32)
    ang_r = (pos // GRID_W).astype(F32)[:, None] * freqs
    ang_c = (pos % GRID_W).astype(F32)[:, None] * freqs
    cos_h = jnp.concatenate([jnp.cos(ang_r)] * 2 + [jnp.cos(ang_c)] * 2, axis=1)
    sin_h = jnp.concatenate([-jnp.sin(ang_r), jnp.sin(ang_r), -jnp.sin(ang_c), jnp.sin(ang_c)], axis=1)
    cos_t = jnp.concatenate([cos_h, cos_h], axis=1)
    sin_t = jnp.concatenate([sin_h, sin_h], axis=1)
    attn = _attention(p_all, lay, cos_t, sin_t, attn_sink, ctx_len, t)

    x1, h2, logits_t = _out_projection(y_f, y_r, bonus, gate, attn, x, lnx_g, lnx_b, _bf(w_out[0]), m2,
                                       norm2_g, w_router[0], ctx_len, tr)
    idx8, gate8, rank8, counts = _router(logits_t, expert_bias[0][:, None])

    n_tok = b * t
    bm = EXPERT_ROWS
    assert (n_tok * TOP_K) % bm == 0
    n_blk = n_tok * TOP_K // bm + ne
    cnt = counts[:, 0].astype(jnp.int32)
    padded = (cnt + bm - 1) // bm * bm
    pend = jnp.cumsum(padded)
    pstart = pend - padded
    pos8 = _slot_positions(idx8, rank8, pstart.astype(F32)[:, None])
    blk_row0 = jnp.arange(n_blk, dtype=jnp.int32) * bm
    blk_expert = jnp.minimum(jnp.sum((pend[None, :] <= blk_row0[:, None]).astype(jnp.int32), axis=1), ne - 1)
    n_used = (pend[-1:] // bm).astype(jnp.int32)

    xs = _dispatch(h2.reshape(n_tok, d), pos8, pstart + cnt, padded - cnt, n_used, n_blk * bm, tr, bm)
    has_rows = cnt > 0
    eids = jnp.arange(ne, dtype=jnp.int32)
    later = jnp.where(has_rows[None, :] & (eids[None, :] > eids[:, None]), eids[None, :], ne)
    next_expert = jnp.min(later, axis=1)
    next_expert = jnp.where(next_expert < ne, next_expert, -1).astype(jnp.int32)
    expert_slot = ((jnp.cumsum(has_rows.astype(jnp.int32)) - 1) % 2).astype(jnp.int32)
    ys = _expert_ffn(xs, blk_expert, n_used, next_expert, expert_slot, e_gate[0], e_up[0], e_down[0], bm)
    return _final(h2, x1, pos8, gate8.T, ys, _bf(s_gate[0]), _bf(s_up[0]), _bf(s_down[0]),
                  m2, final_g[None], tr)
```

```python
import functools
import math

import jax
import jax.numpy as jnp
from jax import lax
from jax.experimental import pallas as pl
from jax.experimental.pallas import tpu as pltpu

GRID_W = 64
WINDOW = 128
ROPE_BASE = 10000.0
N_GROUPS = 8
TOPK_GROUPS = 4
TOP_K = 8
ROUTED_SCALE = 2.5
NORM_EPS = 1e-6
LNX_EPS = 64e-5
KK_EPS = 1e-12

LANES = 128
SUBLANES = 8
VMEM_LIMIT_BYTES = 56 * 1024 * 1024

HEAD_DIM = 64
CHUNK = 64
QBLOCK = 128
EXPERT_ROWS = 256

F32 = jnp.float32
BF16 = jnp.bfloat16
NEG_BIG = -1e30


def _bf(x):
    return x.astype(BF16)


def _dot(a, b, precision=None):
    return jnp.dot(a, b, preferred_element_type=F32, precision=precision)


def _dot_nt(a, b, precision=None):
    return lax.dot_general(a, b, (((1,), (1,)), ((), ())), preferred_element_type=F32,
                           precision=precision)


def _dot_tn(a, b):
    return lax.dot_general(a, b, (((0,), (0,)), ((), ())), preferred_element_type=F32)


def _params(*sem):
    return pltpu.CompilerParams(dimension_semantics=sem, vmem_limit_bytes=VMEM_LIMIT_BYTES)


def _mod_kernel(c_ref, w_ref, b_ref, o_ref):
    c = c_ref[...]
    s = c * jax.nn.sigmoid(c)
    hi = _bf(s)
    lo = _bf(s - hi.astype(F32))
    r = _dot(jnp.concatenate([hi, lo], axis=0), _bf(w_ref[...]))
    o_ref[...] = r[:SUBLANES] + r[SUBLANES:] + b_ref[...]


def _modulation(c8, w_mod, b_mod):
    d, n = w_mod.shape
    tn = 1024 if n % 1024 == 0 else n
    return pl.pallas_call(
        _mod_kernel,
        out_shape=jax.ShapeDtypeStruct((SUBLANES, n), F32),
        grid=(n // tn,),
        in_specs=[pl.BlockSpec((SUBLANES, d), lambda j: (0, 0)),
                  pl.BlockSpec((d, tn), lambda j: (0, j)),
                  pl.BlockSpec((1, tn), lambda j: (0, j))],
        out_specs=pl.BlockSpec((SUBLANES, tn), lambda j: (0, j)),
        compiler_params=_params("parallel"),
        name="adaln_modulation",
    )(c8, w_mod, b_mod)


def _inproj_kernel(c_ref, x_ref, g_ref, m_ref, w_ref, o_ref, *, nct):
    x = jnp.where(pl.program_id(2) < nct, c_ref[0], x_ref[0])
    ms = jnp.mean(x * x, axis=-1, keepdims=True)
    y = x * lax.rsqrt(ms + NORM_EPS) * g_ref[...]
    m = m_ref[0, 0]
    h = y * (1.0 + m[1:2]) + m[0:1]
    o_ref[0] = _dot(_bf(h), w_ref[...])


def _in_projection(ctx, x, g1, m1, w_in_p, tr):
    b, t, d = x.shape
    ctx_len = ctx.shape[1]
    ta = ctx_len + t
    ncols = w_in_p.shape[1]
    tn = ncols
    nct = ctx_len // tr
    return pl.pallas_call(
        functools.partial(_inproj_kernel, nct=nct),
        out_shape=jax.ShapeDtypeStruct((b, ta, ncols), F32),
        grid=(ncols // tn, b, ta // tr),
        in_specs=[pl.BlockSpec((1, tr, d), lambda n, bb, i: (bb, jnp.minimum(i, nct - 1), 0)),
                  pl.BlockSpec((1, tr, d), lambda n, bb, i: (bb, jnp.maximum(i - nct, 0), 0)),
                  pl.BlockSpec((1, d), lambda n, bb, i: (0, 0)),
                  pl.BlockSpec((1, 1, 2, d), lambda n, bb, i: (bb, jnp.where(i >= nct, 1, 0), 0, 0)),
                  pl.BlockSpec((d, tn), lambda n, bb, i: (0, n), pipeline_mode=pl.Buffered(1))],
        out_specs=pl.BlockSpec((1, tr, tn), lambda n, bb, i: (bb, i, n)),
        compiler_params=_params("parallel", "parallel", "parallel"),
        name="norm_modulate_inproj",
    )(ctx, x, g1, m1, w_in_p)


def _wkv_prep_kernel(r_c, r_p, r_n, k_c, k_p, k_n, v_c, v_p, v_n, w_c, w_p, w_n, a_c, a_p, a_n,
                     g_c, g_p, g_n, mu_r, mu_k, mu_v, mu_w, mu_a, mu_g, kk_ref, ka_ref, rk_ref,
                     db_ref, du_ref, ib_ref, iu_ref, gu_ref,
                     rp_out, y0_out, m_out, n_out, bonus_out, gate_out,
                     *, nctx_chunks, n_chunks, prec_t):
    L = CHUNK
    c = pl.program_id(1)
    has_prev = jnp.logical_and(c != 0, c != nctx_chunks)
    has_next = jnp.logical_and(c != nctx_chunks - 1, c != n_chunks - 1)

    def shifted(cur, prv, nxt, mu):
        x = cur[0]
        rows = lax.broadcasted_iota(jnp.int32, (L, 1), 0)
        pr = jnp.where(has_prev, prv[0, SUBLANES - 1:SUBLANES, :], 0.0)
        nx = jnp.where(has_next, nxt[0, 0:1, :], 0.0)
        up = jnp.where(rows == 0, pr, pltpu.roll(x, 1, 0))
        dn = jnp.where(rows == L - 1, nx, pltpu.roll(x, L - 1, 0))
        return x + mu[...] * (0.5 * (up + dn) - x)

    r = shifted(r_c, r_p, r_n, mu_r)
    k = shifted(k_c, k_p, k_n, mu_k)
    v = shifted(v_c, v_p, v_n, mu_v)
    xw = shifted(w_c, w_p, w_n, mu_w)
    xa = shifted(a_c, a_p, a_n, mu_a)
    xg = shifted(g_c, g_p, g_n, mu_g)

    lane = lax.broadcasted_iota(jnp.int32, (1, LANES), 1)
    m0 = lane < HEAD_DIM

    def head_sum(x):
        s0 = jnp.sum(jnp.where(m0, x, 0.0), axis=-1, keepdims=True)
        s1 = jnp.sum(jnp.where(m0, 0.0, x), axis=-1, keepdims=True)
        return jnp.where(m0, s0, s1)

    def stack2(x):
        return jnp.concatenate([jnp.where(m0, x, 0.0), jnp.where(m0, 0.0, x)], axis=0)

    tw = _bf(jnp.tanh(xw))
    xab = _bf(xa)
    lw_all = [(-math.exp(-0.5)) * jax.nn.sigmoid(db_ref[d:d + 1, :] + _dot(tw, du_ref[d]))
              for d in range(2)]
    sa_all = [jax.nn.sigmoid(ib_ref[d:d + 1, :] + _dot(xab, iu_ref[d])) for d in range(2)]
    gate_out[0] = _dot(_bf(jax.nn.sigmoid(xg)), gu_ref[...])

    ri = lax.broadcasted_iota(jnp.int32, (L, L), 0)
    ci = lax.broadcasted_iota(jnp.int32, (L, L), 1)
    ri2 = lax.broadcasted_iota(jnp.int32, (2 * L, 2 * L), 0)
    ci2 = lax.broadcasted_iota(jnp.int32, (2 * L, 2 * L), 1)
    rt = jnp.bitwise_and(ri2, L - 1)
    ct = jnp.bitwise_and(ci2, L - 1)
    eye = ri2 == ci2
    n_sq = int(math.log2(L)) - 1

    tri = [(ci <= ri).astype(F32), (ci >= ri).astype(F32)]
    strict = [ct < rt, ct > rt]
    incl = [ct <= rt, ct >= rt]
    cum = [_dot(tri[d], lw_all[d], precision=lax.Precision.HIGHEST) for d in range(2)]
    cend = [jnp.sum(lw_all[d], axis=0, keepdims=True) for d in range(2)]
    e_in = [jnp.exp(cum[d]) for d in range(2)]
    e_ex = [jnp.exp(cum[d] - lw_all[d]) for d in range(2)]
    e_neg = [jnp.exp(-cum[d]) for d in range(2)]
    e_end = [jnp.exp(cend[d] - cum[d]) for d in range(2)]
    p_end = [jnp.exp(cend[d]) for d in range(2)]
    kd_all = [k * (1.0 + (sa_all[d] - 1.0) * ka_ref[...]) for d in range(2)]

    npp = r.shape[1] // LANES
    chains = [(p, d) for p in range(npp) for d in range(2)]
    sls = [slice(p * LANES, (p + 1) * LANES) for p in range(npp)]
    kk_p, vs_p, vsb_p = [], [], []
    for p in range(npp):
        kx = k[:, sls[p]] * kk_ref[:, sls[p]]
        kk_p.append(kx * lax.rsqrt(head_sum(kx * kx) + KK_EPS))
        vs_p.append(stack2(v[:, sls[p]]))
        vsb_p.append(_bf(vs_p[p]))
        ksum = kd_all[0][:, sls[p]] + kd_all[1][:, sls[p]]
        bonus_out[0, :, sls[p]] = head_sum(r[:, sls[p]] * ksum * rk_ref[:, sls[p]]) * v[:, sls[p]]

    a_s, r_s, bh_s, kh_s, g = [], [], [], [], []
    for p, d in chains:
        sl = sls[p]
        kd = kd_all[d][:, sl]
        bvec = kk_p[p] * sa_all[d][:, sl]
        a_s.append(stack2(-kk_p[p] * e_ex[d][:, sl]))
        r_s.append(stack2(r[:, sl] * e_in[d][:, sl]))
        b_s = stack2(bvec * e_neg[d][:, sl])
        k_s = stack2(kd * e_neg[d][:, sl])
        bh_s.append(stack2(bvec * e_end[d][:, sl]))
        kh_s.append(stack2(kd * e_end[d][:, sl]))
        g.append(_dot_nt(_bf(jnp.concatenate([a_s[-1], r_s[-1]], axis=0)),
                         _bf(jnp.concatenate([b_s, k_s], axis=0))))
    a_ab = [jnp.where(strict[d], g[i][:2 * L, :2 * L], 0.0) for i, (p, d) in enumerate(chains)]
    a_ak = [jnp.where(strict[d], g[i][:2 * L, 2 * L:], 0.0) for i, (p, d) in enumerate(chains)]
    a_rb = [jnp.where(incl[d], g[i][2 * L:, :2 * L], 0.0) for i, (p, d) in enumerate(chains)]
    a_rk = [jnp.where(incl[d], g[i][2 * L:, 2 * L:], 0.0) for i, (p, d) in enumerate(chains)]

    ident = jnp.where(eye, 1.0, 0.0)
    tmat = [ident + a for a in a_ab]
    apow = a_ab
    for _ in range(n_sq):
        if prec_t is None:
            apow = [_dot(_bf(a), _bf(a)) for a in apow]
            tmat = [t + _dot(_bf(a), _bf(t)) for a, t in zip(apow, tmat)]
        else:
            apow = [_dot(a, a, precision=prec_t) for a in apow]
            tmat = [t + _dot(a, t, precision=prec_t) for a, t in zip(apow, tmat)]

    x1 = [_dot(_bf(a_ak[i]), vsb_p[p]) for i, (p, d) in enumerate(chains)]
    y0v = [_dot(_bf(a_rk[i]), vsb_p[p]) for i, (p, d) in enumerate(chains)]
    tu = [_dot(_bf(tmat[i]), _bf(jnp.concatenate([a_s[i], x1[i]], axis=1)))
          for i in range(len(chains))]
    ry = [_dot(_bf(a_rb[i]), _bf(tu[i])) for i in range(len(chains))]
    def fold(z):
        h = z.shape[0] // 2
        return z[:h] + z[h:]

    for i, (p, d) in enumerate(chains):
        rp_out[d, 0, 0, p] = _bf(fold(r_s[i] + ry[i][:, :LANES]))
        y0_out[d, 0, 0, p] = fold(ry[i][:, LANES:] + y0v[i])
    for i, (p, d) in enumerate(chains):
        m_out[d, 0, 0, p] = _bf(fold(jnp.where(eye, p_end[d][:, sls[p]], 0.0)
                                     + _dot_tn(_bf(tu[i][:, :LANES]), _bf(bh_s[i]))))
    for i, (p, d) in enumerate(chains):
        n_out[d, 0, 0, p] = fold(_dot_tn(_bf(jnp.concatenate([tu[i][:, LANES:], vs_p[p]], axis=0)),
                                         _bf(jnp.concatenate([bh_s[i], kh_s[i]], axis=0))))


def _wkv_prep(p_all, lay, prm, ctx_len, prec_t, pp):
    b, ta, _ = p_all.shape
    L = CHUNK
    nc = ta // L
    nctx = ctx_len // L
    npair = lay["rd"] // LANES
    gl = lay["gl"]
    rows8 = ta // SUBLANES

    def cur(colfn, width=LANES):
        return pl.BlockSpec((1, L, width), lambda bb, c, p: (bb, c, colfn(p)))

    def prv(colfn, width=LANES):
        return pl.BlockSpec((1, SUBLANES, width),
                            lambda bb, c, p: (bb, jnp.maximum(c * (L // SUBLANES) - 1, 0), colfn(p)))

    def nxt(colfn, width=LANES):
        return pl.BlockSpec((1, SUBLANES, width),
                            lambda bb, c, p: (bb, jnp.minimum((c + 1) * (L // SUBLANES), rows8 - 1), colfn(p)))

    def trio(colfn, width=LANES):
        return [cur(colfn, width), prv(colfn, width), nxt(colfn, width)]

    pw = pp * LANES
    assert npair % pp == 0 and all(lay[n] % pw == 0 for n in ("r", "k", "v"))
    rb, kb, vb = lay["r"] // pw, lay["k"] // pw, lay["v"] // pw
    wb, ab = lay["xw"] // LANES, lay["xa"] // LANES
    assert lay["xg"] % gl == 0
    gb = lay["xg"] // gl

    def vec(width=pw):
        return pl.BlockSpec((1, width), lambda bb, c, p: (0, p))

    def vec0(width=LANES):
        return pl.BlockSpec((1, width), lambda bb, c, p: (0, 0))

    in_specs = (trio(lambda p: rb + p, pw) + trio(lambda p: kb + p, pw) + trio(lambda p: vb + p, pw)
                + trio(lambda p: wb) + trio(lambda p: ab) + trio(lambda p: gb, gl)
                + [vec(), vec(), vec(), vec0(), vec0(), vec0(gl),
                   vec(), vec(), vec(),
                   pl.BlockSpec((2, pw), lambda bb, c, p: (0, p)),
                   pl.BlockSpec((2, LANES, pw), lambda bb, c, p: (0, 0, p)),
                   pl.BlockSpec((2, pw), lambda bb, c, p: (0, p)),
                   pl.BlockSpec((2, LANES, pw), lambda bb, c, p: (0, 0, p)),
                   pl.BlockSpec((gl, pw), lambda bb, c, p: (0, p))])

    def opspec(rows):
        return pl.BlockSpec((2, 1, 1, pp, rows, LANES), lambda bb, c, p: (0, c, bb, p, 0, 0))

    out_specs = [opspec(L), opspec(L), opspec(HEAD_DIM), opspec(HEAD_DIM),
                 pl.BlockSpec((1, L, pw), lambda bb, c, p: (bb, c, p)),
                 pl.BlockSpec((1, L, pw), lambda bb, c, p: (bb, c, p))]
    out_shape = [jax.ShapeDtypeStruct((2, nc, b, npair, L, LANES), BF16),
                 jax.ShapeDtypeStruct((2, nc, b, npair, L, LANES), F32),
                 jax.ShapeDtypeStruct((2, nc, b, npair, HEAD_DIM, LANES), BF16),
                 jax.ShapeDtypeStruct((2, nc, b, npair, HEAD_DIM, LANES), F32),
                 jax.ShapeDtypeStruct((b, ta, lay["rd"]), F32),
                 jax.ShapeDtypeStruct((b, ta, lay["rd"]), F32)]
    kern = functools.partial(_wkv_prep_kernel, nctx_chunks=nctx, n_chunks=nc, prec_t=prec_t)
    args = [p_all] * 18 + [prm["mu_r"], prm["mu_k"], prm["mu_v"], prm["mu_w"], prm["mu_a"], prm["mu_g"],
                           prm["k_k"], prm["k_a"], prm["r_k"], prm["decay_bias"], prm["decay_up"],
                           prm["iclr_bias"], prm["iclr_up"], prm["gate_up"]]
    return pl.pallas_call(
        kern, out_shape=out_shape, grid=(b, nc, npair // pp), in_specs=in_specs, out_specs=out_specs,
        compiler_params=_params("parallel", "parallel", "parallel"),
        name="wkv_chunk_operators",
    )(*args)


def _wkv_seq_kernel(rpf, y0f, mf, nf, rpr, y0r, mr, nr, yf_ref, yr_ref, s_ref, *, nb, npair):
    L = CHUNK

    @pl.when(pl.program_id(0) == 0)
    def _():
        s_ref[...] = jnp.zeros_like(s_ref)

    m0 = lax.broadcasted_iota(jnp.int32, (1, LANES), 1) < HEAD_DIM

    def unfold(z):
        zero = jnp.zeros_like(z)
        return jnp.concatenate([jnp.where(m0, z, zero), jnp.where(m0, zero, z)], axis=0)

    for d, (rp_ref, y0_ref, m_ref, n_ref, y_ref) in enumerate(((rpf, y0f, mf, nf, yf_ref),
                                                               (rpr, y0r, mr, nr, yr_ref))):
        for bb in range(nb):
            for p in range(npair):
                i = (d * nb + bb) * npair + p
                s0 = _bf(s_ref[i])
                ys = _dot_nt(unfold(rp_ref[0, 0, bb, p]), s0)
                y_ref[bb, :, p * LANES:(p + 1) * LANES] = ys[:L] + ys[L:] + y0_ref[0, 0, bb, p]
                s_ref[i] = _dot(s0, unfold(m_ref[0, 0, bb, p])) + unfold(n_ref[0, 0, bb, p])


def _wkv_sweep(rp, y0, mm, nn, ctx_len):
    _, nc, b, npair, _, _ = rp.shape
    L = CHUNK
    nctx = ctx_len // L

    def chunk_of(d, c):
        return c if d == 0 else jnp.where(c < nctx, nctx - 1 - c, nc - 1 - (c - nctx))

    def opspec(d, rows):
        return pl.BlockSpec((1, 1, b, npair, rows, LANES), lambda c: (d, chunk_of(d, c), 0, 0, 0, 0))

    def yspec(d):
        return pl.BlockSpec((b, L, npair * LANES), lambda c: (0, chunk_of(d, c), 0))

    rows = (L, L, HEAD_DIM, HEAD_DIM)
    y_shape = jax.ShapeDtypeStruct((b, nc * L, npair * LANES), F32)
    return pl.pallas_call(
        functools.partial(_wkv_seq_kernel, nb=b, npair=npair),
        out_shape=[y_shape, y_shape],
        grid=(nc,),
        in_specs=[opspec(d, r) for d in range(2) for r in rows],
        out_specs=[yspec(0), yspec(1)],
        scratch_shapes=[pltpu.VMEM((2 * b * npair, LANES, LANES), F32)],
        compiler_params=_params("arbitrary"),
        name="wkv_state_sweep",
    )(rp, y0, mm, nn, rp, y0, mm, nn)


def _attn_kernel(q_ref, kp_ref, kc_ref, kn_ref, vp_ref, vc_ref, vn_ref, kx_ref, vx_ref,
                 cq_ref, sq_ref, cp_ref, sp_ref, cn_ref, sn_ref, sink_ref, o_ref,
                 *, n_blocks, kv_heads, group):
    n = pl.program_id(1)
    qb = QBLOCK

    def rope(x, cos, sin):
        w = x.shape[1]
        reps = w // LANES
        cs = jnp.concatenate([cos] * reps, axis=1) if reps > 1 else cos
        sn = jnp.concatenate([sin] * reps, axis=1) if reps > 1 else sin
        lane = lax.broadcasted_iota(jnp.int32, (1, w), 1)
        sw = jnp.where(jnp.bitwise_and(lane, 16) == 0, pltpu.roll(x, w - 16, 1), pltpu.roll(x, 16, 1))
        return x * cs + sw * sn

    q = rope(q_ref[0], cq_ref[...], sq_ref[...]) * (HEAD_DIM ** -0.5)
    kwin = jnp.concatenate([rope(kp_ref[0], cp_ref[...], sp_ref[...]),
                            rope(kc_ref[0], cq_ref[...], sq_ref[...]),
                            rope(kn_ref[0], cn_ref[...], sn_ref[...]),
                            kx_ref[0]], axis=0)
    vwin = jnp.concatenate([vp_ref[0], vc_ref[0], vn_ref[0], vx_ref[0]], axis=0)
    nkeys = kwin.shape[0]

    rows = group * qb
    qi = jnp.bitwise_and(lax.broadcasted_iota(jnp.int32, (rows, nkeys), 0), qb - 1)
    s_i = lax.broadcasted_iota(jnp.int32, (rows, nkeys), 1)
    blk = s_i // qb + (n - 1)
    in_win = (s_i >= qi) & (s_i <= qi + 2 * WINDOW) & (blk >= 0) & (blk < n_blocks)
    valid = in_win | (s_i >= 3 * qb)
    rgrp = lax.broadcasted_iota(jnp.int32, (rows, 1), 0) // qb

    scores, sinks = [], []
    for kh in range(kv_heads):
        kk_ = _bf(kwin[:, kh * HEAD_DIM:(kh + 1) * HEAD_DIM])
        qs = jnp.concatenate([q[:, (kh * group + g) * HEAD_DIM:(kh * group + g + 1) * HEAD_DIM]
                              for g in range(group)], axis=0)
        sk = jnp.zeros((rows, 1), F32)
        for g in range(group):
            h = kh * group + g
            sk = jnp.where(rgrp == g, sink_ref[0:1, h:h + 1], sk)
        sinks.append(sk)
        scores.append(jnp.where(valid, _dot_nt(_bf(qs), kk_), NEG_BIG))
    probs, denoms = [], []
    for s, sk in zip(scores, sinks):
        m = jnp.maximum(jnp.max(s, axis=-1, keepdims=True), sk)
        e = jnp.exp(s - m)
        denoms.append(jnp.sum(e, axis=-1, keepdims=True) + jnp.exp(sk - m))
        probs.append(_bf(e))
    pieces = []
    for kh in range(kv_heads):
        vv_ = _bf(vwin[:, kh * HEAD_DIM:(kh + 1) * HEAD_DIM])
        o = _dot(probs[kh], vv_) / denoms[kh]
        for g in range(group):
            pieces.append(o[g * qb:(g + 1) * qb])
    o_ref[0] = _bf(jnp.concatenate(pieces, axis=1))


def _attention(p_all, lay, cos_t, sin_t, sink, ctx_len, seq):
    b = p_all.shape[0]
    qb = QBLOCK
    nb = seq // qb
    cb = ctx_len // qb
    ad, kvd = lay["ad"], lay["kvd"]
    kv_heads = kvd // HEAD_DIM
    group = (ad // HEAD_DIM) // kv_heads
    assert lay["q"] % ad == 0 and lay["ka"] % kvd == 0 and lay["va"] % kvd == 0
    qc, kc, vc = lay["q"] // ad, lay["ka"] // kvd, lay["va"] // kvd

    def clampb(n, off):
        return jnp.clip(n + off, 0, nb - 1)

    def win(col, off):
        return pl.BlockSpec((1, qb, kvd), lambda bb, n: (bb, cb + clampb(n, off), col))

    def tab(off):
        return pl.BlockSpec((qb, LANES), lambda bb, n: (clampb(n, off), 0))

    assert ctx_len % qb == 0
    ctxspec = lambda col: pl.BlockSpec((1, ctx_len, kvd), lambda bb, n: (bb, 0, col))
    in_specs = [pl.BlockSpec((1, qb, ad), lambda bb, n: (bb, cb + n, qc)),
                win(kc, -1), win(kc, 0), win(kc, 1), win(vc, -1), win(vc, 0), win(vc, 1),
                ctxspec(kc), ctxspec(vc),
                tab(0), tab(0), tab(-1), tab(-1), tab(1), tab(1),
                pl.BlockSpec((1, sink.shape[1]), lambda bb, n: (0, 0))]
    return pl.pallas_call(
        functools.partial(_attn_kernel, n_blocks=nb, kv_heads=kv_heads, group=group),
        out_shape=jax.ShapeDtypeStruct((b, seq, ad), BF16),
        grid=(b, nb),
        in_specs=in_specs,
        out_specs=pl.BlockSpec((1, qb, ad), lambda bb, n: (bb, n, 0)),
        compiler_params=_params("parallel", "parallel"),
        name="window_ctx_attention",
    )(p_all, p_all, p_all, p_all, p_all, p_all, p_all, p_all, p_all,
      cos_t, sin_t, cos_t, sin_t, cos_t, sin_t, sink)


def _outproj_kernel(yf_ref, yr_ref, bon_ref, gat_ref, att_ref, x_ref, lg_ref, lb_ref, w_ref,
                    m_ref, g2_ref, wr_ref, x1_ref, h2_ref, lt_ref):
    ysum = yf_ref[0] + yr_ref[0]
    ri = lax.broadcasted_iota(jnp.int32, (LANES, LANES), 0) // HEAD_DIM
    ci = lax.broadcasted_iota(jnp.int32, (LANES, LANES), 1) // HEAD_DIM
    hm = _bf(jnp.where(ri == ci, 1.0 / HEAD_DIM, 0.0))

    def head_mean(z):
        hi = _bf(z)
        lo = _bf(z - hi.astype(F32))
        return _dot(hi, hm) + _dot(lo, hm)

    ybs = [ysum[:, j * LANES:(j + 1) * LANES] for j in range(ysum.shape[1] // LANES)]
    devs = [yb - mu for yb, mu in zip(ybs, [head_mean(yb) for yb in ybs])]
    blocks = [dev * lax.rsqrt(var + LNX_EPS)
              for dev, var in zip(devs, [head_mean(dev * dev) for dev in devs])]
    yn = jnp.concatenate(blocks, axis=1) if len(blocks) > 1 else blocks[0]
    rw = (yn * lg_ref[...] + lb_ref[...] + bon_ref[0]) * gat_ref[0]
    mix = jnp.concatenate([_bf(rw), att_ref[0]], axis=1)
    m = m_ref[0]
    x1 = x_ref[0] + m[0:1] * _dot(mix, w_ref[...])
    x1_ref[0] = x1
    ms = jnp.mean(x1 * x1, axis=-1, keepdims=True)
    h2 = x1 * lax.rsqrt(ms + NORM_EPS) * g2_ref[...] * (1.0 + m[2:3]) + m[1:2]
    h2_ref[0] = h2
    hi = _bf(h2)
    lo = _bf(h2 - hi.astype(F32))
    wr = wr_ref[...]
    wr_hi = _bf(wr)
    wr_lo = _bf(wr - wr_hi.astype(F32))
    lg = _dot(hi, wr_hi) + (_dot(hi, wr_lo) + _dot(lo, wr_hi))
    lt_ref[...] = lg.T[:lt_ref.shape[0]]


def _out_projection(y_f, y_r, bonus, gate, attn, x, lnx_g, lnx_b, w_out_b, m2, g2, w_router, ctx_len, tr):
    b, t, d = x.shape
    rd = y_f.shape[2]
    ad = attn.shape[2]
    ne = w_router.shape[1]
    assert ne <= LANES
    wr = _pad_to(w_router, 1, LANES)
    co = ctx_len // tr
    nt = t // tr
    return pl.pallas_call(
        _outproj_kernel,
        out_shape=[jax.ShapeDtypeStruct((b, t, d), F32),
                   jax.ShapeDtypeStruct((b, t, d), F32),
                   jax.ShapeDtypeStruct((ne, b * t), F32)],
        grid=(b, nt),
        in_specs=[pl.BlockSpec((1, tr, rd), lambda bb, i: (bb, co + i, 0)),
                  pl.BlockSpec((1, tr, rd), lambda bb, i: (bb, co + i, 0)),
                  pl.BlockSpec((1, tr, rd), lambda bb, i: (bb, co + i, 0)),
                  pl.BlockSpec((1, tr, rd), lambda bb, i: (bb, co + i, 0)),
                  pl.BlockSpec((1, tr, ad), lambda bb, i: (bb, i, 0)),
                  pl.BlockSpec((1, tr, d), lambda bb, i: (bb, i, 0)),
                  pl.BlockSpec((1, rd), lambda bb, i: (0, 0)),
                  pl.BlockSpec((1, rd), lambda bb, i: (0, 0)),
                  pl.BlockSpec((rd + ad, d), lambda bb, i: (0, 0)),
                  pl.BlockSpec((1, 4, d), lambda bb, i: (bb, 0, 0)),
                  pl.BlockSpec((1, d), lambda bb, i: (0, 0)),
                  pl.BlockSpec((d, LANES), lambda bb, i: (0, 0))],
        out_specs=[pl.BlockSpec((1, tr, d), lambda bb, i: (bb, i, 0)),
                   pl.BlockSpec((1, tr, d), lambda bb, i: (bb, i, 0)),
                   pl.BlockSpec((ne, tr), lambda bb, i: (0, bb * nt + i))],
        compiler_params=_params("parallel", "parallel"),
        name="rwkv_out_outproj_norm2_router",
    )(y_f, y_r, bonus, gate, attn, x, lnx_g, lnx_b, w_out_b, m2, g2, wr)


def _router_kernel(l_ref, b_ref, idx_ref, gate_ref, rank_ref, cnt_ref, carry_ref):
    @pl.when(pl.program_id(0) == 0)
    def _():
        carry_ref[...] = jnp.zeros_like(carry_ref)

    sc = jax.nn.sigmoid(l_ref[...])
    ne, tn = sc.shape
    eg = ne // N_GROUPS
    bz = sc + b_ref[...]
    g3 = bz.reshape(N_GROUPS, eg, tn)
    io = lax.broadcasted_iota(jnp.int32, (N_GROUPS, eg, tn), 1).astype(F32)
    m1 = jnp.max(g3, axis=1, keepdims=True)
    i1 = jnp.min(jnp.where(g3 == m1, io, float(eg)), axis=1, keepdims=True)
    m2 = jnp.max(jnp.where(io == i1, -jnp.inf, g3), axis=1, keepdims=True)
    gs = (m1 + m2).reshape(N_GROUPS, tn)

    def topk_hits(vals, k):
        n = vals.shape[0]
        idx = lax.broadcasted_iota(jnp.int32, vals.shape, 0).astype(F32)
        hits = []
        cur = vals
        for _ in range(k):
            mx = jnp.max(cur, axis=0, keepdims=True)
            ix = jnp.min(jnp.where(cur == mx, idx, float(n)), axis=0, keepdims=True)
            hit = idx == ix
            hits.append(hit)
            cur = jnp.where(hit, -jnp.inf, cur)
        return hits

    gsel = sum(jnp.where(h, 1.0, 0.0) for h in topk_hits(gs, TOPK_GROUPS))
    masked = jnp.where(gsel.reshape(N_GROUPS, 1, tn) > 0.0, g3, -jnp.inf).reshape(ne, tn)
    hits = topk_hits(masked, TOP_K)
    chosen = sum(jnp.where(h, 1.0, 0.0) for h in hits)
    denom = jnp.sum(chosen * sc, axis=0, keepdims=True)

    si = lax.broadcasted_iota(jnp.int32, (tn, tn), 0)
    ti = lax.broadcasted_iota(jnp.int32, (tn, tn), 1)
    before = _bf(jnp.where(si < ti, 1.0, 0.0))
    rank_full = _dot(_bf(chosen), before) + carry_ref[...]
    carry_ref[...] += jnp.sum(chosen, axis=1, keepdims=True)
    cnt_ref[...] = carry_ref[...]

    eidx = lax.broadcasted_iota(jnp.int32, (ne, tn), 0).astype(F32)
    for j, hit in enumerate(hits):
        idx_ref[j:j + 1, :] = jnp.sum(jnp.where(hit, eidx, 0.0), axis=0, keepdims=True).astype(jnp.int32)
        gate_ref[j:j + 1, :] = (jnp.sum(jnp.where(hit, sc, 0.0), axis=0, keepdims=True)
                                / denom * ROUTED_SCALE)
        rank_ref[j:j + 1, :] = jnp.sum(jnp.where(hit, rank_full, 0.0), axis=0,
                                       keepdims=True).astype(jnp.int32)


def _router(logits_t, bias_col):
    ne, n = logits_t.shape
    tn = 1024 if n % 1024 == 0 else n
    choice = lambda: pl.BlockSpec((TOP_K, tn), lambda i: (0, i))
    return pl.pallas_call(
        _router_kernel,
        out_shape=[jax.ShapeDtypeStruct((TOP_K, n), jnp.int32),
                   jax.ShapeDtypeStruct((TOP_K, n), F32),
                   jax.ShapeDtypeStruct((TOP_K, n), jnp.int32),
                   jax.ShapeDtypeStruct((ne, 1), F32)],
        grid=(n // tn,),
        in_specs=[pl.BlockSpec((ne, tn), lambda i: (0, i)),
                  pl.BlockSpec((ne, 1), lambda i: (0, 0))],
        out_specs=[choice(), choice(), choice(), pl.BlockSpec((ne, 1), lambda i: (0, 0))],
        scratch_shapes=[pltpu.VMEM((ne, 1), F32)],
        compiler_params=_params("arbitrary"),
        name="group_topk_router",
    )(logits_t, bias_col)


def _slots_kernel(idx_ref, rank_ref, ps_ref, o_ref):
    ne, tn = ps_ref.shape[0], idx_ref.shape[1]
    eidx = lax.broadcasted_iota(jnp.int32, (ne, tn), 0)
    for j in range(TOP_K):
        start = jnp.sum(jnp.where(eidx == idx_ref[j:j + 1, :], ps_ref[...], 0.0), axis=0, keepdims=True)
        o_ref[j:j + 1, :] = start.astype(jnp.int32) + rank_ref[j:j + 1, :]


def _slot_positions(idx8, rank8, pstart_col):
    _, n = idx8.shape
    ne = pstart_col.shape[0]
    tn = 1024 if n % 1024 == 0 else n
    choice = lambda: pl.BlockSpec((TOP_K, tn), lambda i: (0, i))
    return pl.pallas_call(
        _slots_kernel,
        out_shape=jax.ShapeDtypeStruct((TOP_K, n), jnp.int32),
        grid=(n // tn,),
        in_specs=[choice(), choice(), pl.BlockSpec((ne, 1), lambda i: (0, 0))],
        out_specs=choice(),
        compiler_params=_params("parallel"),
        name="expert_slot_positions",
    )(idx8, rank8, pstart_col)


def _dispatch_kernel(pad_start_ref, npad_ref, nused_ref, pos_ref, h_ref, xs_ref, zero_ref, sem,
                     *, n_experts):
    tt = h_ref.shape[0]
    bm = zero_ref.shape[0]

    def issue(t, carry):
        for k in range(TOP_K):
            pltpu.make_async_copy(h_ref.at[pl.ds(t, 1)], xs_ref.at[pl.ds(pos_ref[k, t], 1)], sem).start()
        return carry

    lax.fori_loop(0, tt, issue, 0, unroll=4)
    for _ in range(TOP_K):
        pltpu.make_async_copy(h_ref, xs_ref.at[pl.ds(0, tt)], sem).wait()

    @pl.when(pl.program_id(0) == pl.num_programs(0) - 1)
    def _():
        zero_ref[...] = jnp.zeros_like(zero_ref)

        zero_row_src = zero_ref.at[pl.ds(0, 1)]

        def per_expert(e, carry):
            def zero_row(i, c2):
                pltpu.make_async_copy(zero_row_src, xs_ref.at[pl.ds(pad_start_ref[e] + i, 1)], sem).start()
                return c2

            def zero_wait(i, c2):
                pltpu.make_async_copy(zero_row_src, xs_ref.at[pl.ds(0, 1)], sem).wait()
                return c2

            lax.fori_loop(0, npad_ref[e], zero_row, 0)
            lax.fori_loop(0, npad_ref[e], zero_wait, 0)
            return carry

        lax.fori_loop(0, n_experts, per_expert, 0)

        def zero_block(j, carry):
            blk = pltpu.make_async_copy(zero_ref, xs_ref.at[pl.ds(j * bm, bm)], sem)
            blk.start()
            blk.wait()
            return carry

        lax.fori_loop(nused_ref[0], xs_ref.shape[0] // bm, zero_block, 0)


def _dispatch(h2, pos8, pad_start, npad, n_used, n_slots, tt, bm):
    n, d = h2.shape
    ne = pad_start.shape[0]
    grid_spec = pltpu.PrefetchScalarGridSpec(
        num_scalar_prefetch=3,
        grid=(n // tt,),
        in_specs=[pl.BlockSpec((TOP_K, tt), lambda i, ps, npd, nu: (0, i), memory_space=pltpu.SMEM),
                  pl.BlockSpec((tt, d), lambda i, ps, npd, nu: (i, 0))],
        out_specs=pl.BlockSpec(memory_space=pl.ANY),
        scratch_shapes=[pltpu.VMEM((bm, d), h2.dtype), pltpu.SemaphoreType.DMA(())])
    return pl.pallas_call(
        functools.partial(_dispatch_kernel, n_experts=ne),
        out_shape=jax.ShapeDtypeStruct((n_slots, d), h2.dtype),
        grid_spec=grid_spec,
        compiler_params=_params("arbitrary"),
        name="expert_dispatch",
    )(pad_start, npad, n_used, pos8, h2)


def _expert_ffn_kernel(be_ref, nu_ref, nx_ref, sl_ref, x_ref, wg_hbm, wu_hbm, wd_hbm, y_ref,
                       wgf, wuf, wdf, wgb, wub, wdb, sems):
    j = pl.program_id(0)
    e = be_ref[j]
    used = j < nu_ref[0]
    fresh = jnp.logical_or(j == 0, e != be_ref[jnp.maximum(j - 1, 0)])

    def weight_copies(expert, slot):
        return [pltpu.make_async_copy(src.at[expert], dst.at[slot], sems.at[slot])
                for src, dst in ((wg_hbm, wgf), (wu_hbm, wuf), (wd_hbm, wdf))]

    @pl.when(j == 0)
    def _():
        for cp in weight_copies(e, sl_ref[e]):
            cp.start()

    @pl.when(jnp.logical_and(used, fresh))
    def _():
        slot = sl_ref[e]
        for cp in weight_copies(e, slot):
            cp.wait()
        nxt = nx_ref[e]

        @pl.when(nxt >= 0)
        def _():
            for cp in weight_copies(nxt, 1 - slot):
                cp.start()

        wgb[...] = _bf(wgf[slot])
        wub[...] = _bf(wuf[slot])
        wdb[...] = _bf(wdf[slot])

    @pl.when(used)
    def _():
        x = _bf(x_ref[...])
        a = _dot(x, wgb[...])
        u = _dot(x, wub[...])
        y_ref[...] = _dot(_bf(a * jax.nn.sigmoid(a) * u), wdb[...])

    @pl.when(jnp.logical_not(used))
    def _():
        y_ref[...] = jnp.zeros_like(y_ref)


def _expert_ffn(xs, blk_expert, n_used, next_expert, expert_slot, e_gate, e_up, e_down, bm):
    n_slots, dw = xs.shape
    ne, d, ff = e_gate.shape
    assert dw == d
    grid_spec = pltpu.PrefetchScalarGridSpec(
        num_scalar_prefetch=4,
        grid=(n_slots // bm,),
        in_specs=[pl.BlockSpec((bm, dw), lambda j, be, nu, nx, sl: (jnp.minimum(j, nu[0] - 1), 0)),
                  pl.BlockSpec(memory_space=pl.ANY),
                  pl.BlockSpec(memory_space=pl.ANY),
                  pl.BlockSpec(memory_space=pl.ANY)],
        out_specs=pl.BlockSpec((bm, dw), lambda j, be, nu, nx, sl: (j, 0)),
        scratch_shapes=[pltpu.VMEM((2, d, ff), F32), pltpu.VMEM((2, d, ff), F32), pltpu.VMEM((2, ff, d), F32),
                        pltpu.VMEM((d, ff), BF16), pltpu.VMEM((d, ff), BF16), pltpu.VMEM((ff, d), BF16),
                        pltpu.SemaphoreType.DMA((2,))])
    return pl.pallas_call(
        _expert_ffn_kernel,
        out_shape=jax.ShapeDtypeStruct((n_slots, dw), xs.dtype),
        grid_spec=grid_spec,
        compiler_params=_params("arbitrary"),
        name="routed_expert_ffn",
    )(blk_expert, n_used, next_expert, expert_slot, xs, e_gate, e_up, e_down)


def _final_kernel(pos_ref, h_ref, x1_ref, g_ref, ys_ref, wg_ref, wu_ref, wd_ref, m_ref, fg_ref,
                  o_ref, buf, sem):
    tr = h_ref.shape[1]

    def issue(t, carry):
        for k in range(TOP_K):
            pltpu.make_async_copy(ys_ref.at[pl.ds(pos_ref[k, t], 1)], buf.at[k, pl.ds(t, 1)], sem).start()
        return carry

    lax.fori_loop(0, tr, issue, 0, unroll=4)
    h = _bf(h_ref[0])
    a = _dot(h, wg_ref[...])
    u = _dot(h, wu_ref[...])
    moe = _dot(_bf(a * jax.nn.sigmoid(a) * u), wd_ref[...])
    g = g_ref[...]
    for k in range(TOP_K):
        pltpu.make_async_copy(ys_ref.at[pl.ds(0, tr)], buf.at[k], sem).wait()
    for k in range(TOP_K):
        moe = moe + g[:, k:k + 1] * buf[k]
    x2 = x1_ref[0] + m_ref[0, 3:4, :] * moe
    ms = jnp.mean(x2 * x2, axis=-1, keepdims=True)
    o_ref[0] = x2 * lax.rsqrt(ms + NORM_EPS) * fg_ref[...]


def _final(h2, x1, pos8, gate_rows, ys, s_gate_b, s_up_b, s_down_b, m2, final_g, tr):
    b, t, d = x1.shape
    ff = s_gate_b.shape[1]
    nt = t // tr
    dw = h2.shape[2]
    row = lambda width=d: pl.BlockSpec((1, tr, width), lambda bb, i: (bb, i, 0))
    return pl.pallas_call(
        _final_kernel,
        out_shape=jax.ShapeDtypeStruct((b, t, d), F32),
        grid=(b, nt),
        in_specs=[pl.BlockSpec((TOP_K, tr), lambda bb, i: (0, bb * nt + i), memory_space=pltpu.SMEM),
                  row(dw), row(),
                  pl.BlockSpec((tr, TOP_K), lambda bb, i: (bb * nt + i, 0)),
                  pl.BlockSpec(memory_space=pl.ANY),
                  pl.BlockSpec((d, ff), lambda bb, i: (0, 0)),
                  pl.BlockSpec((d, ff), lambda bb, i: (0, 0)),
                  pl.BlockSpec((ff, d), lambda bb, i: (0, 0)),
                  pl.BlockSpec((1, 4, d), lambda bb, i: (bb, 0, 0)),
                  pl.BlockSpec((1, d), lambda bb, i: (0, 0))],
        out_specs=row(),
        scratch_shapes=[pltpu.VMEM((TOP_K, tr, dw), ys.dtype), pltpu.SemaphoreType.DMA(())],
        compiler_params=_params("parallel", "parallel"),
        name="combine_shared_expert_final_norm",
    )(pos8, h2, x1, gate_rows, ys, s_gate_b, s_up_b, s_down_b, m2, final_g)


def _pad_to(a, axis, n):
    pad = [(0, 0)] * a.ndim
    pad[axis] = (0, n - a.shape[axis])
    return jnp.pad(a, pad)


def _layout(rd, wl, al, gl, ad, kvd):
    assert wl <= LANES and al <= LANES
    lay = {"rd": rd, "gl": gl, "ad": ad, "kvd": kvd}
    off = 0
    for name, width in (("q", ad), ("r", rd), ("k", rd), ("v", rd), ("ka", kvd), ("va", kvd),
                        ("xw", LANES), ("xa", LANES), ("xg", gl)):
        assert width % LANES == 0
        lay[name] = off
        off += width
    lay["ncols"] = off
    return lay


def kernel(x, c, ctx, c_ctx, w_mod, b_mod, norm1_g, norm2_g, w_in, shift_mu, k_k, k_a, r_k, decay_bias, decay_up, iclr_bias, iclr_up, gate_up, lnx_g, lnx_b, attn_sink, w_out, w_router, expert_bias, e_gate, e_up, e_down, s_gate, s_up, s_down, final_g):
    assert w_mod.shape[0] == 1, "single-layer block only"
    b, t, d = x.shape
    ctx_len = ctx.shape[1]
    rh, hd = r_k.shape[1:]
    assert hd == HEAD_DIM and b + 1 <= SUBLANES
    rd = rh * hd
    wl, al, gl = decay_up.shape[2], iclr_up.shape[2], gate_up.shape[1]
    ad = attn_sink.shape[1] * hd
    kvd = (w_in.shape[2] - (3 * rd + wl + al + gl) - ad) // 2
    ne = w_router.shape[2]
    lay = _layout(rd, wl, al, gl, ad, kvd)
    tr = 256 if ctx_len % 256 == 0 and t % 256 == 0 else CHUNK
    assert ctx_len % CHUNK == 0 and t % CHUNK == 0 and t % QBLOCK == 0 and WINDOW == QBLOCK

    c8 = _pad_to(jnp.concatenate([c, c_ctx[None]], axis=0), 0, SUBLANES)
    mod = _modulation(c8, w_mod[0], b_mod[0][None])
    mod6 = mod.reshape(SUBLANES, 6, d)
    mx, mc = mod6[:b], mod6[b]
    m1 = jnp.stack([jnp.broadcast_to(mc[None, 0:2], (b, 2, d)), mx[:, 0:2]], axis=1)
    m2 = jnp.stack([mx[:, 2], mx[:, 3], mx[:, 4], mx[:, 5]], axis=1)

    wi = w_in[0]
    o_w, o_a, o_g, o_q = 3 * rd, 3 * rd + wl, 3 * rd + wl + al, 3 * rd + wl + al + gl
    w_in_p = _bf(jnp.concatenate([
        wi[:, o_q:o_q + ad], wi[:, :3 * rd], wi[:, o_q + ad:],
        _pad_to(wi[:, o_w:o_a], 1, LANES), _pad_to(wi[:, o_a:o_g], 1, LANES), wi[:, o_g:o_q]], axis=1))
    mu = shift_mu[0]
    prm = {
        "mu_r": mu[None, :rd], "mu_k": mu[None, rd:2 * rd], "mu_v": mu[None, 2 * rd:3 * rd],
        "mu_w": _pad_to(mu[None, o_w:o_a], 1, LANES), "mu_a": _pad_to(mu[None, o_a:o_g], 1, LANES),
        "mu_g": mu[None, o_g:o_q],
        "k_k": k_k, "k_a": k_a, "r_k": r_k.reshape(1, rd),
        "decay_bias": decay_bias[0], "decay_up": _bf(_pad_to(decay_up[0], 1, LANES)),
        "iclr_bias": iclr_bias[0], "iclr_up": _bf(_pad_to(iclr_up[0], 1, LANES)),
        "gate_up": _bf(gate_up[0]),
    }

    p_all = _in_projection(ctx, x, norm1_g, m1, w_in_p, tr)

    npair = rd // LANES
    pp = max(n for n in (1, 2, 4, 8) if npair % n == 0
             and all(lay[key] % (n * LANES) == 0 for key in ("r", "k", "v")))
    rp, y0, mm, nn, bonus, gate = _wkv_prep(p_all, lay, prm, ctx_len, None, pp)
    y_f, y_r = _wkv_sweep(rp, y0, mm, nn, ctx_len)

    nf = HEAD_DIM // 4
    freqs = ROPE_BASE ** (-jnp.arange(nf, dtype=F32) / nf)
    pos = jnp.arange(t, dtype=jnp.int32)
    ang_r = (pos // GRID_W).astype(F32)[:, None] * freqs
    ang_c = (pos % GRID_W).astype(F32)[:, None] * freqs
    cos_h = jnp.concatenate([jnp.cos(ang_r)] * 2 + [jnp.cos(ang_c)] * 2, axis=1)
    sin_h = jnp.concatenate([-jnp.sin(ang_r), jnp.sin(ang_r), -jnp.sin(ang_c), jnp.sin(ang_c)], axis=1)
    cos_t = jnp.concatenate([cos_h, cos_h], axis=1)
    sin_t = jnp.concatenate([sin_h, sin_h], axis=1)
    attn = _attention(p_all, lay, cos_t, sin_t, attn_sink, ctx_len, t)

    x1, h2, logits_t = _out_projection(y_f, y_r, bonus, gate, attn, x, lnx_g, lnx_b, _bf(w_out[0]), m2,
                                       norm2_g, w_router[0], ctx_len, tr)
    idx8, gate8, rank8, counts = _router(logits_t, expert_bias[0][:, None])

    n_tok = b * t
    bm = EXPERT_ROWS
    assert (n_tok * TOP_K) % bm == 0
    n_blk = n_tok * TOP_K // bm + ne
    cnt = counts[:, 0].astype(jnp.int32)
    padded = (cnt + bm - 1) // bm * bm
    pend = jnp.cumsum(padded)
    pstart = pend - padded
    pos8 = _slot_positions(idx8, rank8, pstart.astype(F32)[:, None])
    blk_row0 = jnp.arange(n_blk, dtype=jnp.int32) * bm
    blk_expert = jnp.minimum(jnp.sum((pend[None, :] <= blk_row0[:, None]).astype(jnp.int32), axis=1), ne - 1)
    n_used = (pend[-1:] // bm).astype(jnp.int32)

    xs = _dispatch(h2.reshape(n_tok, d), pos8, pstart + cnt, padded - cnt, n_used, n_blk * bm, tr, bm)
    has_rows = cnt > 0
    eids = jnp.arange(ne, dtype=jnp.int32)
    later = jnp.where(has_rows[None, :] & (eids[None, :] > eids[:, None]), eids[None, :], ne)
    next_expert = jnp.min(later, axis=1)
    next_expert = jnp.where(next_expert < ne, next_expert, -1).astype(jnp.int32)
    expert_slot = ((jnp.cumsum(has_rows.astype(jnp.int32)) - 1) % 2).astype(jnp.int32)
    ys = _expert_ffn(xs, blk_expert, n_used, next_expert, expert_slot, e_gate[0], e_up[0], e_down[0], bm)
    return _final(h2, x1, pos8, gate8.T, ys, _bf(s_gate[0]), _bf(s_up[0]), _bf(s_down[0]),
                  m2, final_g[None], tr)
```

```python
import functools
import math

import jax
import jax.numpy as jnp
from jax import lax
from jax.experimental import pallas as pl
from jax.experimental.pallas import tpu as pltpu

GRID_W = 64
WINDOW = 128
ROPE_BASE = 10000.0
N_GROUPS = 8
TOPK_GROUPS = 4
TOP_K = 8
ROUTED_SCALE = 2.5
NORM_EPS = 1e-6
LNX_EPS = 64e-5
KK_EPS = 1e-12

LANES = 128
SUBLANES = 8
VMEM_LIMIT_BYTES = 56 * 1024 * 1024

HEAD_DIM = 64
CHUNK = 64
QBLOCK = 128
EXPERT_ROWS = 256

F32 = jnp.float32
BF16 = jnp.bfloat16
NEG_BIG = -1e30


def _bf(x):
    return x.astype(BF16)


def _dot(a, b, precision=None):
    return jnp.dot(a, b, preferred_element_type=F32, precision=precision)


def _dot_nt(a, b, precision=None):
    return lax.dot_general(a, b, (((1,), (1,)), ((), ())), preferred_element_type=F32,
                           precision=precision)


def _dot_tn(a, b):
    return lax.dot_general(a, b, (((0,), (0,)), ((), ())), preferred_element_type=F32)


def _params(*sem):
    return pltpu.CompilerParams(dimension_semantics=sem, vmem_limit_bytes=VMEM_LIMIT_BYTES)


def _mod_kernel(c_ref, w_ref, b_ref, o_ref):
    c = c_ref[...]
    s = c * jax.nn.sigmoid(c)
    hi = _bf(s)
    lo = _bf(s - hi.astype(F32))
    r = _dot(jnp.concatenate([hi, lo], axis=0), _bf(w_ref[...]))
    o_ref[...] = r[:SUBLANES] + r[SUBLANES:] + b_ref[...]


def _modulation(c8, w_mod, b_mod):
    d, n = w_mod.shape
    tn = 1024 if n % 1024 == 0 else n
    return pl.pallas_call(
        _mod_kernel,
        out_shape=jax.ShapeDtypeStruct((SUBLANES, n), F32),
        grid=(n // tn,),
        in_specs=[pl.BlockSpec((SUBLANES, d), lambda j: (0, 0)),
                  pl.BlockSpec((d, tn), lambda j: (0, j)),
                  pl.BlockSpec((1, tn), lambda j: (0, j))],
        out_specs=pl.BlockSpec((SUBLANES, tn), lambda j: (0, j)),
        compiler_params=_params("parallel"),
        name="adaln_modulation",
    )(c8, w_mod, b_mod)


def _inproj_kernel(c_ref, x_ref, g_ref, m_ref, w_ref, o_ref, *, nct):
    x = jnp.where(pl.program_id(2) < nct, c_ref[0], x_ref[0])
    ms = jnp.mean(x * x, axis=-1, keepdims=True)
    y = x * lax.rsqrt(ms + NORM_EPS) * g_ref[...]
    m = m_ref[0, 0]
    h = y * (1.0 + m[1:2]) + m[0:1]
    o_ref[0] = _dot(_bf(h), w_ref[...])


def _in_projection(ctx, x, g1, m1, w_in_p, tr):
    b, t, d = x.shape
    ctx_len = ctx.shape[1]
    ta = ctx_len + t
    ncols = w_in_p.shape[1]
    tn = ncols
    nct = ctx_len // tr
    return pl.pallas_call(
        functools.partial(_inproj_kernel, nct=nct),
        out_shape=jax.ShapeDtypeStruct((b, ta, ncols), F32),
        grid=(ncols // tn, b, ta // tr),
        in_specs=[pl.BlockSpec((1, tr, d), lambda n, bb, i: (bb, jnp.minimum(i, nct - 1), 0)),
                  pl.BlockSpec((1, tr, d), lambda n, bb, i: (bb, jnp.maximum(i - nct, 0), 0)),
                  pl.BlockSpec((1, d), lambda n, bb, i: (0, 0)),
                  pl.BlockSpec((1, 1, 2, d), lambda n, bb, i: (bb, jnp.where(i >= nct, 1, 0), 0, 0)),
                  pl.BlockSpec((d, tn), lambda n, bb, i: (0, n), pipeline_mode=pl.Buffered(1))],
        out_specs=pl.BlockSpec((1, tr, tn), lambda n, bb, i: (bb, i, n)),
        compiler_params=_params("parallel", "parallel", "parallel"),
        name="norm_modulate_inproj",
    )(ctx, x, g1, m1, w_in_p)


def _wkv_prep_kernel(r_c, r_p, r_n, k_c, k_p, k_n, v_c, v_p, v_n, w_c, w_p, w_n, a_c, a_p, a_n,
                     g_c, g_p, g_n, mu_r, mu_k, mu_v, mu_w, mu_a, mu_g, kk_ref, ka_ref, rk_ref,
                     db_ref, du_ref, ib_ref, iu_ref, gu_ref,
                     rp_out, y0_out, m_out, n_out, bonus_out, gate_out,
                     *, nctx_steps, n_steps, prec_t):
    L = CHUNK
    rb = r_c.shape[1]
    halves = [slice(h * L, (h + 1) * L) for h in range(rb // L)]
    s = pl.program_id(1)
    has_prev = jnp.logical_and(s != 0, s != nctx_steps)
    has_next = jnp.logical_and(s != nctx_steps - 1, s != n_steps - 1)

    def shifted(cur, prv, nxt, mu):
        x = cur[0]
        rows = lax.broadcasted_iota(jnp.int32, (rb, 1), 0)
        pr = jnp.where(has_prev, prv[0, SUBLANES - 1:SUBLANES, :], 0.0)
        nx = jnp.where(has_next, nxt[0, 0:1, :], 0.0)
        up = jnp.where(rows == 0, pr, pltpu.roll(x, 1, 0))
        dn = jnp.where(rows == rb - 1, nx, pltpu.roll(x, rb - 1, 0))
        return x + mu[...] * (0.5 * (up + dn) - x)

    r = shifted(r_c, r_p, r_n, mu_r)
    k = shifted(k_c, k_p, k_n, mu_k)
    v = shifted(v_c, v_p, v_n, mu_v)
    xw = shifted(w_c, w_p, w_n, mu_w)
    xa = shifted(a_c, a_p, a_n, mu_a)
    xg = shifted(g_c, g_p, g_n, mu_g)

    lane = lax.broadcasted_iota(jnp.int32, (1, LANES), 1)
    m0 = lane < HEAD_DIM

    def head_sum(x):
        s0 = jnp.sum(jnp.where(m0, x, 0.0), axis=-1, keepdims=True)
        s1 = jnp.sum(jnp.where(m0, 0.0, x), axis=-1, keepdims=True)
        return jnp.where(m0, s0, s1)

    def stack2(x):
        return jnp.concatenate([jnp.where(m0, x, 0.0), jnp.where(m0, 0.0, x)], axis=0)

    tw = _bf(jnp.tanh(xw))
    xab = _bf(xa)
    lw_all = [(-math.exp(-0.5)) * jax.nn.sigmoid(db_ref[d:d + 1, :] + _dot(tw, du_ref[d]))
              for d in range(2)]
    sa_all = [jax.nn.sigmoid(ib_ref[d:d + 1, :] + _dot(xab, iu_ref[d])) for d in range(2)]
    gate_out[0] = _dot(_bf(jax.nn.sigmoid(xg)), gu_ref[...])

    ri = lax.broadcasted_iota(jnp.int32, (L, L), 0)
    ci = lax.broadcasted_iota(jnp.int32, (L, L), 1)
    ri2 = lax.broadcasted_iota(jnp.int32, (2 * L, 2 * L), 0)
    ci2 = lax.broadcasted_iota(jnp.int32, (2 * L, 2 * L), 1)
    rt = jnp.bitwise_and(ri2, L - 1)
    ct = jnp.bitwise_and(ci2, L - 1)
    eye = ri2 == ci2
    n_sq = int(math.log2(L)) - 1

    tri = [(ci <= ri).astype(F32), (ci >= ri).astype(F32)]
    strict = [ct < rt, ct > rt]
    incl = [ct <= rt, ct >= rt]
    lw_c = [[lw_all[d][hs] for hs in halves] for d in range(2)]
    cum = [[_dot(tri[d], lw, precision=lax.Precision.HIGHEST) for lw in lw_c[d]] for d in range(2)]
    cend = [[jnp.sum(lw, axis=0, keepdims=True) for lw in lw_c[d]] for d in range(2)]
    e_in = [[jnp.exp(c) for c in cum[d]] for d in range(2)]
    e_ex = [[jnp.exp(c - lw) for c, lw in zip(cum[d], lw_c[d])] for d in range(2)]
    e_neg = [[jnp.exp(-c) for c in cum[d]] for d in range(2)]
    e_end = [[jnp.exp(ce - c) for c, ce in zip(cum[d], cend[d])] for d in range(2)]
    p_end = [[jnp.exp(ce) for ce in cend[d]] for d in range(2)]
    kd_all = [k * (1.0 + (sa_all[d] - 1.0) * ka_ref[...]) for d in range(2)]

    npp = r.shape[1] // LANES
    chains = [(h, p, d) for h in range(len(halves)) for p in range(npp) for d in range(2)]
    sls = [slice(p * LANES, (p + 1) * LANES) for p in range(npp)]
    kk_p = []
    for p in range(npp):
        kx = k[:, sls[p]] * kk_ref[:, sls[p]]
        kk_p.append(kx * lax.rsqrt(head_sum(kx * kx) + KK_EPS))
        ksum = kd_all[0][:, sls[p]] + kd_all[1][:, sls[p]]
        bonus_out[0, :, sls[p]] = head_sum(r[:, sls[p]] * ksum * rk_ref[:, sls[p]]) * v[:, sls[p]]
    vs_c = {(h, p): stack2(v[halves[h], sls[p]]) for h in range(len(halves)) for p in range(npp)}
    vsb_c = {key: _bf(val) for key, val in vs_c.items()}

    a_s, r_s, bh_s, kh_s, g = [], [], [], [], []
    for h, p, d in chains:
        hs, sl = halves[h], sls[p]
        kd = kd_all[d][hs, sl]
        kk = kk_p[p][hs]
        bvec = kk * sa_all[d][hs, sl]
        a_s.append(stack2(-kk * e_ex[d][h][:, sl]))
        r_s.append(stack2(r[hs, sl] * e_in[d][h][:, sl]))
        b_s = stack2(bvec * e_neg[d][h][:, sl])
        k_s = stack2(kd * e_neg[d][h][:, sl])
        bh_s.append(stack2(bvec * e_end[d][h][:, sl]))
        kh_s.append(stack2(kd * e_end[d][h][:, sl]))
        g.append(_dot_nt(_bf(jnp.concatenate([a_s[-1], r_s[-1]], axis=0)),
                         _bf(jnp.concatenate([b_s, k_s], axis=0))))
    a_ab = [jnp.where(strict[d], g[i][:2 * L, :2 * L], 0.0) for i, (h, p, d) in enumerate(chains)]
    a_ak = [jnp.where(strict[d], g[i][:2 * L, 2 * L:], 0.0) for i, (h, p, d) in enumerate(chains)]
    a_rb = [jnp.where(incl[d], g[i][2 * L:, :2 * L], 0.0) for i, (h, p, d) in enumerate(chains)]
    a_rk = [jnp.where(incl[d], g[i][2 * L:, 2 * L:], 0.0) for i, (h, p, d) in enumerate(chains)]

    ident = jnp.where(eye, 1.0, 0.0)
    tmat = [ident + a for a in a_ab]
    apow = a_ab
    for _ in range(n_sq):
        if prec_t is None:
            apow = [_dot(_bf(a), _bf(a)) for a in apow]
            tmat = [t + _dot(_bf(a), _bf(t)) for a, t in zip(apow, tmat)]
        else:
            apow = [_dot(a, a, precision=prec_t) for a in apow]
            tmat = [t + _dot(a, t, precision=prec_t) for a, t in zip(apow, tmat)]

    x1 = [_dot(_bf(a_ak[i]), vsb_c[h, p]) for i, (h, p, d) in enumerate(chains)]
    y0v = [_dot(_bf(a_rk[i]), vsb_c[h, p]) for i, (h, p, d) in enumerate(chains)]
    tu = [_dot(_bf(tmat[i]), _bf(jnp.concatenate([a_s[i], x1[i]], axis=1)))
          for i in range(len(chains))]
    ry = [_dot(_bf(a_rb[i]), _bf(tu[i])) for i in range(len(chains))]
    def fold(z):
        h = z.shape[0] // 2
        return z[:h] + z[h:]

    for i, (h, p, d) in enumerate(chains):
        rp_out[d, h, 0, p] = _bf(fold(r_s[i] + ry[i][:, :LANES]))
        y0_out[d, h, 0, p] = fold(ry[i][:, LANES:] + y0v[i])
    for i, (h, p, d) in enumerate(chains):
        m_out[d, h, 0, p] = _bf(fold(jnp.where(eye, p_end[d][h][:, sls[p]], 0.0)
                                     + _dot_tn(_bf(tu[i][:, :LANES]), _bf(bh_s[i]))))
    for i, (h, p, d) in enumerate(chains):
        n_out[d, h, 0, p] = fold(_dot_tn(_bf(jnp.concatenate([tu[i][:, LANES:], vs_c[h, p]], axis=0)),
                                         _bf(jnp.concatenate([bh_s[i], kh_s[i]], axis=0))))


def _wkv_prep(p_all, lay, prm, ctx_len, prec_t, pp):
    b, ta, _ = p_all.shape
    L = CHUNK
    nc = ta // L
    nctx = ctx_len // L
    npair = lay["rd"] // LANES
    gl = lay["gl"]
    rows8 = ta // SUBLANES
    cps = 2 if nc % 2 == 0 and nctx % 2 == 0 else 1
    rbk = cps * L

    def cur(colfn, width=LANES):
        return pl.BlockSpec((1, rbk, width), lambda bb, c, p: (bb, c, colfn(p)))

    def prv(colfn, width=LANES):
        return pl.BlockSpec((1, SUBLANES, width),
                            lambda bb, c, p: (bb, jnp.maximum(c * (rbk // SUBLANES) - 1, 0), colfn(p)))

    def nxt(colfn, width=LANES):
        return pl.BlockSpec((1, SUBLANES, width),
                            lambda bb, c, p: (bb, jnp.minimum((c + 1) * (rbk // SUBLANES), rows8 - 1), colfn(p)))

    def trio(colfn, width=LANES):
        return [cur(colfn, width), prv(colfn, width), nxt(colfn, width)]

    pw = pp * LANES
    assert npair % pp == 0 and all(lay[n] % pw == 0 for n in ("r", "k", "v"))
    rb, kb, vb = lay["r"] // pw, lay["k"] // pw, lay["v"] // pw
    wb, ab = lay["xw"] // LANES, lay["xa"] // LANES
    assert lay["xg"] % gl == 0
    gb = lay["xg"] // gl

    def vec(width=pw):
        return pl.BlockSpec((1, width), lambda bb, c, p: (0, p))

    def vec0(width=LANES):
        return pl.BlockSpec((1, width), lambda bb, c, p: (0, 0))

    in_specs = (trio(lambda p: rb + p, pw) + trio(lambda p: kb + p, pw) + trio(lambda p: vb + p, pw)
                + trio(lambda p: wb) + trio(lambda p: ab) + trio(lambda p: gb, gl)
                + [vec(), vec(), vec(), vec0(), vec0(), vec0(gl),
                   vec(), vec(), vec(),
                   pl.BlockSpec((2, pw), lambda bb, c, p: (0, p)),
                   pl.BlockSpec((2, LANES, pw), lambda bb, c, p: (0, 0, p)),
                   pl.BlockSpec((2, pw), lambda bb, c, p: (0, p)),
                   pl.BlockSpec((2, LANES, pw), lambda bb, c, p: (0, 0, p)),
                   pl.BlockSpec((gl, pw), lambda bb, c, p: (0, p))])

    def opspec(rows):
        return pl.BlockSpec((2, cps, 1, pp, rows, LANES), lambda bb, c, p: (0, c, bb, p, 0, 0))

    out_specs = [opspec(L), opspec(L), opspec(HEAD_DIM), opspec(HEAD_DIM),
                 pl.BlockSpec((1, rbk, pw), lambda bb, c, p: (bb, c, p)),
                 pl.BlockSpec((1, rbk, pw), lambda bb, c, p: (bb, c, p))]
    out_shape = [jax.ShapeDtypeStruct((2, nc, b, npair, L, LANES), BF16),
                 jax.ShapeDtypeStruct((2, nc, b, npair, L, LANES), F32),
                 jax.ShapeDtypeStruct((2, nc, b, npair, HEAD_DIM, LANES), BF16),
                 jax.ShapeDtypeStruct((2, nc, b, npair, HEAD_DIM, LANES), F32),
                 jax.ShapeDtypeStruct((b, ta, lay["rd"]), F32),
                 jax.ShapeDtypeStruct((b, ta, lay["rd"]), F32)]
    kern = functools.partial(_wkv_prep_kernel, nctx_steps=nctx // cps, n_steps=nc // cps, prec_t=prec_t)
    args = [p_all] * 18 + [prm["mu_r"], prm["mu_k"], prm["mu_v"], prm["mu_w"], prm["mu_a"], prm["mu_g"],
                           prm["k_k"], prm["k_a"], prm["r_k"], prm["decay_bias"], prm["decay_up"],
                           prm["iclr_bias"], prm["iclr_up"], prm["gate_up"]]
    return pl.pallas_call(
        kern, out_shape=out_shape, grid=(b, nc // cps, npair // pp), in_specs=in_specs, out_specs=out_specs,
        compiler_params=_params("parallel", "parallel", "parallel"),
        name="wkv_chunk_operators",
    )(*args)


def _wkv_seq_kernel(rpf, y0f, mf, nf, rpr, y0r, mr, nr, yf_ref, yr_ref, s_ref, *, nb, npair):
    L = CHUNK

    @pl.when(pl.program_id(0) == 0)
    def _():
        s_ref[...] = jnp.zeros_like(s_ref)

    m0 = lax.broadcasted_iota(jnp.int32, (1, LANES), 1) < HEAD_DIM

    def unfold(z):
        zero = jnp.zeros_like(z)
        return jnp.concatenate([jnp.where(m0, z, zero), jnp.where(m0, zero, z)], axis=0)

    for d, (rp_ref, y0_ref, m_ref, n_ref, y_ref) in enumerate(((rpf, y0f, mf, nf, yf_ref),
                                                               (rpr, y0r, mr, nr, yr_ref))):
        for bb in range(nb):
            for p in range(npair):
                i = (d * nb + bb) * npair + p
                s0 = _bf(s_ref[i])
                ys = _dot_nt(unfold(rp_ref[0, 0, bb, p]), s0)
                y_ref[bb, :, p * LANES:(p + 1) * LANES] = ys[:L] + ys[L:] + y0_ref[0, 0, bb, p]
                s_ref[i] = _dot(s0, unfold(m_ref[0, 0, bb, p])) + unfold(n_ref[0, 0, bb, p])


def _wkv_sweep(rp, y0, mm, nn, ctx_len):
    _, nc, b, npair, _, _ = rp.shape
    L = CHUNK
    nctx = ctx_len // L

    def chunk_of(d, c):
        return c if d == 0 else jnp.where(c < nctx, nctx - 1 - c, nc - 1 - (c - nctx))

    def opspec(d, rows):
        return pl.BlockSpec((1, 1, b, npair, rows, LANES), lambda c: (d, chunk_of(d, c), 0, 0, 0, 0))

    def yspec(d):
        return pl.BlockSpec((b, L, npair * LANES), lambda c: (0, chunk_of(d, c), 0))

    rows = (L, L, HEAD_DIM, HEAD_DIM)
    y_shape = jax.ShapeDtypeStruct((b, nc * L, npair * LANES), F32)
    return pl.pallas_call(
        functools.partial(_wkv_seq_kernel, nb=b, npair=npair),
        out_shape=[y_shape, y_shape],
        grid=(nc,),
        in_specs=[opspec(d, r) for d in range(2) for r in rows],
        out_specs=[yspec(0), yspec(1)],
        scratch_shapes=[pltpu.VMEM((2 * b * npair, LANES, LANES), F32)],
        compiler_params=_params("arbitrary"),
        name="wkv_state_sweep",
    )(rp, y0, mm, nn, rp, y0, mm, nn)


def _attn_kernel(q_ref, kp_ref, kc_ref, kn_ref, vp_ref, vc_ref, vn_ref, kx_ref, vx_ref,
                 cq_ref, sq_ref, cp_ref, sp_ref, cn_ref, sn_ref, sink_ref, o_ref,
                 *, n_blocks, kv_heads, group):
    n = pl.program_id(1)
    qb = QBLOCK

    def rope(x, cos, sin):
        w = x.shape[1]
        reps = w // LANES
        cs = jnp.concatenate([cos] * reps, axis=1) if reps > 1 else cos
        sn = jnp.concatenate([sin] * reps, axis=1) if reps > 1 else sin
        lane = lax.broadcasted_iota(jnp.int32, (1, w), 1)
        sw = jnp.where(jnp.bitwise_and(lane, 16) == 0, pltpu.roll(x, w - 16, 1), pltpu.roll(x, 16, 1))
        return x * cs + sw * sn

    q = rope(q_ref[0], cq_ref[...], sq_ref[...]) * (HEAD_DIM ** -0.5)
    kwin = jnp.concatenate([rope(kp_ref[0], cp_ref[...], sp_ref[...]),
                            rope(kc_ref[0], cq_ref[...], sq_ref[...]),
                            rope(kn_ref[0], cn_ref[...], sn_ref[...]),
                            kx_ref[0]], axis=0)
    vwin = jnp.concatenate([vp_ref[0], vc_ref[0], vn_ref[0], vx_ref[0]], axis=0)
    nkeys = kwin.shape[0]

    rows = group * qb
    qi = jnp.bitwise_and(lax.broadcasted_iota(jnp.int32, (rows, nkeys), 0), qb - 1)
    s_i = lax.broadcasted_iota(jnp.int32, (rows, nkeys), 1)
    blk = s_i // qb + (n - 1)
    in_win = (s_i >= qi) & (s_i <= qi + 2 * WINDOW) & (blk >= 0) & (blk < n_blocks)
    valid = in_win | (s_i >= 3 * qb)
    rgrp = lax.broadcasted_iota(jnp.int32, (rows, 1), 0) // qb

    scores, sinks = [], []
    for kh in range(kv_heads):
        kk_ = _bf(kwin[:, kh * HEAD_DIM:(kh + 1) * HEAD_DIM])
        qs = jnp.concatenate([q[:, (kh * group + g) * HEAD_DIM:(kh * group + g + 1) * HEAD_DIM]
                              for g in range(group)], axis=0)
        sk = jnp.zeros((rows, 1), F32)
        for g in range(group):
            h = kh * group + g
            sk = jnp.where(rgrp == g, sink_ref[0:1, h:h + 1], sk)
        sinks.append(sk)
        scores.append(jnp.where(valid, _dot_nt(_bf(qs), kk_), NEG_BIG))
    probs, denoms = [], []
    for s, sk in zip(scores, sinks):
        m = jnp.maximum(jnp.max(s, axis=-1, keepdims=True), sk)
        e = jnp.exp(s - m)
        denoms.append(jnp.sum(e, axis=-1, keepdims=True) + jnp.exp(sk - m))
        probs.append(_bf(e))
    pieces = []
    for kh in range(kv_heads):
        vv_ = _bf(vwin[:, kh * HEAD_DIM:(kh + 1) * HEAD_DIM])
        o = _dot(probs[kh], vv_) / denoms[kh]
        for g in range(group):
            pieces.append(o[g * qb:(g + 1) * qb])
    o_ref[0] = _bf(jnp.concatenate(pieces, axis=1))


def _attention(p_all, lay, cos_t, sin_t, sink, ctx_len, seq):
    b = p_all.shape[0]
    qb = QBLOCK
    nb = seq // qb
    cb = ctx_len // qb
    ad, kvd = lay["ad"], lay["kvd"]
    kv_heads = kvd // HEAD_DIM
    group = (ad // HEAD_DIM) // kv_heads
    assert lay["q"] % ad == 0 and lay["ka"] % kvd == 0 and lay["va"] % kvd == 0
    qc, kc, vc = lay["q"] // ad, lay["ka"] // kvd, lay["va"] // kvd

    def clampb(n, off):
        return jnp.clip(n + off, 0, nb - 1)

    def win(col, off):
        return pl.BlockSpec((1, qb, kvd), lambda bb, n: (bb, cb + clampb(n, off), col))

    def tab(off):
        return pl.BlockSpec((qb, LANES), lambda bb, n: (clampb(n, off), 0))

    assert ctx_len % qb == 0
    ctxspec = lambda col: pl.BlockSpec((1, ctx_len, kvd), lambda bb, n: (bb, 0, col))
    in_specs = [pl.BlockSpec((1, qb, ad), lambda bb, n: (bb, cb + n, qc)),
                win(kc, -1), win(kc, 0), win(kc, 1), win(vc, -1), win(vc, 0), win(vc, 1),
                ctxspec(kc), ctxspec(vc),
                tab(0), tab(0), tab(-1), tab(-1), tab(1), tab(1),
                pl.BlockSpec((1, sink.shape[1]), lambda bb, n: (0, 0))]
    return pl.pallas_call(
        functools.partial(_attn_kernel, n_blocks=nb, kv_heads=kv_heads, group=group),
        out_shape=jax.ShapeDtypeStruct((b, seq, ad), BF16),
        grid=(b, nb),
        in_specs=in_specs,
        out_specs=pl.BlockSpec((1, qb, ad), lambda bb, n: (bb, n, 0)),
        compiler_params=_params("parallel", "parallel"),
        name="window_ctx_attention",
    )(p_all, p_all, p_all, p_all, p_all, p_all, p_all, p_all, p_all,
      cos_t, sin_t, cos_t, sin_t, cos_t, sin_t, sink)


def _outproj_kernel(yf_ref, yr_ref, bon_ref, gat_ref, att_ref, x_ref, lg_ref, lb_ref, w_ref,
                    m_ref, g2_ref, wr_ref, x1_ref, h2_ref, lt_ref):
    ysum = yf_ref[0] + yr_ref[0]
    ri = lax.broadcasted_iota(jnp.int32, (LANES, LANES), 0) // HEAD_DIM
    ci = lax.broadcasted_iota(jnp.int32, (LANES, LANES), 1) // HEAD_DIM
    hm = _bf(jnp.where(ri == ci, 1.0 / HEAD_DIM, 0.0))

    def head_mean(z):
        hi = _bf(z)
        lo = _bf(z - hi.astype(F32))
        return _dot(hi, hm) + _dot(lo, hm)

    ybs = [ysum[:, j * LANES:(j + 1) * LANES] for j in range(ysum.shape[1] // LANES)]
    devs = [yb - mu for yb, mu in zip(ybs, [head_mean(yb) for yb in ybs])]
    blocks = [dev * lax.rsqrt(var + LNX_EPS)
              for dev, var in zip(devs, [head_mean(dev * dev) for dev in devs])]
    yn = jnp.concatenate(blocks, axis=1) if len(blocks) > 1 else blocks[0]
    rw = (yn * lg_ref[...] + lb_ref[...] + bon_ref[0]) * gat_ref[0]
    mix = jnp.concatenate([_bf(rw), att_ref[0]], axis=1)
    m = m_ref[0]
    x1 = x_ref[0] + m[0:1] * _dot(mix, w_ref[...])
    x1_ref[0] = x1
    ms = jnp.mean(x1 * x1, axis=-1, keepdims=True)
    h2 = x1 * lax.rsqrt(ms + NORM_EPS) * g2_ref[...] * (1.0 + m[2:3]) + m[1:2]
    h2_ref[0] = h2
    hi = _bf(h2)
    lo = _bf(h2 - hi.astype(F32))
    wr = wr_ref[...]
    wr_hi = _bf(wr)
    wr_lo = _bf(wr - wr_hi.astype(F32))
    lg = _dot(hi, wr_hi) + (_dot(hi, wr_lo) + _dot(lo, wr_hi))
    lt_ref[...] = lg.T[:lt_ref.shape[0]]


def _out_projection(y_f, y_r, bonus, gate, attn, x, lnx_g, lnx_b, w_out_b, m2, g2, w_router, ctx_len, tr):
    b, t, d = x.shape
    rd = y_f.shape[2]
    ad = attn.shape[2]
    ne = w_router.shape[1]
    assert ne <= LANES
    wr = _pad_to(w_router, 1, LANES)
    co = ctx_len // tr
    nt = t // tr
    return pl.pallas_call(
        _outproj_kernel,
        out_shape=[jax.ShapeDtypeStruct((b, t, d), F32),
                   jax.ShapeDtypeStruct((b, t, d), F32),
                   jax.ShapeDtypeStruct((ne, b * t), F32)],
        grid=(b, nt),
        in_specs=[pl.BlockSpec((1, tr, rd), lambda bb, i: (bb, co + i, 0)),
                  pl.BlockSpec((1, tr, rd), lambda bb, i: (bb, co + i, 0)),
                  pl.BlockSpec((1, tr, rd), lambda bb, i: (bb, co + i, 0)),
                  pl.BlockSpec((1, tr, rd), lambda bb, i: (bb, co + i, 0)),
                  pl.BlockSpec((1, tr, ad), lambda bb, i: (bb, i, 0)),
                  pl.BlockSpec((1, tr, d), lambda bb, i: (bb, i, 0)),
                  pl.BlockSpec((1, rd), lambda bb, i: (0, 0)),
                  pl.BlockSpec((1, rd), lambda bb, i: (0, 0)),
                  pl.BlockSpec((rd + ad, d), lambda bb, i: (0, 0)),
                  pl.BlockSpec((1, 4, d), lambda bb, i: (bb, 0, 0)),
                  pl.BlockSpec((1, d), lambda bb, i: (0, 0)),
                  pl.BlockSpec((d, LANES), lambda bb, i: (0, 0))],
        out_specs=[pl.BlockSpec((1, tr, d), lambda bb, i: (bb, i, 0)),
                   pl.BlockSpec((1, tr, d), lambda bb, i: (bb, i, 0)),
                   pl.BlockSpec((ne, tr), lambda bb, i: (0, bb * nt + i))],
        compiler_params=_params("parallel", "parallel"),
        name="rwkv_out_outproj_norm2_router",
    )(y_f, y_r, bonus, gate, attn, x, lnx_g, lnx_b, w_out_b, m2, g2, wr)


def _router_kernel(l_ref, b_ref, idx_ref, gate_ref, rank_ref, cnt_ref, carry_ref):
    @pl.when(pl.program_id(0) == 0)
    def _():
        carry_ref[...] = jnp.zeros_like(carry_ref)

    sc = jax.nn.sigmoid(l_ref[...])
    ne, tn = sc.shape
    eg = ne // N_GROUPS
    bz = sc + b_ref[...]
    g3 = bz.reshape(N_GROUPS, eg, tn)
    io = lax.broadcasted_iota(jnp.int32, (N_GROUPS, eg, tn), 1).astype(F32)
    m1 = jnp.max(g3, axis=1, keepdims=True)
    i1 = jnp.min(jnp.where(g3 == m1, io, float(eg)), axis=1, keepdims=True)
    m2 = jnp.max(jnp.where(io == i1, -jnp.inf, g3), axis=1, keepdims=True)
    gs = (m1 + m2).reshape(N_GROUPS, tn)

    def topk_hits(vals, k):
        n = vals.shape[0]
        idx = lax.broadcasted_iota(jnp.int32, vals.shape, 0).astype(F32)
        hits = []
        cur = vals
        for _ in range(k):
            mx = jnp.max(cur, axis=0, keepdims=True)
            ix = jnp.min(jnp.where(cur == mx, idx, float(n)), axis=0, keepdims=True)
            hit = idx == ix
            hits.append(hit)
            cur = jnp.where(hit, -jnp.inf, cur)
        return hits

    gsel = sum(jnp.where(h, 1.0, 0.0) for h in topk_hits(gs, TOPK_GROUPS))
    masked = jnp.where(gsel.reshape(N_GROUPS, 1, tn) > 0.0, g3, -jnp.inf).reshape(ne, tn)
    hits = topk_hits(masked, TOP_K)
    chosen = sum(jnp.where(h, 1.0, 0.0) for h in hits)
    denom = jnp.sum(chosen * sc, axis=0, keepdims=True)

    si = lax.broadcasted_iota(jnp.int32, (tn, tn), 0)
    ti = lax.broadcasted_iota(jnp.int32, (tn, tn), 1)
    before = _bf(jnp.where(si < ti, 1.0, 0.0))
    rank_full = _dot(_bf(chosen), before) + carry_ref[...]
    carry_ref[...] += jnp.sum(chosen, axis=1, keepdims=True)
    cnt_ref[...] = carry_ref[...]

    eidx = lax.broadcasted_iota(jnp.int32, (ne, tn), 0).astype(F32)
    for j, hit in enumerate(hits):
        idx_ref[j:j + 1, :] = jnp.sum(jnp.where(hit, eidx, 0.0), axis=0, keepdims=True).astype(jnp.int32)
        gate_ref[j:j + 1, :] = (jnp.sum(jnp.where(hit, sc, 0.0), axis=0, keepdims=True)
                                / denom * ROUTED_SCALE)
        rank_ref[j:j + 1, :] = jnp.sum(jnp.where(hit, rank_full, 0.0), axis=0,
                                       keepdims=True).astype(jnp.int32)


def _router(logits_t, bias_col):
    ne, n = logits_t.shape
    tn = 1024 if n % 1024 == 0 else n
    choice = lambda: pl.BlockSpec((TOP_K, tn), lambda i: (0, i))
    return pl.pallas_call(
        _router_kernel,
        out_shape=[jax.ShapeDtypeStruct((TOP_K, n), jnp.int32),
                   jax.ShapeDtypeStruct((TOP_K, n), F32),
                   jax.ShapeDtypeStruct((TOP_K, n), jnp.int32),
                   jax.ShapeDtypeStruct((ne, 1), F32)],
        grid=(n // tn,),
        in_specs=[pl.BlockSpec((ne, tn), lambda i: (0, i)),
                  pl.BlockSpec((ne, 1), lambda i: (0, 0))],
        out_specs=[choice(), choice(), choice(), pl.BlockSpec((ne, 1), lambda i: (0, 0))],
        scratch_shapes=[pltpu.VMEM((ne, 1), F32)],
        compiler_params=_params("arbitrary"),
        name="group_topk_router",
    )(logits_t, bias_col)


def _slots_kernel(idx_ref, rank_ref, ps_ref, o_ref):
    ne, tn = ps_ref.shape[0], idx_ref.shape[1]
    eidx = lax.broadcasted_iota(jnp.int32, (ne, tn), 0)
    for j in range(TOP_K):
        start = jnp.sum(jnp.where(eidx == idx_ref[j:j + 1, :], ps_ref[...], 0.0), axis=0, keepdims=True)
        o_ref[j:j + 1, :] = start.astype(jnp.int32) + rank_ref[j:j + 1, :]


def _slot_positions(idx8, rank8, pstart_col):
    _, n = idx8.shape
    ne = pstart_col.shape[0]
    tn = 1024 if n % 1024 == 0 else n
    choice = lambda: pl.BlockSpec((TOP_K, tn), lambda i: (0, i))
    return pl.pallas_call(
        _slots_kernel,
        out_shape=jax.ShapeDtypeStruct((TOP_K, n), jnp.int32),
        grid=(n // tn,),
        in_specs=[choice(), choice(), pl.BlockSpec((ne, 1), lambda i: (0, 0))],
        out_specs=choice(),
        compiler_params=_params("parallel"),
        name="expert_slot_positions",
    )(idx8, rank8, pstart_col)


def _dispatch_kernel(pad_start_ref, npad_ref, nused_ref, pos_ref, h_ref, xs_ref, zero_ref, sem,
                     *, n_experts):
    tt = h_ref.shape[0]
    bm = zero_ref.shape[0]

    def issue(t, carry):
        for k in range(TOP_K):
            pltpu.make_async_copy(h_ref.at[pl.ds(t, 1)], xs_ref.at[pl.ds(pos_ref[k, t], 1)], sem).start()
        return carry

    lax.fori_loop(0, tt, issue, 0, unroll=4)
    for _ in range(TOP_K):
        pltpu.make_async_copy(h_ref, xs_ref.at[pl.ds(0, tt)], sem).wait()

    @pl.when(pl.program_id(0) == pl.num_programs(0) - 1)
    def _():
        zero_ref[...] = jnp.zeros_like(zero_ref)

        zero_row_src = zero_ref.at[pl.ds(0, 1)]

        def per_expert(e, carry):
            def zero_row(i, c2):
                pltpu.make_async_copy(zero_row_src, xs_ref.at[pl.ds(pad_start_ref[e] + i, 1)], sem).start()
                return c2

            def zero_wait(i, c2):
                pltpu.make_async_copy(zero_row_src, xs_ref.at[pl.ds(0, 1)], sem).wait()
                return c2

            lax.fori_loop(0, npad_ref[e], zero_row, 0)
            lax.fori_loop(0, npad_ref[e], zero_wait, 0)
            return carry

        lax.fori_loop(0, n_experts, per_expert, 0)

        def zero_block(j, carry):
            blk = pltpu.make_async_copy(zero_ref, xs_ref.at[pl.ds(j * bm, bm)], sem)
            blk.start()
            blk.wait()
            return carry

        lax.fori_loop(nused_ref[0], xs_ref.shape[0] // bm, zero_block, 0)


def _dispatch(h2, pos8, pad_start, npad, n_used, n_slots, tt, bm):
    n, d = h2.shape
    ne = pad_start.shape[0]
    grid_spec = pltpu.PrefetchScalarGridSpec(
        num_scalar_prefetch=3,
        grid=(n // tt,),
        in_specs=[pl.BlockSpec((TOP_K, tt), lambda i, ps, npd, nu: (0, i), memory_space=pltpu.SMEM),
                  pl.BlockSpec((tt, d), lambda i, ps, npd, nu: (i, 0))],
        out_specs=pl.BlockSpec(memory_space=pl.ANY),
        scratch_shapes=[pltpu.VMEM((bm, d), h2.dtype), pltpu.SemaphoreType.DMA(())])
    return pl.pallas_call(
        functools.partial(_dispatch_kernel, n_experts=ne),
        out_shape=jax.ShapeDtypeStruct((n_slots, d), h2.dtype),
        grid_spec=grid_spec,
        compiler_params=_params("arbitrary"),
        name="expert_dispatch",
    )(pad_start, npad, n_used, pos8, h2)


def _expert_ffn_kernel(be_ref, nu_ref, nx_ref, sl_ref, x_ref, wg_hbm, wu_hbm, wd_hbm, y_ref,
                       wgf, wuf, wdf, wgb, wub, wdb, sems):
    j = pl.program_id(0)
    e = be_ref[j]
    used = j < nu_ref[0]
    fresh = jnp.logical_or(j == 0, e != be_ref[jnp.maximum(j - 1, 0)])

    def weight_copies(expert, slot):
        return [pltpu.make_async_copy(src.at[expert], dst.at[slot], sems.at[slot])
                for src, dst in ((wg_hbm, wgf), (wu_hbm, wuf), (wd_hbm, wdf))]

    @pl.when(j == 0)
    def _():
        for cp in weight_copies(e, sl_ref[e]):
            cp.start()

    @pl.when(jnp.logical_and(used, fresh))
    def _():
        slot = sl_ref[e]
        for cp in weight_copies(e, slot):
            cp.wait()
        nxt = nx_ref[e]

        @pl.when(nxt >= 0)
        def _():
            for cp in weight_copies(nxt, 1 - slot):
                cp.start()

        wgb[...] = _bf(wgf[slot])
        wub[...] = _bf(wuf[slot])
        wdb[...] = _bf(wdf[slot])

    @pl.when(used)
    def _():
        x = _bf(x_ref[...])
        a = _dot(x, wgb[...])
        u = _dot(x, wub[...])
        y_ref[...] = _dot(_bf(a * jax.nn.sigmoid(a) * u), wdb[...])

    @pl.when(jnp.logical_not(used))
    def _():
        y_ref[...] = jnp.zeros_like(y_ref)


def _expert_ffn(xs, blk_expert, n_used, next_expert, expert_slot, e_gate, e_up, e_down, bm):
    n_slots, dw = xs.shape
    ne, d, ff = e_gate.shape
    assert dw == d
    grid_spec = pltpu.PrefetchScalarGridSpec(
        num_scalar_prefetch=4,
        grid=(n_slots // bm,),
        in_specs=[pl.BlockSpec((bm, dw), lambda j, be, nu, nx, sl: (jnp.minimum(j, nu[0] - 1), 0)),
                  pl.BlockSpec(memory_space=pl.ANY),
                  pl.BlockSpec(memory_space=pl.ANY),
                  pl.BlockSpec(memory_space=pl.ANY)],
        out_specs=pl.BlockSpec((bm, dw), lambda j, be, nu, nx, sl: (j, 0)),
        scratch_shapes=[pltpu.VMEM((2, d, ff), F32), pltpu.VMEM((2, d, ff), F32), pltpu.VMEM((2, ff, d), F32),
                        pltpu.VMEM((d, ff), BF16), pltpu.VMEM((d, ff), BF16), pltpu.VMEM((ff, d), BF16),
                        pltpu.SemaphoreType.DMA((2,))])
    return pl.pallas_call(
        _expert_ffn_kernel,
        out_shape=jax.ShapeDtypeStruct((n_slots, dw), xs.dtype),
        grid_spec=grid_spec,
        compiler_params=_params("arbitrary"),
        name="routed_expert_ffn",
    )(blk_expert, n_used, next_expert, expert_slot, xs, e_gate, e_up, e_down)


def _final_kernel(pos_ref, h_ref, x1_ref, g_ref, ys_ref, wg_ref, wu_ref, wd_ref, m_ref, fg_ref,
                  o_ref, buf, sem):
    tr = h_ref.shape[1]

    def issue(t, carry):
        for k in range(TOP_K):
            pltpu.make_async_copy(ys_ref.at[pl.ds(pos_ref[k, t], 1)], buf.at[k, pl.ds(t, 1)], sem).start()
        return carry

    lax.fori_loop(0, tr, issue, 0, unroll=4)
    h = _bf(h_ref[0])
    a = _dot(h, wg_ref[...])
    u = _dot(h, wu_ref[...])
    moe = _dot(_bf(a * jax.nn.sigmoid(a) * u), wd_ref[...])
    g = g_ref[...]
    for k in range(TOP_K):
        pltpu.make_async_copy(ys_ref.at[pl.ds(0, tr)], buf.at[k], sem).wait()
    for k in range(TOP_K):
        moe = moe + g[:, k:k + 1] * buf[k]
    x2 = x1_ref[0] + m_ref[0, 3:4, :] * moe
    ms = jnp.mean(x2 * x2, axis=-1, keepdims=True)
    o_ref[0] = x2 * lax.rsqrt(ms + NORM_EPS) * fg_ref[...]


def _final(h2, x1, pos8, gate_rows, ys, s_gate_b, s_up_b, s_down_b, m2, final_g, tr):
    b, t, d = x1.shape
    ff = s_gate_b.shape[1]
    nt = t // tr
    dw = h2.shape[2]
    row = lambda width=d: pl.BlockSpec((1, tr, width), lambda bb, i: (bb, i, 0))
    return pl.pallas_call(
        _final_kernel,
        out_shape=jax.ShapeDtypeStruct((b, t, d), F32),
        grid=(b, nt),
        in_specs=[pl.BlockSpec((TOP_K, tr), lambda bb, i: (0, bb * nt + i), memory_space=pltpu.SMEM),
                  row(dw), row(),
                  pl.BlockSpec((tr, TOP_K), lambda bb, i: (bb * nt + i, 0)),
                  pl.BlockSpec(memory_space=pl.ANY),
                  pl.BlockSpec((d, ff), lambda bb, i: (0, 0)),
                  pl.BlockSpec((d, ff), lambda bb, i: (0, 0)),
                  pl.BlockSpec((ff, d), lambda bb, i: (0, 0)),
                  pl.BlockSpec((1, 4, d), lambda bb, i: (bb, 0, 0)),
                  pl.BlockSpec((1, d), lambda bb, i: (0, 0))],
        out_specs=row(),
        scratch_shapes=[pltpu.VMEM((TOP_K, tr, dw), ys.dtype), pltpu.SemaphoreType.DMA(())],
        compiler_params=_params("parallel", "parallel"),
        name="combine_shared_expert_final_norm",
    )(pos8, h2, x1, gate_rows, ys, s_gate_b, s_up_b, s_down_b, m2, final_g)


def _pad_to(a, axis, n):
    pad = [(0, 0)] * a.ndim
    pad[axis] = (0, n - a.shape[axis])
    return jnp.pad(a, pad)


def _layout(rd, wl, al, gl, ad, kvd):
    assert wl <= LANES and al <= LANES
    lay = {"rd": rd, "gl": gl, "ad": ad, "kvd": kvd}
    off = 0
    for name, width in (("q", ad), ("r", rd), ("k", rd), ("v", rd), ("ka", kvd), ("va", kvd),
                        ("xw", LANES), ("xa", LANES), ("xg", gl)):
        assert width % LANES == 0
        lay[name] = off
        off += width
    lay["ncols"] = off
    return lay


def kernel(x, c, ctx, c_ctx, w_mod, b_mod, norm1_g, norm2_g, w_in, shift_mu, k_k, k_a, r_k, decay_bias, decay_up, iclr_bias, iclr_up, gate_up, lnx_g, lnx_b, attn_sink, w_out, w_router, expert_bias, e_gate, e_up, e_down, s_gate, s_up, s_down, final_g):
    assert w_mod.shape[0] == 1, "single-layer block only"
    b, t, d = x.shape
    ctx_len = ctx.shape[1]
    rh, hd = r_k.shape[1:]
    assert hd == HEAD_DIM and b + 1 <= SUBLANES
    rd = rh * hd
    wl, al, gl = decay_up.shape[2], iclr_up.shape[2], gate_up.shape[1]
    ad = attn_sink.shape[1] * hd
    kvd = (w_in.shape[2] - (3 * rd + wl + al + gl) - ad) // 2
    ne = w_router.shape[2]
    lay = _layout(rd, wl, al, gl, ad, kvd)
    tr = 256 if ctx_len % 256 == 0 and t % 256 == 0 else CHUNK
    assert ctx_len % CHUNK == 0 and t % CHUNK == 0 and t % QBLOCK == 0 and WINDOW == QBLOCK

    c8 = _pad_to(jnp.concatenate([c, c_ctx[None]], axis=0), 0, SUBLANES)
    mod = _modulation(c8, w_mod[0], b_mod[0][None])
    mod6 = mod.reshape(SUBLANES, 6, d)
    mx, mc = mod6[:b], mod6[b]
    m1 = jnp.stack([jnp.broadcast_to(mc[None, 0:2], (b, 2, d)), mx[:, 0:2]], axis=1)
    m2 = jnp.stack([mx[:, 2], mx[:, 3], mx[:, 4], mx[:, 5]], axis=1)

    wi = w_in[0]
    o_w, o_a, o_g, o_q = 3 * rd, 3 * rd + wl, 3 * rd + wl + al, 3 * rd + wl + al + gl
    w_in_p = _bf(jnp.concatenate([
        wi[:, o_q:o_q + ad], wi[:, :3 * rd], wi[:, o_q + ad:],
        _pad_to(wi[:, o_w:o_a], 1, LANES), _pad_to(wi[:, o_a:o_g], 1, LANES), wi[:, o_g:o_q]], axis=1))
    mu = shift_mu[0]
    prm = {
        "mu_r": mu[None, :rd], "mu_k": mu[None, rd:2 * rd], "mu_v": mu[None, 2 * rd:3 * rd],
        "mu_w": _pad_to(mu[None, o_w:o_a], 1, LANES), "mu_a": _pad_to(mu[None, o_a:o_g], 1, LANES),
        "mu_g": mu[None, o_g:o_q],
        "k_k": k_k, "k_a": k_a, "r_k": r_k.reshape(1, rd),
        "decay_bias": decay_bias[0], "decay_up": _bf(_pad_to(decay_up[0], 1, LANES)),
        "iclr_bias": iclr_bias[0], "iclr_up": _bf(_pad_to(iclr_up[0], 1, LANES)),
        "gate_up": _bf(gate_up[0]),
    }

    p_all = _in_projection(ctx, x, norm1_g, m1, w_in_p, tr)

    npair = rd // LANES
    pp = max(n for n in (1, 2, 4, 8) if npair % n == 0
             and all(lay[key] % (n * LANES) == 0 for key in ("r", "k", "v")))
    rp, y0, mm, nn, bonus, gate = _wkv_prep(p_all, lay, prm, ctx_len, None, pp)
    y_f, y_r = _wkv_sweep(rp, y0, mm, nn, ctx_len)

    nf = HEAD_DIM // 4
    freqs = ROPE_BASE ** (-jnp.arange(nf, dtype=F32) / nf)
    pos = jnp.arange(t, dtype=jnp.int32)
    ang_r = (pos // GRID_W).astype(F32)[:, None] * freqs
    ang_c = (pos % GRID_W).astype(F32)[:, None] * freqs
    cos_h = jnp.concatenate([jnp.cos(ang_r)] * 2 + [jnp.cos(ang_c)] * 2, axis=1)
    sin_h = jnp.concatenate([-jnp.sin(ang_r), jnp.sin(ang_r), -jnp.sin(ang_c), jnp.sin(ang_c)], axis=1)
    cos_t = jnp.concatenate([cos_h, cos_h], axis=1)
    sin_t = jnp.concatenate([sin_h, sin_h], axis=1)
    attn = _attention(p_all, lay, cos_t, sin_t, attn_sink, ctx_len, t)

    x1, h2, logits_t = _out_projection(y_f, y_r, bonus, gate, attn, x, lnx_g, lnx_b, _bf(w_out[0]), m2,
                                       norm2_g, w_router[0], ctx_len, tr)
    idx8, gate8, rank8, counts = _router(logits_t, expert_bias[0][:, None])

    n_tok = b * t
    bm = EXPERT_ROWS
    assert (n_tok * TOP_K) % bm == 0
    n_blk = n_tok * TOP_K // bm + ne
    cnt = counts[:, 0].astype(jnp.int32)
    padded = (cnt + bm - 1) // bm * bm
    pend = jnp.cumsum(padded)
    pstart = pend - padded
    pos8 = _slot_positions(idx8, rank8, pstart.astype(F32)[:, None])
    blk_row0 = jnp.arange(n_blk, dtype=jnp.int32) * bm
    blk_expert = jnp.minimum(jnp.sum((pend[None, :] <= blk_row0[:, None]).astype(jnp.int32), axis=1), ne - 1)
    n_used = (pend[-1:] // bm).astype(jnp.int32)

    xs = _dispatch(h2.reshape(n_tok, d), pos8, pstart + cnt, padded - cnt, n_used, n_blk * bm, tr, bm)
    has_rows = cnt > 0
    eids = jnp.arange(ne, dtype=jnp.int32)
    later = jnp.where(has_rows[None, :] & (eids[None, :] > eids[:, None]), eids[None, :], ne)
    next_expert = jnp.min(later, axis=1)
    next_expert = jnp.where(next_expert < ne, next_expert, -1).astype(jnp.int32)
    expert_slot = ((jnp.cumsum(has_rows.astype(jnp.int32)) - 1) % 2).astype(jnp.int32)
    ys = _expert_ffn(xs, blk_expert, n_used, next_expert, expert_slot, e_gate[0], e_up[0], e_down[0], bm)
    return _final(h2, x1, pos8, gate8.T, ys, _bf(s_gate[0]), _bf(s_up[0]), _bf(s_down[0]),
                  m2, final_g[None], tr)
```

```python
import functools
import math

import jax
import jax.numpy as jnp
from jax import lax
from jax.experimental import pallas as pl
from jax.experimental.pallas import tpu as pltpu

GRID_W = 64
WINDOW = 128
ROPE_BASE = 10000.0
N_GROUPS = 8
TOPK_GROUPS = 4
TOP_K = 8
ROUTED_SCALE = 2.5
NORM_EPS = 1e-6
LNX_EPS = 64e-5
KK_EPS = 1e-12

LANES = 128
SUBLANES = 8
VMEM_LIMIT_BYTES = 56 * 1024 * 1024

HEAD_DIM = 64
CHUNK = 64
QBLOCK = 128
EXPERT_ROWS = 256

F32 = jnp.float32
BF16 = jnp.bfloat16
NEG_BIG = -1e30


def _bf(x):
    return x.astype(BF16)


def _dot(a, b, precision=None):
    return jnp.dot(a, b, preferred_element_type=F32, precision=precision)


def _dot_nt(a, b, precision=None):
    return lax.dot_general(a, b, (((1,), (1,)), ((), ())), preferred_element_type=F32,
                           precision=precision)


def _dot_tn(a, b):
    return lax.dot_general(a, b, (((0,), (0,)), ((), ())), preferred_element_type=F32)


def _params(*sem):
    return pltpu.CompilerParams(dimension_semantics=sem, vmem_limit_bytes=VMEM_LIMIT_BYTES)


def _mod_kernel(c_ref, w_ref, b_ref, o_ref):
    c = c_ref[...]
    s = c * jax.nn.sigmoid(c)
    hi = _bf(s)
    lo = _bf(s - hi.astype(F32))
    r = _dot(jnp.concatenate([hi, lo], axis=0), _bf(w_ref[...]))
    o_ref[...] = r[:SUBLANES] + r[SUBLANES:] + b_ref[...]


def _modulation(c8, w_mod, b_mod):
    d, n = w_mod.shape
    tn = 1024 if n % 1024 == 0 else n
    return pl.pallas_call(
        _mod_kernel,
        out_shape=jax.ShapeDtypeStruct((SUBLANES, n), F32),
        grid=(n // tn,),
        in_specs=[pl.BlockSpec((SUBLANES, d), lambda j: (0, 0)),
                  pl.BlockSpec((d, tn), lambda j: (0, j)),
                  pl.BlockSpec((1, tn), lambda j: (0, j))],
        out_specs=pl.BlockSpec((SUBLANES, tn), lambda j: (0, j)),
        compiler_params=_params("parallel"),
        name="adaln_modulation",
    )(c8, w_mod, b_mod)


def _inproj_kernel(c_ref, x_ref, g_ref, m_ref, w_ref, o_ref, *, nct):
    x = jnp.where(pl.program_id(2) < nct, c_ref[0], x_ref[0])
    ms = jnp.mean(x * x, axis=-1, keepdims=True)
    y = x * lax.rsqrt(ms + NORM_EPS) * g_ref[...]
    m = m_ref[0, 0]
    h = y * (1.0 + m[1:2]) + m[0:1]
    o_ref[0] = _dot(_bf(h), w_ref[...])


def _in_projection(ctx, x, g1, m1, w_in_p, tr):
    b, t, d = x.shape
    ctx_len = ctx.shape[1]
    ta = ctx_len + t
    ncols = w_in_p.shape[1]
    tn = ncols
    nct = ctx_len // tr
    return pl.pallas_call(
        functools.partial(_inproj_kernel, nct=nct),
        out_shape=jax.ShapeDtypeStruct((b, ta, ncols), F32),
        grid=(ncols // tn, b, ta // tr),
        in_specs=[pl.BlockSpec((1, tr, d), lambda n, bb, i: (bb, jnp.minimum(i, nct - 1), 0)),
                  pl.BlockSpec((1, tr, d), lambda n, bb, i: (bb, jnp.maximum(i - nct, 0), 0)),
                  pl.BlockSpec((1, d), lambda n, bb, i: (0, 0)),
                  pl.BlockSpec((1, 1, 2, d), lambda n, bb, i: (bb, jnp.where(i >= nct, 1, 0), 0, 0)),
                  pl.BlockSpec((d, tn), lambda n, bb, i: (0, n), pipeline_mode=pl.Buffered(1))],
        out_specs=pl.BlockSpec((1, tr, tn), lambda n, bb, i: (bb, i, n)),
        compiler_params=_params("parallel", "parallel", "parallel"),
        name="norm_modulate_inproj",
    )(ctx, x, g1, m1, w_in_p)


def _wkv_prep_kernel(r_c, r_p, r_n, k_c, k_p, k_n, v_c, v_p, v_n, w_c, w_p, w_n, a_c, a_p, a_n,
                     g_c, g_p, g_n, mu_r, mu_k, mu_v, mu_w, mu_a, mu_g, kk_ref, ka_ref, rk_ref,
                     db_ref, du_ref, ib_ref, iu_ref, gu_ref,
                     rp_out, y0_out, m_out, n_out, bonus_out, gate_out,
                     *, nctx_steps, n_steps, prec_t):
    L = CHUNK
    rb = r_c.shape[1]
    halves = [slice(h * L, (h + 1) * L) for h in range(rb // L)]
    s = pl.program_id(1)
    has_prev = jnp.logical_and(s != 0, s != nctx_steps)
    has_next = jnp.logical_and(s != nctx_steps - 1, s != n_steps - 1)

    def shifted(cur, prv, nxt, mu):
        x = cur[0]
        rows = lax.broadcasted_iota(jnp.int32, (rb, 1), 0)
        pr = jnp.where(has_prev, prv[0, SUBLANES - 1:SUBLANES, :], 0.0)
        nx = jnp.where(has_next, nxt[0, 0:1, :], 0.0)
        up = jnp.where(rows == 0, pr, pltpu.roll(x, 1, 0))
        dn = jnp.where(rows == rb - 1, nx, pltpu.roll(x, rb - 1, 0))
        return x + mu[...] * (0.5 * (up + dn) - x)

    r = shifted(r_c, r_p, r_n, mu_r)
    k = shifted(k_c, k_p, k_n, mu_k)
    v = shifted(v_c, v_p, v_n, mu_v)
    xw = shifted(w_c, w_p, w_n, mu_w)
    xa = shifted(a_c, a_p, a_n, mu_a)
    xg = shifted(g_c, g_p, g_n, mu_g)

    lane = lax.broadcasted_iota(jnp.int32, (1, LANES), 1)
    m0 = lane < HEAD_DIM

    def head_sum(x):
        s0 = jnp.sum(jnp.where(m0, x, 0.0), axis=-1, keepdims=True)
        s1 = jnp.sum(jnp.where(m0, 0.0, x), axis=-1, keepdims=True)
        return jnp.where(m0, s0, s1)

    def stack2(x):
        return jnp.concatenate([jnp.where(m0, x, 0.0), jnp.where(m0, 0.0, x)], axis=0)

    tw = _bf(jnp.tanh(xw))
    xab = _bf(xa)
    lw_all = [(-math.exp(-0.5)) * jax.nn.sigmoid(db_ref[d:d + 1, :] + _dot(tw, du_ref[d]))
              for d in range(2)]
    sa_all = [jax.nn.sigmoid(ib_ref[d:d + 1, :] + _dot(xab, iu_ref[d])) for d in range(2)]
    gate_out[0] = _dot(_bf(jax.nn.sigmoid(xg)), gu_ref[...])

    ri = lax.broadcasted_iota(jnp.int32, (L, L), 0)
    ci = lax.broadcasted_iota(jnp.int32, (L, L), 1)
    ri2 = lax.broadcasted_iota(jnp.int32, (2 * L, 2 * L), 0)
    ci2 = lax.broadcasted_iota(jnp.int32, (2 * L, 2 * L), 1)
    rt = jnp.bitwise_and(ri2, L - 1)
    ct = jnp.bitwise_and(ci2, L - 1)
    eye = ri2 == ci2
    n_sq = int(math.log2(L)) - 1

    tri = [_bf((ci <= ri).astype(F32)), _bf((ci >= ri).astype(F32))]
    strict = [ct < rt, ct > rt]
    incl = [ct <= rt, ct >= rt]
    lw_c = [[lw_all[d][hs] for hs in halves] for d in range(2)]

    def scan_sum(tmask, lw):
        hi = _bf(lw)
        lo = _bf(lw - hi.astype(F32))
        return _dot(tmask, hi) + _dot(tmask, lo)

    cum = [[scan_sum(tri[d], lw) for lw in lw_c[d]] for d in range(2)]
    cend = [[jnp.sum(lw, axis=0, keepdims=True) for lw in lw_c[d]] for d in range(2)]
    e_in = [[jnp.exp(c) for c in cum[d]] for d in range(2)]
    e_ex = [[jnp.exp(c - lw) for c, lw in zip(cum[d], lw_c[d])] for d in range(2)]
    e_neg = [[jnp.exp(-c) for c in cum[d]] for d in range(2)]
    e_end = [[jnp.exp(ce - c) for c, ce in zip(cum[d], cend[d])] for d in range(2)]
    p_end = [[jnp.exp(ce) for ce in cend[d]] for d in range(2)]
    kd_all = [k * (1.0 + (sa_all[d] - 1.0) * ka_ref[...]) for d in range(2)]

    npp = r.shape[1] // LANES
    chains = [(h, p, d) for h in range(len(halves)) for p in range(npp) for d in range(2)]
    sls = [slice(p * LANES, (p + 1) * LANES) for p in range(npp)]
    kk_p = []
    for p in range(npp):
        kx = k[:, sls[p]] * kk_ref[:, sls[p]]
        kk_p.append(kx * lax.rsqrt(head_sum(kx * kx) + KK_EPS))
        ksum = kd_all[0][:, sls[p]] + kd_all[1][:, sls[p]]
        bonus_out[0, :, sls[p]] = head_sum(r[:, sls[p]] * ksum * rk_ref[:, sls[p]]) * v[:, sls[p]]
    vs_c = {(h, p): stack2(v[halves[h], sls[p]]) for h in range(len(halves)) for p in range(npp)}
    vsb_c = {key: _bf(val) for key, val in vs_c.items()}

    a_s, r_s, bh_s, kh_s, g = [], [], [], [], []
    for h, p, d in chains:
        hs, sl = halves[h], sls[p]
        kd = kd_all[d][hs, sl]
        kk = kk_p[p][hs]
        bvec = kk * sa_all[d][hs, sl]
        a_s.append(stack2(-kk * e_ex[d][h][:, sl]))
        r_s.append(stack2(r[hs, sl] * e_in[d][h][:, sl]))
        b_s = stack2(bvec * e_neg[d][h][:, sl])
        k_s = stack2(kd * e_neg[d][h][:, sl])
        bh_s.append(stack2(bvec * e_end[d][h][:, sl]))
        kh_s.append(stack2(kd * e_end[d][h][:, sl]))
        g.append(_dot_nt(_bf(jnp.concatenate([a_s[-1], r_s[-1]], axis=0)),
                         _bf(jnp.concatenate([b_s, k_s], axis=0))))
    a_ab = [jnp.where(strict[d], g[i][:2 * L, :2 * L], 0.0) for i, (h, p, d) in enumerate(chains)]
    a_ak = [jnp.where(strict[d], g[i][:2 * L, 2 * L:], 0.0) for i, (h, p, d) in enumerate(chains)]
    a_rb = [jnp.where(incl[d], g[i][2 * L:, :2 * L], 0.0) for i, (h, p, d) in enumerate(chains)]
    a_rk = [jnp.where(incl[d], g[i][2 * L:, 2 * L:], 0.0) for i, (h, p, d) in enumerate(chains)]

    ident = jnp.where(eye, 1.0, 0.0)
    tmat = [ident + a for a in a_ab]
    apow = a_ab
    for _ in range(n_sq):
        if prec_t is None:
            apow = [_dot(_bf(a), _bf(a)) for a in apow]
            tmat = [t + _dot(_bf(a), _bf(t)) for a, t in zip(apow, tmat)]
        else:
            apow = [_dot(a, a, precision=prec_t) for a in apow]
            tmat = [t + _dot(a, t, precision=prec_t) for a, t in zip(apow, tmat)]

    x1 = [_dot(_bf(a_ak[i]), vsb_c[h, p]) for i, (h, p, d) in enumerate(chains)]
    y0v = [_dot(_bf(a_rk[i]), vsb_c[h, p]) for i, (h, p, d) in enumerate(chains)]
    tu = [_dot(_bf(tmat[i]), _bf(jnp.concatenate([a_s[i], x1[i]], axis=1)))
          for i in range(len(chains))]
    ry = [_dot(_bf(a_rb[i]), _bf(tu[i])) for i in range(len(chains))]
    def fold(z):
        h = z.shape[0] // 2
        return z[:h] + z[h:]

    for i, (h, p, d) in enumerate(chains):
        rp_out[d, h, 0, p] = _bf(fold(r_s[i] + ry[i][:, :LANES]))
        y0_out[d, h, 0, p] = fold(ry[i][:, LANES:] + y0v[i])
    for i, (h, p, d) in enumerate(chains):
        m_out[d, h, 0, p] = _bf(fold(jnp.where(eye, p_end[d][h][:, sls[p]], 0.0)
                                     + _dot_tn(_bf(tu[i][:, :LANES]), _bf(bh_s[i]))))
    for i, (h, p, d) in enumerate(chains):
        n_out[d, h, 0, p] = fold(_dot_tn(_bf(jnp.concatenate([tu[i][:, LANES:], vs_c[h, p]], axis=0)),
                                         _bf(jnp.concatenate([bh_s[i], kh_s[i]], axis=0))))


def _wkv_prep(p_all, lay, prm, ctx_len, prec_t, pp):
    b, ta, _ = p_all.shape
    L = CHUNK
    nc = ta // L
    nctx = ctx_len // L
    npair = lay["rd"] // LANES
    gl = lay["gl"]
    rows8 = ta // SUBLANES
    cps = 2 if nc % 2 == 0 and nctx % 2 == 0 else 1
    rbk = cps * L

    def cur(colfn, width=LANES):
        return pl.BlockSpec((1, rbk, width), lambda bb, c, p: (bb, c, colfn(p)))

    def prv(colfn, width=LANES):
        return pl.BlockSpec((1, SUBLANES, width),
                            lambda bb, c, p: (bb, jnp.maximum(c * (rbk // SUBLANES) - 1, 0), colfn(p)))

    def nxt(colfn, width=LANES):
        return pl.BlockSpec((1, SUBLANES, width),
                            lambda bb, c, p: (bb, jnp.minimum((c + 1) * (rbk // SUBLANES), rows8 - 1), colfn(p)))

    def trio(colfn, width=LANES):
        return [cur(colfn, width), prv(colfn, width), nxt(colfn, width)]

    pw = pp * LANES
    assert npair % pp == 0 and all(lay[n] % pw == 0 for n in ("r", "k", "v"))
    rb, kb, vb = lay["r"] // pw, lay["k"] // pw, lay["v"] // pw
    wb, ab = lay["xw"] // LANES, lay["xa"] // LANES
    assert lay["xg"] % gl == 0
    gb = lay["xg"] // gl

    def vec(width=pw):
        return pl.BlockSpec((1, width), lambda bb, c, p: (0, p))

    def vec0(width=LANES):
        return pl.BlockSpec((1, width), lambda bb, c, p: (0, 0))

    in_specs = (trio(lambda p: rb + p, pw) + trio(lambda p: kb + p, pw) + trio(lambda p: vb + p, pw)
                + trio(lambda p: wb) + trio(lambda p: ab) + trio(lambda p: gb, gl)
                + [vec(), vec(), vec(), vec0(), vec0(), vec0(gl),
                   vec(), vec(), vec(),
                   pl.BlockSpec((2, pw), lambda bb, c, p: (0, p)),
                   pl.BlockSpec((2, LANES, pw), lambda bb, c, p: (0, 0, p)),
                   pl.BlockSpec((2, pw), lambda bb, c, p: (0, p)),
                   pl.BlockSpec((2, LANES, pw), lambda bb, c, p: (0, 0, p)),
                   pl.BlockSpec((gl, pw), lambda bb, c, p: (0, p))])

    def opspec(rows):
        return pl.BlockSpec((2, cps, 1, pp, rows, LANES), lambda bb, c, p: (0, c, bb, p, 0, 0))

    out_specs = [opspec(L), opspec(L), opspec(HEAD_DIM), opspec(HEAD_DIM),
                 pl.BlockSpec((1, rbk, pw), lambda bb, c, p: (bb, c, p)),
                 pl.BlockSpec((1, rbk, pw), lambda bb, c, p: (bb, c, p))]
    out_shape = [jax.ShapeDtypeStruct((2, nc, b, npair, L, LANES), BF16),
                 jax.ShapeDtypeStruct((2, nc, b, npair, L, LANES), F32),
                 jax.ShapeDtypeStruct((2, nc, b, npair, HEAD_DIM, LANES), BF16),
                 jax.ShapeDtypeStruct((2, nc, b, npair, HEAD_DIM, LANES), F32),
                 jax.ShapeDtypeStruct((b, ta, lay["rd"]), F32),
                 jax.ShapeDtypeStruct((b, ta, lay["rd"]), F32)]
    kern = functools.partial(_wkv_prep_kernel, nctx_steps=nctx // cps, n_steps=nc // cps, prec_t=prec_t)
    args = [p_all] * 18 + [prm["mu_r"], prm["mu_k"], prm["mu_v"], prm["mu_w"], prm["mu_a"], prm["mu_g"],
                           prm["k_k"], prm["k_a"], prm["r_k"], prm["decay_bias"], prm["decay_up"],
                           prm["iclr_bias"], prm["iclr_up"], prm["gate_up"]]
    return pl.pallas_call(
        kern, out_shape=out_shape, grid=(b, nc // cps, npair // pp), in_specs=in_specs, out_specs=out_specs,
        compiler_params=_params("parallel", "parallel", "parallel"),
        name="wkv_chunk_operators",
    )(*args)


def _wkv_seq_kernel(rpf, y0f, mf, nf, rpr, y0r, mr, nr, yf_ref, yr_ref, s_ref, *, nb, npair):
    L = CHUNK

    @pl.when(pl.program_id(0) == 0)
    def _():
        s_ref[...] = jnp.zeros_like(s_ref)

    m0 = lax.broadcasted_iota(jnp.int32, (1, LANES), 1) < HEAD_DIM

    def unfold(z):
        zero = jnp.zeros_like(z)
        return jnp.concatenate([jnp.where(m0, z, zero), jnp.where(m0, zero, z)], axis=0)

    for d, (rp_ref, y0_ref, m_ref, n_ref, y_ref) in enumerate(((rpf, y0f, mf, nf, yf_ref),
                                                               (rpr, y0r, mr, nr, yr_ref))):
        for bb in range(nb):
            for p in range(npair):
                i = (d * nb + bb) * npair + p
                s0 = _bf(s_ref[i])
                ys = _dot_nt(unfold(rp_ref[0, 0, bb, p]), s0)
                y_ref[bb, :, p * LANES:(p + 1) * LANES] = ys[:L] + ys[L:] + y0_ref[0, 0, bb, p]
                s_ref[i] = _dot(s0, unfold(m_ref[0, 0, bb, p])) + unfold(n_ref[0, 0, bb, p])


def _wkv_sweep(rp, y0, mm, nn, ctx_len):
    _, nc, b, npair, _, _ = rp.shape
    L = CHUNK
    nctx = ctx_len // L

    def chunk_of(d, c):
        return c if d == 0 else jnp.where(c < nctx, nctx - 1 - c, nc - 1 - (c - nctx))

    def opspec(d, rows):
        return pl.BlockSpec((1, 1, b, npair, rows, LANES), lambda c: (d, chunk_of(d, c), 0, 0, 0, 0))

    def yspec(d):
        return pl.BlockSpec((b, L, npair * LANES), lambda c: (0, chunk_of(d, c), 0))

    rows = (L, L, HEAD_DIM, HEAD_DIM)
    y_shape = jax.ShapeDtypeStruct((b, nc * L, npair * LANES), F32)
    return pl.pallas_call(
        functools.partial(_wkv_seq_kernel, nb=b, npair=npair),
        out_shape=[y_shape, y_shape],
        grid=(nc,),
        in_specs=[opspec(d, r) for d in range(2) for r in rows],
        out_specs=[yspec(0), yspec(1)],
        scratch_shapes=[pltpu.VMEM((2 * b * npair, LANES, LANES), F32)],
        compiler_params=_params("arbitrary"),
        name="wkv_state_sweep",
    )(rp, y0, mm, nn, rp, y0, mm, nn)


def _attn_kernel(q_ref, kp_ref, kc_ref, kn_ref, vp_ref, vc_ref, vn_ref, kx_ref, vx_ref,
                 cq_ref, sq_ref, cp_ref, sp_ref, cn_ref, sn_ref, sink_ref, o_ref,
                 *, n_blocks, kv_heads, group):
    n = pl.program_id(1)
    qb = QBLOCK

    def rope(x, cos, sin):
        w = x.shape[1]
        reps = w // LANES
        cs = jnp.concatenate([cos] * reps, axis=1) if reps > 1 else cos
        sn = jnp.concatenate([sin] * reps, axis=1) if reps > 1 else sin
        lane = lax.broadcasted_iota(jnp.int32, (1, w), 1)
        sw = jnp.where(jnp.bitwise_and(lane, 16) == 0, pltpu.roll(x, w - 16, 1), pltpu.roll(x, 16, 1))
        return x * cs + sw * sn

    q = rope(q_ref[0], cq_ref[...], sq_ref[...]) * (HEAD_DIM ** -0.5)
    kwin = jnp.concatenate([rope(kp_ref[0], cp_ref[...], sp_ref[...]),
                            rope(kc_ref[0], cq_ref[...], sq_ref[...]),
                            rope(kn_ref[0], cn_ref[...], sn_ref[...]),
                            kx_ref[0]], axis=0)
    vwin = jnp.concatenate([vp_ref[0], vc_ref[0], vn_ref[0], vx_ref[0]], axis=0)
    nkeys = kwin.shape[0]

    rows = group * qb
    qi = jnp.bitwise_and(lax.broadcasted_iota(jnp.int32, (rows, nkeys), 0), qb - 1)
    s_i = lax.broadcasted_iota(jnp.int32, (rows, nkeys), 1)
    blk = s_i // qb + (n - 1)
    in_win = (s_i >= qi) & (s_i <= qi + 2 * WINDOW) & (blk >= 0) & (blk < n_blocks)
    valid = in_win | (s_i >= 3 * qb)
    rgrp = lax.broadcasted_iota(jnp.int32, (rows, 1), 0) // qb

    scores, sinks = [], []
    for kh in range(kv_heads):
        kk_ = _bf(kwin[:, kh * HEAD_DIM:(kh + 1) * HEAD_DIM])
        qs = jnp.concatenate([q[:, (kh * group + g) * HEAD_DIM:(kh * group + g + 1) * HEAD_DIM]
                              for g in range(group)], axis=0)
        sk = jnp.zeros((rows, 1), F32)
        for g in range(group):
            h = kh * group + g
            sk = jnp.where(rgrp == g, sink_ref[0:1, h:h + 1], sk)
        sinks.append(sk)
        scores.append(jnp.where(valid, _dot_nt(_bf(qs), kk_), NEG_BIG))
    probs, denoms = [], []
    for s, sk in zip(scores, sinks):
        m = jnp.maximum(jnp.max(s, axis=-1, keepdims=True), sk)
        e = jnp.exp(s - m)
        denoms.append(jnp.sum(e, axis=-1, keepdims=True) + jnp.exp(sk - m))
        probs.append(_bf(e))
    pieces = []
    for kh in range(kv_heads):
        vv_ = _bf(vwin[:, kh * HEAD_DIM:(kh + 1) * HEAD_DIM])
        o = _dot(probs[kh], vv_) / denoms[kh]
        for g in range(group):
            pieces.append(o[g * qb:(g + 1) * qb])
    o_ref[0] = _bf(jnp.concatenate(pieces, axis=1))


def _attention(p_all, lay, cos_t, sin_t, sink, ctx_len, seq):
    b = p_all.shape[0]
    qb = QBLOCK
    nb = seq // qb
    cb = ctx_len // qb
    ad, kvd = lay["ad"], lay["kvd"]
    kv_heads = kvd // HEAD_DIM
    group = (ad // HEAD_DIM) // kv_heads
    assert lay["q"] % ad == 0 and lay["ka"] % kvd == 0 and lay["va"] % kvd == 0
    qc, kc, vc = lay["q"] // ad, lay["ka"] // kvd, lay["va"] // kvd

    def clampb(n, off):
        return jnp.clip(n + off, 0, nb - 1)

    def win(col, off):
        return pl.BlockSpec((1, qb, kvd), lambda bb, n: (bb, cb + clampb(n, off), col))

    def tab(off):
        return pl.BlockSpec((qb, LANES), lambda bb, n: (clampb(n, off), 0))

    assert ctx_len % qb == 0
    ctxspec = lambda col: pl.BlockSpec((1, ctx_len, kvd), lambda bb, n: (bb, 0, col))
    in_specs = [pl.BlockSpec((1, qb, ad), lambda bb, n: (bb, cb + n, qc)),
                win(kc, -1), win(kc, 0), win(kc, 1), win(vc, -1), win(vc, 0), win(vc, 1),
                ctxspec(kc), ctxspec(vc),
                tab(0), tab(0), tab(-1), tab(-1), tab(1), tab(1),
                pl.BlockSpec((1, sink.shape[1]), lambda bb, n: (0, 0))]
    return pl.pallas_call(
        functools.partial(_attn_kernel, n_blocks=nb, kv_heads=kv_heads, group=group),
        out_shape=jax.ShapeDtypeStruct((b, seq, ad), BF16),
        grid=(b, nb),
        in_specs=in_specs,
        out_specs=pl.BlockSpec((1, qb, ad), lambda bb, n: (bb, n, 0)),
        compiler_params=_params("parallel", "parallel"),
        name="window_ctx_attention",
    )(p_all, p_all, p_all, p_all, p_all, p_all, p_all, p_all, p_all,
      cos_t, sin_t, cos_t, sin_t, cos_t, sin_t, sink)


def _outproj_kernel(yf_ref, yr_ref, bon_ref, gat_ref, att_ref, x_ref, lg_ref, lb_ref, w_ref,
                    m_ref, g2_ref, wr_ref, x1_ref, h2_ref, lt_ref):
    ysum = yf_ref[0] + yr_ref[0]
    ri = lax.broadcasted_iota(jnp.int32, (LANES, LANES), 0) // HEAD_DIM
    ci = lax.broadcasted_iota(jnp.int32, (LANES, LANES), 1) // HEAD_DIM
    hm = _bf(jnp.where(ri == ci, 1.0 / HEAD_DIM, 0.0))

    def head_mean(z):
        hi = _bf(z)
        lo = _bf(z - hi.astype(F32))
        return _dot(hi, hm) + _dot(lo, hm)

    ybs = [ysum[:, j * LANES:(j + 1) * LANES] for j in range(ysum.shape[1] // LANES)]
    devs = [yb - mu for yb, mu in zip(ybs, [head_mean(yb) for yb in ybs])]
    blocks = [dev * lax.rsqrt(var + LNX_EPS)
              for dev, var in zip(devs, [head_mean(dev * dev) for dev in devs])]
    yn = jnp.concatenate(blocks, axis=1) if len(blocks) > 1 else blocks[0]
    rw = (yn * lg_ref[...] + lb_ref[...] + bon_ref[0]) * gat_ref[0]
    mix = jnp.concatenate([_bf(rw), att_ref[0]], axis=1)
    m = m_ref[0]
    x1 = x_ref[0] + m[0:1] * _dot(mix, w_ref[...])
    x1_ref[0] = x1
    ms = jnp.mean(x1 * x1, axis=-1, keepdims=True)
    h2 = x1 * lax.rsqrt(ms + NORM_EPS) * g2_ref[...] * (1.0 + m[2:3]) + m[1:2]
    h2_ref[0] = h2
    hi = _bf(h2)
    lo = _bf(h2 - hi.astype(F32))
    wr = wr_ref[...]
    wr_hi = _bf(wr)
    wr_lo = _bf(wr - wr_hi.astype(F32))
    lg = _dot(hi, wr_hi) + (_dot(hi, wr_lo) + _dot(lo, wr_hi))
    lt_ref[...] = lg.T[:lt_ref.shape[0]]


def _out_projection(y_f, y_r, bonus, gate, attn, x, lnx_g, lnx_b, w_out_b, m2, g2, w_router, ctx_len, tr):
    b, t, d = x.shape
    rd = y_f.shape[2]
    ad = attn.shape[2]
    ne = w_router.shape[1]
    assert ne <= LANES
    wr = _pad_to(w_router, 1, LANES)
    co = ctx_len // tr
    nt = t // tr
    return pl.pallas_call(
        _outproj_kernel,
        out_shape=[jax.ShapeDtypeStruct((b, t, d), F32),
                   jax.ShapeDtypeStruct((b, t, d), F32),
                   jax.ShapeDtypeStruct((ne, b * t), F32)],
        grid=(b, nt),
        in_specs=[pl.BlockSpec((1, tr, rd), lambda bb, i: (bb, co + i, 0)),
                  pl.BlockSpec((1, tr, rd), lambda bb, i: (bb, co + i, 0)),
                  pl.BlockSpec((1, tr, rd), lambda bb, i: (bb, co + i, 0)),
                  pl.BlockSpec((1, tr, rd), lambda bb, i: (bb, co + i, 0)),
                  pl.BlockSpec((1, tr, ad), lambda bb, i: (bb, i, 0)),
                  pl.BlockSpec((1, tr, d), lambda bb, i: (bb, i, 0)),
                  pl.BlockSpec((1, rd), lambda bb, i: (0, 0)),
                  pl.BlockSpec((1, rd), lambda bb, i: (0, 0)),
                  pl.BlockSpec((rd + ad, d), lambda bb, i: (0, 0)),
                  pl.BlockSpec((1, 4, d), lambda bb, i: (bb, 0, 0)),
                  pl.BlockSpec((1, d), lambda bb, i: (0, 0)),
                  pl.BlockSpec((d, LANES), lambda bb, i: (0, 0))],
        out_specs=[pl.BlockSpec((1, tr, d), lambda bb, i: (bb, i, 0)),
                   pl.BlockSpec((1, tr, d), lambda bb, i: (bb, i, 0)),
                   pl.BlockSpec((ne, tr), lambda bb, i: (0, bb * nt + i))],
        compiler_params=_params("parallel", "parallel"),
        name="rwkv_out_outproj_norm2_router",
    )(y_f, y_r, bonus, gate, attn, x, lnx_g, lnx_b, w_out_b, m2, g2, wr)


def _router_kernel(l_ref, b_ref, idx_ref, gate_ref, rank_ref, cnt_ref, carry_ref):
    @pl.when(pl.program_id(0) == 0)
    def _():
        carry_ref[...] = jnp.zeros_like(carry_ref)

    sc = jax.nn.sigmoid(l_ref[...])
    ne, tn = sc.shape
    eg = ne // N_GROUPS
    bz = sc + b_ref[...]
    g3 = bz.reshape(N_GROUPS, eg, tn)
    io = lax.broadcasted_iota(jnp.int32, (N_GROUPS, eg, tn), 1).astype(F32)
    m1 = jnp.max(g3, axis=1, keepdims=True)
    i1 = jnp.min(jnp.where(g3 == m1, io, float(eg)), axis=1, keepdims=True)
    m2 = jnp.max(jnp.where(io == i1, -jnp.inf, g3), axis=1, keepdims=True)
    gs = (m1 + m2).reshape(N_GROUPS, tn)

    def topk_hits(vals, k):
        n = vals.shape[0]
        idx = lax.broadcasted_iota(jnp.int32, vals.shape, 0).astype(F32)
        hits = []
        cur = vals
        for _ in range(k):
            mx = jnp.max(cur, axis=0, keepdims=True)
            ix = jnp.min(jnp.where(cur == mx, idx, float(n)), axis=0, keepdims=True)
            hit = idx == ix
            hits.append(hit)
            cur = jnp.where(hit, -jnp.inf, cur)
        return hits

    gsel = sum(jnp.where(h, 1.0, 0.0) for h in topk_hits(gs, TOPK_GROUPS))
    masked = jnp.where(gsel.reshape(N_GROUPS, 1, tn) > 0.0, g3, -jnp.inf).reshape(ne, tn)
    hits = topk_hits(masked, TOP_K)
    chosen = sum(jnp.where(h, 1.0, 0.0) for h in hits)
    denom = jnp.sum(chosen * sc, axis=0, keepdims=True)

    si = lax.broadcasted_iota(jnp.int32, (tn, tn), 0)
    ti = lax.broadcasted_iota(jnp.int32, (tn, tn), 1)
    before = _bf(jnp.where(si < ti, 1.0, 0.0))
    rank_full = _dot(_bf(chosen), before) + carry_ref[...]
    carry_ref[...] += jnp.sum(chosen, axis=1, keepdims=True)
    cnt_ref[...] = carry_ref[...]

    eidx = lax.broadcasted_iota(jnp.int32, (ne, tn), 0).astype(F32)
    for j, hit in enumerate(hits):
        idx_ref[j:j + 1, :] = jnp.sum(jnp.where(hit, eidx, 0.0), axis=0, keepdims=True).astype(jnp.int32)
        gate_ref[j:j + 1, :] = (jnp.sum(jnp.where(hit, sc, 0.0), axis=0, keepdims=True)
                                / denom * ROUTED_SCALE)
        rank_ref[j:j + 1, :] = jnp.sum(jnp.where(hit, rank_full, 0.0), axis=0,
                                       keepdims=True).astype(jnp.int32)


def _router(logits_t, bias_col):
    ne, n = logits_t.shape
    tn = 1024 if n % 1024 == 0 else n
    choice = lambda: pl.BlockSpec((TOP_K, tn), lambda i: (0, i))
    return pl.pallas_call(
        _router_kernel,
        out_shape=[jax.ShapeDtypeStruct((TOP_K, n), jnp.int32),
                   jax.ShapeDtypeStruct((TOP_K, n), F32),
                   jax.ShapeDtypeStruct((TOP_K, n), jnp.int32),
                   jax.ShapeDtypeStruct((ne, 1), F32)],
        grid=(n // tn,),
        in_specs=[pl.BlockSpec((ne, tn), lambda i: (0, i)),
                  pl.BlockSpec((ne, 1), lambda i: (0, 0))],
        out_specs=[choice(), choice(), choice(), pl.BlockSpec((ne, 1), lambda i: (0, 0))],
        scratch_shapes=[pltpu.VMEM((ne, 1), F32)],
        compiler_params=_params("arbitrary"),
        name="group_topk_router",
    )(logits_t, bias_col)


def _slots_kernel(idx_ref, rank_ref, ps_ref, o_ref):
    ne, tn = ps_ref.shape[0], idx_ref.shape[1]
    eidx = lax.broadcasted_iota(jnp.int32, (ne, tn), 0)
    for j in range(TOP_K):
        start = jnp.sum(jnp.where(eidx == idx_ref[j:j + 1, :], ps_ref[...], 0.0), axis=0, keepdims=True)
        o_ref[j:j + 1, :] = start.astype(jnp.int32) + rank_ref[j:j + 1, :]


def _slot_positions(idx8, rank8, pstart_col):
    _, n = idx8.shape
    ne = pstart_col.shape[0]
    tn = 1024 if n % 1024 == 0 else n
    choice = lambda: pl.BlockSpec((TOP_K, tn), lambda i: (0, i))
    return pl.pallas_call(
        _slots_kernel,
        out_shape=jax.ShapeDtypeStruct((TOP_K, n), jnp.int32),
        grid=(n // tn,),
        in_specs=[choice(), choice(), pl.BlockSpec((ne, 1), lambda i: (0, 0))],
        out_specs=choice(),
        compiler_params=_params("parallel"),
        name="expert_slot_positions",
    )(idx8, rank8, pstart_col)


def _dispatch_kernel(pad_start_ref, npad_ref, nused_ref, pos_ref, h_ref, xs_ref, zero_ref, sems,
                     *, n_experts):
    tt = h_ref.shape[0]
    bm = zero_ref.shape[0]

    sem = sems.at[0]

    def issue(t, carry):
        for k in range(TOP_K):
            pltpu.make_async_copy(h_ref.at[pl.ds(t, 1)], xs_ref.at[pl.ds(pos_ref[k, t], 1)],
                                  sems.at[k]).start()
        return carry

    lax.fori_loop(0, tt, issue, 0, unroll=4)
    for k in range(TOP_K):
        pltpu.make_async_copy(h_ref, xs_ref.at[pl.ds(0, tt)], sems.at[k]).wait()

    @pl.when(pl.program_id(0) == pl.num_programs(0) - 1)
    def _():
        zero_ref[...] = jnp.zeros_like(zero_ref)

        zero_row_src = zero_ref.at[pl.ds(0, 1)]

        def per_expert(e, carry):
            def zero_row(i, c2):
                pltpu.make_async_copy(zero_row_src, xs_ref.at[pl.ds(pad_start_ref[e] + i, 1)], sem).start()
                return c2

            def zero_wait(i, c2):
                pltpu.make_async_copy(zero_row_src, xs_ref.at[pl.ds(0, 1)], sem).wait()
                return c2

            lax.fori_loop(0, npad_ref[e], zero_row, 0)
            lax.fori_loop(0, npad_ref[e], zero_wait, 0)
            return carry

        lax.fori_loop(0, n_experts, per_expert, 0)

        def zero_block(j, carry):
            blk = pltpu.make_async_copy(zero_ref, xs_ref.at[pl.ds(j * bm, bm)], sem)
            blk.start()
            blk.wait()
            return carry

        lax.fori_loop(nused_ref[0], xs_ref.shape[0] // bm, zero_block, 0)


def _dispatch(h2, pos8, pad_start, npad, n_used, n_slots, tt, bm):
    n, d = h2.shape
    ne = pad_start.shape[0]
    grid_spec = pltpu.PrefetchScalarGridSpec(
        num_scalar_prefetch=3,
        grid=(n // tt,),
        in_specs=[pl.BlockSpec((TOP_K, tt), lambda i, ps, npd, nu: (0, i), memory_space=pltpu.SMEM),
                  pl.BlockSpec((tt, d), lambda i, ps, npd, nu: (i, 0))],
        out_specs=pl.BlockSpec(memory_space=pl.ANY),
        scratch_shapes=[pltpu.VMEM((bm, d), h2.dtype), pltpu.SemaphoreType.DMA((TOP_K,))])
    return pl.pallas_call(
        functools.partial(_dispatch_kernel, n_experts=ne),
        out_shape=jax.ShapeDtypeStruct((n_slots, d), h2.dtype),
        grid_spec=grid_spec,
        compiler_params=_params("arbitrary"),
        name="expert_dispatch",
    )(pad_start, npad, n_used, pos8, h2)


def _expert_ffn_kernel(be_ref, nu_ref, nx_ref, sl_ref, x_ref, wg_hbm, wu_hbm, wd_hbm, y_ref,
                       wgf, wuf, wdf, wgb, wub, wdb, sems):
    j = pl.program_id(0)
    e = be_ref[j]
    used = j < nu_ref[0]
    fresh = jnp.logical_or(j == 0, e != be_ref[jnp.maximum(j - 1, 0)])

    def weight_copies(expert, slot):
        return [pltpu.make_async_copy(src.at[expert], dst.at[slot], sems.at[slot])
                for src, dst in ((wg_hbm, wgf), (wu_hbm, wuf), (wd_hbm, wdf))]

    @pl.when(j == 0)
    def _():
        for cp in weight_copies(e, sl_ref[e]):
            cp.start()

    @pl.when(jnp.logical_and(used, fresh))
    def _():
        slot = sl_ref[e]
        for cp in weight_copies(e, slot):
            cp.wait()
        nxt = nx_ref[e]

        @pl.when(nxt >= 0)
        def _():
            for cp in weight_copies(nxt, 1 - slot):
                cp.start()

        wgb[...] = _bf(wgf[slot])
        wub[...] = _bf(wuf[slot])
        wdb[...] = _bf(wdf[slot])

    @pl.when(used)
    def _():
        x = _bf(x_ref[...])
        a = _dot(x, wgb[...])
        u = _dot(x, wub[...])
        y_ref[...] = _dot(_bf(a * jax.nn.sigmoid(a) * u), wdb[...])

    @pl.when(jnp.logical_not(used))
    def _():
        y_ref[...] = jnp.zeros_like(y_ref)


def _expert_ffn(xs, blk_expert, n_used, next_expert, expert_slot, e_gate, e_up, e_down, bm):
    n_slots, dw = xs.shape
    ne, d, ff = e_gate.shape
    assert dw == d
    grid_spec = pltpu.PrefetchScalarGridSpec(
        num_scalar_prefetch=4,
        grid=(n_slots // bm,),
        in_specs=[pl.BlockSpec((bm, dw), lambda j, be, nu, nx, sl: (jnp.minimum(j, nu[0] - 1), 0)),
                  pl.BlockSpec(memory_space=pl.ANY),
                  pl.BlockSpec(memory_space=pl.ANY),
                  pl.BlockSpec(memory_space=pl.ANY)],
        out_specs=pl.BlockSpec((bm, dw), lambda j, be, nu, nx, sl: (j, 0)),
        scratch_shapes=[pltpu.VMEM((2, d, ff), F32), pltpu.VMEM((2, d, ff), F32), pltpu.VMEM((2, ff, d), F32),
                        pltpu.VMEM((d, ff), BF16), pltpu.VMEM((d, ff), BF16), pltpu.VMEM((ff, d), BF16),
                        pltpu.SemaphoreType.DMA((2,))])
    return pl.pallas_call(
        _expert_ffn_kernel,
        out_shape=jax.ShapeDtypeStruct((n_slots, dw), xs.dtype),
        grid_spec=grid_spec,
        compiler_params=_params("arbitrary"),
        name="routed_expert_ffn",
    )(blk_expert, n_used, next_expert, expert_slot, xs, e_gate, e_up, e_down)


def _final_kernel(pos_ref, h_ref, x1_ref, g_ref, ys_ref, wg_ref, wu_ref, wd_ref, m_ref, fg_ref,
                  o_ref, buf, sems):
    tr = h_ref.shape[1]

    def issue(t, carry):
        for k in range(TOP_K):
            pltpu.make_async_copy(ys_ref.at[pl.ds(pos_ref[k, t], 1)], buf.at[k, pl.ds(t, 1)],
                                  sems.at[k]).start()
        return carry

    lax.fori_loop(0, tr, issue, 0, unroll=4)
    h = _bf(h_ref[0])
    a = _dot(h, wg_ref[...])
    u = _dot(h, wu_ref[...])
    moe = _dot(_bf(a * jax.nn.sigmoid(a) * u), wd_ref[...])
    g = g_ref[...]
    for k in range(TOP_K):
        pltpu.make_async_copy(ys_ref.at[pl.ds(0, tr)], buf.at[k], sems.at[k]).wait()
    for k in range(TOP_K):
        moe = moe + g[:, k:k + 1] * buf[k]
    x2 = x1_ref[0] + m_ref[0, 3:4, :] * moe
    ms = jnp.mean(x2 * x2, axis=-1, keepdims=True)
    o_ref[0] = x2 * lax.rsqrt(ms + NORM_EPS) * fg_ref[...]


def _final(h2, x1, pos8, gate_rows, ys, s_gate_b, s_up_b, s_down_b, m2, final_g, tr):
    b, t, d = x1.shape
    ff = s_gate_b.shape[1]
    nt = t // tr
    dw = h2.shape[2]
    row = lambda width=d: pl.BlockSpec((1, tr, width), lambda bb, i: (bb, i, 0))
    return pl.pallas_call(
        _final_kernel,
        out_shape=jax.ShapeDtypeStruct((b, t, d), F32),
        grid=(b, nt),
        in_specs=[pl.BlockSpec((TOP_K, tr), lambda bb, i: (0, bb * nt + i), memory_space=pltpu.SMEM),
                  row(dw), row(),
                  pl.BlockSpec((tr, TOP_K), lambda bb, i: (bb * nt + i, 0)),
                  pl.BlockSpec(memory_space=pl.ANY),
                  pl.BlockSpec((d, ff), lambda bb, i: (0, 0)),
                  pl.BlockSpec((d, ff), lambda bb, i: (0, 0)),
                  pl.BlockSpec((ff, d), lambda bb, i: (0, 0)),
                  pl.BlockSpec((1, 4, d), lambda bb, i: (bb, 0, 0)),
                  pl.BlockSpec((1, d), lambda bb, i: (0, 0))],
        out_specs=row(),
        scratch_shapes=[pltpu.VMEM((TOP_K, tr, dw), ys.dtype), pltpu.SemaphoreType.DMA((TOP_K,))],
        compiler_params=_params("parallel", "parallel"),
        name="combine_shared_expert_final_norm",
    )(pos8, h2, x1, gate_rows, ys, s_gate_b, s_up_b, s_down_b, m2, final_g)


def _pad_to(a, axis, n):
    pad = [(0, 0)] * a.ndim
    pad[axis] = (0, n - a.shape[axis])
    return jnp.pad(a, pad)


def _layout(rd, wl, al, gl, ad, kvd):
    assert wl <= LANES and al <= LANES
    lay = {"rd": rd, "gl": gl, "ad": ad, "kvd": kvd}
    off = 0
    for name, width in (("q", ad), ("r", rd), ("k", rd), ("v", rd), ("ka", kvd), ("va", kvd),
                        ("xw", LANES), ("xa", LANES), ("xg", gl)):
        assert width % LANES == 0
        lay[name] = off
        off += width
    lay["ncols"] = off
    return lay


def kernel(x, c, ctx, c_ctx, w_mod, b_mod, norm1_g, norm2_g, w_in, shift_mu, k_k, k_a, r_k, decay_bias, decay_up, iclr_bias, iclr_up, gate_up, lnx_g, lnx_b, attn_sink, w_out, w_router, expert_bias, e_gate, e_up, e_down, s_gate, s_up, s_down, final_g):
    assert w_mod.shape[0] == 1, "single-layer block only"
    b, t, d = x.shape
    ctx_len = ctx.shape[1]
    rh, hd = r_k.shape[1:]
    assert hd == HEAD_DIM and b + 1 <= SUBLANES
    rd = rh * hd
    wl, al, gl = decay_up.shape[2], iclr_up.shape[2], gate_up.shape[1]
    ad = attn_sink.shape[1] * hd
    kvd = (w_in.shape[2] - (3 * rd + wl + al + gl) - ad) // 2
    ne = w_router.shape[2]
    lay = _layout(rd, wl, al, gl, ad, kvd)
    tr = 256 if ctx_len % 256 == 0 and t % 256 == 0 else CHUNK
    assert ctx_len % CHUNK == 0 and t % CHUNK == 0 and t % QBLOCK == 0 and WINDOW == QBLOCK

    c8 = _pad_to(jnp.concatenate([c, c_ctx[None]], axis=0), 0, SUBLANES)
    mod = _modulation(c8, w_mod[0], b_mod[0][None])
    mod6 = mod.reshape(SUBLANES, 6, d)
    mx, mc = mod6[:b], mod6[b]
    m1 = jnp.stack([jnp.broadcast_to(mc[None, 0:2], (b, 2, d)), mx[:, 0:2]], axis=1)
    m2 = jnp.stack([mx[:, 2], mx[:, 3], mx[:, 4], mx[:, 5]], axis=1)

    wi = w_in[0]
    o_w, o_a, o_g, o_q = 3 * rd, 3 * rd + wl, 3 * rd + wl + al, 3 * rd + wl + al + gl
    w_in_p = _bf(jnp.concatenate([
        wi[:, o_q:o_q + ad], wi[:, :3 * rd], wi[:, o_q + ad:],
        _pad_to(wi[:, o_w:o_a], 1, LANES), _pad_to(wi[:, o_a:o_g], 1, LANES), wi[:, o_g:o_q]], axis=1))
    mu = shift_mu[0]
    prm = {
        "mu_r": mu[None, :rd], "mu_k": mu[None, rd:2 * rd], "mu_v": mu[None, 2 * rd:3 * rd],
        "mu_w": _pad_to(mu[None, o_w:o_a], 1, LANES), "mu_a": _pad_to(mu[None, o_a:o_g], 1, LANES),
        "mu_g": mu[None, o_g:o_q],
        "k_k": k_k, "k_a": k_a, "r_k": r_k.reshape(1, rd),
        "decay_bias": decay_bias[0], "decay_up": _bf(_pad_to(decay_up[0], 1, LANES)),
        "iclr_bias": iclr_bias[0], "iclr_up": _bf(_pad_to(iclr_up[0], 1, LANES)),
        "gate_up": _bf(gate_up[0]),
    }

    p_all = _in_projection(ctx, x, norm1_g, m1, w_in_p, tr)

    npair = rd // LANES
    pp = max(n for n in (1, 2, 4, 8) if npair % n == 0
             and all(lay[key] % (n * LANES) == 0 for key in ("r", "k", "v")))
    rp, y0, mm, nn, bonus, gate = _wkv_prep(p_all, lay, prm, ctx_len, None, pp)
    y_f, y_r = _wkv_sweep(rp, y0, mm, nn, ctx_len)

    nf = HEAD_DIM // 4
    freqs = ROPE_BASE ** (-jnp.arange(nf, dtype=F32) / nf)
    pos = jnp.arange(t, dtype=jnp.int32)
    ang_r = (pos // GRID_W).astype(F32)[:, None] * freqs
    ang_c = (pos % GRID_W).astype(F32)[:, None] * freqs
    cos_h = jnp.concatenate([jnp.cos(ang_r)] * 2 + [jnp.cos(ang_c)] * 2, axis=1)
    sin_h = jnp.concatenate([-jnp.sin(ang_r), jnp.sin(ang_r), -jnp.sin(ang_c), jnp.sin(ang_c)], axis=1)
    cos_t = jnp.concatenate([cos_h, cos_h], axis=1)
    sin_t = jnp.concatenate([sin_h, sin_h], axis=1)
    attn = _attention(p_all, lay, cos_t, sin_t, attn_sink, ctx_len, t)

    x1, h2, logits_t = _out_projection(y_f, y_r, bonus, gate, attn, x, lnx_g, lnx_b, _bf(w_out[0]), m2,
                                       norm2_g, w_router[0], ctx_len, tr)
    idx8, gate8, rank8, counts = _router(logits_t, expert_bias[0][:, None])

    n_tok = b * t
    bm = EXPERT_ROWS
    assert (n_tok * TOP_K) % bm == 0
    n_blk = n_tok * TOP_K // bm + ne
    cnt = counts[:, 0].astype(jnp.int32)
    padded = (cnt + bm - 1) // bm * bm
    pend = jnp.cumsum(padded)
    pstart = pend - padded
    pos8 = _slot_positions(idx8, rank8, pstart.astype(F32)[:, None])
    blk_row0 = jnp.arange(n_blk, dtype=jnp.int32) * bm
    blk_expert = jnp.minimum(jnp.sum((pend[None, :] <= blk_row0[:, None]).astype(jnp.int32), axis=1), ne - 1)
    n_used = (pend[-1:] // bm).astype(jnp.int32)

    xs = _dispatch(h2.reshape(n_tok, d), pos8, pstart + cnt, padded - cnt, n_used, n_blk * bm, tr, bm)
    has_rows = cnt > 0
    eids = jnp.arange(ne, dtype=jnp.int32)
    later = jnp.where(has_rows[None, :] & (eids[None, :] > eids[:, None]), eids[None, :], ne)
    next_expert = jnp.min(later, axis=1)
    next_expert = jnp.where(next_expert < ne, next_expert, -1).astype(jnp.int32)
    expert_slot = ((jnp.cumsum(has_rows.astype(jnp.int32)) - 1) % 2).astype(jnp.int32)
    ys = _expert_ffn(xs, blk_expert, n_used, next_expert, expert_slot, e_gate[0], e_up[0], e_down[0], bm)
    return _final(h2, x1, pos8, gate8.T, ys, _bf(s_gate[0]), _bf(s_up[0]), _bf(s_down[0]),
                  m2, final_g[None], tr)
```

```python
import functools
import math

import jax
import jax.numpy as jnp
from jax import lax
from jax.experimental import pallas as pl
from jax.experimental.pallas import tpu as pltpu

GRID_W = 64
WINDOW = 128
ROPE_BASE = 10000.0
N_GROUPS = 8
TOPK_GROUPS = 4
TOP_K = 8
ROUTED_SCALE = 2.5
NORM_EPS = 1e-6
LNX_EPS = 64e-5
KK_EPS = 1e-12

LANES = 128
SUBLANES = 8
VMEM_LIMIT_BYTES = 56 * 1024 * 1024

HEAD_DIM = 64
CHUNK = 64
QBLOCK = 128
EXPERT_ROWS = 256

F32 = jnp.float32
BF16 = jnp.bfloat16
NEG_BIG = -1e30


def _bf(x):
    return x.astype(BF16)


def _dot(a, b, precision=None):
    return jnp.dot(a, b, preferred_element_type=F32, precision=precision)


def _dot_nt(a, b, precision=None):
    return lax.dot_general(a, b, (((1,), (1,)), ((), ())), preferred_element_type=F32,
                           precision=precision)


def _dot_tn(a, b):
    return lax.dot_general(a, b, (((0,), (0,)), ((), ())), preferred_element_type=F32)


def _params(*sem):
    return pltpu.CompilerParams(dimension_semantics=sem, vmem_limit_bytes=VMEM_LIMIT_BYTES)


def _mod_kernel(c_ref, w_ref, b_ref, o_ref):
    c = c_ref[...]
    s = c * jax.nn.sigmoid(c)
    hi = _bf(s)
    lo = _bf(s - hi.astype(F32))
    r = _dot(jnp.concatenate([hi, lo], axis=0), _bf(w_ref[...]))
    o_ref[...] = r[:SUBLANES] + r[SUBLANES:] + b_ref[...]


def _modulation(c8, w_mod, b_mod):
    d, n = w_mod.shape
    tn = 1024 if n % 1024 == 0 else n
    return pl.pallas_call(
        _mod_kernel,
        out_shape=jax.ShapeDtypeStruct((SUBLANES, n), F32),
        grid=(n // tn,),
        in_specs=[pl.BlockSpec((SUBLANES, d), lambda j: (0, 0)),
                  pl.BlockSpec((d, tn), lambda j: (0, j)),
                  pl.BlockSpec((1, tn), lambda j: (0, j))],
        out_specs=pl.BlockSpec((SUBLANES, tn), lambda j: (0, j)),
        compiler_params=_params("parallel"),
        name="adaln_modulation",
    )(c8, w_mod, b_mod)


def _inproj_kernel(c_ref, x_ref, g_ref, m_ref, w_ref, o_ref, *, nct):
    x = jnp.where(pl.program_id(2) < nct, c_ref[0], x_ref[0])
    ms = jnp.mean(x * x, axis=-1, keepdims=True)
    y = x * lax.rsqrt(ms + NORM_EPS) * g_ref[...]
    m = m_ref[0, 0]
    h = y * (1.0 + m[1:2]) + m[0:1]
    o_ref[0] = _dot(_bf(h), w_ref[...])


def _in_projection(ctx, x, g1, m1, w_in_p, tr):
    b, t, d = x.shape
    ctx_len = ctx.shape[1]
    ta = ctx_len + t
    ncols = w_in_p.shape[1]
    tn = ncols
    nct = ctx_len // tr
    return pl.pallas_call(
        functools.partial(_inproj_kernel, nct=nct),
        out_shape=jax.ShapeDtypeStruct((b, ta, ncols), F32),
        grid=(ncols // tn, b, ta // tr),
        in_specs=[pl.BlockSpec((1, tr, d), lambda n, bb, i: (bb, jnp.minimum(i, nct - 1), 0)),
                  pl.BlockSpec((1, tr, d), lambda n, bb, i: (bb, jnp.maximum(i - nct, 0), 0)),
                  pl.BlockSpec((1, d), lambda n, bb, i: (0, 0)),
                  pl.BlockSpec((1, 1, 2, d), lambda n, bb, i: (bb, jnp.where(i >= nct, 1, 0), 0, 0)),
                  pl.BlockSpec((d, tn), lambda n, bb, i: (0, n), pipeline_mode=pl.Buffered(1))],
        out_specs=pl.BlockSpec((1, tr, tn), lambda n, bb, i: (bb, i, n)),
        compiler_params=_params("parallel", "parallel", "parallel"),
        name="norm_modulate_inproj",
    )(ctx, x, g1, m1, w_in_p)


def _wkv_prep_kernel(r_c, r_p, r_n, k_c, k_p, k_n, v_c, v_p, v_n, w_c, w_p, w_n, a_c, a_p, a_n,
                     g_c, g_p, g_n, mu_r, mu_k, mu_v, mu_w, mu_a, mu_g, kk_ref, ka_ref, rk_ref,
                     db_ref, du_ref, ib_ref, iu_ref, gu_ref,
                     rp_out, y0_out, m_out, n_out, bonus_out, gate_out,
                     *, nctx_steps, n_steps, prec_t):
    L = CHUNK
    rb = r_c.shape[1]
    halves = [slice(h * L, (h + 1) * L) for h in range(rb // L)]
    s = pl.program_id(1)
    has_prev = jnp.logical_and(s != 0, s != nctx_steps)
    has_next = jnp.logical_and(s != nctx_steps - 1, s != n_steps - 1)

    def shifted(cur, prv, nxt, mu):
        x = cur[0]
        rows = lax.broadcasted_iota(jnp.int32, (rb, 1), 0)
        pr = jnp.where(has_prev, prv[0, SUBLANES - 1:SUBLANES, :], 0.0)
        nx = jnp.where(has_next, nxt[0, 0:1, :], 0.0)
        up = jnp.where(rows == 0, pr, pltpu.roll(x, 1, 0))
        dn = jnp.where(rows == rb - 1, nx, pltpu.roll(x, rb - 1, 0))
        return x + mu[...] * (0.5 * (up + dn) - x)

    r = shifted(r_c, r_p, r_n, mu_r)
    k = shifted(k_c, k_p, k_n, mu_k)
    v = shifted(v_c, v_p, v_n, mu_v)
    xw = shifted(w_c, w_p, w_n, mu_w)
    xa = shifted(a_c, a_p, a_n, mu_a)
    xg = shifted(g_c, g_p, g_n, mu_g)

    lane = lax.broadcasted_iota(jnp.int32, (1, LANES), 1)
    m0 = lane < HEAD_DIM

    def head_sum(x):
        s0 = jnp.sum(jnp.where(m0, x, 0.0), axis=-1, keepdims=True)
        s1 = jnp.sum(jnp.where(m0, 0.0, x), axis=-1, keepdims=True)
        return jnp.where(m0, s0, s1)

    def stack2(x):
        return jnp.concatenate([jnp.where(m0, x, 0.0), jnp.where(m0, 0.0, x)], axis=0)

    tw = _bf(jnp.tanh(xw))
    xab = _bf(xa)
    lw_all = [(-math.exp(-0.5)) * jax.nn.sigmoid(db_ref[d:d + 1, :] + _dot(tw, du_ref[d]))
              for d in range(2)]
    sa_all = [jax.nn.sigmoid(ib_ref[d:d + 1, :] + _dot(xab, iu_ref[d])) for d in range(2)]
    gate_out[0] = _dot(_bf(jax.nn.sigmoid(xg)), gu_ref[...])

    ri = lax.broadcasted_iota(jnp.int32, (L, L), 0)
    ci = lax.broadcasted_iota(jnp.int32, (L, L), 1)
    ri2 = lax.broadcasted_iota(jnp.int32, (2 * L, 2 * L), 0)
    ci2 = lax.broadcasted_iota(jnp.int32, (2 * L, 2 * L), 1)
    rt = jnp.bitwise_and(ri2, L - 1)
    ct = jnp.bitwise_and(ci2, L - 1)
    eye = ri2 == ci2
    n_sq = int(math.log2(L)) - 1

    tri = [_bf((ci <= ri).astype(F32)), _bf((ci >= ri).astype(F32))]
    strict = [ct < rt, ct > rt]
    incl = [ct <= rt, ct >= rt]
    lw_c = [[lw_all[d][hs] for hs in halves] for d in range(2)]

    def scan_sum(tmask, lw):
        hi = _bf(lw)
        lo = _bf(lw - hi.astype(F32))
        return _dot(tmask, hi) + _dot(tmask, lo)

    cum = [[scan_sum(tri[d], lw) for lw in lw_c[d]] for d in range(2)]
    cend = [[jnp.sum(lw, axis=0, keepdims=True) for lw in lw_c[d]] for d in range(2)]
    e_in = [[jnp.exp(c) for c in cum[d]] for d in range(2)]
    e_ex = [[jnp.exp(c - lw) for c, lw in zip(cum[d], lw_c[d])] for d in range(2)]
    e_neg = [[jnp.exp(-c) for c in cum[d]] for d in range(2)]
    e_end = [[jnp.exp(ce - c) for c, ce in zip(cum[d], cend[d])] for d in range(2)]
    p_end = [[jnp.exp(ce) for ce in cend[d]] for d in range(2)]
    kd_all = [k * (1.0 + (sa_all[d] - 1.0) * ka_ref[...]) for d in range(2)]

    npp = r.shape[1] // LANES
    chains = [(h, p, d) for h in range(len(halves)) for p in range(npp) for d in range(2)]
    sls = [slice(p * LANES, (p + 1) * LANES) for p in range(npp)]
    kk_p = []
    for p in range(npp):
        kx = k[:, sls[p]] * kk_ref[:, sls[p]]
        kk_p.append(kx * lax.rsqrt(head_sum(kx * kx) + KK_EPS))
        ksum = kd_all[0][:, sls[p]] + kd_all[1][:, sls[p]]
        bonus_out[0, :, sls[p]] = head_sum(r[:, sls[p]] * ksum * rk_ref[:, sls[p]]) * v[:, sls[p]]
    vs_c = {(h, p): stack2(v[halves[h], sls[p]]) for h in range(len(halves)) for p in range(npp)}
    vsb_c = {key: _bf(val) for key, val in vs_c.items()}

    a_s, r_s, bh_s, kh_s, g = [], [], [], [], []
    for h, p, d in chains:
        hs, sl = halves[h], sls[p]
        kd = kd_all[d][hs, sl]
        kk = kk_p[p][hs]
        bvec = kk * sa_all[d][hs, sl]
        a_s.append(stack2(-kk * e_ex[d][h][:, sl]))
        r_s.append(stack2(r[hs, sl] * e_in[d][h][:, sl]))
        b_s = stack2(bvec * e_neg[d][h][:, sl])
        k_s = stack2(kd * e_neg[d][h][:, sl])
        bh_s.append(stack2(bvec * e_end[d][h][:, sl]))
        kh_s.append(stack2(kd * e_end[d][h][:, sl]))
        g.append(_dot_nt(_bf(jnp.concatenate([a_s[-1], r_s[-1]], axis=0)),
                         _bf(jnp.concatenate([b_s, k_s], axis=0))))
    a_ab = [jnp.where(strict[d], g[i][:2 * L, :2 * L], 0.0) for i, (h, p, d) in enumerate(chains)]
    a_ak = [jnp.where(strict[d], g[i][:2 * L, 2 * L:], 0.0) for i, (h, p, d) in enumerate(chains)]
    a_rb = [jnp.where(incl[d], g[i][2 * L:, :2 * L], 0.0) for i, (h, p, d) in enumerate(chains)]
    a_rk = [jnp.where(incl[d], g[i][2 * L:, 2 * L:], 0.0) for i, (h, p, d) in enumerate(chains)]

    ident = jnp.where(eye, 1.0, 0.0)
    tmat = [ident + a for a in a_ab]
    apow = a_ab
    for _ in range(n_sq):
        if prec_t is None:
            apow = [_dot(_bf(a), _bf(a)) for a in apow]
            tmat = [t + _dot(_bf(a), _bf(t)) for a, t in zip(apow, tmat)]
        else:
            apow = [_dot(a, a, precision=prec_t) for a in apow]
            tmat = [t + _dot(a, t, precision=prec_t) for a, t in zip(apow, tmat)]

    x1 = [_dot(_bf(a_ak[i]), vsb_c[h, p]) for i, (h, p, d) in enumerate(chains)]
    y0v = [_dot(_bf(a_rk[i]), vsb_c[h, p]) for i, (h, p, d) in enumerate(chains)]
    tu = [_dot(_bf(tmat[i]), _bf(jnp.concatenate([a_s[i], x1[i]], axis=1)))
          for i in range(len(chains))]
    ry = [_dot(_bf(a_rb[i]), _bf(tu[i])) for i in range(len(chains))]
    def fold(z):
        h = z.shape[0] // 2
        return z[:h] + z[h:]

    for i, (h, p, d) in enumerate(chains):
        rp_out[d, h, 0, p] = _bf(fold(r_s[i] + ry[i][:, :LANES]))
        y0_out[d, h, 0, p] = fold(ry[i][:, LANES:] + y0v[i])
    for i, (h, p, d) in enumerate(chains):
        m_out[d, h, 0, p] = _bf(fold(jnp.where(eye, p_end[d][h][:, sls[p]], 0.0)
                                     + _dot_tn(_bf(tu[i][:, :LANES]), _bf(bh_s[i]))))
    for i, (h, p, d) in enumerate(chains):
        n_out[d, h, 0, p] = fold(_dot_tn(_bf(jnp.concatenate([tu[i][:, LANES:], vs_c[h, p]], axis=0)),
                                         _bf(jnp.concatenate([bh_s[i], kh_s[i]], axis=0))))


def _wkv_prep(p_all, lay, prm, ctx_len, prec_t, pp):
    b, ta, _ = p_all.shape
    L = CHUNK
    nc = ta // L
    nctx = ctx_len // L
    npair = lay["rd"] // LANES
    gl = lay["gl"]
    rows8 = ta // SUBLANES
    cps = 2 if nc % 2 == 0 and nctx % 2 == 0 else 1
    rbk = cps * L

    def cur(colfn, width=LANES):
        return pl.BlockSpec((1, rbk, width), lambda bb, c, p: (bb, c, colfn(p)))

    def prv(colfn, width=LANES):
        return pl.BlockSpec((1, SUBLANES, width),
                            lambda bb, c, p: (bb, jnp.maximum(c * (rbk // SUBLANES) - 1, 0), colfn(p)))

    def nxt(colfn, width=LANES):
        return pl.BlockSpec((1, SUBLANES, width),
                            lambda bb, c, p: (bb, jnp.minimum((c + 1) * (rbk // SUBLANES), rows8 - 1), colfn(p)))

    def trio(colfn, width=LANES):
        return [cur(colfn, width), prv(colfn, width), nxt(colfn, width)]

    pw = pp * LANES
    assert npair % pp == 0 and all(lay[n] % pw == 0 for n in ("r", "k", "v"))
    rb, kb, vb = lay["r"] // pw, lay["k"] // pw, lay["v"] // pw
    wb, ab = lay["xw"] // LANES, lay["xa"] // LANES
    assert lay["xg"] % gl == 0
    gb = lay["xg"] // gl

    def vec(width=pw):
        return pl.BlockSpec((1, width), lambda bb, c, p: (0, p))

    def vec0(width=LANES):
        return pl.BlockSpec((1, width), lambda bb, c, p: (0, 0))

    in_specs = (trio(lambda p: rb + p, pw) + trio(lambda p: kb + p, pw) + trio(lambda p: vb + p, pw)
                + trio(lambda p: wb) + trio(lambda p: ab) + trio(lambda p: gb, gl)
                + [vec(), vec(), vec(), vec0(), vec0(), vec0(gl),
                   vec(), vec(), vec(),
                   pl.BlockSpec((2, pw), lambda bb, c, p: (0, p)),
                   pl.BlockSpec((2, LANES, pw), lambda bb, c, p: (0, 0, p)),
                   pl.BlockSpec((2, pw), lambda bb, c, p: (0, p)),
                   pl.BlockSpec((2, LANES, pw), lambda bb, c, p: (0, 0, p)),
                   pl.BlockSpec((gl, pw), lambda bb, c, p: (0, p))])

    def opspec(rows):
        return pl.BlockSpec((2, cps, 1, pp, rows, LANES), lambda bb, c, p: (0, c, bb, p, 0, 0))

    out_specs = [opspec(L), opspec(L), opspec(HEAD_DIM), opspec(HEAD_DIM),
                 pl.BlockSpec((1, rbk, pw), lambda bb, c, p: (bb, c, p)),
                 pl.BlockSpec((1, rbk, pw), lambda bb, c, p: (bb, c, p))]
    out_shape = [jax.ShapeDtypeStruct((2, nc, b, npair, L, LANES), BF16),
                 jax.ShapeDtypeStruct((2, nc, b, npair, L, LANES), F32),
                 jax.ShapeDtypeStruct((2, nc, b, npair, HEAD_DIM, LANES), BF16),
                 jax.ShapeDtypeStruct((2, nc, b, npair, HEAD_DIM, LANES), F32),
                 jax.ShapeDtypeStruct((b, ta, lay["rd"]), F32),
                 jax.ShapeDtypeStruct((b, ta, lay["rd"]), F32)]
    kern = functools.partial(_wkv_prep_kernel, nctx_steps=nctx // cps, n_steps=nc // cps, prec_t=prec_t)
    args = [p_all] * 18 + [prm["mu_r"], prm["mu_k"], prm["mu_v"], prm["mu_w"], prm["mu_a"], prm["mu_g"],
                           prm["k_k"], prm["k_a"], prm["r_k"], prm["decay_bias"], prm["decay_up"],
                           prm["iclr_bias"], prm["iclr_up"], prm["gate_up"]]
    return pl.pallas_call(
        kern, out_shape=out_shape, grid=(b, nc // cps, npair // pp), in_specs=in_specs, out_specs=out_specs,
        compiler_params=_params("parallel", "parallel", "parallel"),
        name="wkv_chunk_operators",
    )(*args)


def _wkv_seq_kernel(rpf, y0f, mf, nf, rpr, y0r, mr, nr, yf_ref, yr_ref, s_ref, *, nb, npair):
    L = CHUNK
    cps = rpf.shape[1]

    @pl.when(pl.program_id(0) == 0)
    def _():
        s_ref[...] = jnp.zeros_like(s_ref)

    m0 = lax.broadcasted_iota(jnp.int32, (1, LANES), 1) < HEAD_DIM

    def unfold(z):
        zero = jnp.zeros_like(z)
        return jnp.concatenate([jnp.where(m0, z, zero), jnp.where(m0, zero, z)], axis=0)

    for turn in range(cps):
        for d, (rp_ref, y0_ref, m_ref, n_ref, y_ref) in enumerate(((rpf, y0f, mf, nf, yf_ref),
                                                                   (rpr, y0r, mr, nr, yr_ref))):
            h = turn if d == 0 else cps - 1 - turn
            for bb in range(nb):
                for p in range(npair):
                    i = (d * nb + bb) * npair + p
                    s0 = _bf(s_ref[i])
                    ys = _dot_nt(unfold(rp_ref[0, h, bb, p]), s0)
                    y_ref[bb, h * L:(h + 1) * L, p * LANES:(p + 1) * LANES] = (
                        ys[:L] + ys[L:] + y0_ref[0, h, bb, p])
                    s_ref[i] = _dot(s0, unfold(m_ref[0, h, bb, p])) + unfold(n_ref[0, h, bb, p])


def _wkv_sweep(rp, y0, mm, nn, ctx_len):
    _, nc, b, npair, _, _ = rp.shape
    L = CHUNK
    nctx = ctx_len // L
    cps = 2 if nc % 2 == 0 and nctx % 2 == 0 else 1
    ng, ngctx = nc // cps, nctx // cps

    def group_of(d, s):
        return s if d == 0 else jnp.where(s < ngctx, ngctx - 1 - s, ng - 1 - (s - ngctx))

    def opspec(d, rows):
        return pl.BlockSpec((1, cps, b, npair, rows, LANES), lambda s: (d, group_of(d, s), 0, 0, 0, 0))

    def yspec(d):
        return pl.BlockSpec((b, cps * L, npair * LANES), lambda s: (0, group_of(d, s), 0))

    rows = (L, L, HEAD_DIM, HEAD_DIM)
    y_shape = jax.ShapeDtypeStruct((b, nc * L, npair * LANES), F32)
    return pl.pallas_call(
        functools.partial(_wkv_seq_kernel, nb=b, npair=npair),
        out_shape=[y_shape, y_shape],
        grid=(ng,),
        in_specs=[opspec(d, r) for d in range(2) for r in rows],
        out_specs=[yspec(0), yspec(1)],
        scratch_shapes=[pltpu.VMEM((2 * b * npair, LANES, LANES), F32)],
        compiler_params=_params("arbitrary"),
        name="wkv_state_sweep",
    )(rp, y0, mm, nn, rp, y0, mm, nn)


def _attn_kernel(q_ref, kp_ref, kc_ref, kn_ref, vp_ref, vc_ref, vn_ref, kx_ref, vx_ref,
                 cq_ref, sq_ref, cp_ref, sp_ref, cn_ref, sn_ref, sink_ref, o_ref,
                 *, n_blocks, kv_heads, group):
    n = pl.program_id(1)
    qb = QBLOCK

    def rope(x, cos, sin):
        w = x.shape[1]
        reps = w // LANES
        cs = jnp.concatenate([cos] * reps, axis=1) if reps > 1 else cos
        sn = jnp.concatenate([sin] * reps, axis=1) if reps > 1 else sin
        lane = lax.broadcasted_iota(jnp.int32, (1, w), 1)
        sw = jnp.where(jnp.bitwise_and(lane, 16) == 0, pltpu.roll(x, w - 16, 1), pltpu.roll(x, 16, 1))
        return x * cs + sw * sn

    q = rope(q_ref[0], cq_ref[...], sq_ref[...]) * (HEAD_DIM ** -0.5)
    kwin = jnp.concatenate([rope(kp_ref[0], cp_ref[...], sp_ref[...]),
                            rope(kc_ref[0], cq_ref[...], sq_ref[...]),
                            rope(kn_ref[0], cn_ref[...], sn_ref[...]),
                            kx_ref[0]], axis=0)
    vwin = jnp.concatenate([vp_ref[0], vc_ref[0], vn_ref[0], vx_ref[0]], axis=0)
    nkeys = kwin.shape[0]

    rows = group * qb
    qi = jnp.bitwise_and(lax.broadcasted_iota(jnp.int32, (rows, nkeys), 0), qb - 1)
    s_i = lax.broadcasted_iota(jnp.int32, (rows, nkeys), 1)
    blk = s_i // qb + (n - 1)
    in_win = (s_i >= qi) & (s_i <= qi + 2 * WINDOW) & (blk >= 0) & (blk < n_blocks)
    valid = in_win | (s_i >= 3 * qb)
    rgrp = lax.broadcasted_iota(jnp.int32, (rows, 1), 0) // qb

    scores, sinks = [], []
    for kh in range(kv_heads):
        kk_ = _bf(kwin[:, kh * HEAD_DIM:(kh + 1) * HEAD_DIM])
        qs = jnp.concatenate([q[:, (kh * group + g) * HEAD_DIM:(kh * group + g + 1) * HEAD_DIM]
                              for g in range(group)], axis=0)
        sk = jnp.zeros((rows, 1), F32)
        for g in range(group):
            h = kh * group + g
            sk = jnp.where(rgrp == g, sink_ref[0:1, h:h + 1], sk)
        sinks.append(sk)
        scores.append(jnp.where(valid, _dot_nt(_bf(qs), kk_), NEG_BIG))
    probs, denoms = [], []
    for s, sk in zip(scores, sinks):
        m = jnp.maximum(jnp.max(s, axis=-1, keepdims=True), sk)
        e = jnp.exp(s - m)
        denoms.append(jnp.sum(e, axis=-1, keepdims=True) + jnp.exp(sk - m))
        probs.append(_bf(e))
    pieces = []
    for kh in range(kv_heads):
        vv_ = _bf(vwin[:, kh * HEAD_DIM:(kh + 1) * HEAD_DIM])
        o = _dot(probs[kh], vv_) / denoms[kh]
        for g in range(group):
            pieces.append(o[g * qb:(g + 1) * qb])
    o_ref[0] = _bf(jnp.concatenate(pieces, axis=1))


def _attention(p_all, lay, cos_t, sin_t, sink, ctx_len, seq):
    b = p_all.shape[0]
    qb = QBLOCK
    nb = seq // qb
    cb = ctx_len // qb
    ad, kvd = lay["ad"], lay["kvd"]
    kv_heads = kvd // HEAD_DIM
    group = (ad // HEAD_DIM) // kv_heads
    assert lay["q"] % ad == 0 and lay["ka"] % kvd == 0 and lay["va"] % kvd == 0
    qc, kc, vc = lay["q"] // ad, lay["ka"] // kvd, lay["va"] // kvd

    def clampb(n, off):
        return jnp.clip(n + off, 0, nb - 1)

    def win(col, off):
        return pl.BlockSpec((1, qb, kvd), lambda bb, n: (bb, cb + clampb(n, off), col))

    def tab(off):
        return pl.BlockSpec((qb, LANES), lambda bb, n: (clampb(n, off), 0))

    assert ctx_len % qb == 0
    ctxspec = lambda col: pl.BlockSpec((1, ctx_len, kvd), lambda bb, n: (bb, 0, col))
    in_specs = [pl.BlockSpec((1, qb, ad), lambda bb, n: (bb, cb + n, qc)),
                win(kc, -1), win(kc, 0), win(kc, 1), win(vc, -1), win(vc, 0), win(vc, 1),
                ctxspec(kc), ctxspec(vc),
                tab(0), tab(0), tab(-1), tab(-1), tab(1), tab(1),
                pl.BlockSpec((1, sink.shape[1]), lambda bb, n: (0, 0))]
    return pl.pallas_call(
        functools.partial(_attn_kernel, n_blocks=nb, kv_heads=kv_heads, group=group),
        out_shape=jax.ShapeDtypeStruct((b, seq, ad), BF16),
        grid=(b, nb),
        in_specs=in_specs,
        out_specs=pl.BlockSpec((1, qb, ad), lambda bb, n: (bb, n, 0)),
        compiler_params=_params("parallel", "parallel"),
        name="window_ctx_attention",
    )(p_all, p_all, p_all, p_all, p_all, p_all, p_all, p_all, p_all,
      cos_t, sin_t, cos_t, sin_t, cos_t, sin_t, sink)


def _outproj_kernel(yf_ref, yr_ref, bon_ref, gat_ref, att_ref, x_ref, lg_ref, lb_ref, w_ref,
                    m_ref, g2_ref, wr_ref, x1_ref, h2_ref, lt_ref):
    ysum = yf_ref[0] + yr_ref[0]
    ri = lax.broadcasted_iota(jnp.int32, (LANES, LANES), 0) // HEAD_DIM
    ci = lax.broadcasted_iota(jnp.int32, (LANES, LANES), 1) // HEAD_DIM
    hm = _bf(jnp.where(ri == ci, 1.0 / HEAD_DIM, 0.0))

    def head_mean(z):
        hi = _bf(z)
        lo = _bf(z - hi.astype(F32))
        return _dot(hi, hm) + _dot(lo, hm)

    ybs = [ysum[:, j * LANES:(j + 1) * LANES] for j in range(ysum.shape[1] // LANES)]
    devs = [yb - mu for yb, mu in zip(ybs, [head_mean(yb) for yb in ybs])]
    blocks = [dev * lax.rsqrt(var + LNX_EPS)
              for dev, var in zip(devs, [head_mean(dev * dev) for dev in devs])]
    yn = jnp.concatenate(blocks, axis=1) if len(blocks) > 1 else blocks[0]
    rw = (yn * lg_ref[...] + lb_ref[...] + bon_ref[0]) * gat_ref[0]
    mix = jnp.concatenate([_bf(rw), att_ref[0]], axis=1)
    m = m_ref[0]
    x1 = x_ref[0] + m[0:1] * _dot(mix, w_ref[...])
    x1_ref[0] = x1
    ms = jnp.mean(x1 * x1, axis=-1, keepdims=True)
    h2 = x1 * lax.rsqrt(ms + NORM_EPS) * g2_ref[...] * (1.0 + m[2:3]) + m[1:2]
    h2_ref[0] = h2
    hi = _bf(h2)
    lo = _bf(h2 - hi.astype(F32))
    wr = wr_ref[...]
    wr_hi = _bf(wr)
    wr_lo = _bf(wr - wr_hi.astype(F32))
    lg = _dot(hi, wr_hi) + (_dot(hi, wr_lo) + _dot(lo, wr_hi))
    lt_ref[...] = lg.T[:lt_ref.shape[0]]


def _out_projection(y_f, y_r, bonus, gate, attn, x, lnx_g, lnx_b, w_out_b, m2, g2, w_router, ctx_len, tr):
    b, t, d = x.shape
    rd = y_f.shape[2]
    ad = attn.shape[2]
    ne = w_router.shape[1]
    assert ne <= LANES
    wr = _pad_to(w_router, 1, LANES)
    co = ctx_len // tr
    nt = t // tr
    return pl.pallas_call(
        _outproj_kernel,
        out_shape=[jax.ShapeDtypeStruct((b, t, d), F32),
                   jax.ShapeDtypeStruct((b, t, d), F32),
                   jax.ShapeDtypeStruct((ne, b * t), F32)],
        grid=(b, nt),
        in_specs=[pl.BlockSpec((1, tr, rd), lambda bb, i: (bb, co + i, 0)),
                  pl.BlockSpec((1, tr, rd), lambda bb, i: (bb, co + i, 0)),
                  pl.BlockSpec((1, tr, rd), lambda bb, i: (bb, co + i, 0)),
                  pl.BlockSpec((1, tr, rd), lambda bb, i: (bb, co + i, 0)),
                  pl.BlockSpec((1, tr, ad), lambda bb, i: (bb, i, 0)),
                  pl.BlockSpec((1, tr, d), lambda bb, i: (bb, i, 0)),
                  pl.BlockSpec((1, rd), lambda bb, i: (0, 0)),
                  pl.BlockSpec((1, rd), lambda bb, i: (0, 0)),
                  pl.BlockSpec((rd + ad, d), lambda bb, i: (0, 0)),
                  pl.BlockSpec((1, 4, d), lambda bb, i: (bb, 0, 0)),
                  pl.BlockSpec((1, d), lambda bb, i: (0, 0)),
                  pl.BlockSpec((d, LANES), lambda bb, i: (0, 0))],
        out_specs=[pl.BlockSpec((1, tr, d), lambda bb, i: (bb, i, 0)),
                   pl.BlockSpec((1, tr, d), lambda bb, i: (bb, i, 0)),
                   pl.BlockSpec((ne, tr), lambda bb, i: (0, bb * nt + i))],
        compiler_params=_params("parallel", "parallel"),
        name="rwkv_out_outproj_norm2_router",
    )(y_f, y_r, bonus, gate, attn, x, lnx_g, lnx_b, w_out_b, m2, g2, wr)


def _router_kernel(l_ref, b_ref, idx_ref, gate_ref, rank_ref, cnt_ref, carry_ref):
    @pl.when(pl.program_id(0) == 0)
    def _():
        carry_ref[...] = jnp.zeros_like(carry_ref)

    sc = jax.nn.sigmoid(l_ref[...])
    ne, tn = sc.shape
    eg = ne // N_GROUPS
    bz = sc + b_ref[...]
    g3 = bz.reshape(N_GROUPS, eg, tn)
    io = lax.broadcasted_iota(jnp.int32, (N_GROUPS, eg, tn), 1).astype(F32)
    m1 = jnp.max(g3, axis=1, keepdims=True)
    i1 = jnp.min(jnp.where(g3 == m1, io, float(eg)), axis=1, keepdims=True)
    m2 = jnp.max(jnp.where(io == i1, -jnp.inf, g3), axis=1, keepdims=True)
    gs = (m1 + m2).reshape(N_GROUPS, tn)

    def topk_hits(vals, k):
        n = vals.shape[0]
        idx = lax.broadcasted_iota(jnp.int32, vals.shape, 0).astype(F32)
        hits = []
        cur = vals
        for _ in range(k):
            mx = jnp.max(cur, axis=0, keepdims=True)
            ix = jnp.min(jnp.where(cur == mx, idx, float(n)), axis=0, keepdims=True)
            hit = idx == ix
            hits.append(hit)
            cur = jnp.where(hit, -jnp.inf, cur)
        return hits

    gsel = sum(jnp.where(h, 1.0, 0.0) for h in topk_hits(gs, TOPK_GROUPS))
    masked = jnp.where(gsel.reshape(N_GROUPS, 1, tn) > 0.0, g3, -jnp.inf).reshape(ne, tn)
    hits = topk_hits(masked, TOP_K)
    chosen = sum(jnp.where(h, 1.0, 0.0) for h in hits)
    denom = jnp.sum(chosen * sc, axis=0, keepdims=True)

    si = lax.broadcasted_iota(jnp.int32, (tn, tn), 0)
    ti = lax.broadcasted_iota(jnp.int32, (tn, tn), 1)
    before = _bf(jnp.where(si < ti, 1.0, 0.0))
    rank_full = _dot(_bf(chosen), before) + carry_ref[...]
    carry_ref[...] += jnp.sum(chosen, axis=1, keepdims=True)
    cnt_ref[...] = carry_ref[...]

    eidx = lax.broadcasted_iota(jnp.int32, (ne, tn), 0).astype(F32)
    for j, hit in enumerate(hits):
        idx_ref[j:j + 1, :] = jnp.sum(jnp.where(hit, eidx, 0.0), axis=0, keepdims=True).astype(jnp.int32)
        gate_ref[j:j + 1, :] = (jnp.sum(jnp.where(hit, sc, 0.0), axis=0, keepdims=True)
                                / denom * ROUTED_SCALE)
        rank_ref[j:j + 1, :] = jnp.sum(jnp.where(hit, rank_full, 0.0), axis=0,
                                       keepdims=True).astype(jnp.int32)


def _router(logits_t, bias_col):
    ne, n = logits_t.shape
    tn = 1024 if n % 1024 == 0 else n
    choice = lambda: pl.BlockSpec((TOP_K, tn), lambda i: (0, i))
    return pl.pallas_call(
        _router_kernel,
        out_shape=[jax.ShapeDtypeStruct((TOP_K, n), jnp.int32),
                   jax.ShapeDtypeStruct((TOP_K, n), F32),
                   jax.ShapeDtypeStruct((TOP_K, n), jnp.int32),
                   jax.ShapeDtypeStruct((ne, 1), F32)],
        grid=(n // tn,),
        in_specs=[pl.BlockSpec((ne, tn), lambda i: (0, i)),
                  pl.BlockSpec((ne, 1), lambda i: (0, 0))],
        out_specs=[choice(), choice(), choice(), pl.BlockSpec((ne, 1), lambda i: (0, 0))],
        scratch_shapes=[pltpu.VMEM((ne, 1), F32)],
        compiler_params=_params("arbitrary"),
        name="group_topk_router",
    )(logits_t, bias_col)


def _slots_kernel(idx_ref, rank_ref, ps_ref, o_ref):
    ne, tn = ps_ref.shape[0], idx_ref.shape[1]
    eidx = lax.broadcasted_iota(jnp.int32, (ne, tn), 0)
    for j in range(TOP_K):
        start = jnp.sum(jnp.where(eidx == idx_ref[j:j + 1, :], ps_ref[...], 0.0), axis=0, keepdims=True)
        o_ref[j:j + 1, :] = start.astype(jnp.int32) + rank_ref[j:j + 1, :]


def _slot_positions(idx8, rank8, pstart_col):
    _, n = idx8.shape
    ne = pstart_col.shape[0]
    tn = 1024 if n % 1024 == 0 else n
    choice = lambda: pl.BlockSpec((TOP_K, tn), lambda i: (0, i))
    return pl.pallas_call(
        _slots_kernel,
        out_shape=jax.ShapeDtypeStruct((TOP_K, n), jnp.int32),
        grid=(n // tn,),
        in_specs=[choice(), choice(), pl.BlockSpec((ne, 1), lambda i: (0, 0))],
        out_specs=choice(),
        compiler_params=_params("parallel"),
        name="expert_slot_positions",
    )(idx8, rank8, pstart_col)


def _dispatch_kernel(pad_start_ref, npad_ref, nused_ref, pos_ref, h_ref, xs_ref, zero_ref, sems,
                     *, n_experts):
    tt = h_ref.shape[0]
    bm = zero_ref.shape[0]

    sem = sems.at[0]

    def issue(t, carry):
        for k in range(TOP_K):
            pltpu.make_async_copy(h_ref.at[pl.ds(t, 1)], xs_ref.at[pl.ds(pos_ref[k, t], 1)],
                                  sems.at[k]).start()
        return carry

    lax.fori_loop(0, tt, issue, 0, unroll=4)
    for k in range(TOP_K):
        pltpu.make_async_copy(h_ref, xs_ref.at[pl.ds(0, tt)], sems.at[k]).wait()

    @pl.when(pl.program_id(0) == pl.num_programs(0) - 1)
    def _():
        zero_ref[...] = jnp.zeros_like(zero_ref)

        zero_row_src = zero_ref.at[pl.ds(0, 1)]

        def per_expert(e, carry):
            def zero_row(i, c2):
                pltpu.make_async_copy(zero_row_src, xs_ref.at[pl.ds(pad_start_ref[e] + i, 1)], sem).start()
                return c2

            def zero_wait(i, c2):
                pltpu.make_async_copy(zero_row_src, xs_ref.at[pl.ds(0, 1)], sem).wait()
                return c2

            lax.fori_loop(0, npad_ref[e], zero_row, 0)
            lax.fori_loop(0, npad_ref[e], zero_wait, 0)
            return carry

        lax.fori_loop(0, n_experts, per_expert, 0)

        def zero_block(j, carry):
            blk = pltpu.make_async_copy(zero_ref, xs_ref.at[pl.ds(j * bm, bm)], sem)
            blk.start()
            blk.wait()
            return carry

        lax.fori_loop(nused_ref[0], xs_ref.shape[0] // bm, zero_block, 0)


def _dispatch(h2, pos8, pad_start, npad, n_used, n_slots, tt, bm):
    n, d = h2.shape
    ne = pad_start.shape[0]
    grid_spec = pltpu.PrefetchScalarGridSpec(
        num_scalar_prefetch=3,
        grid=(n // tt,),
        in_specs=[pl.BlockSpec((TOP_K, tt), lambda i, ps, npd, nu: (0, i), memory_space=pltpu.SMEM),
                  pl.BlockSpec((tt, d), lambda i, ps, npd, nu: (i, 0))],
        out_specs=pl.BlockSpec(memory_space=pl.ANY),
        scratch_shapes=[pltpu.VMEM((bm, d), h2.dtype), pltpu.SemaphoreType.DMA((TOP_K,))])
    return pl.pallas_call(
        functools.partial(_dispatch_kernel, n_experts=ne),
        out_shape=jax.ShapeDtypeStruct((n_slots, d), h2.dtype),
        grid_spec=grid_spec,
        compiler_params=_params("arbitrary"),
        name="expert_dispatch",
    )(pad_start, npad, n_used, pos8, h2)


def _expert_ffn_kernel(be_ref, nu_ref, nx_ref, sl_ref, x_ref, wg_hbm, wu_hbm, wd_hbm, y_ref,
                       wgf, wuf, wdf, wgb, wub, wdb, sems):
    j = pl.program_id(0)
    e = be_ref[j]
    used = j < nu_ref[0]
    fresh = jnp.logical_or(j == 0, e != be_ref[jnp.maximum(j - 1, 0)])

    def weight_copies(expert, slot):
        return [pltpu.make_async_copy(src.at[expert], dst.at[slot], sems.at[slot])
                for src, dst in ((wg_hbm, wgf), (wu_hbm, wuf), (wd_hbm, wdf))]

    @pl.when(j == 0)
    def _():
        for cp in weight_copies(e, sl_ref[e]):
            cp.start()

    @pl.when(jnp.logical_and(used, fresh))
    def _():
        slot = sl_ref[e]
        for cp in weight_copies(e, slot):
            cp.wait()
        nxt = nx_ref[e]

        @pl.when(nxt >= 0)
        def _():
            for cp in weight_copies(nxt, 1 - slot):
                cp.start()

        wgb[...] = _bf(wgf[slot])
        wub[...] = _bf(wuf[slot])
        wdb[...] = _bf(wdf[slot])

    @pl.when(used)
    def _():
        x = _bf(x_ref[...])
        a = _dot(x, wgb[...])
        u = _dot(x, wub[...])
        y_ref[...] = _dot(_bf(a * jax.nn.sigmoid(a) * u), wdb[...])

    @pl.when(jnp.logical_not(used))
    def _():
        y_ref[...] = jnp.zeros_like(y_ref)


def _expert_ffn(xs, blk_expert, n_used, next_expert, expert_slot, e_gate, e_up, e_down, bm):
    n_slots, dw = xs.shape
    ne, d, ff = e_gate.shape
    assert dw == d
    grid_spec = pltpu.PrefetchScalarGridSpec(
        num_scalar_prefetch=4,
        grid=(n_slots // bm,),
        in_specs=[pl.BlockSpec((bm, dw), lambda j, be, nu, nx, sl: (jnp.minimum(j, nu[0] - 1), 0)),
                  pl.BlockSpec(memory_space=pl.ANY),
                  pl.BlockSpec(memory_space=pl.ANY),
                  pl.BlockSpec(memory_space=pl.ANY)],
        out_specs=pl.BlockSpec((bm, dw), lambda j, be, nu, nx, sl: (j, 0)),
        scratch_shapes=[pltpu.VMEM((2, d, ff), F32), pltpu.VMEM((2, d, ff), F32), pltpu.VMEM((2, ff, d), F32),
                        pltpu.VMEM((d, ff), BF16), pltpu.VMEM((d, ff), BF16), pltpu.VMEM((ff, d), BF16),
                        pltpu.SemaphoreType.DMA((2,))])
    return pl.pallas_call(
        _expert_ffn_kernel,
        out_shape=jax.ShapeDtypeStruct((n_slots, dw), xs.dtype),
        grid_spec=grid_spec,
        compiler_params=_params("arbitrary"),
        name="routed_expert_ffn",
    )(blk_expert, n_used, next_expert, expert_slot, xs, e_gate, e_up, e_down)


def _final_kernel(pos_ref, h_ref, x1_ref, g_ref, ys_ref, wg_ref, wu_ref, wd_ref, m_ref, fg_ref,
                  o_ref, buf, sems):
    tr = h_ref.shape[1]

    def issue(t, carry):
        for k in range(TOP_K):
            pltpu.make_async_copy(ys_ref.at[pl.ds(pos_ref[k, t], 1)], buf.at[k, pl.ds(t, 1)],
                                  sems.at[k]).start()
        return carry

    lax.fori_loop(0, tr, issue, 0, unroll=4)
    h = _bf(h_ref[0])
    a = _dot(h, wg_ref[...])
    u = _dot(h, wu_ref[...])
    moe = _dot(_bf(a * jax.nn.sigmoid(a) * u), wd_ref[...])
    g = g_ref[...]
    for k in range(TOP_K):
        pltpu.make_async_copy(ys_ref.at[pl.ds(0, tr)], buf.at[k], sems.at[k]).wait()
    for k in range(TOP_K):
        moe = moe + g[:, k:k + 1] * buf[k]
    x2 = x1_ref[0] + m_ref[0, 3:4, :] * moe
    ms = jnp.mean(x2 * x2, axis=-1, keepdims=True)
    o_ref[0] = x2 * lax.rsqrt(ms + NORM_EPS) * fg_ref[...]


def _final(h2, x1, pos8, gate_rows, ys, s_gate_b, s_up_b, s_down_b, m2, final_g, tr):
    b, t, d = x1.shape
    ff = s_gate_b.shape[1]
    nt = t // tr
    dw = h2.shape[2]
    row = lambda width=d: pl.BlockSpec((1, tr, width), lambda bb, i: (bb, i, 0))
    return pl.pallas_call(
        _final_kernel,
        out_shape=jax.ShapeDtypeStruct((b, t, d), F32),
        grid=(b, nt),
        in_specs=[pl.BlockSpec((TOP_K, tr), lambda bb, i: (0, bb * nt + i), memory_space=pltpu.SMEM),
                  row(dw), row(),
                  pl.BlockSpec((tr, TOP_K), lambda bb, i: (bb * nt + i, 0)),
                  pl.BlockSpec(memory_space=pl.ANY),
                  pl.BlockSpec((d, ff), lambda bb, i: (0, 0)),
                  pl.BlockSpec((d, ff), lambda bb, i: (0, 0)),
                  pl.BlockSpec((ff, d), lambda bb, i: (0, 0)),
                  pl.BlockSpec((1, 4, d), lambda bb, i: (bb, 0, 0)),
                  pl.BlockSpec((1, d), lambda bb, i: (0, 0))],
        out_specs=row(),
        scratch_shapes=[pltpu.VMEM((TOP_K, tr, dw), ys.dtype), pltpu.SemaphoreType.DMA((TOP_K,))],
        compiler_params=_params("parallel", "parallel"),
        name="combine_shared_expert_final_norm",
    )(pos8, h2, x1, gate_rows, ys, s_gate_b, s_up_b, s_down_b, m2, final_g)


def _pad_to(a, axis, n):
    pad = [(0, 0)] * a.ndim
    pad[axis] = (0, n - a.shape[axis])
    return jnp.pad(a, pad)


def _layout(rd, wl, al, gl, ad, kvd):
    assert wl <= LANES and al <= LANES
    lay = {"rd": rd, "gl": gl, "ad": ad, "kvd": kvd}
    off = 0
    for name, width in (("q", ad), ("r", rd), ("k", rd), ("v", rd), ("ka", kvd), ("va", kvd),
                        ("xw", LANES), ("xa", LANES), ("xg", gl)):
        assert width % LANES == 0
        lay[name] = off
        off += width
    lay["ncols"] = off
    return lay


def kernel(x, c, ctx, c_ctx, w_mod, b_mod, norm1_g, norm2_g, w_in, shift_mu, k_k, k_a, r_k, decay_bias, decay_up, iclr_bias, iclr_up, gate_up, lnx_g, lnx_b, attn_sink, w_out, w_router, expert_bias, e_gate, e_up, e_down, s_gate, s_up, s_down, final_g):
    assert w_mod.shape[0] == 1, "single-layer block only"
    b, t, d = x.shape
    ctx_len = ctx.shape[1]
    rh, hd = r_k.shape[1:]
    assert hd == HEAD_DIM and b + 1 <= SUBLANES
    rd = rh * hd
    wl, al, gl = decay_up.shape[2], iclr_up.shape[2], gate_up.shape[1]
    ad = attn_sink.shape[1] * hd
    kvd = (w_in.shape[2] - (3 * rd + wl + al + gl) - ad) // 2
    ne = w_router.shape[2]
    lay = _layout(rd, wl, al, gl, ad, kvd)
    tr = 256 if ctx_len % 256 == 0 and t % 256 == 0 else CHUNK
    assert ctx_len % CHUNK == 0 and t % CHUNK == 0 and t % QBLOCK == 0 and WINDOW == QBLOCK

    c8 = _pad_to(jnp.concatenate([c, c_ctx[None]], axis=0), 0, SUBLANES)
    mod = _modulation(c8, w_mod[0], b_mod[0][None])
    mod6 = mod.reshape(SUBLANES, 6, d)
    mx, mc = mod6[:b], mod6[b]
    m1 = jnp.stack([jnp.broadcast_to(mc[None, 0:2], (b, 2, d)), mx[:, 0:2]], axis=1)
    m2 = jnp.stack([mx[:, 2], mx[:, 3], mx[:, 4], mx[:, 5]], axis=1)

    wi = w_in[0]
    o_w, o_a, o_g, o_q = 3 * rd, 3 * rd + wl, 3 * rd + wl + al, 3 * rd + wl + al + gl
    w_in_p = _bf(jnp.concatenate([
        wi[:, o_q:o_q + ad], wi[:, :3 * rd], wi[:, o_q + ad:],
        _pad_to(wi[:, o_w:o_a], 1, LANES), _pad_to(wi[:, o_a:o_g], 1, LANES), wi[:, o_g:o_q]], axis=1))
    mu = shift_mu[0]
    prm = {
        "mu_r": mu[None, :rd], "mu_k": mu[None, rd:2 * rd], "mu_v": mu[None, 2 * rd:3 * rd],
        "mu_w": _pad_to(mu[None, o_w:o_a], 1, LANES), "mu_a": _pad_to(mu[None, o_a:o_g], 1, LANES),
        "mu_g": mu[None, o_g:o_q],
        "k_k": k_k, "k_a": k_a, "r_k": r_k.reshape(1, rd),
        "decay_bias": decay_bias[0], "decay_up": _bf(_pad_to(decay_up[0], 1, LANES)),
        "iclr_bias": iclr_bias[0], "iclr_up": _bf(_pad_to(iclr_up[0], 1, LANES)),
        "gate_up": _bf(gate_up[0]),
    }

    p_all = _in_projection(ctx, x, norm1_g, m1, w_in_p, tr)

    npair = rd // LANES
    pp = max(n for n in (1, 2, 4, 8) if npair % n == 0
             and all(lay[key] % (n * LANES) == 0 for key in ("r", "k", "v")))
    rp, y0, mm, nn, bonus, gate = _wkv_prep(p_all, lay, prm, ctx_len, None, pp)
    y_f, y_r = _wkv_sweep(rp, y0, mm, nn, ctx_len)

    nf = HEAD_DIM // 4
    freqs = ROPE_BASE ** (-jnp.arange(nf, dtype=F32) / nf)
    pos = jnp.arange(t, dtype=jnp.int32)
    ang_r = (pos // GRID_W).astype(F32)[:, None] * freqs
    ang_c = (pos % GRID_W).astype(F32)[:, None] * freqs
    cos_h = jnp.concatenate([jnp.cos(ang_r)] * 2 + [jnp.cos(ang_c)] * 2, axis=1)
    sin_h = jnp.concatenate([-jnp.sin(ang_r), jnp.sin(ang_r), -jnp.sin(ang_c), jnp.sin(ang_c)], axis=1)
    cos_t = jnp.concatenate([cos_h, cos_h], axis=1)
    sin_t = jnp.concatenate([sin_h, sin_h], axis=1)
    attn = _attention(p_all, lay, cos_t, sin_t, attn_sink, ctx_len, t)

    x1, h2, logits_t = _out_projection(y_f, y_r, bonus, gate, attn, x, lnx_g, lnx_b, _bf(w_out[0]), m2,
                                       norm2_g, w_router[0], ctx_len, tr)
    idx8, gate8, rank8, counts = _router(logits_t, expert_bias[0][:, None])

    n_tok = b * t
    bm = EXPERT_ROWS
    assert (n_tok * TOP_K) % bm == 0
    n_blk = n_tok * TOP_K // bm + ne
    cnt = counts[:, 0].astype(jnp.int32)
    padded = (cnt + bm - 1) // bm * bm
    pend = jnp.cumsum(padded)
    pstart = pend - padded
    pos8 = _slot_positions(idx8, rank8, pstart.astype(F32)[:, None])
    blk_row0 = jnp.arange(n_blk, dtype=jnp.int32) * bm
    blk_expert = jnp.minimum(jnp.sum((pend[None, :] <= blk_row0[:, None]).astype(jnp.int32), axis=1), ne - 1)
    n_used = (pend[-1:] // bm).astype(jnp.int32)

    xs = _dispatch(h2.reshape(n_tok, d), pos8, pstart + cnt, padded - cnt, n_used, n_blk * bm, tr, bm)
    has_rows = cnt > 0
    eids = jnp.arange(ne, dtype=jnp.int32)
    later = jnp.where(has_rows[None, :] & (eids[None, :] > eids[:, None]), eids[None, :], ne)
    next_expert = jnp.min(later, axis=1)
    next_expert = jnp.where(next_expert < ne, next_expert, -1).astype(jnp.int32)
    expert_slot = ((jnp.cumsum(has_rows.astype(jnp.int32)) - 1) % 2).astype(jnp.int32)
    ys = _expert_ffn(xs, blk_expert, n_used, next_expert, expert_slot, e_gate[0], e_up[0], e_down[0], bm)
    return _final(h2, x1, pos8, gate8.T, ys, _bf(s_gate[0]), _bf(s_up[0]), _bf(s_down[0]),
                  m2, final_g[None], tr)
```

```python
import functools
import math

import jax
import jax.numpy as jnp
from jax import lax
from jax.experimental import pallas as pl
from jax.experimental.pallas import tpu as pltpu

GRID_W = 64
WINDOW = 128
ROPE_BASE = 10000.0
N_GROUPS = 8
TOPK_GROUPS = 4
TOP_K = 8
ROUTED_SCALE = 2.5
NORM_EPS = 1e-6
LNX_EPS = 64e-5
KK_EPS = 1e-12

LANES = 128
SUBLANES = 8
VMEM_LIMIT_BYTES = 56 * 1024 * 1024

HEAD_DIM = 64
CHUNK = 64
QBLOCK = 128
EXPERT_ROWS = 256

F32 = jnp.float32
BF16 = jnp.bfloat16
NEG_BIG = -1e30


def _bf(x):
    return x.astype(BF16)


def _dot(a, b, precision=None):
    return jnp.dot(a, b, preferred_element_type=F32, precision=precision)


def _dot_nt(a, b, precision=None):
    return lax.dot_general(a, b, (((1,), (1,)), ((), ())), preferred_element_type=F32,
                           precision=precision)


def _dot_tn(a, b):
    return lax.dot_general(a, b, (((0,), (0,)), ((), ())), preferred_element_type=F32)


def _params(*sem):
    return pltpu.CompilerParams(dimension_semantics=sem, vmem_limit_bytes=VMEM_LIMIT_BYTES)


def _mod_kernel(c_ref, w_ref, b_ref, o_ref):
    c = c_ref[...]
    s = c * jax.nn.sigmoid(c)
    hi = _bf(s)
    lo = _bf(s - hi.astype(F32))
    r = _dot(jnp.concatenate([hi, lo], axis=0), _bf(w_ref[...]))
    o_ref[...] = r[:SUBLANES] + r[SUBLANES:] + b_ref[...]


def _modulation(c8, w_mod, b_mod):
    d, n = w_mod.shape
    tn = 1024 if n % 1024 == 0 else n
    return pl.pallas_call(
        _mod_kernel,
        out_shape=jax.ShapeDtypeStruct((SUBLANES, n), F32),
        grid=(n // tn,),
        in_specs=[pl.BlockSpec((SUBLANES, d), lambda j: (0, 0)),
                  pl.BlockSpec((d, tn), lambda j: (0, j)),
                  pl.BlockSpec((1, tn), lambda j: (0, j))],
        out_specs=pl.BlockSpec((SUBLANES, tn), lambda j: (0, j)),
        compiler_params=_params("parallel"),
        name="adaln_modulation",
    )(c8, w_mod, b_mod)


def _inproj_kernel(c_ref, x_ref, g_ref, m_ref, w_ref, o_ref, *, nct):
    x = jnp.where(pl.program_id(2) < nct, c_ref[0], x_ref[0])
    ms = jnp.mean(x * x, axis=-1, keepdims=True)
    y = x * lax.rsqrt(ms + NORM_EPS) * g_ref[...]
    m = m_ref[0, 0]
    h = y * (1.0 + m[1:2]) + m[0:1]
    o_ref[0] = _dot(_bf(h), w_ref[...])


def _in_projection(ctx, x, g1, m1, w_in_p, tr):
    b, t, d = x.shape
    ctx_len = ctx.shape[1]
    ta = ctx_len + t
    ncols = w_in_p.shape[1]
    tn = ncols
    nct = ctx_len // tr
    return pl.pallas_call(
        functools.partial(_inproj_kernel, nct=nct),
        out_shape=jax.ShapeDtypeStruct((b, ta, ncols), F32),
        grid=(ncols // tn, b, ta // tr),
        in_specs=[pl.BlockSpec((1, tr, d), lambda n, bb, i: (bb, jnp.minimum(i, nct - 1), 0)),
                  pl.BlockSpec((1, tr, d), lambda n, bb, i: (bb, jnp.maximum(i - nct, 0), 0)),
                  pl.BlockSpec((1, d), lambda n, bb, i: (0, 0)),
                  pl.BlockSpec((1, 1, 2, d), lambda n, bb, i: (bb, jnp.where(i >= nct, 1, 0), 0, 0)),
                  pl.BlockSpec((d, tn), lambda n, bb, i: (0, n), pipeline_mode=pl.Buffered(1))],
        out_specs=pl.BlockSpec((1, tr, tn), lambda n, bb, i: (bb, i, n)),
        compiler_params=_params("parallel", "parallel", "parallel"),
        name="norm_modulate_inproj",
    )(ctx, x, g1, m1, w_in_p)


def _wkv_prep_kernel(r_c, r_p, r_n, k_c, k_p, k_n, v_c, v_p, v_n, w_c, w_p, w_n, a_c, a_p, a_n,
                     g_c, g_p, g_n, mu_r, mu_k, mu_v, mu_w, mu_a, mu_g, kk_ref, ka_ref, rk_ref,
                     db_ref, du_ref, ib_ref, iu_ref, gu_ref,
                     rp_out, y0_out, m_out, n_out, bonus_out, gate_out,
                     *, nctx_steps, n_steps, prec_t):
    L = CHUNK
    rb = r_c.shape[1]
    halves = [slice(h * L, (h + 1) * L) for h in range(rb // L)]
    s = pl.program_id(1)
    has_prev = jnp.logical_and(s != 0, s != nctx_steps)
    has_next = jnp.logical_and(s != nctx_steps - 1, s != n_steps - 1)

    def shifted(cur, prv, nxt, mu):
        x = cur[0]
        rows = lax.broadcasted_iota(jnp.int32, (rb, 1), 0)
        pr = jnp.where(has_prev, prv[0, SUBLANES - 1:SUBLANES, :], 0.0)
        nx = jnp.where(has_next, nxt[0, 0:1, :], 0.0)
        up = jnp.where(rows == 0, pr, pltpu.roll(x, 1, 0))
        dn = jnp.where(rows == rb - 1, nx, pltpu.roll(x, rb - 1, 0))
        return x + mu[...] * (0.5 * (up + dn) - x)

    r = shifted(r_c, r_p, r_n, mu_r)
    k = shifted(k_c, k_p, k_n, mu_k)
    v = shifted(v_c, v_p, v_n, mu_v)
    xw = shifted(w_c, w_p, w_n, mu_w)
    xa = shifted(a_c, a_p, a_n, mu_a)
    xg = shifted(g_c, g_p, g_n, mu_g)

    lane = lax.broadcasted_iota(jnp.int32, (1, LANES), 1)
    m0 = lane < HEAD_DIM

    def head_sum(x):
        s0 = jnp.sum(jnp.where(m0, x, 0.0), axis=-1, keepdims=True)
        s1 = jnp.sum(jnp.where(m0, 0.0, x), axis=-1, keepdims=True)
        return jnp.where(m0, s0, s1)

    def stack2(x):
        return jnp.concatenate([jnp.where(m0, x, 0.0), jnp.where(m0, 0.0, x)], axis=0)

    tw = _bf(jnp.tanh(xw))
    xab = _bf(xa)
    lw_all = [(-math.exp(-0.5)) * jax.nn.sigmoid(db_ref[d:d + 1, :] + _dot(tw, du_ref[d]))
              for d in range(2)]
    sa_all = [jax.nn.sigmoid(ib_ref[d:d + 1, :] + _dot(xab, iu_ref[d])) for d in range(2)]
    gate_out[0] = _dot(_bf(jax.nn.sigmoid(xg)), gu_ref[...])

    ri = lax.broadcasted_iota(jnp.int32, (L, L), 0)
    ci = lax.broadcasted_iota(jnp.int32, (L, L), 1)
    ri2 = lax.broadcasted_iota(jnp.int32, (2 * L, 2 * L), 0)
    ci2 = lax.broadcasted_iota(jnp.int32, (2 * L, 2 * L), 1)
    rt = jnp.bitwise_and(ri2, L - 1)
    ct = jnp.bitwise_and(ci2, L - 1)
    eye = ri2 == ci2
    n_sq = int(math.log2(L)) - 1

    tri = [_bf((ci <= ri).astype(F32)), _bf((ci >= ri).astype(F32))]
    strict = [ct < rt, ct > rt]
    incl = [ct <= rt, ct >= rt]
    lw_c = [[lw_all[d][hs] for hs in halves] for d in range(2)]

    def scan_sum(tmask, lw):
        hi = _bf(lw)
        lo = _bf(lw - hi.astype(F32))
        return _dot(tmask, hi) + _dot(tmask, lo)

    cum = [[scan_sum(tri[d], lw) for lw in lw_c[d]] for d in range(2)]
    cend = [[jnp.sum(lw, axis=0, keepdims=True) for lw in lw_c[d]] for d in range(2)]
    e_in = [[jnp.exp(c) for c in cum[d]] for d in range(2)]
    e_ex = [[jnp.exp(c - lw) for c, lw in zip(cum[d], lw_c[d])] for d in range(2)]
    e_neg = [[jnp.exp(-c) for c in cum[d]] for d in range(2)]
    e_end = [[jnp.exp(ce - c) for c, ce in zip(cum[d], cend[d])] for d in range(2)]
    p_end = [[jnp.exp(ce) for ce in cend[d]] for d in range(2)]
    kd_all = [k * (1.0 + (sa_all[d] - 1.0) * ka_ref[...]) for d in range(2)]

    npp = r.shape[1] // LANES
    chains = [(h, p, d) for h in range(len(halves)) for p in range(npp) for d in range(2)]
    sls = [slice(p * LANES, (p + 1) * LANES) for p in range(npp)]
    kk_p = []
    for p in range(npp):
        kx = k[:, sls[p]] * kk_ref[:, sls[p]]
        kk_p.append(kx * lax.rsqrt(head_sum(kx * kx) + KK_EPS))
        ksum = kd_all[0][:, sls[p]] + kd_all[1][:, sls[p]]
        bonus_out[0, :, sls[p]] = head_sum(r[:, sls[p]] * ksum * rk_ref[:, sls[p]]) * v[:, sls[p]]
    vs_c = {(h, p): stack2(v[halves[h], sls[p]]) for h in range(len(halves)) for p in range(npp)}
    vsb_c = {key: _bf(val) for key, val in vs_c.items()}

    a_s, r_s, bh_s, kh_s, g = [], [], [], [], []
    for h, p, d in chains:
        hs, sl = halves[h], sls[p]
        kd = kd_all[d][hs, sl]
        kk = kk_p[p][hs]
        bvec = kk * sa_all[d][hs, sl]
        a_s.append(stack2(-kk * e_ex[d][h][:, sl]))
        r_s.append(stack2(r[hs, sl] * e_in[d][h][:, sl]))
        b_s = stack2(bvec * e_neg[d][h][:, sl])
        k_s = stack2(kd * e_neg[d][h][:, sl])
        bh_s.append(stack2(bvec * e_end[d][h][:, sl]))
        kh_s.append(stack2(kd * e_end[d][h][:, sl]))
        g.append(_dot_nt(_bf(jnp.concatenate([a_s[-1], r_s[-1]], axis=0)),
                         _bf(jnp.concatenate([b_s, k_s], axis=0))))
    a_ab = [jnp.where(strict[d], g[i][:2 * L, :2 * L], 0.0) for i, (h, p, d) in enumerate(chains)]
    a_ak = [jnp.where(strict[d], g[i][:2 * L, 2 * L:], 0.0) for i, (h, p, d) in enumerate(chains)]
    a_rb = [jnp.where(incl[d], g[i][2 * L:, :2 * L], 0.0) for i, (h, p, d) in enumerate(chains)]
    a_rk = [jnp.where(incl[d], g[i][2 * L:, 2 * L:], 0.0) for i, (h, p, d) in enumerate(chains)]

    ident = jnp.where(eye, 1.0, 0.0)
    tmat = [ident + a for a in a_ab]
    apow = a_ab
    for _ in range(n_sq):
        if prec_t is None:
            apow = [_dot(_bf(a), _bf(a)) for a in apow]
            tmat = [t + _dot(_bf(a), _bf(t)) for a, t in zip(apow, tmat)]
        else:
            apow = [_dot(a, a, precision=prec_t) for a in apow]
            tmat = [t + _dot(a, t, precision=prec_t) for a, t in zip(apow, tmat)]

    xy = [_dot(_bf(jnp.concatenate([a_ak[i], a_rk[i]], axis=0)), vsb_c[h, p])
          for i, (h, p, d) in enumerate(chains)]
    x1 = [z[:2 * L] for z in xy]
    y0v = [z[2 * L:] for z in xy]
    tu = [_dot(_bf(tmat[i]), _bf(jnp.concatenate([a_s[i], x1[i]], axis=1)))
          for i in range(len(chains))]
    ry = [_dot(_bf(a_rb[i]), _bf(tu[i])) for i in range(len(chains))]
    def fold(z):
        h = z.shape[0] // 2
        return z[:h] + z[h:]

    for i, (h, p, d) in enumerate(chains):
        rp_out[d, h, 0, p] = _bf(fold(r_s[i] + ry[i][:, :LANES]))
        y0_out[d, h, 0, p] = fold(ry[i][:, LANES:] + y0v[i])
    for i, (h, p, d) in enumerate(chains):
        m_out[d, h, 0, p] = _bf(fold(jnp.where(eye, p_end[d][h][:, sls[p]], 0.0)
                                     + _dot_tn(_bf(tu[i][:, :LANES]), _bf(bh_s[i]))))
    for i, (h, p, d) in enumerate(chains):
        n_out[d, h, 0, p] = fold(_dot_tn(_bf(jnp.concatenate([tu[i][:, LANES:], vs_c[h, p]], axis=0)),
                                         _bf(jnp.concatenate([bh_s[i], kh_s[i]], axis=0))))


def _wkv_prep(p_all, lay, prm, ctx_len, prec_t, pp):
    b, ta, _ = p_all.shape
    L = CHUNK
    nc = ta // L
    nctx = ctx_len // L
    npair = lay["rd"] // LANES
    gl = lay["gl"]
    rows8 = ta // SUBLANES
    cps = 2 if nc % 2 == 0 and nctx % 2 == 0 else 1
    rbk = cps * L

    def cur(colfn, width=LANES):
        return pl.BlockSpec((1, rbk, width), lambda bb, c, p: (bb, c, colfn(p)))

    def prv(colfn, width=LANES):
        return pl.BlockSpec((1, SUBLANES, width),
                            lambda bb, c, p: (bb, jnp.maximum(c * (rbk // SUBLANES) - 1, 0), colfn(p)))

    def nxt(colfn, width=LANES):
        return pl.BlockSpec((1, SUBLANES, width),
                            lambda bb, c, p: (bb, jnp.minimum((c + 1) * (rbk // SUBLANES), rows8 - 1), colfn(p)))

    def trio(colfn, width=LANES):
        return [cur(colfn, width), prv(colfn, width), nxt(colfn, width)]

    pw = pp * LANES
    assert npair % pp == 0 and all(lay[n] % pw == 0 for n in ("r", "k", "v"))
    rb, kb, vb = lay["r"] // pw, lay["k"] // pw, lay["v"] // pw
    wb, ab = lay["xw"] // LANES, lay["xa"] // LANES
    assert lay["xg"] % gl == 0
    gb = lay["xg"] // gl

    def vec(width=pw):
        return pl.BlockSpec((1, width), lambda bb, c, p: (0, p))

    def vec0(width=LANES):
        return pl.BlockSpec((1, width), lambda bb, c, p: (0, 0))

    in_specs = (trio(lambda p: rb + p, pw) + trio(lambda p: kb + p, pw) + trio(lambda p: vb + p, pw)
                + trio(lambda p: wb) + trio(lambda p: ab) + trio(lambda p: gb, gl)
                + [vec(), vec(), vec(), vec0(), vec0(), vec0(gl),
                   vec(), vec(), vec(),
                   pl.BlockSpec((2, pw), lambda bb, c, p: (0, p)),
                   pl.BlockSpec((2, LANES, pw), lambda bb, c, p: (0, 0, p)),
                   pl.BlockSpec((2, pw), lambda bb, c, p: (0, p)),
                   pl.BlockSpec((2, LANES, pw), lambda bb, c, p: (0, 0, p)),
                   pl.BlockSpec((gl, pw), lambda bb, c, p: (0, p))])

    def opspec(rows):
        return pl.BlockSpec((2, cps, 1, pp, rows, LANES), lambda bb, c, p: (0, c, bb, p, 0, 0))

    out_specs = [opspec(L), opspec(L), opspec(HEAD_DIM), opspec(HEAD_DIM),
                 pl.BlockSpec((1, rbk, pw), lambda bb, c, p: (bb, c, p)),
                 pl.BlockSpec((1, rbk, pw), lambda bb, c, p: (bb, c, p))]
    out_shape = [jax.ShapeDtypeStruct((2, nc, b, npair, L, LANES), BF16),
                 jax.ShapeDtypeStruct((2, nc, b, npair, L, LANES), F32),
                 jax.ShapeDtypeStruct((2, nc, b, npair, HEAD_DIM, LANES), BF16),
                 jax.ShapeDtypeStruct((2, nc, b, npair, HEAD_DIM, LANES), F32),
                 jax.ShapeDtypeStruct((b, ta, lay["rd"]), F32),
                 jax.ShapeDtypeStruct((b, ta, lay["rd"]), F32)]
    kern = functools.partial(_wkv_prep_kernel, nctx_steps=nctx // cps, n_steps=nc // cps, prec_t=prec_t)
    args = [p_all] * 18 + [prm["mu_r"], prm["mu_k"], prm["mu_v"], prm["mu_w"], prm["mu_a"], prm["mu_g"],
                           prm["k_k"], prm["k_a"], prm["r_k"], prm["decay_bias"], prm["decay_up"],
                           prm["iclr_bias"], prm["iclr_up"], prm["gate_up"]]
    return pl.pallas_call(
        kern, out_shape=out_shape, grid=(b, nc // cps, npair // pp), in_specs=in_specs, out_specs=out_specs,
        compiler_params=_params("parallel", "parallel", "parallel"),
        name="wkv_chunk_operators",
    )(*args)


def _wkv_seq_kernel(rpf, y0f, mf, nf, rpr, y0r, mr, nr, yf_ref, yr_ref, s_ref, *, nb, npair):
    L = CHUNK
    cps = rpf.shape[1]

    @pl.when(pl.program_id(0) == 0)
    def _():
        s_ref[...] = jnp.zeros_like(s_ref)

    m0 = lax.broadcasted_iota(jnp.int32, (1, LANES), 1) < HEAD_DIM

    def unfold(z):
        zero = jnp.zeros_like(z)
        return jnp.concatenate([jnp.where(m0, z, zero), jnp.where(m0, zero, z)], axis=0)

    for turn in range(cps):
        for d, (rp_ref, y0_ref, m_ref, n_ref, y_ref) in enumerate(((rpf, y0f, mf, nf, yf_ref),
                                                                   (rpr, y0r, mr, nr, yr_ref))):
            h = turn if d == 0 else cps - 1 - turn
            for bb in range(nb):
                for p in range(npair):
                    i = (d * nb + bb) * npair + p
                    s0 = _bf(s_ref[i])
                    ys = _dot_nt(unfold(rp_ref[0, h, bb, p]), s0)
                    y_ref[bb, h * L:(h + 1) * L, p * LANES:(p + 1) * LANES] = (
                        ys[:L] + ys[L:] + y0_ref[0, h, bb, p])
                    s_ref[i] = _dot(s0, unfold(m_ref[0, h, bb, p])) + unfold(n_ref[0, h, bb, p])


def _wkv_sweep(rp, y0, mm, nn, ctx_len):
    _, nc, b, npair, _, _ = rp.shape
    L = CHUNK
    nctx = ctx_len // L
    cps = 2 if nc % 2 == 0 and nctx % 2 == 0 else 1
    ng, ngctx = nc // cps, nctx // cps

    def group_of(d, s):
        return s if d == 0 else jnp.where(s < ngctx, ngctx - 1 - s, ng - 1 - (s - ngctx))

    def opspec(d, rows):
        return pl.BlockSpec((1, cps, b, npair, rows, LANES), lambda s: (d, group_of(d, s), 0, 0, 0, 0))

    def yspec(d):
        return pl.BlockSpec((b, cps * L, npair * LANES), lambda s: (0, group_of(d, s), 0))

    rows = (L, L, HEAD_DIM, HEAD_DIM)
    y_shape = jax.ShapeDtypeStruct((b, nc * L, npair * LANES), F32)
    return pl.pallas_call(
        functools.partial(_wkv_seq_kernel, nb=b, npair=npair),
        out_shape=[y_shape, y_shape],
        grid=(ng,),
        in_specs=[opspec(d, r) for d in range(2) for r in rows],
        out_specs=[yspec(0), yspec(1)],
        scratch_shapes=[pltpu.VMEM((2 * b * npair, LANES, LANES), F32)],
        compiler_params=_params("arbitrary"),
        name="wkv_state_sweep",
    )(rp, y0, mm, nn, rp, y0, mm, nn)


def _attn_kernel(q_ref, kp_ref, kc_ref, kn_ref, vp_ref, vc_ref, vn_ref, kx_ref, vx_ref,
                 cq_ref, sq_ref, cp_ref, sp_ref, cn_ref, sn_ref, sink_ref, o_ref,
                 *, n_blocks, kv_heads, group):
    n = pl.program_id(1)
    qb = QBLOCK

    def rope(x, cos, sin):
        w = x.shape[1]
        reps = w // LANES
        cs = jnp.concatenate([cos] * reps, axis=1) if reps > 1 else cos
        sn = jnp.concatenate([sin] * reps, axis=1) if reps > 1 else sin
        lane = lax.broadcasted_iota(jnp.int32, (1, w), 1)
        sw = jnp.where(jnp.bitwise_and(lane, 16) == 0, pltpu.roll(x, w - 16, 1), pltpu.roll(x, 16, 1))
        return x * cs + sw * sn

    q = rope(q_ref[0], cq_ref[...], sq_ref[...]) * (HEAD_DIM ** -0.5)
    kwin = jnp.concatenate([rope(kp_ref[0], cp_ref[...], sp_ref[...]),
                            rope(kc_ref[0], cq_ref[...], sq_ref[...]),
                            rope(kn_ref[0], cn_ref[...], sn_ref[...]),
                            kx_ref[0]], axis=0)
    vwin = jnp.concatenate([vp_ref[0], vc_ref[0], vn_ref[0], vx_ref[0]], axis=0)
    nkeys = kwin.shape[0]

    rows = group * qb
    qi = jnp.bitwise_and(lax.broadcasted_iota(jnp.int32, (rows, nkeys), 0), qb - 1)
    s_i = lax.broadcasted_iota(jnp.int32, (rows, nkeys), 1)
    blk = s_i // qb + (n - 1)
    in_win = (s_i >= qi) & (s_i <= qi + 2 * WINDOW) & (blk >= 0) & (blk < n_blocks)
    valid = in_win | (s_i >= 3 * qb)
    rgrp = lax.broadcasted_iota(jnp.int32, (rows, 1), 0) // qb

    scores, sinks = [], []
    for kh in range(kv_heads):
        kk_ = _bf(kwin[:, kh * HEAD_DIM:(kh + 1) * HEAD_DIM])
        qs = jnp.concatenate([q[:, (kh * group + g) * HEAD_DIM:(kh * group + g + 1) * HEAD_DIM]
                              for g in range(group)], axis=0)
        sk = jnp.zeros((rows, 1), F32)
        for g in range(group):
            h = kh * group + g
            sk = jnp.where(rgrp == g, sink_ref[0:1, h:h + 1], sk)
        sinks.append(sk)
        scores.append(jnp.where(valid, _dot_nt(_bf(qs), kk_), NEG_BIG))
    probs, denoms = [], []
    for s, sk in zip(scores, sinks):
        m = jnp.maximum(jnp.max(s, axis=-1, keepdims=True), sk)
        e = jnp.exp(s - m)
        denoms.append(jnp.sum(e, axis=-1, keepdims=True) + jnp.exp(sk - m))
        probs.append(_bf(e))
    pieces = []
    for kh in range(kv_heads):
        vv_ = _bf(vwin[:, kh * HEAD_DIM:(kh + 1) * HEAD_DIM])
        o = _dot(probs[kh], vv_) / denoms[kh]
        for g in range(group):
            pieces.append(o[g * qb:(g + 1) * qb])
    o_ref[0] = _bf(jnp.concatenate(pieces, axis=1))


def _attention(p_all, lay, cos_t, sin_t, sink, ctx_len, seq):
    b = p_all.shape[0]
    qb = QBLOCK
    nb = seq // qb
    cb = ctx_len // qb
    ad, kvd = lay["ad"], lay["kvd"]
    kv_heads = kvd // HEAD_DIM
    group = (ad // HEAD_DIM) // kv_heads
    assert lay["q"] % ad == 0 and lay["ka"] % kvd == 0 and lay["va"] % kvd == 0
    qc, kc, vc = lay["q"] // ad, lay["ka"] // kvd, lay["va"] // kvd

    def clampb(n, off):
        return jnp.clip(n + off, 0, nb - 1)

    def win(col, off):
        return pl.BlockSpec((1, qb, kvd), lambda bb, n: (bb, cb + clampb(n, off), col))

    def tab(off):
        return pl.BlockSpec((qb, LANES), lambda bb, n: (clampb(n, off), 0))

    assert ctx_len % qb == 0
    ctxspec = lambda col: pl.BlockSpec((1, ctx_len, kvd), lambda bb, n: (bb, 0, col))
    in_specs = [pl.BlockSpec((1, qb, ad), lambda bb, n: (bb, cb + n, qc)),
                win(kc, -1), win(kc, 0), win(kc, 1), win(vc, -1), win(vc, 0), win(vc, 1),
                ctxspec(kc), ctxspec(vc),
                tab(0), tab(0), tab(-1), tab(-1), tab(1), tab(1),
                pl.BlockSpec((1, sink.shape[1]), lambda bb, n: (0, 0))]
    return pl.pallas_call(
        functools.partial(_attn_kernel, n_blocks=nb, kv_heads=kv_heads, group=group),
        out_shape=jax.ShapeDtypeStruct((b, seq, ad), BF16),
        grid=(b, nb),
        in_specs=in_specs,
        out_specs=pl.BlockSpec((1, qb, ad), lambda bb, n: (bb, n, 0)),
        compiler_params=_params("parallel", "parallel"),
        name="window_ctx_attention",
    )(p_all, p_all, p_all, p_all, p_all, p_all, p_all, p_all, p_all,
      cos_t, sin_t, cos_t, sin_t, cos_t, sin_t, sink)


def _outproj_kernel(yf_ref, yr_ref, bon_ref, gat_ref, att_ref, x_ref, lg_ref, lb_ref, w_ref,
                    m_ref, g2_ref, wr_ref, x1_ref, h2_ref, lt_ref):
    ysum = yf_ref[0] + yr_ref[0]
    ri = lax.broadcasted_iota(jnp.int32, (LANES, LANES), 0) // HEAD_DIM
    ci = lax.broadcasted_iota(jnp.int32, (LANES, LANES), 1) // HEAD_DIM
    hm = _bf(jnp.where(ri == ci, 1.0 / HEAD_DIM, 0.0))

    def head_mean(z):
        hi = _bf(z)
        lo = _bf(z - hi.astype(F32))
        return _dot(hi, hm) + _dot(lo, hm)

    ybs = [ysum[:, j * LANES:(j + 1) * LANES] for j in range(ysum.shape[1] // LANES)]
    devs = [yb - mu for yb, mu in zip(ybs, [head_mean(yb) for yb in ybs])]
    blocks = [dev * lax.rsqrt(var + LNX_EPS)
              for dev, var in zip(devs, [head_mean(dev * dev) for dev in devs])]
    yn = jnp.concatenate(blocks, axis=1) if len(blocks) > 1 else blocks[0]
    rw = (yn * lg_ref[...] + lb_ref[...] + bon_ref[0]) * gat_ref[0]
    mix = jnp.concatenate([_bf(rw), att_ref[0]], axis=1)
    m = m_ref[0]
    x1 = x_ref[0] + m[0:1] * _dot(mix, w_ref[...])
    x1_ref[0] = x1
    ms = jnp.mean(x1 * x1, axis=-1, keepdims=True)
    h2 = x1 * lax.rsqrt(ms + NORM_EPS) * g2_ref[...] * (1.0 + m[2:3]) + m[1:2]
    h2_ref[0] = h2
    hi = _bf(h2)
    lo = _bf(h2 - hi.astype(F32))
    wr = wr_ref[...]
    wr_hi = _bf(wr)
    wr_lo = _bf(wr - wr_hi.astype(F32))
    lg = _dot(hi, wr_hi) + (_dot(hi, wr_lo) + _dot(lo, wr_hi))
    lt_ref[...] = lg.T[:lt_ref.shape[0]]


def _out_projection(y_f, y_r, bonus, gate, attn, x, lnx_g, lnx_b, w_out_b, m2, g2, w_router, ctx_len, tr):
    b, t, d = x.shape
    rd = y_f.shape[2]
    ad = attn.shape[2]
    ne = w_router.shape[1]
    assert ne <= LANES
    wr = _pad_to(w_router, 1, LANES)
    co = ctx_len // tr
    nt = t // tr
    return pl.pallas_call(
        _outproj_kernel,
        out_shape=[jax.ShapeDtypeStruct((b, t, d), F32),
                   jax.ShapeDtypeStruct((b, t, d), F32),
                   jax.ShapeDtypeStruct((ne, b * t), F32)],
        grid=(b, nt),
        in_specs=[pl.BlockSpec((1, tr, rd), lambda bb, i: (bb, co + i, 0)),
                  pl.BlockSpec((1, tr, rd), lambda bb, i: (bb, co + i, 0)),
                  pl.BlockSpec((1, tr, rd), lambda bb, i: (bb, co + i, 0)),
                  pl.BlockSpec((1, tr, rd), lambda bb, i: (bb, co + i, 0)),
                  pl.BlockSpec((1, tr, ad), lambda bb, i: (bb, i, 0)),
                  pl.BlockSpec((1, tr, d), lambda bb, i: (bb, i, 0)),
                  pl.BlockSpec((1, rd), lambda bb, i: (0, 0)),
                  pl.BlockSpec((1, rd), lambda bb, i: (0, 0)),
                  pl.BlockSpec((rd + ad, d), lambda bb, i: (0, 0)),
                  pl.BlockSpec((1, 4, d), lambda bb, i: (bb, 0, 0)),
                  pl.BlockSpec((1, d), lambda bb, i: (0, 0)),
                  pl.BlockSpec((d, LANES), lambda bb, i: (0, 0))],
        out_specs=[pl.BlockSpec((1, tr, d), lambda bb, i: (bb, i, 0)),
                   pl.BlockSpec((1, tr, d), lambda bb, i: (bb, i, 0)),
                   pl.BlockSpec((ne, tr), lambda bb, i: (0, bb * nt + i))],
        compiler_params=_params("parallel", "parallel"),
        name="rwkv_out_outproj_norm2_router",
    )(y_f, y_r, bonus, gate, attn, x, lnx_g, lnx_b, w_out_b, m2, g2, wr)


def _router_kernel(l_ref, b_ref, idx_ref, gate_ref, rank_ref, cnt_ref, carry_ref):
    @pl.when(pl.program_id(0) == 0)
    def _():
        carry_ref[...] = jnp.zeros_like(carry_ref)

    sc = jax.nn.sigmoid(l_ref[...])
    ne, tn = sc.shape
    eg = ne // N_GROUPS
    bz = sc + b_ref[...]
    g3 = bz.reshape(N_GROUPS, eg, tn)
    io = lax.broadcasted_iota(jnp.int32, (N_GROUPS, eg, tn), 1).astype(F32)
    m1 = jnp.max(g3, axis=1, keepdims=True)
    i1 = jnp.min(jnp.where(g3 == m1, io, float(eg)), axis=1, keepdims=True)
    m2 = jnp.max(jnp.where(io == i1, -jnp.inf, g3), axis=1, keepdims=True)
    gs = (m1 + m2).reshape(N_GROUPS, tn)

    def topk_hits(vals, k):
        n = vals.shape[0]
        idx = lax.broadcasted_iota(jnp.int32, vals.shape, 0).astype(F32)
        hits = []
        cur = vals
        for _ in range(k):
            mx = jnp.max(cur, axis=0, keepdims=True)
            ix = jnp.min(jnp.where(cur == mx, idx, float(n)), axis=0, keepdims=True)
            hit = idx == ix
            hits.append(hit)
            cur = jnp.where(hit, -jnp.inf, cur)
        return hits

    gsel = sum(jnp.where(h, 1.0, 0.0) for h in topk_hits(gs, TOPK_GROUPS))
    masked = jnp.where(gsel.reshape(N_GROUPS, 1, tn) > 0.0, g3, -jnp.inf).reshape(ne, tn)
    hits = topk_hits(masked, TOP_K)
    chosen = sum(jnp.where(h, 1.0, 0.0) for h in hits)
    denom = jnp.sum(chosen * sc, axis=0, keepdims=True)

    si = lax.broadcasted_iota(jnp.int32, (tn, tn), 0)
    ti = lax.broadcasted_iota(jnp.int32, (tn, tn), 1)
    before = _bf(jnp.where(si < ti, 1.0, 0.0))
    rank_full = _dot(_bf(chosen), before) + carry_ref[...]
    carry_ref[...] += jnp.sum(chosen, axis=1, keepdims=True)
    cnt_ref[...] = carry_ref[...]

    eidx = lax.broadcasted_iota(jnp.int32, (ne, tn), 0).astype(F32)
    for j, hit in enumerate(hits):
        idx_ref[j:j + 1, :] = jnp.sum(jnp.where(hit, eidx, 0.0), axis=0, keepdims=True).astype(jnp.int32)
        gate_ref[j:j + 1, :] = (jnp.sum(jnp.where(hit, sc, 0.0), axis=0, keepdims=True)
                                / denom * ROUTED_SCALE)
        rank_ref[j:j + 1, :] = jnp.sum(jnp.where(hit, rank_full, 0.0), axis=0,
                                       keepdims=True).astype(jnp.int32)


def _router(logits_t, bias_col):
    ne, n = logits_t.shape
    tn = 1024 if n % 1024 == 0 else n
    choice = lambda: pl.BlockSpec((TOP_K, tn), lambda i: (0, i))
    return pl.pallas_call(
        _router_kernel,
        out_shape=[jax.ShapeDtypeStruct((TOP_K, n), jnp.int32),
                   jax.ShapeDtypeStruct((TOP_K, n), F32),
                   jax.ShapeDtypeStruct((TOP_K, n), jnp.int32),
                   jax.ShapeDtypeStruct((ne, 1), F32)],
        grid=(n // tn,),
        in_specs=[pl.BlockSpec((ne, tn), lambda i: (0, i)),
                  pl.BlockSpec((ne, 1), lambda i: (0, 0))],
        out_specs=[choice(), choice(), choice(), pl.BlockSpec((ne, 1), lambda i: (0, 0))],
        scratch_shapes=[pltpu.VMEM((ne, 1), F32)],
        compiler_params=_params("arbitrary"),
        name="group_topk_router",
    )(logits_t, bias_col)


def _slots_kernel(idx_ref, rank_ref, ps_ref, o_ref):
    ne, tn = ps_ref.shape[0], idx_ref.shape[1]
    eidx = lax.broadcasted_iota(jnp.int32, (ne, tn), 0)
    for j in range(TOP_K):
        start = jnp.sum(jnp.where(eidx == idx_ref[j:j + 1, :], ps_ref[...], 0.0), axis=0, keepdims=True)
        o_ref[j:j + 1, :] = start.astype(jnp.int32) + rank_ref[j:j + 1, :]


def _slot_positions(idx8, rank8, pstart_col):
    _, n = idx8.shape
    ne = pstart_col.shape[0]
    tn = 1024 if n % 1024 == 0 else n
    choice = lambda: pl.BlockSpec((TOP_K, tn), lambda i: (0, i))
    return pl.pallas_call(
        _slots_kernel,
        out_shape=jax.ShapeDtypeStruct((TOP_K, n), jnp.int32),
        grid=(n // tn,),
        in_specs=[choice(), choice(), pl.BlockSpec((ne, 1), lambda i: (0, 0))],
        out_specs=choice(),
        compiler_params=_params("parallel"),
        name="expert_slot_positions",
    )(idx8, rank8, pstart_col)


def _dispatch_kernel(pad_start_ref, npad_ref, nused_ref, pos_ref, h_ref, xs_ref, zero_ref, sems,
                     *, n_experts):
    tt = h_ref.shape[0]
    bm = zero_ref.shape[0]

    sem = sems.at[0]

    def issue(t, carry):
        for k in range(TOP_K):
            pltpu.make_async_copy(h_ref.at[pl.ds(t, 1)], xs_ref.at[pl.ds(pos_ref[k, t], 1)],
                                  sems.at[k]).start()
        return carry

    lax.fori_loop(0, tt, issue, 0, unroll=4)
    for k in range(TOP_K):
        pltpu.make_async_copy(h_ref, xs_ref.at[pl.ds(0, tt)], sems.at[k]).wait()

    @pl.when(pl.program_id(0) == pl.num_programs(0) - 1)
    def _():
        zero_ref[...] = jnp.zeros_like(zero_ref)

        zero_row_src = zero_ref.at[pl.ds(0, 1)]

        def per_expert(e, carry):
            def zero_row(i, c2):
                pltpu.make_async_copy(zero_row_src, xs_ref.at[pl.ds(pad_start_ref[e] + i, 1)], sem).start()
                return c2

            def zero_wait(i, c2):
                pltpu.make_async_copy(zero_row_src, xs_ref.at[pl.ds(0, 1)], sem).wait()
                return c2

            lax.fori_loop(0, npad_ref[e], zero_row, 0)
            lax.fori_loop(0, npad_ref[e], zero_wait, 0)
            return carry

        lax.fori_loop(0, n_experts, per_expert, 0)

        def zero_block(j, carry):
            blk = pltpu.make_async_copy(zero_ref, xs_ref.at[pl.ds(j * bm, bm)], sem)
            blk.start()
            blk.wait()
            return carry

        lax.fori_loop(nused_ref[0], xs_ref.shape[0] // bm, zero_block, 0)


def _dispatch(h2, pos8, pad_start, npad, n_used, n_slots, tt, bm):
    n, d = h2.shape
    ne = pad_start.shape[0]
    grid_spec = pltpu.PrefetchScalarGridSpec(
        num_scalar_prefetch=3,
        grid=(n // tt,),
        in_specs=[pl.BlockSpec((TOP_K, tt), lambda i, ps, npd, nu: (0, i), memory_space=pltpu.SMEM),
                  pl.BlockSpec((tt, d), lambda i, ps, npd, nu: (i, 0))],
        out_specs=pl.BlockSpec(memory_space=pl.ANY),
        scratch_shapes=[pltpu.VMEM((bm, d), h2.dtype), pltpu.SemaphoreType.DMA((TOP_K,))])
    return pl.pallas_call(
        functools.partial(_dispatch_kernel, n_experts=ne),
        out_shape=jax.ShapeDtypeStruct((n_slots, d), h2.dtype),
        grid_spec=grid_spec,
        compiler_params=_params("arbitrary"),
        name="expert_dispatch",
    )(pad_start, npad, n_used, pos8, h2)


def _expert_ffn_kernel(be_ref, nu_ref, nx_ref, sl_ref, x_ref, wg_hbm, wu_hbm, wd_hbm, y_ref,
                       wgf, wuf, wdf, wgb, wub, wdb, sems):
    j = pl.program_id(0)
    e = be_ref[j]
    used = j < nu_ref[0]
    fresh = jnp.logical_or(j == 0, e != be_ref[jnp.maximum(j - 1, 0)])

    def weight_copies(expert, slot):
        return [pltpu.make_async_copy(src.at[expert], dst.at[slot], sems.at[slot])
                for src, dst in ((wg_hbm, wgf), (wu_hbm, wuf), (wd_hbm, wdf))]

    @pl.when(j == 0)
    def _():
        for cp in weight_copies(e, sl_ref[e]):
            cp.start()

    @pl.when(jnp.logical_and(used, fresh))
    def _():
        slot = sl_ref[e]
        for cp in weight_copies(e, slot):
            cp.wait()
        nxt = nx_ref[e]

        @pl.when(nxt >= 0)
        def _():
            for cp in weight_copies(nxt, 1 - slot):
                cp.start()

        wgb[...] = _bf(wgf[slot])
        wub[...] = _bf(wuf[slot])
        wdb[...] = _bf(wdf[slot])

    @pl.when(used)
    def _():
        x = _bf(x_ref[...])
        a = _dot(x, wgb[...])
        u = _dot(x, wub[...])
        y_ref[...] = _dot(_bf(a * jax.nn.sigmoid(a) * u), wdb[...])

    @pl.when(jnp.logical_not(used))
    def _():
        y_ref[...] = jnp.zeros_like(y_ref)


def _expert_ffn(xs, blk_expert, n_used, next_expert, expert_slot, e_gate, e_up, e_down, bm):
    n_slots, dw = xs.shape
    ne, d, ff = e_gate.shape
    assert dw == d
    grid_spec = pltpu.PrefetchScalarGridSpec(
        num_scalar_prefetch=4,
        grid=(n_slots // bm,),
        in_specs=[pl.BlockSpec((bm, dw), lambda j, be, nu, nx, sl: (jnp.minimum(j, nu[0] - 1), 0)),
                  pl.BlockSpec(memory_space=pl.ANY),
                  pl.BlockSpec(memory_space=pl.ANY),
                  pl.BlockSpec(memory_space=pl.ANY)],
        out_specs=pl.BlockSpec((bm, dw), lambda j, be, nu, nx, sl: (j, 0)),
        scratch_shapes=[pltpu.VMEM((2, d, ff), F32), pltpu.VMEM((2, d, ff), F32), pltpu.VMEM((2, ff, d), F32),
                        pltpu.VMEM((d, ff), BF16), pltpu.VMEM((d, ff), BF16), pltpu.VMEM((ff, d), BF16),
                        pltpu.SemaphoreType.DMA((2,))])
    return pl.pallas_call(
        _expert_ffn_kernel,
        out_shape=jax.ShapeDtypeStruct((n_slots, dw), xs.dtype),
        grid_spec=grid_spec,
        compiler_params=_params("arbitrary"),
        name="routed_expert_ffn",
    )(blk_expert, n_used, next_expert, expert_slot, xs, e_gate, e_up, e_down)


def _final_kernel(pos_ref, h_ref, x1_ref, g_ref, ys_ref, wg_ref, wu_ref, wd_ref, m_ref, fg_ref,
                  o_ref, buf, sems):
    tr = h_ref.shape[1]

    def issue(t, carry):
        for k in range(TOP_K):
            pltpu.make_async_copy(ys_ref.at[pl.ds(pos_ref[k, t], 1)], buf.at[k, pl.ds(t, 1)],
                                  sems.at[k]).start()
        return carry

    lax.fori_loop(0, tr, issue, 0, unroll=4)
    h = _bf(h_ref[0])
    a = _dot(h, wg_ref[...])
    u = _dot(h, wu_ref[...])
    moe = _dot(_bf(a * jax.nn.sigmoid(a) * u), wd_ref[...])
    g = g_ref[...]
    for k in range(TOP_K):
        pltpu.make_async_copy(ys_ref.at[pl.ds(0, tr)], buf.at[k], sems.at[k]).wait()
    for k in range(TOP_K):
        moe = moe + g[:, k:k + 1] * buf[k]
    x2 = x1_ref[0] + m_ref[0, 3:4, :] * moe
    ms = jnp.mean(x2 * x2, axis=-1, keepdims=True)
    o_ref[0] = x2 * lax.rsqrt(ms + NORM_EPS) * fg_ref[...]


def _final(h2, x1, pos8, gate_rows, ys, s_gate_b, s_up_b, s_down_b, m2, final_g, tr):
    b, t, d = x1.shape
    ff = s_gate_b.shape[1]
    nt = t // tr
    dw = h2.shape[2]
    row = lambda width=d: pl.BlockSpec((1, tr, width), lambda bb, i: (bb, i, 0))
    return pl.pallas_call(
        _final_kernel,
        out_shape=jax.ShapeDtypeStruct((b, t, d), F32),
        grid=(b, nt),
        in_specs=[pl.BlockSpec((TOP_K, tr), lambda bb, i: (0, bb * nt + i), memory_space=pltpu.SMEM),
                  row(dw), row(),
                  pl.BlockSpec((tr, TOP_K), lambda bb, i: (bb * nt + i, 0)),
                  pl.BlockSpec(memory_space=pl.ANY),
                  pl.BlockSpec((d, ff), lambda bb, i: (0, 0)),
                  pl.BlockSpec((d, ff), lambda bb, i: (0, 0)),
                  pl.BlockSpec((ff, d), lambda bb, i: (0, 0)),
                  pl.BlockSpec((1, 4, d), lambda bb, i: (bb, 0, 0)),
                  pl.BlockSpec((1, d), lambda bb, i: (0, 0))],
        out_specs=row(),
        scratch_shapes=[pltpu.VMEM((TOP_K, tr, dw), ys.dtype), pltpu.SemaphoreType.DMA((TOP_K,))],
        compiler_params=_params("parallel", "parallel"),
        name="combine_shared_expert_final_norm",
    )(pos8, h2, x1, gate_rows, ys, s_gate_b, s_up_b, s_down_b, m2, final_g)


def _pad_to(a, axis, n):
    pad = [(0, 0)] * a.ndim
    pad[axis] = (0, n - a.shape[axis])
    return jnp.pad(a, pad)


def _layout(rd, wl, al, gl, ad, kvd):
    assert wl <= LANES and al <= LANES
    lay = {"rd": rd, "gl": gl, "ad": ad, "kvd": kvd}
    off = 0
    for name, width in (("q", ad), ("r", rd), ("k", rd), ("v", rd), ("ka", kvd), ("va", kvd),
                        ("xw", LANES), ("xa", LANES), ("xg", gl)):
        assert width % LANES == 0
        lay[name] = off
        off += width
    lay["ncols"] = off
    return lay


def kernel(x, c, ctx, c_ctx, w_mod, b_mod, norm1_g, norm2_g, w_in, shift_mu, k_k, k_a, r_k, decay_bias, decay_up, iclr_bias, iclr_up, gate_up, lnx_g, lnx_b, attn_sink, w_out, w_router, expert_bias, e_gate, e_up, e_down, s_gate, s_up, s_down, final_g):
    assert w_mod.shape[0] == 1, "single-layer block only"
    b, t, d = x.shape
    ctx_len = ctx.shape[1]
    rh, hd = r_k.shape[1:]
    assert hd == HEAD_DIM and b + 1 <= SUBLANES
    rd = rh * hd
    wl, al, gl = decay_up.shape[2], iclr_up.shape[2], gate_up.shape[1]
    ad = attn_sink.shape[1] * hd
    kvd = (w_in.shape[2] - (3 * rd + wl + al + gl) - ad) // 2
    ne = w_router.shape[2]
    lay = _layout(rd, wl, al, gl, ad, kvd)
    tr = 256 if ctx_len % 256 == 0 and t % 256 == 0 else CHUNK
    assert ctx_len % CHUNK == 0 and t % CHUNK == 0 and t % QBLOCK == 0 and WINDOW == QBLOCK

    c8 = _pad_to(jnp.concatenate([c, c_ctx[None]], axis=0), 0, SUBLANES)
    mod = _modulation(c8, w_mod[0], b_mod[0][None])
    mod6 = mod.reshape(SUBLANES, 6, d)
    mx, mc = mod6[:b], mod6[b]
    m1 = jnp.stack([jnp.broadcast_to(mc[None, 0:2], (b, 2, d)), mx[:, 0:2]], axis=1)
    m2 = jnp.stack([mx[:, 2], mx[:, 3], mx[:, 4], mx[:, 5]], axis=1)

    wi = w_in[0]
    o_w, o_a, o_g, o_q = 3 * rd, 3 * rd + wl, 3 * rd + wl + al, 3 * rd + wl + al + gl
    w_in_p = _bf(jnp.concatenate([
        wi[:, o_q:o_q + ad], wi[:, :3 * rd], wi[:, o_q + ad:],
        _pad_to(wi[:, o_w:o_a], 1, LANES), _pad_to(wi[:, o_a:o_g], 1, LANES), wi[:, o_g:o_q]], axis=1))
    mu = shift_mu[0]
    prm = {
        "mu_r": mu[None, :rd], "mu_k": mu[None, rd:2 * rd], "mu_v": mu[None, 2 * rd:3 * rd],
        "mu_w": _pad_to(mu[None, o_w:o_a], 1, LANES), "mu_a": _pad_to(mu[None, o_a:o_g], 1, LANES),
        "mu_g": mu[None, o_g:o_q],
        "k_k": k_k, "k_a": k_a, "r_k": r_k.reshape(1, rd),
        "decay_bias": decay_bias[0], "decay_up": _bf(_pad_to(decay_up[0], 1, LANES)),
        "iclr_bias": iclr_bias[0], "iclr_up": _bf(_pad_to(iclr_up[0], 1, LANES)),
        "gate_up": _bf(gate_up[0]),
    }

    p_all = _in_projection(ctx, x, norm1_g, m1, w_in_p, tr)

    npair = rd // LANES
    pp = max(n for n in (1, 2, 4, 8) if npair % n == 0
             and all(lay[key] % (n * LANES) == 0 for key in ("r", "k", "v")))
    rp, y0, mm, nn, bonus, gate = _wkv_prep(p_all, lay, prm, ctx_len, None, pp)
    y_f, y_r = _wkv_sweep(rp, y0, mm, nn, ctx_len)

    nf = HEAD_DIM // 4
    freqs = ROPE_BASE ** (-jnp.arange(nf, dtype=F32) / nf)
    pos = jnp.arange(t, dtype=jnp.int32)
    ang_r = (pos // GRID_W).astype(F32)[:, None] * freqs
    ang_c = (pos % GRID_W).astype(F32)[:, None] * freqs
    cos_h = jnp.concatenate([jnp.cos(ang_r)] * 2 + [jnp.cos(ang_c)] * 2, axis=1)
    sin_h = jnp.concatenate([-jnp.sin(ang_r), jnp.sin(ang_r), -jnp.sin(ang_c), jnp.sin(ang_c)], axis=1)
    cos_t = jnp.concatenate([cos_h, cos_h], axis=1)
    sin_t = jnp.concatenate([sin_h, sin_h], axis=1)
    attn = _attention(p_all, lay, cos_t, sin_t, attn_sink, ctx_len, t)

    x1, h2, logits_t = _out_projection(y_f, y_r, bonus, gate, attn, x, lnx_g, lnx_b, _bf(w_out[0]), m2,
                                       norm2_g, w_router[0], ctx_len, tr)
    idx8, gate8, rank8, counts = _router(logits_t, expert_bias[0][:, None])

    n_tok = b * t
    bm = EXPERT_ROWS
    assert (n_tok * TOP_K) % bm == 0
    n_blk = n_tok * TOP_K // bm + ne
    cnt = counts[:, 0].astype(jnp.int32)
    padded = (cnt + bm - 1) // bm * bm
    pend = jnp.cumsum(padded)
    pstart = pend - padded
    pos8 = _slot_positions(idx8, rank8, pstart.astype(F32)[:, None])
    blk_row0 = jnp.arange(n_blk, dtype=jnp.int32) * bm
    blk_expert = jnp.minimum(jnp.sum((pend[None, :] <= blk_row0[:, None]).astype(jnp.int32), axis=1), ne - 1)
    n_used = (pend[-1:] // bm).astype(jnp.int32)

    xs = _dispatch(h2.reshape(n_tok, d), pos8, pstart + cnt, padded - cnt, n_used, n_blk * bm, tr, bm)
    has_rows = cnt > 0
    eids = jnp.arange(ne, dtype=jnp.int32)
    later = jnp.where(has_rows[None, :] & (eids[None, :] > eids[:, None]), eids[None, :], ne)
    next_expert = jnp.min(later, axis=1)
    next_expert = jnp.where(next_expert < ne, next_expert, -1).astype(jnp.int32)
    expert_slot = ((jnp.cumsum(has_rows.astype(jnp.int32)) - 1) % 2).astype(jnp.int32)
    ys = _expert_ffn(xs, blk_expert, n_used, next_expert, expert_slot, e_gate[0], e_up[0], e_down[0], bm)
    return _final(h2, x1, pos8, gate8.T, ys, _bf(s_gate[0]), _bf(s_up[0]), _bf(s_down[0]),
                  m2, final_g[None], tr)
```
